```python
import jax, jax.numpy as jnp
from jax import lax
import numpy as np

D_MODEL = 1024
BATCH = 2
SEQ = 8192
DEPTH = 2
DEC_BATCH = 128
DEC_SEQ = 4
PAST_LEN = 8192
PAGE_SIZE = 128

N_A_LAYERS = DEPTH // 2
N_B_LAYERS = DEPTH - N_A_LAYERS
ML_HEADS = 4
ML_DK = D_MODEL // 8
ML_DV = D_MODEL // 4
ML_CHUNK = 64
ATT_Q_HEADS = 16
ATT_KV_HEADS = 4
ATT_GROUP = ATT_Q_HEADS // ATT_KV_HEADS
ATT_HD = 64
WINDOW = 128
ROT_DIM = ATT_HD // 4
ROPE_THETA = 500000.0
D_FF = 2816
ALPHA = (2 * DEPTH) ** 0.25
BETA = (8 * DEPTH) ** -0.25
LN_EPS = 1e-5
N_MOD = 9
F32 = jnp.float32

kernel_name = 'yoco_mlstm_swa_sink_macaron_step'


def layer_norm(x, g, b):
    xf = x.astype(F32)
    mu = jnp.mean(xf, -1, keepdims=True)
    var = jnp.mean(jnp.square(xf - mu), -1, keepdims=True)
    return ((xf - mu) * lax.rsqrt(var + LN_EPS) * g.astype(F32) + b.astype(F32)).astype(x.dtype)


def modulate(x, shift, scale):
    return x * (1.0 + scale[:, None, :]) + shift[:, None, :]


def swiglu(x, w_up, w_down):
    a, u = jnp.split(x @ w_up, 2, axis=-1)
    return (jax.nn.silu(a) * u) @ w_down


def rotary(x, pos):
    half = ROT_DIM // 2
    inv_freq = ROPE_THETA ** (-jnp.arange(half, dtype=F32) * 2.0 / ROT_DIM)
    ang = pos.astype(F32)[:, None] * inv_freq[None, :]
    cos = jnp.cos(ang)[:, None, :].astype(x.dtype)
    sin = jnp.sin(ang)[:, None, :].astype(x.dtype)
    x1 = x[..., :half]
    x2 = x[..., half:ROT_DIM]
    return jnp.concatenate([x1 * cos - x2 * sin, x2 * cos + x1 * sin, x[..., ROT_DIM:]], axis=-1)


def mlstm_chunkwise(q, k, v, i_pre, log_f, C0, n0, m0, chunk):
    B, S, H, DK = q.shape
    DV = v.shape[-1]
    nc = S // chunk

    def blocks(a):
        return a.astype(F32).reshape(B, nc, chunk, H, a.shape[-1]).transpose(1, 0, 3, 2, 4)

    def gate_blocks(a):
        return a.astype(F32).reshape(B, nc, chunk, H).transpose(1, 0, 3, 2)

    causal = jnp.tril(jnp.ones((chunk, chunk), dtype=bool))

    def step(carry, inp):
        C, n, m = carry
        qc, kc, vc, ic, fc = inp
        b = jnp.cumsum(fc, axis=-1)
        log_w = b[..., :, None] - b[..., None, :] + ic[..., None, :]
        log_w = jnp.where(causal, log_w, -jnp.inf)
        log_init = b + m[..., None]
        m_t = jnp.maximum(log_init, jnp.max(log_w, axis=-1))
        w = jnp.exp(log_w - m_t[..., None])
        w_init = jnp.exp(log_init - m_t)
        s = jnp.einsum('bhtd,bhsd->bhts', qc, kc) * w
        num = jnp.einsum('bhts,bhsv->bhtv', s, vc) + w_init[..., None] * jnp.einsum('bhtd,bhdv->bhtv', qc, C)
        den = jnp.sum(s, axis=-1) + w_init * jnp.einsum('bhtd,bhd->bht', qc, n)
        h = num / jnp.maximum(jnp.abs(den), jnp.exp(-m_t))[..., None]
        m_new = m_t[..., -1]
        w_end = jnp.exp(b[..., -1:] - b + ic - m_new[..., None])
        decay = jnp.exp(b[..., -1] + m - m_new)
        C_new = decay[..., None, None] * C + jnp.einsum('bhs,bhsd,bhsv->bhdv', w_end, kc, vc)
        n_new = decay[..., None] * n + jnp.einsum('bhs,bhsd->bhd', w_end, kc)
        return (C_new, n_new, m_new), h

    carry0 = (C0.astype(F32), n0.astype(F32), m0.astype(F32))
    (C, n, m), hs = lax.scan(step, carry0, (blocks(q), blocks(k), blocks(v), gate_blocks(i_pre), gate_blocks(log_f)))
    hs = hs.transpose(1, 0, 3, 2, 4).reshape(B, S, H, DV)
    return hs, (C, n, m)


def mlstm_mixer(h, w_in, b_gates, norm_w, w_out, C0, n0, m0, chunk):
    B, S, _ = h.shape
    HQK = ML_HEADS * ML_DK
    HV = ML_HEADS * ML_DV
    proj = h @ w_in
    q = proj[..., :HQK].reshape(B, S, ML_HEADS, ML_DK)
    k = proj[..., HQK:2 * HQK].reshape(B, S, ML_HEADS, ML_DK) * (ML_DK ** -0.5)
    v = proj[..., 2 * HQK:2 * HQK + HV].reshape(B, S, ML_HEADS, ML_DV)
    o = proj[..., 2 * HQK + HV:2 * HQK + 2 * HV]
    gates = proj[..., 2 * HQK + 2 * HV:].astype(F32) + b_gates.astype(F32)
    i_pre = gates[..., :ML_HEADS]
    log_f = jax.nn.log_sigmoid(gates[..., ML_HEADS:])
    hh, state = mlstm_chunkwise(q, k, v, i_pre, log_f, C0, n0, m0, chunk)
    mu = jnp.mean(hh, -1, keepdims=True)
    var = jnp.mean(jnp.square(hh - mu), -1, keepdims=True)
    hn = (hh - mu) * lax.rsqrt(var + LN_EPS) * norm_w.astype(F32).reshape(ML_HEADS, ML_DV)
    y = hn.reshape(B, S, HV).astype(h.dtype) * jax.nn.sigmoid(o)
    return y @ w_out, state


def shared_kv(x, c_act, ada_w, ada_b, w_kv, pos):
    B, S, _ = x.shape
    mods = (c_act @ ada_w + ada_b).reshape(B, 2, D_MODEL)
    hkv = modulate(x, mods[:, 0], mods[:, 1])
    k, v = jnp.split(hkv @ w_kv, 2, axis=-1)
    k = rotary(k.reshape(B, S, ATT_KV_HEADS, ATT_HD), pos)
    return k, v.reshape(B, S, ATT_KV_HEADS, ATT_HD)


def window_mask(qpos, kpos):
    kp = kpos[..., None, :]
    qp = qpos[..., :, None]
    return (kp <= qp) & (kp > qp - WINDOW) & (kp >= 0)


def sink_softmax(s, sinks):
    sink = sinks.astype(F32).reshape(ATT_KV_HEADS, ATT_GROUP, 1, 1)
    mx = jnp.maximum(jnp.max(s, axis=-1, keepdims=True), sink)
    p = jnp.exp(s - mx)
    return p / (jnp.sum(p, axis=-1, keepdims=True) + jnp.exp(sink - mx))


def swa_banded(q, k, v, sinks):
    B, S, _, _ = q.shape
    nb = S // WINDOW
    qb = q.reshape(B, nb, WINDOW, ATT_KV_HEADS, ATT_GROUP, ATT_HD).astype(F32)
    kb = k.reshape(B, nb, WINDOW, ATT_KV_HEADS, ATT_HD)
    vb = v.reshape(B, nb, WINDOW, ATT_KV_HEADS, ATT_HD)
    prev = lambda a: jnp.concatenate([jnp.zeros_like(a[:, :1]), a[:, :-1]], axis=1)
    kk = jnp.concatenate([prev(kb), kb], axis=2).astype(F32)
    vv = jnp.concatenate([prev(vb), vb], axis=2)
    qpos = jnp.arange(nb)[:, None] * WINDOW + jnp.arange(WINDOW)[None, :]
    kpos = qpos[:, :1] - WINDOW + jnp.arange(2 * WINDOW)[None, :]
    mask = window_mask(qpos, kpos)
    s = jnp.einsum('bnqkgd,bnskd->bnkgqs', qb, kk) * (ATT_HD ** -0.5)
    s = jnp.where(mask[None, :, None, None], s, -jnp.inf)
    p = sink_softmax(s, sinks)
    o = jnp.einsum('bnkgqs,bnskd->bnqkgd', p.astype(vv.dtype), vv)
    return o.reshape(B, S, ATT_Q_HEADS * ATT_HD)


def swa_cached(q, k_new, v_new, k_buf, v_buf, sinks):
    B, T, _, _ = q.shape
    Wb = k_buf.shape[1]
    kk = jnp.concatenate([k_buf.astype(k_new.dtype), k_new], axis=1).astype(F32)
    vv = jnp.concatenate([v_buf.astype(v_new.dtype), v_new], axis=1)
    qpos = PAST_LEN + jnp.arange(T)
    kpos = PAST_LEN - Wb + jnp.arange(Wb + T)
    mask = window_mask(qpos, kpos)
    qg = q.reshape(B, T, ATT_KV_HEADS, ATT_GROUP, ATT_HD).astype(F32)
    s = jnp.einsum('btkgd,bskd->bkgts', qg, kk) * (ATT_HD ** -0.5)
    s = jnp.where(mask, s, -jnp.inf)
    p = sink_softmax(s, sinks)
    o = jnp.einsum('bkgts,bskd->btkgd', p.astype(vv.dtype), vv)
    return o.reshape(B, T, ATT_Q_HEADS * ATT_HD)


def trunk(x, c, is_prompt, C0, n0, m0, k_buf, v_buf, ada_w, ada_b, ffn1_up, ffn1_down, ffn2_up, ffn2_down,
          ln_g, ln_b, ml_w_in, ml_b_gates, ml_norm_w, ml_w_out, kv_ada_w, kv_ada_b, kv_w,
          att_w_q, att_sinks, att_w_o):
    B, S, _ = x.shape
    if is_prompt:
        pos = jnp.arange(S, dtype=jnp.int32)
        chunk = min(ML_CHUNK, S)
    else:
        pos = PAST_LEN + jnp.arange(S, dtype=jnp.int32)
        chunk = S
    c_act = jax.nn.silu(c)
    Cs, ns, ms = [], [], []
    k_sh, v_sh = None, None
    for layer in range(DEPTH):
        mods = (c_act @ ada_w[layer] + ada_b[layer]).reshape(B, N_MOD, D_MODEL)
        f1 = swiglu(modulate(x, mods[:, 0], mods[:, 1]), ffn1_up[layer], ffn1_down[layer])
        x = layer_norm(ALPHA * x + 0.5 * mods[:, 2, None, :] * f1, ln_g[layer, 0], ln_b[layer, 0])
        h = modulate(x, mods[:, 3], mods[:, 4])
        if layer < N_A_LAYERS:
            y, (C, n, m) = mlstm_mixer(h, ml_w_in[layer], ml_b_gates[layer], ml_norm_w[layer], ml_w_out[layer],
                                       C0[layer], n0[layer], m0[layer], chunk)
            Cs.append(C)
            ns.append(n)
            ms.append(m)
        else:
            j = layer - N_A_LAYERS
            q = rotary((h @ att_w_q[j]).reshape(B, S, ATT_Q_HEADS, ATT_HD), pos)
            if is_prompt:
                a = swa_banded(q, k_sh, v_sh, att_sinks[j])
            else:
                a = swa_cached(q, k_sh, v_sh, k_buf, v_buf, att_sinks[j])
            y = a @ att_w_o[j]
        x = layer_norm(ALPHA * x + mods[:, 5, None, :] * y, ln_g[layer, 1], ln_b[layer, 1])
        f2 = swiglu(modulate(x, mods[:, 6], mods[:, 7]), ffn2_up[layer], ffn2_down[layer])
        x = layer_norm(ALPHA * x + 0.5 * mods[:, 8, None, :] * f2, ln_g[layer, 2], ln_b[layer, 2])
        if layer == N_A_LAYERS - 1:
            k_sh, v_sh = shared_kv(x, c_act, kv_ada_w, kv_ada_b, kv_w, pos)
    if is_prompt:
        keep = min(WINDOW, S)
        win_k = k_sh[:, S - keep:]
        win_v = v_sh[:, S - keep:]
    else:
        keep = k_buf.shape[1]
        win_k = jnp.concatenate([k_buf.astype(k_sh.dtype), k_sh], axis=1)[:, -keep:]
        win_v = jnp.concatenate([v_buf.astype(v_sh.dtype), v_sh], axis=1)[:, -keep:]
    return x, jnp.stack(Cs), jnp.stack(ns), jnp.stack(ms), win_k, win_v


def setup_inputs(seed: int = 0) -> dict:
    key = jax.random.key(seed)
    keys = iter(jax.random.split(key, 48))

    def nrm(shape, scale=1.0):
        return jax.random.normal(next(keys), shape, F32) * scale

    D, F = D_MODEL, D_FF
    HQK = ML_HEADS * ML_DK
    HV = ML_HEADS * ML_DV
    KVW = ATT_KV_HEADS * ATT_HD
    win_buf = min(WINDOW, PAST_LEN)
    x_prompt = nrm((BATCH, SEQ, D))
    x_sample = nrm((DEC_BATCH, DEC_SEQ, D))
    c_prompt = nrm((BATCH, D))
    c_sample = nrm((DEC_BATCH, D))
    state_mlstm_C = nrm((N_A_LAYERS, DEC_BATCH, ML_HEADS, ML_DK, ML_DV), 0.1)
    state_mlstm_n = nrm((N_A_LAYERS, DEC_BATCH, ML_HEADS, ML_DK), 0.1)
    state_mlstm_m = nrm((N_A_LAYERS, DEC_BATCH, ML_HEADS))
    cache_win_k = nrm((DEC_BATCH, win_buf, ATT_KV_HEADS, ATT_HD))
    cache_win_v = nrm((DEC_BATCH, win_buf, ATT_KV_HEADS, ATT_HD))
    ada_w = nrm((DEPTH, D, N_MOD * D), 0.5 * D ** -0.5)
    ada_b = nrm((DEPTH, N_MOD * D), 0.02)
    ffn1_up = nrm((DEPTH, D, 2 * F), D ** -0.5)
    ffn1_down = nrm((DEPTH, F, D), BETA * F ** -0.5)
    ffn2_up = nrm((DEPTH, D, 2 * F), D ** -0.5)
    ffn2_down = nrm((DEPTH, F, D), BETA * F ** -0.5)
    ln_g = 1.0 + nrm((DEPTH, 3, D), 0.02)
    ln_b = nrm((DEPTH, 3, D), 0.02)
    ml_w_in = jnp.concatenate([
        nrm((N_A_LAYERS, D, 2 * HQK), D ** -0.5),
        nrm((N_A_LAYERS, D, HV), BETA * D ** -0.5),
        nrm((N_A_LAYERS, D, HV), D ** -0.5),
        nrm((N_A_LAYERS, D, 2 * ML_HEADS), 0.1 * D ** -0.5)], axis=-1)
    ml_b_gates = jnp.concatenate([
        nrm((N_A_LAYERS, ML_HEADS), 0.1),
        jnp.linspace(3.0, 6.0, ML_HEADS)[None, :] + nrm((N_A_LAYERS, ML_HEADS), 0.1)], axis=-1)
    ml_norm_w = 1.0 + nrm((N_A_LAYERS, HV), 0.02)
    ml_w_out = nrm((N_A_LAYERS, HV, D), BETA * HV ** -0.5)
    kv_ada_w = nrm((D, 2 * D), 0.5 * D ** -0.5)
    kv_ada_b = nrm((2 * D,), 0.02)
    kv_w = jnp.concatenate([nrm((D, KVW), D ** -0.5), nrm((D, KVW), BETA * D ** -0.5)], axis=-1)
    att_w_q = nrm((N_B_LAYERS, D, ATT_Q_HEADS * ATT_HD), D ** -0.5)
    att_sinks = nrm((N_B_LAYERS, ATT_Q_HEADS), 0.5)
    att_w_o = nrm((N_B_LAYERS, ATT_Q_HEADS * ATT_HD, D), BETA * (ATT_Q_HEADS * ATT_HD) ** -0.5)
    return {'x_prompt': x_prompt, 'x_sample': x_sample, 'c_prompt': c_prompt, 'c_sample': c_sample,
            'state_mlstm_C': state_mlstm_C, 'state_mlstm_n': state_mlstm_n, 'state_mlstm_m': state_mlstm_m,
            'cache_win_k': cache_win_k, 'cache_win_v': cache_win_v,
            'ada_w': ada_w, 'ada_b': ada_b, 'ffn1_up': ffn1_up, 'ffn1_down': ffn1_down,
            'ffn2_up': ffn2_up, 'ffn2_down': ffn2_down, 'ln_g': ln_g, 'ln_b': ln_b,
            'ml_w_in': ml_w_in, 'ml_b_gates': ml_b_gates, 'ml_norm_w': ml_norm_w, 'ml_w_out': ml_w_out,
            'kv_ada_w': kv_ada_w, 'kv_ada_b': kv_ada_b, 'kv_w': kv_w,
            'att_w_q': att_w_q, 'att_sinks': att_sinks, 'att_w_o': att_w_o}


def reference(x_prompt, x_sample, c_prompt, c_sample, state_mlstm_C, state_mlstm_n, state_mlstm_m,
              cache_win_k, cache_win_v, ada_w, ada_b, ffn1_up, ffn1_down, ffn2_up, ffn2_down, ln_g, ln_b,
              ml_w_in, ml_b_gates, ml_norm_w, ml_w_out, kv_ada_w, kv_ada_b, kv_w,
              att_w_q, att_sinks, att_w_o):
    B = x_prompt.shape[0]
    C0 = jnp.zeros((N_A_LAYERS, B, ML_HEADS, ML_DK, ML_DV), F32)
    n0 = jnp.zeros((N_A_LAYERS, B, ML_HEADS, ML_DK), F32)
    m0 = jnp.zeros((N_A_LAYERS, B, ML_HEADS), F32)
    y_prompt, C_p, n_p, m_p, wk_p, wv_p = trunk(
        x_prompt, c_prompt, True, C0, n0, m0, None, None, ada_w, ada_b, ffn1_up, ffn1_down, ffn2_up, ffn2_down,
        ln_g, ln_b, ml_w_in, ml_b_gates, ml_norm_w, ml_w_out, kv_ada_w, kv_ada_b, kv_w, att_w_q, att_sinks, att_w_o)
    y_sample, C_s, n_s, m_s, wk_s, wv_s = trunk(
        x_sample, c_sample, False, state_mlstm_C, state_mlstm_n, state_mlstm_m, cache_win_k, cache_win_v,
        ada_w, ada_b, ffn1_up, ffn1_down, ffn2_up, ffn2_down, ln_g, ln_b, ml_w_in, ml_b_gates, ml_norm_w,
        ml_w_out, kv_ada_w, kv_ada_b, kv_w, att_w_q, att_sinks, att_w_o)
    return (y_prompt, y_sample, C_p, n_p, m_p, wk_p, wv_p, C_s, n_s, m_s, wk_s, wv_s)
```

```python
import functools

import jax
import jax.numpy as jnp
from jax import lax
from jax.experimental import pallas as pl
from jax.experimental.pallas import tpu as pltpu

F32 = jnp.float32
BF16 = jnp.bfloat16

D_MODEL = 1024
DEPTH = 2
PAST_LEN = 8192
ML_HEADS = 4
ML_DK = D_MODEL // 8
ML_DV = D_MODEL // 4
ATT_Q_HEADS = 16
ATT_KV_HEADS = 4
ATT_GROUP = ATT_Q_HEADS // ATT_KV_HEADS
ATT_HD = 64
WINDOW = 128
ROT_DIM = ATT_HD // 4
ROPE_THETA = 500000.0
D_FF = 2816
ALPHA = (2 * DEPTH) ** 0.25
LN_EPS = 1e-5
N_MOD = 9
HQK = ML_HEADS * ML_DK
HV = ML_HEADS * ML_DV
KVW = ATT_KV_HEADS * ATT_HD
QW = ATT_Q_HEADS * ATT_HD

LANES = 128
SUBLANES = 8
VMEM_LIMIT_BYTES = 56 * 1024 * 1024

FFN_ROWS = 512
FFN_CHUNK = 512
ML_ROWS = 512
ML_CHUNK = 128
ATT_ROWS = 256
KV_ROWS = 512
ADA_COLS = 1024
SAMPLE_GROUP = 8

NT_DIMS = (((1,), (1,)), ((), ()))


def _params(semantics):
    return pltpu.CompilerParams(dimension_semantics=semantics, vmem_limit_bytes=VMEM_LIMIT_BYTES)


def _resident(shape):
    zeros = (0,) * len(shape)
    return pl.BlockSpec(shape, lambda *_: zeros, pipeline_mode=pl.Buffered(1))


def _dot(a, b):
    return jnp.dot(a, b, preferred_element_type=F32)


def _silu(x):
    return x * jax.nn.sigmoid(x)


def _log_sigmoid(x):
    return jnp.minimum(x, 0.0) - jnp.log1p(jnp.exp(-jnp.abs(x)))


def _layer_norm(y, g, b):
    mu = jnp.mean(y, axis=-1, keepdims=True)
    d = y - mu
    var = jnp.mean(d * d, axis=-1, keepdims=True)
    return d * lax.rsqrt(var + LN_EPS) * g + b


def _modulate_bf16(x, m_ref):
    return (x * (1.0 + m_ref[1]) + m_ref[0]).astype(BF16)


def _rope(x, cos, sin_lo, sin_hi):
    width = x.shape[1]
    reps = width // LANES
    tile = lambda t: jnp.concatenate([t] * reps, axis=1) if reps > 1 else t
    x_up = pltpu.roll(x, width - ROT_DIM // 2, 1)
    x_dn = pltpu.roll(x, ROT_DIM // 2, 1)
    return x * tile(cos) + x_up * tile(sin_lo) + x_dn * tile(sin_hi)


def _ada_kernel(c_ref, w_ref, b_ref, o_ref):
    ca = _silu(c_ref[...]).astype(BF16)
    o_ref[...] = _dot(ca, w_ref[...].astype(BF16)) + b_ref[...]


def _ada(c, w, b):
    n_l, k, n = w.shape
    m = c.shape[0]
    return pl.pallas_call(
        _ada_kernel,
        grid=(n_l, n // ADA_COLS),
        in_specs=[pl.BlockSpec((m, k), lambda l, j: (0, 0)),
                  pl.BlockSpec((None, k, ADA_COLS), lambda l, j: (l, 0, j)),
                  pl.BlockSpec((None, 1, ADA_COLS), lambda l, j: (l, 0, j))],
        out_specs=pl.BlockSpec((None, m, ADA_COLS), lambda l, j: (l, 0, j)),
        out_shape=jax.ShapeDtypeStruct((n_l, m, n), F32),
        compiler_params=_params(("arbitrary", "arbitrary")),
        name="ada",
    )(c, w, b)


def _ffn_kernel(x_ref, m_ref, wa_ref, wu_ref, wd_ref, g_ref, b_ref, o_ref, acc_ref):
    x = x_ref[...]
    xm = _modulate_bf16(x, m_ref)
    for c0 in range(0, D_FF, FFN_CHUNK):
        cw = min(FFN_CHUNK, D_FF - c0)
        a = _dot(xm, wa_ref[:, c0:c0 + cw])
        u = _dot(xm, wu_ref[:, c0:c0 + cw])
        p = _dot((_silu(a) * u).astype(BF16), wd_ref[c0:c0 + cw, :])
        if c0 == 0:
            acc_ref[...] = p
        else:
            acc_ref[...] += p
    y = ALPHA * x + (0.5 * m_ref[2]) * acc_ref[...]
    o_ref[...] = _layer_norm(y, g_ref[...], b_ref[...])


def _mod_spec(mods, group, rows, rows_per_batch):
    if mods.ndim == 3:
        return pl.BlockSpec((3, rows, D_MODEL), lambda i: (group, i, 0))
    per_batch = rows_per_batch // rows
    return pl.BlockSpec((3, None, 1, D_MODEL), lambda i: (group, i // per_batch, 0, 0))


def _ffn(x, mods, group, wa, wu, wd, g, b, rows_per_batch):
    t = x.shape[0]
    rows = min(FFN_ROWS, t)
    return pl.pallas_call(
        _ffn_kernel,
        grid=(t // rows,),
        in_specs=[pl.BlockSpec((rows, D_MODEL), lambda i: (i, 0)),
                  _mod_spec(mods, group, rows, rows_per_batch),
                  _resident(wa.shape), _resident(wu.shape), _resident(wd.shape),
                  _resident(g.shape), _resident(b.shape)],
        out_specs=pl.BlockSpec((rows, D_MODEL), lambda i: (i, 0)),
        out_shape=jax.ShapeDtypeStruct((t, D_MODEL), F32),
        scratch_shapes=[pltpu.VMEM((rows, D_MODEL), F32)],
        compiler_params=_params(("arbitrary",)),
        name="ffn",
    )(x, mods, wa, wu, wd, g, b)


def _kv_kernel(x_ref, m_ref, w_ref, cos_ref, slo_ref, shi_ref, k_ref, v_ref):
    kv = _dot(_modulate_bf16(x_ref[...], m_ref), w_ref[...])
    k_ref[...] = _rope(kv[:, :KVW], cos_ref[...], slo_ref[...], shi_ref[...])
    v_ref[...] = kv[:, KVW:]


def _shared_kv(x, mods, w, tables, rows_per_batch):
    t = x.shape[0]
    rows = min(KV_ROWS, t)
    pos_blocks = tables[0].shape[0] // rows
    if mods.ndim == 3:
        mod_spec = pl.BlockSpec((2, rows, D_MODEL), lambda i: (0, i, 0))
    else:
        per_batch = rows_per_batch // rows
        mod_spec = pl.BlockSpec((2, None, 1, D_MODEL), lambda i: (0, i // per_batch, 0, 0))
    tab_spec = pl.BlockSpec((rows, LANES), lambda i: (i % pos_blocks, 0))
    out_spec = pl.BlockSpec((rows, KVW), lambda i: (i, 0))
    return pl.pallas_call(
        _kv_kernel,
        grid=(t // rows,),
        in_specs=[pl.BlockSpec((rows, D_MODEL), lambda i: (i, 0)), mod_spec, _resident(w.shape),
                  tab_spec, tab_spec, tab_spec],
        out_specs=[out_spec, out_spec],
        out_shape=[jax.ShapeDtypeStruct((t, KVW), F32)] * 2,
        compiler_params=_params(("arbitrary",)),
        name="shared_kv",
    )(x, mods, w, *tables)


def _mlstm_chunk(q, k, v, bcol, icol, c_mat, n_row, m_prev, *, on_mxu, rows):
    size = q.shape[0]
    r = lax.broadcasted_iota(jnp.int32, (size, size), 0)
    c = lax.broadcasted_iota(jnp.int32, (size, size), 1)
    causal = c <= r
    eye = c == r
    brow = jnp.sum(jnp.where(eye, bcol, 0.0), axis=0, keepdims=True)
    irow = jnp.sum(jnp.where(eye, icol, 0.0), axis=0, keepdims=True)
    log_w = jnp.where(causal, bcol - brow + irow, -jnp.inf)
    log_init = bcol + m_prev
    m_t = jnp.maximum(log_init, jnp.max(log_w, axis=-1, keepdims=True))
    w = jnp.exp(log_w - m_t)
    w_init = jnp.exp(log_init - m_t)
    qb = q.astype(BF16)
    s = lax.dot_general(qb, k.astype(BF16), NT_DIMS, preferred_element_type=F32) * w
    inter = _dot(qb, c_mat.astype(BF16))
    if on_mxu:
        intra = _dot(s.astype(BF16), v.astype(BF16))
    else:
        intra = sum(s[:, j:j + 1] * v[j:j + 1, :] for j in rows)
    num = intra + w_init * inter
    den = jnp.sum(s, axis=-1, keepdims=True) + w_init * jnp.sum(q * n_row, axis=-1, keepdims=True)
    hh = num / jnp.maximum(jnp.abs(den), jnp.exp(-m_t))
    m_new = m_t[size - 1:size, :]
    b_last = bcol[size - 1:size, :]
    w_end = jnp.exp(b_last - bcol + icol - m_new)
    decay = jnp.exp(b_last + m_prev - m_new)
    kw = k * w_end
    if on_mxu:
        upd = _dot(kw.T.astype(BF16), v.astype(BF16))
    else:
        ident = (lax.broadcasted_iota(jnp.int32, (ML_DK, ML_DK), 0)
                 == lax.broadcasted_iota(jnp.int32, (ML_DK, ML_DK), 1)).astype(F32)
        kw_t = lax.dot_general(ident, kw, NT_DIMS, preferred_element_type=F32,
                               precision=lax.Precision.HIGHEST)
        upd = sum(kw_t[:, j:j + 1] * v[j:j + 1, :] for j in rows)
    c_new = decay * c_mat + upd
    n_new = decay * n_row + jnp.sum(kw, axis=0, keepdims=True)
    return hh, c_new, n_new, m_new


def _head_norm(hh, w_row):
    mu = jnp.mean(hh, axis=-1, keepdims=True)
    d = hh - mu
    var = jnp.mean(d * d, axis=-1, keepdims=True)
    return d * lax.rsqrt(var + LN_EPS) * w_row


def _mlstm_prompt_kernel(x_ref, m_ref, wp_ref, wg_ref, bg_ref, nw_ref, wo_ref, g_ref, b_ref,
                         o_ref, c_out, n_out, m_out, proj_ref, gate_ref, y_ref, c_ref, n_ref, mm_ref):
    step = pl.program_id(1)

    @pl.when(step == 0)
    def _():
        c_ref[...] = jnp.zeros_like(c_ref)
        n_ref[...] = jnp.zeros_like(n_ref)
        mm_ref[...] = jnp.zeros_like(mm_ref)

    x = x_ref[...]
    h = _modulate_bf16(x, m_ref)
    proj_ref[...] = _dot(h, wp_ref[...])
    gate_ref[...] = _dot(h, wg_ref[...]) + bg_ref[...]

    size = ML_CHUNK
    tri = (lax.broadcasted_iota(jnp.int32, (size, size), 1)
           <= lax.broadcasted_iota(jnp.int32, (size, size), 0)).astype(F32)

    def chunk_body(ci, carry):
        r0 = pl.multiple_of(ci * size, size)
        gates = gate_ref[pl.ds(r0, size), :]
        bwide = jnp.dot(tri, _log_sigmoid(gates), preferred_element_type=F32,
                        precision=lax.Precision.HIGHEST)
        for hd in range(ML_HEADS):
            q = proj_ref[pl.ds(r0, size), hd * ML_DK:(hd + 1) * ML_DK]
            k = proj_ref[pl.ds(r0, size), HQK + hd * ML_DK:HQK + (hd + 1) * ML_DK] * (ML_DK ** -0.5)
            v = proj_ref[pl.ds(r0, size), 2 * HQK + hd * ML_DV:2 * HQK + (hd + 1) * ML_DV]
            hh, c_new, n_new, m_new = _mlstm_chunk(
                q, k, v, bwide[:, ML_HEADS + hd:ML_HEADS + hd + 1], gates[:, hd:hd + 1],
                c_ref[hd], n_ref[hd:hd + 1, :], mm_ref[hd:hd + 1, :1], on_mxu=True, rows=None)
            c_ref[hd] = c_new
            n_ref[hd:hd + 1, :] = n_new
            mm_ref[hd:hd + 1, :] = jnp.broadcast_to(m_new, (1, LANES))
            y_ref[pl.ds(r0, size), hd * ML_DV:(hd + 1) * ML_DV] = _head_norm(
                hh, nw_ref[:, hd * ML_DV:(hd + 1) * ML_DV])
        return carry

    lax.fori_loop(0, x.shape[0] // size, chunk_body, 0)

    o_gate = jax.nn.sigmoid(proj_ref[:, 2 * HQK + HV:2 * HQK + 2 * HV])
    y = _dot((y_ref[...] * o_gate).astype(BF16), wo_ref[...])
    o_ref[...] = _layer_norm(ALPHA * x + m_ref[2] * y, g_ref[...], b_ref[...])

    @pl.when(step == pl.num_programs(1) - 1)
    def _():
        c_out[...] = c_ref[...]
        n_out[...] = n_ref[...]
        m_out[...] = mm_ref[...]


def _mlstm_prompt(x, mods, wp, wg, bg, nw, wo, g, b):
    bsz, seq, _ = x.shape
    rows = min(ML_ROWS, seq)
    return pl.pallas_call(
        _mlstm_prompt_kernel,
        grid=(bsz, seq // rows),
        in_specs=[pl.BlockSpec((None, rows, D_MODEL), lambda bi, i: (bi, i, 0)),
                  pl.BlockSpec((3, None, 1, D_MODEL), lambda bi, i: (1, bi, 0, 0)),
                  _resident(wp.shape), _resident(wg.shape), _resident(bg.shape), _resident(nw.shape),
                  _resident(wo.shape), _resident(g.shape), _resident(b.shape)],
        out_specs=[pl.BlockSpec((None, rows, D_MODEL), lambda bi, i: (bi, i, 0)),
                   pl.BlockSpec((None, ML_HEADS, ML_DK, ML_DV), lambda bi, i: (bi, 0, 0, 0)),
                   pl.BlockSpec((None, SUBLANES, LANES), lambda bi, i: (bi, 0, 0)),
                   pl.BlockSpec((None, SUBLANES, LANES), lambda bi, i: (bi, 0, 0))],
        out_shape=[jax.ShapeDtypeStruct((bsz, seq, D_MODEL), F32),
                   jax.ShapeDtypeStruct((bsz, ML_HEADS, ML_DK, ML_DV), F32),
                   jax.ShapeDtypeStruct((bsz, SUBLANES, LANES), F32),
                   jax.ShapeDtypeStruct((bsz, SUBLANES, LANES), F32)],
        scratch_shapes=[pltpu.VMEM((rows, 2 * HQK + 2 * HV), F32),
                        pltpu.VMEM((rows, LANES), F32),
                        pltpu.VMEM((rows, HV), F32),
                        pltpu.VMEM((ML_HEADS, ML_DK, ML_DV), F32),
                        pltpu.VMEM((SUBLANES, LANES), F32),
                        pltpu.VMEM((SUBLANES, LANES), F32)],
        compiler_params=_params(("arbitrary", "arbitrary")),
        name="mlstm_prompt",
    )(x, mods, wp, wg, bg, nw, wo, g, b)


def _mlstm_sample_kernel(x_ref, m_ref, wp_ref, wg_ref, bg_ref, nw_ref, wo_ref, g_ref, b_ref,
                         c0_ref, n0_ref, m0_ref, o_ref, c_out, n_out, m_out,
                         proj_ref, gate_ref, y_ref, *, seq):
    step = pl.program_id(0)
    pair = 2 * seq
    assert pair == SUBLANES

    @pl.when(step == 0)
    def _():
        h = _modulate_bf16(x_ref[...], m_ref)
        proj_ref[...] = _dot(h, wp_ref[...])
        gate_ref[...] = _dot(h, wg_ref[...]) + bg_ref[...]

    low = lax.broadcasted_iota(jnp.int32, (pair, 1), 0) < seq
    low_wide = lax.broadcasted_iota(jnp.int32, (pair, ML_DV), 0) < seq
    tri = (lax.broadcasted_iota(jnp.int32, (pair, pair), 1)
           <= lax.broadcasted_iota(jnp.int32, (pair, pair), 0)).astype(F32)
    for p in range(SAMPLE_GROUP // 2):
        r0 = pl.multiple_of(step * (SAMPLE_GROUP * seq) + p * pair, pair)
        gates = gate_ref[pl.ds(r0, pair), :]
        log_f = _log_sigmoid(gates)
        for hd in range(ML_HEADS):
            q = proj_ref[pl.ds(r0, pair), hd * ML_DK:(hd + 1) * ML_DK]
            k = proj_ref[pl.ds(r0, pair), HQK + hd * ML_DK:HQK + (hd + 1) * ML_DK] * (ML_DK ** -0.5)
            v = proj_ref[pl.ds(r0, pair), 2 * HQK + hd * ML_DV:2 * HQK + (hd + 1) * ML_DV]
            icol = gates[:, hd:hd + 1]
            fcol = log_f[:, ML_HEADS + hd:ML_HEADS + hd + 1]
            halves = []
            for half in range(2):
                mine = low if half == 0 else jnp.logical_not(low)
                f_h = jnp.where(mine, fcol, 0.0)
                i_h = jnp.where(mine, icol, -jnp.inf)
                b_h = sum(tri[:, j:j + 1] * f_h[j:j + 1, :] for j in range(pair))
                bi = 2 * p + half
                hh, c_new, n_new, m_new = _mlstm_chunk(
                    q, k, v, b_h, i_h, c0_ref[bi, hd], n0_ref[bi, hd:hd + 1, :],
                    m0_ref[bi, hd:hd + 1, :1], on_mxu=False,
                    rows=range(half * seq, (half + 1) * seq))
                c_out[bi, hd] = c_new
                n_out[bi, hd:hd + 1, :] = n_new
                m_out[bi, hd:hd + 1, :] = jnp.broadcast_to(m_new, (1, LANES))
                halves.append(hh)
            hh = jnp.where(low_wide, halves[0], halves[1])
            y_ref[pl.ds(r0, pair), hd * ML_DV:(hd + 1) * ML_DV] = _head_norm(
                hh, nw_ref[:, hd * ML_DV:(hd + 1) * ML_DV])

    @pl.when(step == pl.num_programs(0) - 1)
    def _():
        o_gate = jax.nn.sigmoid(proj_ref[:, 2 * HQK + HV:2 * HQK + 2 * HV])
        y = _dot((y_ref[...] * o_gate).astype(BF16), wo_ref[...])
        o_ref[...] = _layer_norm(ALPHA * x_ref[...] + m_ref[2] * y, g_ref[...], b_ref[...])


def _mlstm_sample(x, mods, wp, wg, bg, nw, wo, g, b, c0, n0, m0, seq):
    t = x.shape[0]
    bsz = t // seq
    grp = SAMPLE_GROUP
    state_specs = [pl.BlockSpec((grp, ML_HEADS, ML_DK, ML_DV), lambda i: (i, 0, 0, 0)),
                   pl.BlockSpec((grp, ML_HEADS, ML_DK), lambda i: (i, 0, 0)),
                   pl.BlockSpec((grp, ML_HEADS, LANES), lambda i: (i, 0, 0))]
    return pl.pallas_call(
        functools.partial(_mlstm_sample_kernel, seq=seq),
        grid=(bsz // grp,),
        in_specs=[_resident(x.shape),
                  pl.BlockSpec((3, t, D_MODEL), lambda i: (1, 0, 0), pipeline_mode=pl.Buffered(1)),
                  _resident(wp.shape), _resident(wg.shape), _resident(bg.shape), _resident(nw.shape),
                  _resident(wo.shape), _resident(g.shape), _resident(b.shape)] + state_specs,
        out_specs=[pl.BlockSpec((t, D_MODEL), lambda i: (0, 0))] + state_specs,
        out_shape=[jax.ShapeDtypeStruct((t, D_MODEL), F32),
                   jax.ShapeDtypeStruct(c0.shape, F32),
                   jax.ShapeDtypeStruct(n0.shape, F32),
                   jax.ShapeDtypeStruct(m0.shape, F32)],
        scratch_shapes=[pltpu.VMEM((t, 2 * HQK + 2 * HV), F32),
                        pltpu.VMEM((t, LANES), F32),
                        pltpu.VMEM((t, HV), F32)],
        compiler_params=_params(("arbitrary",)),
        name="mlstm_sample",
    )(x, mods, wp, wg, bg, nw, wo, g, b, c0, n0, m0)


def _sink_softmax_pv(s_parts, v_parts, sink_col):
    mx = sink_col
    for s in s_parts:
        mx = jnp.maximum(mx, jnp.max(s, axis=-1, keepdims=True))
    ps = [jnp.exp(s - mx) for s in s_parts]
    den = jnp.exp(sink_col - mx)
    for p in ps:
        den = den + jnp.sum(p, axis=-1, keepdims=True)
    out = None
    for p, v in zip(ps, v_parts):
        pn = p / den
        if v.shape[0] >= LANES:
            o = _dot(pn.astype(BF16), v.astype(BF16))
        else:
            o = sum(pn[:, j:j + 1] * v[j:j + 1, :] for j in range(v.shape[0]))
        out = o if out is None else out + o
    return out


def _sink_column(sink_ref, kh, rows_per_head):
    rid = lax.broadcasted_iota(jnp.int32, (ATT_GROUP * rows_per_head, 1), 0)
    col = jnp.full((ATT_GROUP * rows_per_head, 1), sink_ref[kh * ATT_GROUP], F32)
    for gq in range(1, ATT_GROUP):
        col = jnp.where(rid >= gq * rows_per_head, sink_ref[kh * ATT_GROUP + gq], col)
    return col


def _attn_prompt_kernel(sink_ref, x_ref, m_ref, wq_ref, cos_ref, slo_ref, shi_ref,
                        kp_ref, kc_ref, vp_ref, vc_ref, wo_ref, g_ref, b_ref, o_ref, a_ref):
    step = pl.program_id(1)
    x = x_ref[...]
    rows = x.shape[0]
    q = _rope(_dot(_modulate_bf16(x, m_ref), wq_ref[...]), cos_ref[...], slo_ref[...], shi_ref[...])
    q = (q * (ATT_HD ** -0.5)).astype(BF16)

    rid = lax.broadcasted_iota(jnp.int32, (ATT_GROUP * WINDOW, 2 * WINDOW), 0) % WINDOW
    cid = lax.broadcasted_iota(jnp.int32, (ATT_GROUP * WINDOW, 2 * WINDOW), 1)
    band = (cid > rid) & (cid <= rid + WINDOW)
    for n in range(rows // WINDOW):
        lo, hi = n * WINDOW, (n + 1) * WINDOW
        if n == 0:
            k_prev, v_prev = kp_ref[...], vp_ref[...]
            mask = band & (cid >= jnp.where(step > 0, 0, WINDOW))
        else:
            k_prev, v_prev = kc_ref[lo - WINDOW:lo, :], vc_ref[lo - WINDOW:lo, :]
            mask = band
        kk = jnp.concatenate([k_prev, kc_ref[lo:hi, :]], axis=0).astype(BF16)
        vv = jnp.concatenate([v_prev, vc_ref[lo:hi, :]], axis=0)
        for kh in range(ATT_KV_HEADS):
            heads = range(kh * ATT_GROUP, (kh + 1) * ATT_GROUP)
            qs = jnp.concatenate([q[lo:hi, hq * ATT_HD:(hq + 1) * ATT_HD] for hq in heads], axis=0)
            s = lax.dot_general(qs, kk[:, kh * ATT_HD:(kh + 1) * ATT_HD], NT_DIMS,
                                preferred_element_type=F32)
            s = jnp.where(mask, s, -jnp.inf)
            o = _sink_softmax_pv([s], [vv[:, kh * ATT_HD:(kh + 1) * ATT_HD]],
                                 _sink_column(sink_ref, kh, WINDOW))
            for gq, hq in enumerate(heads):
                a_ref[lo:hi, hq * ATT_HD:(hq + 1) * ATT_HD] = o[gq * WINDOW:(gq + 1) * WINDOW, :]

    y = _dot(a_ref[...].astype(BF16), wo_ref[...])
    o_ref[...] = _layer_norm(ALPHA * x + m_ref[2] * y, g_ref[...], b_ref[...])


def _attn_prompt(x, mods, wq, tables, k, v, sinks, wo, g, b):
    bsz, seq, _ = x.shape
    rows = min(ATT_ROWS, seq)
    per = rows // WINDOW
    cur = pl.BlockSpec((None, rows, KVW), lambda bi, i: (bi, i, 0))
    prev = pl.BlockSpec((None, WINDOW, KVW), lambda bi, i: (bi, jnp.maximum(i * per - 1, 0), 0))
    tab = pl.BlockSpec((rows, LANES), lambda bi, i: (i, 0))
    return pl.pallas_call(
        _attn_prompt_kernel,
        grid=(bsz, seq // rows),
        in_specs=[pl.BlockSpec(memory_space=pltpu.SMEM),
                  pl.BlockSpec((None, rows, D_MODEL), lambda bi, i: (bi, i, 0)),
                  pl.BlockSpec((3, None, 1, D_MODEL), lambda bi, i: (1, bi, 0, 0)),
                  _resident(wq.shape), tab, tab, tab, prev, cur, prev, cur,
                  _resident(wo.shape), _resident(g.shape), _resident(b.shape)],
        out_specs=pl.BlockSpec((None, rows, D_MODEL), lambda bi, i: (bi, i, 0)),
        out_shape=jax.ShapeDtypeStruct((bsz, seq, D_MODEL), F32),
        scratch_shapes=[pltpu.VMEM((rows, QW), F32)],
        compiler_params=_params(("arbitrary", "arbitrary")),
        name="attn_prompt",
    )(sinks, x, mods, wq, *tables, k, k, v, v, wo, g, b)


def _attn_sample_kernel(sink_ref, x_ref, m_ref, wq_ref, cos_ref, slo_ref, shi_ref,
                        kn_ref, vn_ref, kb_ref, vb_ref, wo_ref, g_ref, b_ref, o_ref,
                        q_ref, a_ref, *, seq):
    step = pl.program_id(0)
    pair = 2 * seq
    assert pair == SUBLANES
    n_buf = kb_ref.shape[1]

    @pl.when(step == 0)
    def _():
        q = _rope(_dot(_modulate_bf16(x_ref[...], m_ref), wq_ref[...]),
                  cos_ref[...], slo_ref[...], shi_ref[...])
        q_ref[...] = q * (ATT_HD ** -0.5)

    n_rows = ATT_GROUP * pair
    low = lax.broadcasted_iota(jnp.int32, (n_rows, ATT_HD), 0) % pair < seq
    tq = lax.broadcasted_iota(jnp.int32, (n_rows, n_buf), 0) % seq + PAST_LEN
    kpos = lax.broadcasted_iota(jnp.int32, (n_rows, n_buf), 1) + (PAST_LEN - n_buf)
    buf_mask = (kpos <= tq) & (kpos > tq - WINDOW) & (kpos >= 0)
    tq_new = lax.broadcasted_iota(jnp.int32, (n_rows, pair), 0) % seq
    col_new = lax.broadcasted_iota(jnp.int32, (n_rows, pair), 1)
    for p in range(SAMPLE_GROUP // 2):
        r0 = pl.multiple_of(step * (SAMPLE_GROUP * seq) + p * pair, pair)
        qp = q_ref[pl.ds(r0, pair), :].astype(BF16)
        k_new = kn_ref[pl.ds(r0, pair), :]
        v_new = vn_ref[pl.ds(r0, pair), :]
        for kh in range(ATT_KV_HEADS):
            heads = range(kh * ATT_GROUP, (kh + 1) * ATT_GROUP)
            sl = slice(kh * ATT_HD, (kh + 1) * ATT_HD)
            qs = jnp.concatenate([qp[:, hq * ATT_HD:(hq + 1) * ATT_HD] for hq in heads], axis=0)
            s_new = lax.dot_general(qs, k_new[:, sl].astype(BF16), NT_DIMS, preferred_element_type=F32)
            sink_col = _sink_column(sink_ref, kh, pair)
            halves = []
            for half in range(2):
                own = (col_new >= half * seq) & (col_new < (half + 1) * seq)
                new_mask = own & (col_new - half * seq <= tq_new)
                bi = 2 * p + half
                s_buf = lax.dot_general(qs, kb_ref[bi, :, sl].astype(BF16), NT_DIMS,
                                        preferred_element_type=F32)
                halves.append(_sink_softmax_pv(
                    [jnp.where(buf_mask, s_buf, -jnp.inf), jnp.where(new_mask, s_new, -jnp.inf)],
                    [vb_ref[bi, :, sl], v_new[:, sl]], sink_col))
            o = jnp.where(low, halves[0], halves[1])
            for gq, hq in enumerate(heads):
                a_ref[pl.ds(r0, pair), hq * ATT_HD:(hq + 1) * ATT_HD] = o[gq * pair:(gq + 1) * pair, :]

    @pl.when(step == pl.num_programs(0) - 1)
    def _():
        y = _dot(a_ref[...].astype(BF16), wo_ref[...])
        o_ref[...] = _layer_norm(ALPHA * x_ref[...] + m_ref[2] * y, g_ref[...], b_ref[...])


def _attn_sample(x, mods, wq, tables, k_new, v_new, k_buf, v_buf, sinks, wo, g, b, seq):
    t = x.shape[0]
    bsz = t // seq
    grp = SAMPLE_GROUP
    buf_spec = pl.BlockSpec((grp,) + k_buf.shape[1:], lambda i: (i, 0, 0))
    return pl.pallas_call(
        functools.partial(_attn_sample_kernel, seq=seq),
        grid=(bsz // grp,),
        in_specs=[pl.BlockSpec(memory_space=pltpu.SMEM),
                  _resident(x.shape),
                  pl.BlockSpec((3, t, D_MODEL), lambda i: (1, 0, 0), pipeline_mode=pl.Buffered(1)),
                  _resident(wq.shape), _resident(tables[0].shape), _resident(tables[1].shape),
                  _resident(tables[2].shape), _resident(k_new.shape), _resident(v_new.shape),
                  buf_spec, buf_spec, _resident(wo.shape), _resident(g.shape), _resident(b.shape)],
        out_specs=pl.BlockSpec((t, D_MODEL), lambda i: (0, 0)),
        out_shape=jax.ShapeDtypeStruct((t, D_MODEL), F32),
        scratch_shapes=[pltpu.VMEM((t, QW), F32), pltpu.VMEM((t, QW), F32)],
        compiler_params=_params(("arbitrary",)),
        name="attn_sample",
    )(sinks, x, mods, wq, *tables, k_new, v_new, k_buf, v_buf, wo, g, b)


def _rope_tables(pos):
    half = ROT_DIM // 2
    inv_freq = ROPE_THETA ** (-jnp.arange(half, dtype=F32) * 2.0 / ROT_DIM)
    ang = pos.astype(F32)[:, None] * inv_freq[None, :]
    cos, sin = jnp.cos(ang), jnp.sin(ang)
    n = pos.shape[0]
    rest = ATT_HD - ROT_DIM
    head = lambda a, b, fill: jnp.concatenate([a, b, jnp.full((n, rest), fill, F32)], axis=1)
    zero = jnp.zeros_like(sin)
    reps = LANES // ATT_HD
    return tuple(jnp.tile(t, (1, reps)) for t in
                 (head(cos, cos, 1.0), head(-sin, zero, 0.0), head(zero, sin, 0.0)))


def kernel(x_prompt, x_sample, c_prompt, c_sample, state_mlstm_C, state_mlstm_n, state_mlstm_m,
           cache_win_k, cache_win_v, ada_w, ada_b, ffn1_up, ffn1_down, ffn2_up, ffn2_down, ln_g, ln_b,
           ml_w_in, ml_b_gates, ml_norm_w, ml_w_out, kv_ada_w, kv_ada_b, kv_w,
           att_w_q, att_sinks, att_w_o):
    bp, sp, _ = x_prompt.shape
    bs, ss, _ = x_sample.shape
    n_buf = cache_win_k.shape[1]

    n_c = bp + bs
    c_all = jnp.concatenate([c_prompt, c_sample, jnp.zeros((-n_c % SUBLANES, D_MODEL), F32)], axis=0)
    mods = _ada(c_all, ada_w, ada_b[:, None, :])
    kv_mods = _ada(c_all, kv_ada_w[None], kv_ada_b[None, None, :])[0]

    def split_mods(m, n_vec):
        m = m[:n_c].reshape(n_c, n_vec, D_MODEL).transpose(1, 0, 2)
        return m[:, :bp, None, :], jnp.repeat(m[:, bp:], ss, axis=1)
    mods_p, mods_s = zip(*(split_mods(mods[l], N_MOD) for l in range(DEPTH)))
    kv_mods_p, kv_mods_s = split_mods(kv_mods, 2)

    bf = lambda w: w.astype(BF16)
    ffn_w = [[(bf(up[l, :, :D_FF]), bf(up[l, :, D_FF:]), bf(down[l]))
              for up, down in ((ffn1_up, ffn1_down), (ffn2_up, ffn2_down))] for l in range(DEPTH)]
    w_proj = bf(ml_w_in[0, :, :2 * HQK + 2 * HV])
    gate_pad = LANES - 2 * ML_HEADS
    w_gate = bf(jnp.pad(ml_w_in[0, :, 2 * HQK + 2 * HV:], ((0, 0), (0, gate_pad))))
    b_gate = jnp.pad(ml_b_gates[0], (0, gate_pad))[None, :]
    norm_w = ml_norm_w[0][None, :]
    w_out = bf(ml_w_out[0])
    w_kv = bf(kv_w)
    w_q = bf(att_w_q[0])
    w_o = bf(att_w_o[0])
    sinks = att_sinks[0]
    row = lambda a: a[None, :]

    tab_p = _rope_tables(jnp.arange(sp, dtype=jnp.int32))
    tab_s = _rope_tables(jnp.tile(PAST_LEN + jnp.arange(ss, dtype=jnp.int32), bs))

    xp = x_prompt.reshape(bp * sp, D_MODEL)
    xp = _ffn(xp, mods_p[0], 0, *ffn_w[0][0], row(ln_g[0, 0]), row(ln_b[0, 0]), sp)
    xp, c_p, n_p, m_p = _mlstm_prompt(xp.reshape(bp, sp, D_MODEL), mods_p[0], w_proj, w_gate, b_gate,
                                      norm_w, w_out, row(ln_g[0, 1]), row(ln_b[0, 1]))
    xp = _ffn(xp.reshape(bp * sp, D_MODEL), mods_p[0], 2, *ffn_w[0][1], row(ln_g[0, 2]), row(ln_b[0, 2]), sp)
    k_p, v_p = _shared_kv(xp, kv_mods_p, w_kv, tab_p, sp)
    xp = _ffn(xp, mods_p[1], 0, *ffn_w[1][0], row(ln_g[1, 0]), row(ln_b[1, 0]), sp)
    xp = _attn_prompt(xp.reshape(bp, sp, D_MODEL), mods_p[1], w_q, tab_p,
                      k_p.reshape(bp, sp, KVW), v_p.reshape(bp, sp, KVW), sinks, w_o,
                      row(ln_g[1, 1]), row(ln_b[1, 1]))
    xp = _ffn(xp.reshape(bp * sp, D_MODEL), mods_p[1], 2, *ffn_w[1][1], row(ln_g[1, 2]), row(ln_b[1, 2]), sp)
    y_prompt = xp.reshape(bp, sp, D_MODEL)
    keep = min(WINDOW, sp)
    win_k_p = k_p.reshape(bp, sp, ATT_KV_HEADS, ATT_HD)[:, sp - keep:]
    win_v_p = v_p.reshape(bp, sp, ATT_KV_HEADS, ATT_HD)[:, sp - keep:]

    ts = bs * ss
    xs = x_sample.reshape(ts, D_MODEL)
    xs = _ffn(xs, mods_s[0], 0, *ffn_w[0][0], row(ln_g[0, 0]), row(ln_b[0, 0]), ss)
    m0 = jnp.broadcast_to(state_mlstm_m[0][:, :, None], (bs, ML_HEADS, LANES))
    xs, c_s, n_s, m_s = _mlstm_sample(xs, mods_s[0], w_proj, w_gate, b_gate, norm_w, w_out,
                                      row(ln_g[0, 1]), row(ln_b[0, 1]),
                                      state_mlstm_C[0], state_mlstm_n[0], m0, ss)
    xs = _ffn(xs, mods_s[0], 2, *ffn_w[0][1], row(ln_g[0, 2]), row(ln_b[0, 2]), ss)
    k_s, v_s = _shared_kv(xs, kv_mods_s, w_kv, tab_s, ss)
    xs = _ffn(xs, mods_s[1], 0, *ffn_w[1][0], row(ln_g[1, 0]), row(ln_b[1, 0]), ss)
    xs = _attn_sample(xs, mods_s[1], w_q, tab_s, k_s, v_s,
                      cache_win_k.reshape(bs, n_buf, KVW), cache_win_v.reshape(bs, n_buf, KVW),
                      sinks, w_o, row(ln_g[1, 1]), row(ln_b[1, 1]), ss)
    xs = _ffn(xs, mods_s[1], 2, *ffn_w[1][1], row(ln_g[1, 2]), row(ln_b[1, 2]), ss)
    y_sample = xs.reshape(bs, ss, D_MODEL)
    win_k_s = jnp.concatenate([cache_win_k, k_s.reshape(bs, ss, ATT_KV_HEADS, ATT_HD)], axis=1)[:, -n_buf:]
    win_v_s = jnp.concatenate([cache_win_v, v_s.reshape(bs, ss, ATT_KV_HEADS, ATT_HD)], axis=1)[:, -n_buf:]

    return (y_prompt, y_sample,
            c_p[None], n_p[None, :, :ML_HEADS, :], m_p[None, :, :ML_HEADS, 0],
            win_k_p, win_v_p,
            c_s[None], n_s[None], m_s[None, :, :, 0],
            win_k_s, win_v_s)
```

```python
import functools

import numpy as np
import jax
import jax.numpy as jnp
from jax import lax
from jax.experimental import pallas as pl
from jax.experimental.pallas import tpu as pltpu

F32 = jnp.float32
BF16 = jnp.bfloat16

D_MODEL = 1024
DEPTH = 2
PAST_LEN = 8192
ML_HEADS = 4
ML_DK = D_MODEL // 8
ML_DV = D_MODEL // 4
ATT_Q_HEADS = 16
ATT_KV_HEADS = 4
ATT_GROUP = ATT_Q_HEADS // ATT_KV_HEADS
ATT_HD = 64
WINDOW = 128
ROT_DIM = ATT_HD // 4
ROPE_THETA = 500000.0
D_FF = 2816
ALPHA = (2 * DEPTH) ** 0.25
LN_EPS = 1e-5
N_MOD = 9
HQK = ML_HEADS * ML_DK
HV = ML_HEADS * ML_DV
KVW = ATT_KV_HEADS * ATT_HD
QW = ATT_Q_HEADS * ATT_HD

LANES = 128
SUBLANES = 8
VMEM_LIMIT_BYTES = 56 * 1024 * 1024

FFN_ROWS = 512
FFN_CHUNK = 512
ML_ROWS = 512
ML_CHUNK = 128
ATT_ROWS = 256
KV_ROWS = 512
ADA_COLS = 1024
SAMPLE_GROUP = 8

NT_DIMS = (((1,), (1,)), ((), ()))
HIGHEST = lax.Precision.HIGHEST


def _params(semantics):
    return pltpu.CompilerParams(dimension_semantics=semantics, vmem_limit_bytes=VMEM_LIMIT_BYTES)


def _const_spec(block, index):
    return pl.BlockSpec(block, lambda *_: index, pipeline_mode=pl.Buffered(1))


def _resident(shape):
    return _const_spec(shape, (0,) * len(shape))


def _dot(a, b):
    return jnp.dot(a, b, preferred_element_type=F32)


def _dot_exact(a, b):
    return jnp.dot(a, b, preferred_element_type=F32, precision=HIGHEST)


def _nt(a, b):
    return lax.dot_general(a, b, NT_DIMS, preferred_element_type=F32)


def _iota(shape, dim):
    return lax.broadcasted_iota(jnp.int32, shape, dim)


def _silu(x):
    return x * jax.nn.sigmoid(x)


def _log_sigmoid(x):
    return jnp.minimum(x, 0.0) - jnp.log1p(jnp.exp(-jnp.abs(x)))


def _layer_norm(y, g, b):
    mu = jnp.mean(y, axis=-1, keepdims=True)
    d = y - mu
    var = jnp.mean(d * d, axis=-1, keepdims=True)
    return d * lax.rsqrt(var + LN_EPS) * g + b


def _mod(ref, batch):
    return ref[...] if batch is None else ref[pl.ds(batch, 1), :]


def _modulate_bf16(x, shift, scale):
    return (x * (1.0 + scale) + shift).astype(BF16)


def _rope(x, cos, sin_lo, sin_hi):
    width = x.shape[1]
    reps = width // LANES
    tile = lambda t: jnp.concatenate([t] * reps, axis=1) if reps > 1 else t
    x_up = pltpu.roll(x, width - ROT_DIM // 2, 1)
    x_dn = pltpu.roll(x, ROT_DIM // 2, 1)
    return x * tile(cos) + x_up * tile(sin_lo) + x_dn * tile(sin_hi)


def _mod_specs(layer, first, count, n_sample_rows, prompt):
    if prompt:
        blk = n_sample_rows // SUBLANES
        return [_const_spec((None, SUBLANES, D_MODEL), (layer, blk, first + j)) for j in range(count)]
    return [_const_spec((None, n_sample_rows, D_MODEL), (layer, 0, first + j)) for j in range(count)]


def _ln_specs(index):
    return [_const_spec((None, 1, D_MODEL), (index, 0, 0))] * 2


def _ada_kernel(c_ref, w_ref, b_ref, o_ref):
    ca = _silu(c_ref[...]).astype(BF16)
    o_ref[...] = _dot(ca, w_ref[...].astype(BF16)) + b_ref[...]


def _ada(c, w, b):
    n_l, k, n = w.shape
    m = c.shape[0]
    return pl.pallas_call(
        _ada_kernel,
        grid=(n_l, n // ADA_COLS),
        in_specs=[pl.BlockSpec((m, k), lambda l, j: (0, 0)),
                  pl.BlockSpec((None, k, ADA_COLS), lambda l, j: (l, 0, j)),
                  pl.BlockSpec((None, 1, ADA_COLS), lambda l, j: (l, 0, j))],
        out_specs=pl.BlockSpec((None, m, ADA_COLS), lambda l, j: (l, 0, j)),
        out_shape=jax.ShapeDtypeStruct((n_l, m, n), F32),
        compiler_params=_params(("arbitrary", "arbitrary")),
        name="ada",
    )(c, w, b)


def _ffn_kernel(x_ref, sh_ref, sc_ref, gt_ref, wa_ref, wu_ref, wd_ref, g_ref, b_ref, o_ref, acc_ref,
                *, steps_per_seq):
    batch = None if steps_per_seq is None else pl.program_id(0) // steps_per_seq
    x = x_ref[...]
    xm = _modulate_bf16(x, _mod(sh_ref, batch), _mod(sc_ref, batch))
    for c0 in range(0, D_FF, FFN_CHUNK):
        cw = min(FFN_CHUNK, D_FF - c0)
        a = _dot(xm, wa_ref[:, c0:c0 + cw])
        u = _dot(xm, wu_ref[:, c0:c0 + cw])
        p = _dot((_silu(a) * u).astype(BF16), wd_ref[c0:c0 + cw, :])
        if c0 == 0:
            acc_ref[...] = p
        else:
            acc_ref[...] += p
    y = ALPHA * x + (0.5 * _mod(gt_ref, batch)) * acc_ref[...]
    o_ref[...] = _layer_norm(y, g_ref[...], b_ref[...])


def _ffn(x, mods, layer, first, w_up, w_down, ln_g, ln_b, ln_index, n_sample_rows, seq):
    t = x.shape[0]
    rows = min(FFN_ROWS, t)
    return pl.pallas_call(
        functools.partial(_ffn_kernel, steps_per_seq=None if seq is None else seq // rows),
        grid=(t // rows,),
        in_specs=[pl.BlockSpec((rows, D_MODEL), lambda i: (i, 0))]
                 + _mod_specs(layer, first, 3, n_sample_rows, seq is not None)
                 + [_const_spec((None, D_MODEL, D_FF), (layer, 0, 0)),
                    _const_spec((None, D_MODEL, D_FF), (layer, 0, 1)),
                    _const_spec((None, D_FF, D_MODEL), (layer, 0, 0))]
                 + _ln_specs(ln_index),
        out_specs=pl.BlockSpec((rows, D_MODEL), lambda i: (i, 0)),
        out_shape=jax.ShapeDtypeStruct((t, D_MODEL), F32),
        scratch_shapes=[pltpu.VMEM((rows, D_MODEL), F32)],
        compiler_params=_params(("arbitrary",)),
        name="ffn",
    )(x, mods, mods, mods, w_up, w_up, w_down, ln_g, ln_b)


def _kv_kernel(x_ref, sh_ref, sc_ref, w_ref, cos_ref, slo_ref, shi_ref, k_ref, v_ref, *, steps_per_seq):
    batch = None if steps_per_seq is None else pl.program_id(0) // steps_per_seq
    kv = _dot(_modulate_bf16(x_ref[...], _mod(sh_ref, batch), _mod(sc_ref, batch)), w_ref[...])
    k_ref[...] = _rope(kv[:, :KVW], cos_ref[...], slo_ref[...], shi_ref[...])
    v_ref[...] = kv[:, KVW:]


def _shared_kv(x, mods, w, tables, n_sample_rows, seq):
    t = x.shape[0]
    rows = min(KV_ROWS, t)
    pos_blocks = tables[0].shape[0] // rows
    tab_spec = pl.BlockSpec((rows, LANES), lambda i: (i % pos_blocks, 0))
    out_spec = pl.BlockSpec((rows, KVW), lambda i: (i, 0))
    return pl.pallas_call(
        functools.partial(_kv_kernel, steps_per_seq=None if seq is None else seq // rows),
        grid=(t // rows,),
        in_specs=[pl.BlockSpec((rows, D_MODEL), lambda i: (i, 0))]
                 + _mod_specs(0, 0, 2, n_sample_rows, seq is not None)
                 + [_resident(w.shape), tab_spec, tab_spec, tab_spec],
        out_specs=[out_spec, out_spec],
        out_shape=[jax.ShapeDtypeStruct((t, KVW), F32)] * 2,
        compiler_params=_params(("arbitrary",)),
        name="shared_kv",
    )(x, mods, mods, w, *tables)


def _head_norm(hh, w_row):
    mu = jnp.mean(hh, axis=-1, keepdims=True)
    d = hh - mu
    var = jnp.mean(d * d, axis=-1, keepdims=True)
    return d * lax.rsqrt(var + LN_EPS) * w_row


def _mlstm_chunk(q, k, v, bcol, icol, c_mat, n_row, m_prev):
    size = q.shape[0]
    r = _iota((size, size), 0)
    c = _iota((size, size), 1)
    eye = c == r
    brow = jnp.sum(jnp.where(eye, bcol, 0.0), axis=0, keepdims=True)
    irow = jnp.sum(jnp.where(eye, icol, 0.0), axis=0, keepdims=True)
    log_w = jnp.where(c <= r, bcol - brow + irow, -jnp.inf)
    log_init = bcol + m_prev
    m_t = jnp.maximum(log_init, jnp.max(log_w, axis=-1, keepdims=True))
    w = jnp.exp(log_w - m_t)
    w_init = jnp.exp(log_init - m_t)
    qb = q.astype(BF16)
    vb = v.astype(BF16)
    s = _nt(qb, k.astype(BF16)) * w
    num = _dot(s.astype(BF16), vb) + w_init * _dot(qb, c_mat.astype(BF16))
    den = jnp.sum(s, axis=-1, keepdims=True) + w_init * jnp.sum(q * n_row, axis=-1, keepdims=True)
    hh = num / jnp.maximum(jnp.abs(den), jnp.exp(-m_t))
    m_new = m_t[size - 1:size, :]
    b_last = bcol[size - 1:size, :]
    w_end = jnp.exp(b_last - bcol + icol - m_new)
    decay = jnp.exp(b_last + m_prev - m_new)
    kw = k * w_end
    c_new = decay * c_mat + _dot(kw.T.astype(BF16), vb)
    n_new = decay * n_row + jnp.sum(kw, axis=0, keepdims=True)
    return hh, c_new, n_new, m_new


def _mlstm_prompt_kernel(x_ref, sh_ref, sc_ref, gt_ref, wp_ref, wg_ref, bg_ref, nw_ref, wo_ref, g_ref, b_ref,
                         o_ref, c_out, n_out, m_out, proj_ref, gate_ref, y_ref, c_ref, n_ref, mm_ref):
    batch = pl.program_id(0)
    step = pl.program_id(1)

    @pl.when(step == 0)
    def _():
        c_ref[...] = jnp.zeros_like(c_ref)
        n_ref[...] = jnp.zeros_like(n_ref)
        mm_ref[...] = jnp.zeros_like(mm_ref)

    x = x_ref[...]
    h = _modulate_bf16(x, _mod(sh_ref, batch), _mod(sc_ref, batch))
    proj_ref[...] = _dot(h, wp_ref[...])
    gate_ref[...] = _dot(h, wg_ref[...]) + bg_ref[...]

    size = ML_CHUNK
    tri = (_iota((size, size), 1) <= _iota((size, size), 0)).astype(F32)

    def chunk_body(ci, carry):
        r0 = pl.multiple_of(ci * size, size)
        gates = gate_ref[pl.ds(r0, size), :]
        bwide = _dot_exact(tri, _log_sigmoid(gates))
        for hd in range(ML_HEADS):
            q = proj_ref[pl.ds(r0, size), hd * ML_DK:(hd + 1) * ML_DK]
            k = proj_ref[pl.ds(r0, size), HQK + hd * ML_DK:HQK + (hd + 1) * ML_DK] * (ML_DK ** -0.5)
            v = proj_ref[pl.ds(r0, size), 2 * HQK + hd * ML_DV:2 * HQK + (hd + 1) * ML_DV]
            hh, c_new, n_new, m_new = _mlstm_chunk(
                q, k, v, bwide[:, ML_HEADS + hd:ML_HEADS + hd + 1], gates[:, hd:hd + 1],
                c_ref[hd], n_ref[hd:hd + 1, :], mm_ref[hd:hd + 1, :1])
            c_ref[hd] = c_new
            n_ref[hd:hd + 1, :] = n_new
            mm_ref[hd:hd + 1, :] = jnp.broadcast_to(m_new, (1, LANES))
            y_ref[pl.ds(r0, size), hd * ML_DV:(hd + 1) * ML_DV] = _head_norm(
                hh, nw_ref[:, hd * ML_DV:(hd + 1) * ML_DV])
        return carry

    lax.fori_loop(0, x.shape[0] // size, chunk_body, 0)

    o_gate = jax.nn.sigmoid(proj_ref[:, 2 * HQK + HV:2 * HQK + 2 * HV])
    y = _dot((y_ref[...] * o_gate).astype(BF16), wo_ref[...])
    o_ref[...] = _layer_norm(ALPHA * x + _mod(gt_ref, batch) * y, g_ref[...], b_ref[...])

    @pl.when(step == pl.num_programs(1) - 1)
    def _():
        c_out[...] = c_ref[...]
        n_out[...] = n_ref[...]
        m_out[...] = mm_ref[...]


def _mlstm_weight_specs(wg, bg, nw, wo):
    return [_const_spec((None, D_MODEL, 2 * HQK + 2 * HV), (0, 0, 0)), _resident(wg.shape),
            _resident(bg.shape), _resident(nw.shape), _resident(wo.shape)] + _ln_specs(1)


def _mlstm_prompt(x, mods, w_in, wg, bg, nw, wo, ln_g, ln_b, n_sample_rows):
    bsz, seq, _ = x.shape
    rows = min(ML_ROWS, seq)
    x_spec = pl.BlockSpec((None, rows, D_MODEL), lambda bi, i: (bi, i, 0))
    small = pl.BlockSpec((None, SUBLANES, LANES), lambda bi, i: (bi, 0, 0))
    return pl.pallas_call(
        _mlstm_prompt_kernel,
        grid=(bsz, seq // rows),
        in_specs=[x_spec] + _mod_specs(0, 3, 3, n_sample_rows, True) + _mlstm_weight_specs(wg, bg, nw, wo),
        out_specs=[x_spec, pl.BlockSpec((None, ML_HEADS, ML_DK, ML_DV), lambda bi, i: (bi, 0, 0, 0)),
                   small, small],
        out_shape=[jax.ShapeDtypeStruct((bsz, seq, D_MODEL), F32),
                   jax.ShapeDtypeStruct((bsz, ML_HEADS, ML_DK, ML_DV), F32),
                   jax.ShapeDtypeStruct((bsz, SUBLANES, LANES), F32),
                   jax.ShapeDtypeStruct((bsz, SUBLANES, LANES), F32)],
        scratch_shapes=[pltpu.VMEM((rows, 2 * HQK + 2 * HV), F32),
                        pltpu.VMEM((rows, LANES), F32),
                        pltpu.VMEM((rows, HV), F32),
                        pltpu.VMEM((ML_HEADS, ML_DK, ML_DV), F32),
                        pltpu.VMEM((SUBLANES, LANES), F32),
                        pltpu.VMEM((SUBLANES, LANES), F32)],
        compiler_params=_params(("arbitrary", "arbitrary")),
        name="mlstm_prompt",
    )(x, mods, mods, mods, w_in, wg, bg, nw, wo, ln_g, ln_b)


def _mlstm_sample_kernel(x_ref, sh_ref, sc_ref, gt_ref, wp_ref, wg_ref, bg_ref, nw_ref, wo_ref, g_ref, b_ref,
                         c0_ref, n0_ref, m0_ref, o_ref, c_out, n_out, m_out,
                         proj_ref, gate_ref, y_ref, *, seq):
    step = pl.program_id(0)
    grp = c0_ref.shape[0]
    rows = grp * seq

    @pl.when(step == 0)
    def _():
        h = _modulate_bf16(x_ref[...], sh_ref[...], sc_ref[...])
        proj_ref[...] = _dot(h, wp_ref[...])
        gate_ref[...] = _dot(h, wg_ref[...]) + bg_ref[...]

    r = _iota((rows, rows), 0)
    c = _iota((rows, rows), 1)
    same = (r // seq) == (c // seq)
    causal = same & (c <= r)
    eye = r == c
    last_of_row = c == (r // seq) * seq + (seq - 1)
    same_f = same.astype(F32)
    expand = (_iota((rows, grp), 0) // seq == _iota((rows, grp), 1)).astype(F32)
    pick_last = (_iota((grp, rows), 1) == _iota((grp, rows), 0) * seq + (seq - 1)).astype(F32)
    ident = (_iota((ML_DK, ML_DK), 0) == _iota((ML_DK, ML_DK), 1)).astype(F32)
    seq_of_row = _iota((rows, ML_DV), 0) // seq
    lane = _iota((rows, LANES), 1)
    to_row = lambda col: jnp.sum(jnp.where(eye, col, 0.0), axis=0, keepdims=True)

    r0 = pl.multiple_of(step * rows, rows)
    gates = gate_ref[pl.ds(r0, rows), :]
    log_f = _log_sigmoid(gates)
    bwide = _dot_exact(causal.astype(F32), log_f)
    bsum = _dot_exact(same_f, log_f)
    m_prev_wide = _dot_exact(expand, m0_ref[...])
    n_rows_wide = _dot_exact(expand, n0_ref[...])
    n_pieces = []
    m_wide = jnp.zeros((rows, LANES), F32)
    for hd in range(ML_HEADS):
        q = proj_ref[pl.ds(r0, rows), hd * ML_DK:(hd + 1) * ML_DK]
        k = proj_ref[pl.ds(r0, rows), HQK + hd * ML_DK:HQK + (hd + 1) * ML_DK] * (ML_DK ** -0.5)
        v = proj_ref[pl.ds(r0, rows), 2 * HQK + hd * ML_DV:2 * HQK + (hd + 1) * ML_DV]
        bcol = bwide[:, ML_HEADS + hd:ML_HEADS + hd + 1]
        b_last = bsum[:, ML_HEADS + hd:ML_HEADS + hd + 1]
        icol = gates[:, hd:hd + 1]
        m_prev = m_prev_wide[:, hd:hd + 1]
        n_rows = n_rows_wide[:, hd * ML_DK:(hd + 1) * ML_DK]
        log_w = jnp.where(causal, bcol - to_row(bcol) + to_row(icol), -jnp.inf)
        log_init = bcol + m_prev
        m_t = jnp.maximum(log_init, jnp.max(log_w, axis=-1, keepdims=True))
        w = jnp.exp(log_w - m_t)
        w_init = jnp.exp(log_init - m_t)
        qb = q.astype(BF16)
        s = _nt(qb, k.astype(BF16)) * w
        inter = None
        for bi in range(grp):
            cand = _dot(qb, c0_ref[bi, hd].astype(BF16))
            inter = cand if inter is None else jnp.where(seq_of_row == bi, cand, inter)
        num = _dot(s.astype(BF16), v.astype(BF16)) + w_init * inter
        den = jnp.sum(s, axis=-1, keepdims=True) + w_init * jnp.sum(q * n_rows, axis=-1, keepdims=True)
        hh = num / jnp.maximum(jnp.abs(den), jnp.exp(-m_t))
        m_new = jnp.sum(jnp.where(last_of_row, to_row(m_t), 0.0), axis=1, keepdims=True)
        w_end = jnp.exp(b_last - bcol + icol - m_new)
        decay = jnp.exp(b_last + m_prev - m_new)
        kw = k * w_end
        kw_t = lax.dot_general(ident, kw, NT_DIMS, preferred_element_type=F32, precision=HIGHEST)
        for bi in range(grp):
            upd = sum(kw_t[:, j:j + 1] * v[j:j + 1, :] for j in range(bi * seq, (bi + 1) * seq))
            c_out[bi, hd] = decay[bi * seq:bi * seq + 1, :] * c0_ref[bi, hd] + upd
        n_pieces.append(decay * n_rows + _dot_exact(same_f, kw))
        m_wide = jnp.where(lane == hd, m_new, m_wide)
        y_ref[pl.ds(r0, rows), hd * ML_DV:(hd + 1) * ML_DV] = _head_norm(
            hh, nw_ref[:, hd * ML_DV:(hd + 1) * ML_DV])
    n_out[...] = _dot_exact(pick_last, jnp.concatenate(n_pieces, axis=1))
    m_out[...] = _dot_exact(pick_last, m_wide)

    @pl.when(step == pl.num_programs(0) - 1)
    def _():
        o_gate = jax.nn.sigmoid(proj_ref[:, 2 * HQK + HV:2 * HQK + 2 * HV])
        y = _dot((y_ref[...] * o_gate).astype(BF16), wo_ref[...])
        o_ref[...] = _layer_norm(ALPHA * x_ref[...] + gt_ref[...] * y, g_ref[...], b_ref[...])


def _mlstm_sample(x, mods, w_in, wg, bg, nw, wo, ln_g, ln_b, c0, n0, m0, seq):
    t = x.shape[0]
    grp = SAMPLE_GROUP
    state_specs = [pl.BlockSpec((grp, ML_HEADS, ML_DK, ML_DV), lambda i: (i, 0, 0, 0)),
                   pl.BlockSpec((grp, HQK), lambda i: (i, 0)),
                   pl.BlockSpec((grp, LANES), lambda i: (i, 0))]
    return pl.pallas_call(
        functools.partial(_mlstm_sample_kernel, seq=seq),
        grid=(t // (grp * seq),),
        in_specs=[_resident(x.shape)] + _mod_specs(0, 3, 3, t, False)
                 + _mlstm_weight_specs(wg, bg, nw, wo) + state_specs,
        out_specs=[pl.BlockSpec((t, D_MODEL), lambda i: (0, 0))] + state_specs,
        out_shape=[jax.ShapeDtypeStruct((t, D_MODEL), F32),
                   jax.ShapeDtypeStruct(c0.shape, F32),
                   jax.ShapeDtypeStruct(n0.shape, F32),
                   jax.ShapeDtypeStruct(m0.shape, F32)],
        scratch_shapes=[pltpu.VMEM((t, 2 * HQK + 2 * HV), F32),
                        pltpu.VMEM((t, LANES), F32),
                        pltpu.VMEM((t, HV), F32)],
        compiler_params=_params(("arbitrary",)),
        name="mlstm_sample",
    )(x, mods, mods, mods, w_in, wg, bg, nw, wo, ln_g, ln_b, c0, n0, m0)


def _sink_column(sink_ref, kh, rows_per_head):
    rid = _iota((ATT_GROUP * rows_per_head, 1), 0)
    col = jnp.full((ATT_GROUP * rows_per_head, 1), sink_ref[kh * ATT_GROUP], F32)
    for gq in range(1, ATT_GROUP):
        col = jnp.where(rid >= gq * rows_per_head, sink_ref[kh * ATT_GROUP + gq], col)
    return col


def _head_lanes(x, kh):
    return jnp.where(_iota(x.shape, 1) // ATT_HD == kh, x, 0.0).astype(BF16)


def _stack_groups(x):
    return jnp.concatenate([x[:, g * KVW:(g + 1) * KVW] for g in range(ATT_GROUP)], axis=0)


def _attn_prompt_kernel(sink_ref, x_ref, sh_ref, sc_ref, gt_ref, wq_ref, cos_ref, slo_ref, shi_ref,
                        kp_ref, kc_ref, vp_ref, vc_ref, wo_ref, g_ref, b_ref, o_ref, a_ref):
    batch = pl.program_id(0)
    step = pl.program_id(1)
    x = x_ref[...]
    rows = x.shape[0]
    n_blk = rows // WINDOW
    q = _rope(_dot(_modulate_bf16(x, _mod(sh_ref, batch), _mod(sc_ref, batch)), wq_ref[...]),
              cos_ref[...], slo_ref[...], shi_ref[...])
    q = (q * (ATT_HD ** -0.5)).astype(BF16)

    keys = jnp.concatenate([kp_ref[...], kc_ref[...]], axis=0)
    vals = jnp.concatenate([vp_ref[...], vc_ref[...]], axis=0)
    k_heads = [_head_lanes(keys, kh) for kh in range(ATT_KV_HEADS)]
    v_heads = [_head_lanes(vals, kh) for kh in range(ATT_KV_HEADS)]

    rid = _iota((ATT_GROUP * WINDOW, 2 * WINDOW), 0) % WINDOW
    cid = _iota((ATT_GROUP * WINDOW, 2 * WINDOW), 1)
    band = (cid > rid) & (cid <= rid + WINDOW)
    first = band & (cid >= jnp.where(step > 0, 0, WINDOW))
    sinks = [_sink_column(sink_ref, kh, WINDOW) for kh in range(ATT_KV_HEADS)]

    scores = [[_nt(_stack_groups(q[n * WINDOW:(n + 1) * WINDOW, :]),
                   k_heads[kh][n * WINDOW:(n + 2) * WINDOW, :])
               for kh in range(ATT_KV_HEADS)] for n in range(n_blk)]
    probs = []
    for n in range(n_blk):
        row = []
        for kh in range(ATT_KV_HEADS):
            s = jnp.where(first if n == 0 else band, scores[n][kh], -jnp.inf)
            mx = jnp.maximum(jnp.max(s, axis=-1, keepdims=True), sinks[kh])
            p = jnp.exp(s - mx)
            den = jnp.sum(p, axis=-1, keepdims=True) + jnp.exp(sinks[kh] - mx)
            row.append((p / den).astype(BF16))
        probs.append(jnp.concatenate(row, axis=1))
    for n in range(n_blk):
        v_all = jnp.concatenate([v_heads[kh][n * WINDOW:(n + 2) * WINDOW, :]
                                 for kh in range(ATT_KV_HEADS)], axis=0)
        o = _dot(probs[n], v_all)
        for g in range(ATT_GROUP):
            a_ref[n * WINDOW:(n + 1) * WINDOW, g * KVW:(g + 1) * KVW] = o[g * WINDOW:(g + 1) * WINDOW, :]

    y = _dot(a_ref[...].astype(BF16), wo_ref[...])
    o_ref[...] = _layer_norm(ALPHA * x + _mod(gt_ref, batch) * y, g_ref[...], b_ref[...])


def _attn_prompt(x, mods, wq, tables, k, v, sinks, wo, ln_g, ln_b, n_sample_rows):
    bsz, seq, _ = x.shape
    rows = min(ATT_ROWS, seq)
    per = rows // WINDOW
    x_spec = pl.BlockSpec((None, rows, D_MODEL), lambda bi, i: (bi, i, 0))
    cur = pl.BlockSpec((None, rows, KVW), lambda bi, i: (bi, i, 0))
    prev = pl.BlockSpec((None, WINDOW, KVW), lambda bi, i: (bi, jnp.maximum(i * per - 1, 0), 0))
    tab = pl.BlockSpec((rows, LANES), lambda bi, i: (i, 0))
    return pl.pallas_call(
        _attn_prompt_kernel,
        grid=(bsz, seq // rows),
        in_specs=[pl.BlockSpec(memory_space=pltpu.SMEM), x_spec]
                 + _mod_specs(1, 3, 3, n_sample_rows, True)
                 + [_resident(wq.shape), tab, tab, tab, prev, cur, prev, cur, _resident(wo.shape)]
                 + _ln_specs(4),
        out_specs=x_spec,
        out_shape=jax.ShapeDtypeStruct((bsz, seq, D_MODEL), F32),
        scratch_shapes=[pltpu.VMEM((rows, QW), F32)],
        compiler_params=_params(("arbitrary", "arbitrary")),
        name="attn_prompt",
    )(sinks, x, mods, mods, mods, wq, *tables, k, k, v, v, wo, ln_g, ln_b)


def _attn_sample_kernel(sink_ref, x_ref, sh_ref, sc_ref, gt_ref, wq_ref, cos_ref, slo_ref, shi_ref,
                        kn_ref, vn_ref, kb_ref, vb_ref, wo_ref, g_ref, b_ref, o_ref,
                        q_ref, a_ref, *, seq):
    step = pl.program_id(0)
    pair = 2 * seq
    assert pair == SUBLANES
    grp, n_buf, _ = kb_ref.shape

    @pl.when(step == 0)
    def _():
        q = _rope(_dot(_modulate_bf16(x_ref[...], sh_ref[...], sc_ref[...]), wq_ref[...]),
                  cos_ref[...], slo_ref[...], shi_ref[...])
        q_ref[...] = q * (ATT_HD ** -0.5)

    n_rows = ATT_GROUP * pair
    n_new = ATT_KV_HEADS * pair
    q_pos = _iota((n_rows, n_buf), 0) % seq + PAST_LEN
    k_pos = _iota((n_rows, n_buf), 1) + (PAST_LEN - n_buf)
    buf_mask = (k_pos <= q_pos) & (k_pos > q_pos - WINDOW) & (k_pos >= 0)
    q_tok = _iota((n_rows, n_new), 0) % seq
    new_col = _iota((n_rows, n_new), 1)
    new_row_head = _iota((n_new, KVW), 0) // pair
    new_lane_head = _iota((n_new, KVW), 1) // ATT_HD
    low = _iota((n_rows, KVW), 0) % pair < seq
    sinks = [_sink_column(sink_ref, kh, pair) for kh in range(ATT_KV_HEADS)]
    for p in range(grp // 2):
        r0 = pl.multiple_of(step * (grp * seq) + p * pair, pair)
        q_all = _stack_groups(q_ref[pl.ds(r0, pair), :].astype(BF16))
        stack = lambda a: jnp.where(new_row_head == new_lane_head,
                                    jnp.concatenate([a] * ATT_KV_HEADS, axis=0), 0.0).astype(BF16)
        k_new = stack(kn_ref[pl.ds(r0, pair), :])
        v_new = stack(vn_ref[pl.ds(r0, pair), :])
        s_new_all = _nt(q_all, k_new)
        halves = []
        for half in range(2):
            bi = 2 * p + half
            k_buf, v_buf = kb_ref[bi], vb_ref[bi]
            tok = new_col % pair - half * seq
            own_new = (tok >= 0) & (tok < seq) & (tok <= q_tok)
            p_buf, p_new = [], None
            for kh in range(ATT_KV_HEADS):
                s_buf = jnp.where(buf_mask, _nt(q_all, _head_lanes(k_buf, kh)), -jnp.inf)
                s_new = jnp.where(own_new & (new_col // pair == kh), s_new_all, -jnp.inf)
                mx = jnp.maximum(jnp.maximum(jnp.max(s_buf, axis=-1, keepdims=True),
                                             jnp.max(s_new, axis=-1, keepdims=True)), sinks[kh])
                e_buf = jnp.exp(s_buf - mx)
                e_new = jnp.exp(s_new - mx)
                den = (jnp.sum(e_buf, axis=-1, keepdims=True) + jnp.sum(e_new, axis=-1, keepdims=True)
                       + jnp.exp(sinks[kh] - mx))
                p_buf.append((e_buf / den).astype(BF16))
                p_new = e_new / den if p_new is None else p_new + e_new / den
            v_all = jnp.concatenate([_head_lanes(v_buf, kh) for kh in range(ATT_KV_HEADS)], axis=0)
            halves.append(_dot(jnp.concatenate(p_buf, axis=1), v_all) + _dot(p_new.astype(BF16), v_new))
        o = jnp.where(low, halves[0], halves[1])
        for g in range(ATT_GROUP):
            a_ref[pl.ds(r0, pair), g * KVW:(g + 1) * KVW] = o[g * pair:(g + 1) * pair, :]

    @pl.when(step == pl.num_programs(0) - 1)
    def _():
        y = _dot(a_ref[...].astype(BF16), wo_ref[...])
        o_ref[...] = _layer_norm(ALPHA * x_ref[...] + gt_ref[...] * y, g_ref[...], b_ref[...])


def _attn_sample(x, mods, wq, tables, k_new, v_new, k_buf, v_buf, sinks, wo, ln_g, ln_b, seq):
    t = x.shape[0]
    grp = SAMPLE_GROUP
    buf_spec = pl.BlockSpec((grp,) + k_buf.shape[1:], lambda i: (i, 0, 0))
    return pl.pallas_call(
        functools.partial(_attn_sample_kernel, seq=seq),
        grid=(t // (grp * seq),),
        in_specs=[pl.BlockSpec(memory_space=pltpu.SMEM), _resident(x.shape)]
                 + _mod_specs(1, 3, 3, t, False)
                 + [_resident(wq.shape)] + [_resident(tab.shape) for tab in tables]
                 + [_resident(k_new.shape), _resident(v_new.shape), buf_spec, buf_spec, _resident(wo.shape)]
                 + _ln_specs(4),
        out_specs=pl.BlockSpec((t, D_MODEL), lambda i: (0, 0)),
        out_shape=jax.ShapeDtypeStruct((t, D_MODEL), F32),
        scratch_shapes=[pltpu.VMEM((t, QW), F32), pltpu.VMEM((t, QW), F32)],
        compiler_params=_params(("arbitrary",)),
        name="attn_sample",
    )(sinks, x, mods, mods, mods, wq, *tables, k_new, v_new, k_buf, v_buf, wo, ln_g, ln_b)


def _rope_tables(pos):
    half = ROT_DIM // 2
    inv_freq = ROPE_THETA ** (-jnp.arange(half, dtype=F32) * 2.0 / ROT_DIM)
    ang = pos.astype(F32)[:, None] * inv_freq[None, :]
    cos, sin = jnp.cos(ang), jnp.sin(ang)
    n = pos.shape[0]
    rest = ATT_HD - ROT_DIM
    head = lambda a, b, fill: jnp.concatenate([a, b, jnp.full((n, rest), fill, F32)], axis=1)
    zero = jnp.zeros_like(sin)
    reps = LANES // ATT_HD
    return tuple(jnp.tile(t, (1, reps)) for t in
                 (head(cos, cos, 1.0), head(-sin, zero, 0.0), head(zero, sin, 0.0)))


def kernel(x_prompt, x_sample, c_prompt, c_sample, state_mlstm_C, state_mlstm_n, state_mlstm_m,
           cache_win_k, cache_win_v, ada_w, ada_b, ffn1_up, ffn1_down, ffn2_up, ffn2_down, ln_g, ln_b,
           ml_w_in, ml_b_gates, ml_norm_w, ml_w_out, kv_ada_w, kv_ada_b, kv_w,
           att_w_q, att_sinks, att_w_o):
    bp, sp, _ = x_prompt.shape
    bs, ss, _ = x_sample.shape
    n_buf = cache_win_k.shape[1]
    ts = bs * ss

    c_all = jnp.concatenate([jnp.repeat(c_sample, ss, axis=0), c_prompt,
                             jnp.zeros((-bp % SUBLANES, D_MODEL), F32)], axis=0)
    mods = _ada(c_all, ada_w, ada_b[:, None, :])
    kv_mods = _ada(c_all, kv_ada_w[None], kv_ada_b[None, None, :])

    bf = lambda w: w.astype(BF16)
    up1, down1, up2, down2 = bf(ffn1_up), bf(ffn1_down), bf(ffn2_up), bf(ffn2_down)
    w_in = bf(ml_w_in)
    gate_pad = LANES - 2 * ML_HEADS
    w_gate = bf(jnp.pad(ml_w_in[0, :, 2 * HQK + 2 * HV:], ((0, 0), (0, gate_pad))))
    b_gate = jnp.pad(ml_b_gates[0], (0, gate_pad))[None, :]
    norm_w = ml_norm_w[0][None, :]
    w_out = bf(ml_w_out[0])
    w_kv = bf(kv_w)
    order = np.arange(QW).reshape(ATT_KV_HEADS, ATT_GROUP, ATT_HD).transpose(1, 0, 2).reshape(-1)
    w_q = bf(att_w_q[0][:, order])
    w_o = bf(att_w_o[0][order, :])
    sinks = att_sinks[0]
    lng = ln_g.reshape(DEPTH * 3, 1, D_MODEL)
    lnb = ln_b.reshape(DEPTH * 3, 1, D_MODEL)

    tab_p = _rope_tables(jnp.arange(sp, dtype=jnp.int32))
    tab_s = _rope_tables(jnp.tile(PAST_LEN + jnp.arange(ss, dtype=jnp.int32), bs))

    xp = x_prompt.reshape(bp * sp, D_MODEL)
    xp = _ffn(xp, mods, 0, 0, up1, down1, lng, lnb, 0, ts, sp)
    xp, c_p, n_p, m_p = _mlstm_prompt(xp.reshape(bp, sp, D_MODEL), mods, w_in, w_gate, b_gate,
                                      norm_w, w_out, lng, lnb, ts)
    xp = _ffn(xp.reshape(bp * sp, D_MODEL), mods, 0, 6, up2, down2, lng, lnb, 2, ts, sp)
    k_p, v_p = _shared_kv(xp, kv_mods, w_kv, tab_p, ts, sp)
    k_p = k_p.reshape(bp, sp, KVW)
    v_p = v_p.reshape(bp, sp, KVW)
    xp = _ffn(xp, mods, 1, 0, up1, down1, lng, lnb, 3, ts, sp)
    xp = _attn_prompt(xp.reshape(bp, sp, D_MODEL), mods, w_q, tab_p, k_p, v_p, sinks, w_o, lng, lnb, ts)
    xp = _ffn(xp.reshape(bp * sp, D_MODEL), mods, 1, 6, up2, down2, lng, lnb, 5, ts, sp)
    y_prompt = xp.reshape(bp, sp, D_MODEL)
    keep = min(WINDOW, sp)
    win_k_p = k_p[:, sp - keep:].reshape(bp, keep, ATT_KV_HEADS, ATT_HD)
    win_v_p = v_p[:, sp - keep:].reshape(bp, keep, ATT_KV_HEADS, ATT_HD)

    xs = x_sample.reshape(ts, D_MODEL)
    xs = _ffn(xs, mods, 0, 0, up1, down1, lng, lnb, 0, ts, None)
    m0 = jnp.pad(state_mlstm_m[0], ((0, 0), (0, LANES - ML_HEADS)))
    xs, c_s, n_s, m_s = _mlstm_sample(xs, mods, w_in, w_gate, b_gate, norm_w, w_out, lng, lnb,
                                      state_mlstm_C[0], state_mlstm_n[0].reshape(bs, HQK), m0, ss)
    xs = _ffn(xs, mods, 0, 6, up2, down2, lng, lnb, 2, ts, None)
    k_s, v_s = _shared_kv(xs, kv_mods, w_kv, tab_s, ts, None)
    xs = _ffn(xs, mods, 1, 0, up1, down1, lng, lnb, 3, ts, None)
    xs = _attn_sample(xs, mods, w_q, tab_s, k_s, v_s,
                      cache_win_k.reshape(bs, n_buf, KVW), cache_win_v.reshape(bs, n_buf, KVW),
                      sinks, w_o, lng, lnb, ss)
    xs = _ffn(xs, mods, 1, 6, up2, down2, lng, lnb, 5, ts, None)
    y_sample = xs.reshape(bs, ss, D_MODEL)
    win_k_s = jnp.concatenate([cache_win_k, k_s.reshape(bs, ss, ATT_KV_HEADS, ATT_HD)], axis=1)[:, -n_buf:]
    win_v_s = jnp.concatenate([cache_win_v, v_s.reshape(bs, ss, ATT_KV_HEADS, ATT_HD)], axis=1)[:, -n_buf:]

    return (y_prompt, y_sample,
            c_p[None], n_p[None, :, :ML_HEADS, :], m_p[None, :, :ML_HEADS, 0],
            win_k_p, win_v_p,
            c_s[None], n_s.reshape(1, bs, ML_HEADS, ML_DK), m_s[None, :, :ML_HEADS],
            win_k_s, win_v_s)
```

```python
import functools

import numpy as np
import jax
import jax.numpy as jnp
from jax import lax
from jax.experimental import pallas as pl
from jax.experimental.pallas import tpu as pltpu

F32 = jnp.float32
BF16 = jnp.bfloat16

D_MODEL = 1024
DEPTH = 2
PAST_LEN = 8192
ML_HEADS = 4
ML_DK = D_MODEL // 8
ML_DV = D_MODEL // 4
ATT_Q_HEADS = 16
ATT_KV_HEADS = 4
ATT_GROUP = ATT_Q_HEADS // ATT_KV_HEADS
ATT_HD = 64
WINDOW = 128
ROT_DIM = ATT_HD // 4
ROPE_THETA = 500000.0
D_FF = 2816
ALPHA = (2 * DEPTH) ** 0.25
LN_EPS = 1e-5
N_MOD = 9
HQK = ML_HEADS * ML_DK
HV = ML_HEADS * ML_DV
KVW = ATT_KV_HEADS * ATT_HD
QW = ATT_Q_HEADS * ATT_HD

LANES = 128
SUBLANES = 8
VMEM_LIMIT_BYTES = 56 * 1024 * 1024

FFN_ROWS = 512
FFN_CHUNK = 512
ML_ROWS = 512
ML_CHUNK = 128
ATT_ROWS = 256
ADA_COLS = 1024
SAMPLE_GROUP = 8

NT_DIMS = (((1,), (1,)), ((), ()))
HIGHEST = lax.Precision.HIGHEST


def _params(semantics):
    return pltpu.CompilerParams(dimension_semantics=semantics, vmem_limit_bytes=VMEM_LIMIT_BYTES)


def _const_spec(block, index):
    return pl.BlockSpec(block, lambda *_: index, pipeline_mode=pl.Buffered(1))


def _resident(shape):
    return _const_spec(shape, (0,) * len(shape))


def _dot(a, b):
    return jnp.dot(a, b, preferred_element_type=F32)


def _dot_exact(a, b):
    return jnp.dot(a, b, preferred_element_type=F32, precision=HIGHEST)


def _nt(a, b):
    return lax.dot_general(a, b, NT_DIMS, preferred_element_type=F32)


def _iota(shape, dim):
    return lax.broadcasted_iota(jnp.int32, shape, dim)


def _silu(x):
    return x * jax.nn.sigmoid(x)


def _log_sigmoid(x):
    return jnp.minimum(x, 0.0) - jnp.log1p(jnp.exp(-jnp.abs(x)))


def _layer_norm(y, g, b):
    mu = jnp.mean(y, axis=-1, keepdims=True)
    d = y - mu
    var = jnp.mean(d * d, axis=-1, keepdims=True)
    return d * lax.rsqrt(var + LN_EPS) * g + b


def _mod(ref, batch):
    return ref[...] if batch is None else ref[pl.ds(batch, 1), :]


def _modulate_bf16(x, shift, scale):
    return (x * (1.0 + scale) + shift).astype(BF16)


def _rope(x, cos, sin_lo, sin_hi):
    width = x.shape[1]
    reps = width // LANES
    tile = lambda t: jnp.concatenate([t] * reps, axis=1) if reps > 1 else t
    x_up = pltpu.roll(x, width - ROT_DIM // 2, 1)
    x_dn = pltpu.roll(x, ROT_DIM // 2, 1)
    return x * tile(cos) + x_up * tile(sin_lo) + x_dn * tile(sin_hi)


def _mod_specs(layer, first, count, n_sample_rows, prompt):
    if prompt:
        blk = n_sample_rows // SUBLANES
        return [_const_spec((None, SUBLANES, D_MODEL), (layer, blk, first + j)) for j in range(count)]
    return [_const_spec((None, n_sample_rows, D_MODEL), (layer, 0, first + j)) for j in range(count)]


def _ln_specs(index):
    return [_const_spec((None, 1, D_MODEL), (index, 0, 0))] * 2


def _ada_kernel(c_ref, w_ref, b_ref, o_ref):
    ca = _silu(c_ref[...]).astype(BF16)
    o_ref[...] = _dot(ca, w_ref[...].astype(BF16)) + b_ref[...]


def _ada(c, w, b):
    n_l, k, n = w.shape
    m = c.shape[0]
    return pl.pallas_call(
        _ada_kernel,
        grid=(n_l, n // ADA_COLS),
        in_specs=[pl.BlockSpec((m, k), lambda l, j: (0, 0)),
                  pl.BlockSpec((None, k, ADA_COLS), lambda l, j: (l, 0, j)),
                  pl.BlockSpec((None, 1, ADA_COLS), lambda l, j: (l, 0, j))],
        out_specs=pl.BlockSpec((None, m, ADA_COLS), lambda l, j: (l, 0, j)),
        out_shape=jax.ShapeDtypeStruct((n_l, m, n), F32),
        compiler_params=_params(("arbitrary", "arbitrary")),
        name="ada",
    )(c, w, b)


def _ffn_kernel(x_ref, sh_ref, sc_ref, gt_ref, wa_ref, wu_ref, wd_ref, g_ref, b_ref, *rest, steps_per_seq):
    batch = None if steps_per_seq is None else pl.program_id(0) // steps_per_seq
    x = x_ref[...]
    xm = _modulate_bf16(x, _mod(sh_ref, batch), _mod(sc_ref, batch))
    acc_ref = rest[-1]
    for c0 in range(0, D_FF, FFN_CHUNK):
        cw = min(FFN_CHUNK, D_FF - c0)
        a = _dot(xm, wa_ref[:, c0:c0 + cw])
        u = _dot(xm, wu_ref[:, c0:c0 + cw])
        p = _dot((_silu(a) * u).astype(BF16), wd_ref[c0:c0 + cw, :])
        if c0 == 0:
            acc_ref[...] = p
        else:
            acc_ref[...] += p
    y = ALPHA * x + (0.5 * _mod(gt_ref, batch)) * acc_ref[...]
    out = _layer_norm(y, g_ref[...], b_ref[...])
    if len(rest) == 2:
        rest[0][...] = out
        return
    ksh_ref, ksc_ref, wkv_ref, cos_ref, slo_ref, shi_ref, o_ref, k_ref, v_ref, _ = rest
    o_ref[...] = out
    kv = _dot(_modulate_bf16(out, _mod(ksh_ref, batch), _mod(ksc_ref, batch)), wkv_ref[...])
    k_ref[...] = _rope(kv[:, :KVW], cos_ref[...], slo_ref[...], shi_ref[...])
    v_ref[...] = kv[:, KVW:]


def _ffn(x, mods, layer, first, w_up, w_down, ln_g, ln_b, ln_index, n_sample_rows, seq, kv=None):
    t = x.shape[0]
    rows = min(FFN_ROWS, t)
    x_spec = pl.BlockSpec((rows, D_MODEL), lambda i: (i, 0))
    in_specs = ([x_spec] + _mod_specs(layer, first, 3, n_sample_rows, seq is not None)
                + [_const_spec((None, D_MODEL, D_FF), (layer, 0, 0)),
                   _const_spec((None, D_MODEL, D_FF), (layer, 0, 1)),
                   _const_spec((None, D_FF, D_MODEL), (layer, 0, 0))]
                + _ln_specs(ln_index))
    args = [x, mods, mods, mods, w_up, w_up, w_down, ln_g, ln_b]
    out_specs = x_spec
    out_shape = jax.ShapeDtypeStruct((t, D_MODEL), F32)
    if kv is not None:
        kv_mods, w_kv, tables = kv
        pos_blocks = tables[0].shape[0] // rows
        tab_spec = pl.BlockSpec((rows, LANES), lambda i: (i % pos_blocks, 0))
        kv_spec = pl.BlockSpec((rows, KVW), lambda i: (i, 0))
        in_specs += (_mod_specs(0, 0, 2, n_sample_rows, seq is not None)
                     + [_resident(w_kv.shape), tab_spec, tab_spec, tab_spec])
        args += [kv_mods, kv_mods, w_kv, *tables]
        out_specs = [x_spec, kv_spec, kv_spec]
        out_shape = [out_shape] + [jax.ShapeDtypeStruct((t, KVW), F32)] * 2
    return pl.pallas_call(
        functools.partial(_ffn_kernel, steps_per_seq=None if seq is None else seq // rows),
        grid=(t // rows,),
        in_specs=in_specs,
        out_specs=out_specs,
        out_shape=out_shape,
        scratch_shapes=[pltpu.VMEM((rows, D_MODEL), F32)],
        compiler_params=_params(("arbitrary",)),
        name="ffn" if kv is None else "ffn_kv",
    )(*args)


def _head_norm(hh, w_row):
    mu = jnp.mean(hh, axis=-1, keepdims=True)
    d = hh - mu
    var = jnp.mean(d * d, axis=-1, keepdims=True)
    return d * lax.rsqrt(var + LN_EPS) * w_row


def _split3(x, axis):
    hi = x.astype(BF16)
    r1 = x - hi.astype(F32)
    mid = r1.astype(BF16)
    lo = (r1 - mid.astype(F32)).astype(BF16)
    return jnp.concatenate([hi, mid, lo], axis=axis)


def _tile_lanes(t, reps):
    return jnp.concatenate([t] * reps, axis=1)


def _mlstm_prompt_kernel(x_ref, sh_ref, sc_ref, gt_ref, wp_ref, wg_ref, bg_ref, nw_ref, wo_ref, g_ref, b_ref,
                         o_ref, c_out, m_out, proj_ref, rep_ref, y_ref, c_ref, mm_ref):
    batch = pl.program_id(0)
    step = pl.program_id(1)
    size = ML_CHUNK
    assert size == LANES and ML_DK == LANES

    @pl.when(step == 0)
    def _():
        c_ref[...] = jnp.zeros_like(c_ref)
        mm_ref[...] = jnp.zeros_like(mm_ref)

    x = x_ref[...]
    rows = x.shape[0]
    n_chunks = rows // size
    h = _modulate_bf16(x, _mod(sh_ref, batch), _mod(sc_ref, batch))
    proj_ref[...] = _dot(h, wp_ref[...])
    gates = _dot(h, wg_ref[...]) + bg_ref[...]

    r = _iota((size, size), 0)
    c = _iota((size, size), 1)
    causal = c <= r
    eye = c == r
    tri3 = _tile_lanes(jnp.where(causal, 1.0, 0.0).astype(BF16), 3)
    n_rep = 2 * ML_HEADS
    spread = jnp.where(_iota((LANES, n_rep * LANES), 1) // LANES == _iota((LANES, n_rep * LANES), 0),
                       1.0, 0.0).astype(BF16)
    spread3 = jnp.concatenate([spread] * 3, axis=0)
    ones3 = jnp.ones((3 * size, LANES), BF16)
    ones_v = jnp.ones((size, LANES), BF16)
    sr = _iota((2 * SUBLANES, LANES), 0)
    sl = _iota((2 * SUBLANES, LANES), 1)
    pick = (jnp.where((sr < ML_HEADS) & (sl == sr), 1.0, 0.0)
            - jnp.where((sr < ML_HEADS) & (sl == sr + ML_HEADS), 1.0, 0.0)
            + jnp.where((sr >= SUBLANES) & (sr < SUBLANES + ML_HEADS) & (sl == sr - SUBLANES + ML_HEADS),
                        1.0, 0.0))
    pick3 = _tile_lanes(pick.astype(BF16), 3)

    log_f = _log_sigmoid(gates)
    bwide = jnp.concatenate([_dot(tri3, _split3(log_f[ci * size:(ci + 1) * size, :], 0))
                             for ci in range(n_chunks)], axis=0)
    src3 = _split3(jnp.where(_iota((rows, LANES), 1) < ML_HEADS, gates, bwide), 1)
    rep_ref[...] = _dot(src3, spread3)
    ab_rows = _nt(pick3, src3)
    a_rows = ab_rows[:SUBLANES, :]
    b_rows = ab_rows[SUBLANES:, :]
    lane_in_chunk = _iota((SUBLANES, rows), 1) % size
    run = a_rows
    shift = 1
    while shift < size:
        run = jnp.maximum(run, jnp.where(lane_in_chunk >= shift, pltpu.roll(run, shift, 1), -jnp.inf))
        shift *= 2
    m_rows = [mm_ref[...]]
    g_rows = []
    for ci in range(n_chunks):
        run_c = run[:, ci * size:(ci + 1) * size]
        g_rows.append(jnp.maximum(run_c, m_rows[-1]))
        end = (ci + 1) * size - 1
        m_end = b_rows[:, end:end + 1] + jnp.maximum(run[:, end:end + 1], m_rows[-1][:, :1])
        m_rows.append(jnp.broadcast_to(m_end, (SUBLANES, LANES)))
    mm_ref[...] = m_rows[-1]

    states = [c_ref[hd] for hd in range(ML_HEADS)]
    for ci in range(n_chunks):
        lo, hi = ci * size, (ci + 1) * size
        for hd in range(ML_HEADS):
            i_rep = rep_ref[lo:hi, hd * LANES:(hd + 1) * LANES]
            b_rep = rep_ref[lo:hi, (ML_HEADS + hd) * LANES:(ML_HEADS + hd + 1) * LANES]
            m_prev = m_rows[ci][hd:hd + 1, :]
            m_new = m_rows[ci + 1][hd:hd + 1, :]
            a_row = a_rows[hd:hd + 1, lo:hi]
            g_rep = _dot(_split3(jnp.where(eye, g_rows[ci][hd:hd + 1, :], 0.0), 1), ones3)
            w = jnp.where(causal, jnp.exp(a_row - g_rep), 0.0)
            w_init = jnp.exp(m_prev - g_rep)
            floor = jnp.exp(-(b_rep + g_rep))
            b_last = b_rep[size - 1:size, :]
            w_end = jnp.exp(b_last - b_rep + i_rep - m_new)
            decay = jnp.exp(b_last + m_prev - m_new)

            q = proj_ref[lo:hi, hd * ML_DK:(hd + 1) * ML_DK]
            k = proj_ref[lo:hi, HQK + hd * ML_DK:HQK + (hd + 1) * ML_DK] * (ML_DK ** -0.5)
            v = proj_ref[lo:hi, 2 * HQK + hd * ML_DV:2 * HQK + (hd + 1) * ML_DV]
            qb = q.astype(BF16)
            v_ext = jnp.concatenate([v.astype(BF16), ones_v], axis=1)
            s = _nt(qb, k.astype(BF16)) * w
            num = _dot(s.astype(BF16), v_ext) + _tile_lanes(w_init, 3) * _dot(qb, states[hd].astype(BF16))
            den = jnp.maximum(jnp.abs(num[:, ML_DV:]), floor)
            hh = num[:, :ML_DV] / _tile_lanes(den, ML_DV // LANES)
            states[hd] = _tile_lanes(decay, 3) * states[hd] + _dot((k * w_end).T.astype(BF16), v_ext)
            y_ref[lo:hi, hd * ML_DV:(hd + 1) * ML_DV] = _head_norm(hh, nw_ref[:, hd * ML_DV:(hd + 1) * ML_DV])
    for hd in range(ML_HEADS):
        c_ref[hd] = states[hd]

    o_gate = jax.nn.sigmoid(proj_ref[:, 2 * HQK + HV:2 * HQK + 2 * HV])
    y = _dot((y_ref[...] * o_gate).astype(BF16), wo_ref[...])
    o_ref[...] = _layer_norm(ALPHA * x + _mod(gt_ref, batch) * y, g_ref[...], b_ref[...])

    @pl.when(step == pl.num_programs(1) - 1)
    def _():
        c_out[...] = c_ref[...]
        m_out[...] = mm_ref[...]


def _mlstm_weight_specs(wg, bg, nw, wo):
    return [_const_spec((None, D_MODEL, 2 * HQK + 2 * HV), (0, 0, 0)), _resident(wg.shape),
            _resident(bg.shape), _resident(nw.shape), _resident(wo.shape)] + _ln_specs(1)


def _mlstm_prompt(x, mods, w_in, wg, bg, nw, wo, ln_g, ln_b, n_sample_rows):
    bsz, seq, _ = x.shape
    rows = min(ML_ROWS, seq)
    x_spec = pl.BlockSpec((None, rows, D_MODEL), lambda bi, i: (bi, i, 0))
    state_shape = (ML_HEADS, ML_DK, ML_DV + LANES)
    return pl.pallas_call(
        _mlstm_prompt_kernel,
        grid=(bsz, seq // rows),
        in_specs=[x_spec] + _mod_specs(0, 3, 3, n_sample_rows, True) + _mlstm_weight_specs(wg, bg, nw, wo),
        out_specs=[x_spec, pl.BlockSpec((None,) + state_shape, lambda bi, i: (bi, 0, 0, 0)),
                   pl.BlockSpec((None, SUBLANES, LANES), lambda bi, i: (bi, 0, 0))],
        out_shape=[jax.ShapeDtypeStruct((bsz, seq, D_MODEL), F32),
                   jax.ShapeDtypeStruct((bsz,) + state_shape, F32),
                   jax.ShapeDtypeStruct((bsz, SUBLANES, LANES), F32)],
        scratch_shapes=[pltpu.VMEM((rows, 2 * HQK + 2 * HV), F32),
                        pltpu.VMEM((rows, 2 * ML_HEADS * LANES), F32),
                        pltpu.VMEM((rows, HV), F32),
                        pltpu.VMEM(state_shape, F32),
                        pltpu.VMEM((SUBLANES, LANES), F32)],
        compiler_params=_params(("arbitrary", "arbitrary")),
        name="mlstm_prompt",
    )(x, mods, mods, mods, w_in, wg, bg, nw, wo, ln_g, ln_b)


def _mlstm_sample_kernel(x_ref, sh_ref, sc_ref, gt_ref, wp_ref, wg_ref, bg_ref, nw_ref, wo_ref, g_ref, b_ref,
                         c0_ref, n0_ref, m0_ref, o_ref, c_out, n_out, m_out,
                         proj_ref, gate_ref, y_ref, *, seq):
    step = pl.program_id(0)
    grp = c0_ref.shape[0]
    rows = grp * seq

    @pl.when(step == 0)
    def _():
        h = _modulate_bf16(x_ref[...], sh_ref[...], sc_ref[...])
        proj_ref[...] = _dot(h, wp_ref[...])
        gate_ref[...] = _dot(h, wg_ref[...]) + bg_ref[...]

    r = _iota((rows, rows), 0)
    c = _iota((rows, rows), 1)
    same = (r // seq) == (c // seq)
    causal = same & (c <= r)
    eye = r == c
    last_of_row = c == (r // seq) * seq + (seq - 1)
    same_f = same.astype(F32)
    expand = (_iota((rows, grp), 0) // seq == _iota((rows, grp), 1)).astype(F32)
    pick_last = (_iota((grp, rows), 1) == _iota((grp, rows), 0) * seq + (seq - 1)).astype(F32)
    ident = (_iota((ML_DK, ML_DK), 0) == _iota((ML_DK, ML_DK), 1)).astype(F32)
    seq_of_row = _iota((rows, ML_DV), 0) // seq
    col_seq = _iota((ML_DK, rows), 1) // seq
    lane = _iota((rows, LANES), 1)
    to_row = lambda col: jnp.sum(jnp.where(eye, col, 0.0), axis=0, keepdims=True)

    r0 = pl.multiple_of(step * rows, rows)
    gates = gate_ref[pl.ds(r0, rows), :]
    log_f = _log_sigmoid(gates)
    bwide = _dot_exact(causal.astype(F32), log_f)
    bsum = _dot_exact(same_f, log_f)
    m_prev_wide = _dot_exact(expand, m0_ref[...])
    n_rows_wide = _dot_exact(expand, n0_ref[...])
    n_pieces = []
    m_wide = jnp.zeros((rows, LANES), F32)
    for hd in range(ML_HEADS):
        q = proj_ref[pl.ds(r0, rows), hd * ML_DK:(hd + 1) * ML_DK]
        k = proj_ref[pl.ds(r0, rows), HQK + hd * ML_DK:HQK + (hd + 1) * ML_DK] * (ML_DK ** -0.5)
        v = proj_ref[pl.ds(r0, rows), 2 * HQK + hd * ML_DV:2 * HQK + (hd + 1) * ML_DV]
        bcol = bwide[:, ML_HEADS + hd:ML_HEADS + hd + 1]
        b_last = bsum[:, ML_HEADS + hd:ML_HEADS + hd + 1]
        icol = gates[:, hd:hd + 1]
        m_prev = m_prev_wide[:, hd:hd + 1]
        n_rows = n_rows_wide[:, hd * ML_DK:(hd + 1) * ML_DK]
        log_w = jnp.where(causal, bcol - to_row(bcol) + to_row(icol), -jnp.inf)
        log_init = bcol + m_prev
        m_t = jnp.maximum(log_init, jnp.max(log_w, axis=-1, keepdims=True))
        w = jnp.exp(log_w - m_t)
        w_init = jnp.exp(log_init - m_t)
        qb = q.astype(BF16)
        s = _nt(qb, k.astype(BF16)) * w
        inter = None
        for bi in range(grp):
            cand = _dot(qb, c0_ref[bi, hd].astype(BF16))
            inter = cand if inter is None else jnp.where(seq_of_row == bi, cand, inter)
        num = _dot(s.astype(BF16), v.astype(BF16)) + w_init * inter
        den = jnp.sum(s, axis=-1, keepdims=True) + w_init * jnp.sum(q * n_rows, axis=-1, keepdims=True)
        hh = num / jnp.maximum(jnp.abs(den), jnp.exp(-m_t))
        m_new = jnp.sum(jnp.where(last_of_row, to_row(m_t), 0.0), axis=1, keepdims=True)
        w_end = jnp.exp(b_last - bcol + icol - m_new)
        decay = jnp.exp(b_last + m_prev - m_new)
        kw = k * w_end
        kw_t = lax.dot_general(ident, kw, NT_DIMS, preferred_element_type=F32, precision=HIGHEST)
        lhs = jnp.concatenate([jnp.where(col_seq == bi, kw_t, 0.0) for bi in range(grp)], axis=0)
        upd = _dot(lhs.astype(BF16), v.astype(BF16))
        for bi in range(grp):
            c_out[bi, hd] = (decay[bi * seq:bi * seq + 1, :] * c0_ref[bi, hd]
                             + upd[bi * ML_DK:(bi + 1) * ML_DK, :])
        n_pieces.append(decay * n_rows + _dot_exact(same_f, kw))
        m_wide = jnp.where(lane == hd, m_new, m_wide)
        y_ref[pl.ds(r0, rows), hd * ML_DV:(hd + 1) * ML_DV] = _head_norm(
            hh, nw_ref[:, hd * ML_DV:(hd + 1) * ML_DV])
    n_out[...] = _dot_exact(pick_last, jnp.concatenate(n_pieces, axis=1))
    m_out[...] = _dot_exact(pick_last, m_wide)

    @pl.when(step == pl.num_programs(0) - 1)
    def _():
        o_gate = jax.nn.sigmoid(proj_ref[:, 2 * HQK + HV:2 * HQK + 2 * HV])
        y = _dot((y_ref[...] * o_gate).astype(BF16), wo_ref[...])
        o_ref[...] = _layer_norm(ALPHA * x_ref[...] + gt_ref[...] * y, g_ref[...], b_ref[...])


def _mlstm_sample(x, mods, w_in, wg, bg, nw, wo, ln_g, ln_b, c0, n0, m0, seq):
    t = x.shape[0]
    grp = SAMPLE_GROUP
    state_specs = [pl.BlockSpec((grp, ML_HEADS, ML_DK, ML_DV), lambda i: (i, 0, 0, 0)),
                   pl.BlockSpec((grp, HQK), lambda i: (i, 0)),
                   pl.BlockSpec((grp, LANES), lambda i: (i, 0))]
    return pl.pallas_call(
        functools.partial(_mlstm_sample_kernel, seq=seq),
        grid=(t // (grp * seq),),
        in_specs=[_resident(x.shape)] + _mod_specs(0, 3, 3, t, False)
                 + _mlstm_weight_specs(wg, bg, nw, wo) + state_specs,
        out_specs=[pl.BlockSpec((t, D_MODEL), lambda i: (0, 0))] + state_specs,
        out_shape=[jax.ShapeDtypeStruct((t, D_MODEL), F32),
                   jax.ShapeDtypeStruct(c0.shape, F32),
                   jax.ShapeDtypeStruct(n0.shape, F32),
                   jax.ShapeDtypeStruct(m0.shape, F32)],
        scratch_shapes=[pltpu.VMEM((t, 2 * HQK + 2 * HV), F32),
                        pltpu.VMEM((t, LANES), F32),
                        pltpu.VMEM((t, HV), F32)],
        compiler_params=_params(("arbitrary",)),
        name="mlstm_sample",
    )(x, mods, mods, mods, w_in, wg, bg, nw, wo, ln_g, ln_b, c0, n0, m0)


def _sink_column(sink_ref, kh, rows_per_head):
    rid = _iota((ATT_GROUP * rows_per_head, 1), 0)
    col = jnp.full((ATT_GROUP * rows_per_head, 1), sink_ref[kh * ATT_GROUP], F32)
    for gq in range(1, ATT_GROUP):
        col = jnp.where(rid >= gq * rows_per_head, sink_ref[kh * ATT_GROUP + gq], col)
    return col


def _head_lanes(x, kh):
    return jnp.where(_iota(x.shape, 1) // ATT_HD == kh, x, 0.0).astype(BF16)


def _stack_groups(x):
    return jnp.concatenate([x[:, g * KVW:(g + 1) * KVW] for g in range(ATT_GROUP)], axis=0)


def _attn_prompt_kernel(sink_ref, x_ref, sh_ref, sc_ref, gt_ref, wq_ref, cos_ref, slo_ref, shi_ref,
                        kp_ref, kc_ref, vp_ref, vc_ref, wo_ref, g_ref, b_ref, o_ref, a_ref):
    batch = pl.program_id(0)
    step = pl.program_id(1)
    x = x_ref[...]
    rows = x.shape[0]
    n_blk = rows // WINDOW
    q = _rope(_dot(_modulate_bf16(x, _mod(sh_ref, batch), _mod(sc_ref, batch)), wq_ref[...]),
              cos_ref[...], slo_ref[...], shi_ref[...])
    q = (q * (ATT_HD ** -0.5)).astype(BF16)

    keys = jnp.concatenate([kp_ref[...], kc_ref[...]], axis=0)
    vals = jnp.concatenate([vp_ref[...], vc_ref[...]], axis=0)
    k_heads = [_head_lanes(keys, kh) for kh in range(ATT_KV_HEADS)]
    v_heads = [_head_lanes(vals, kh) for kh in range(ATT_KV_HEADS)]

    rid = _iota((ATT_GROUP * WINDOW, 2 * WINDOW), 0) % WINDOW
    cid = _iota((ATT_GROUP * WINDOW, 2 * WINDOW), 1)
    band = (cid > rid) & (cid <= rid + WINDOW)
    first = band & (cid >= jnp.where(step > 0, 0, WINDOW))
    sinks = [_sink_column(sink_ref, kh, WINDOW) for kh in range(ATT_KV_HEADS)]

    scores = [[_nt(_stack_groups(q[n * WINDOW:(n + 1) * WINDOW, :]),
                   k_heads[kh][n * WINDOW:(n + 2) * WINDOW, :])
               for kh in range(ATT_KV_HEADS)] for n in range(n_blk)]
    probs = []
    for n in range(n_blk):
        row = []
        for kh in range(ATT_KV_HEADS):
            s = jnp.where(first if n == 0 else band, scores[n][kh], -jnp.inf)
            mx = jnp.maximum(jnp.max(s, axis=-1, keepdims=True), sinks[kh])
            p = jnp.exp(s - mx)
            den = jnp.sum(p, axis=-1, keepdims=True) + jnp.exp(sinks[kh] - mx)
            row.append((p / den).astype(BF16))
        probs.append(jnp.concatenate(row, axis=1))
    for n in range(n_blk):
        v_all = jnp.concatenate([v_heads[kh][n * WINDOW:(n + 2) * WINDOW, :]
                                 for kh in range(ATT_KV_HEADS)], axis=0)
        o = _dot(probs[n], v_all)
        for g in range(ATT_GROUP):
            a_ref[n * WINDOW:(n + 1) * WINDOW, g * KVW:(g + 1) * KVW] = o[g * WINDOW:(g + 1) * WINDOW, :]

    y = _dot(a_ref[...].astype(BF16), wo_ref[...])
    o_ref[...] = _layer_norm(ALPHA * x + _mod(gt_ref, batch) * y, g_ref[...], b_ref[...])


def _attn_prompt(x, mods, wq, tables, k, v, sinks, wo, ln_g, ln_b, n_sample_rows):
    bsz, seq, _ = x.shape
    rows = min(ATT_ROWS, seq)
    per = rows // WINDOW
    x_spec = pl.BlockSpec((None, rows, D_MODEL), lambda bi, i: (bi, i, 0))
    cur = pl.BlockSpec((None, rows, KVW), lambda bi, i: (bi, i, 0))
    prev = pl.BlockSpec((None, WINDOW, KVW), lambda bi, i: (bi, jnp.maximum(i * per - 1, 0), 0))
    tab = pl.BlockSpec((rows, LANES), lambda bi, i: (i, 0))
    return pl.pallas_call(
        _attn_prompt_kernel,
        grid=(bsz, seq // rows),
        in_specs=[pl.BlockSpec(memory_space=pltpu.SMEM), x_spec]
                 + _mod_specs(1, 3, 3, n_sample_rows, True)
                 + [_resident(wq.shape), tab, tab, tab, prev, cur, prev, cur, _resident(wo.shape)]
                 + _ln_specs(4),
        out_specs=x_spec,
        out_shape=jax.ShapeDtypeStruct((bsz, seq, D_MODEL), F32),
        scratch_shapes=[pltpu.VMEM((rows, QW), F32)],
        compiler_params=_params(("arbitrary", "arbitrary")),
        name="attn_prompt",
    )(sinks, x, mods, mods, mods, wq, *tables, k, k, v, v, wo, ln_g, ln_b)


def _attn_sample_kernel(sink_ref, x_ref, sh_ref, sc_ref, gt_ref, wq_ref, cos_ref, slo_ref, shi_ref,
                        kn_ref, vn_ref, kb_ref, vb_ref, wo_ref, g_ref, b_ref, o_ref,
                        q_ref, a_ref, *, seq):
    step = pl.program_id(0)
    pair = 2 * seq
    assert pair == SUBLANES
    grp, n_buf, _ = kb_ref.shape

    @pl.when(step == 0)
    def _():
        q = _rope(_dot(_modulate_bf16(x_ref[...], sh_ref[...], sc_ref[...]), wq_ref[...]),
                  cos_ref[...], slo_ref[...], shi_ref[...])
        q_ref[...] = q * (ATT_HD ** -0.5)

    n_rows = ATT_GROUP * pair
    n_new = ATT_KV_HEADS * pair
    q_pos = _iota((n_rows, n_buf), 0) % seq + PAST_LEN
    k_pos = _iota((n_rows, n_buf), 1) + (PAST_LEN - n_buf)
    buf_mask = (k_pos <= q_pos) & (k_pos > q_pos - WINDOW) & (k_pos >= 0)
    q_tok = _iota((n_rows, n_new), 0) % seq
    new_col = _iota((n_rows, n_new), 1)
    new_row_head = _iota((n_new, KVW), 0) // pair
    new_lane_head = _iota((n_new, KVW), 1) // ATT_HD
    low = _iota((n_rows, KVW), 0) % pair < seq
    sinks = [_sink_column(sink_ref, kh, pair) for kh in range(ATT_KV_HEADS)]
    for p in range(grp // 2):
        r0 = pl.multiple_of(step * (grp * seq) + p * pair, pair)
        q_all = _stack_groups(q_ref[pl.ds(r0, pair), :].astype(BF16))
        stack = lambda a: jnp.where(new_row_head == new_lane_head,
                                    jnp.concatenate([a] * ATT_KV_HEADS, axis=0), 0.0).astype(BF16)
        k_new = stack(kn_ref[pl.ds(r0, pair), :])
        v_new = stack(vn_ref[pl.ds(r0, pair), :])
        s_new_all = _nt(q_all, k_new)
        halves = []
        for half in range(2):
            bi = 2 * p + half
            k_buf, v_buf = kb_ref[bi], vb_ref[bi]
            tok = new_col % pair - half * seq
            own_new = (tok >= 0) & (tok < seq) & (tok <= q_tok)
            p_buf, p_new = [], None
            for kh in range(ATT_KV_HEADS):
                s_buf = jnp.where(buf_mask, _nt(q_all, _head_lanes(k_buf, kh)), -jnp.inf)
                s_new = jnp.where(own_new & (new_col // pair == kh), s_new_all, -jnp.inf)
                mx = jnp.maximum(jnp.maximum(jnp.max(s_buf, axis=-1, keepdims=True),
                                             jnp.max(s_new, axis=-1, keepdims=True)), sinks[kh])
                e_buf = jnp.exp(s_buf - mx)
                e_new = jnp.exp(s_new - mx)
                den = (jnp.sum(e_buf, axis=-1, keepdims=True) + jnp.sum(e_new, axis=-1, keepdims=True)
                       + jnp.exp(sinks[kh] - mx))
                p_buf.append((e_buf / den).astype(BF16))
                p_new = e_new / den if p_new is None else p_new + e_new / den
            v_all = jnp.concatenate([_head_lanes(v_buf, kh) for kh in range(ATT_KV_HEADS)], axis=0)
            halves.append(_dot(jnp.concatenate(p_buf, axis=1), v_all) + _dot(p_new.astype(BF16), v_new))
        o = jnp.where(low, halves[0], halves[1])
        for g in range(ATT_GROUP):
            a_ref[pl.ds(r0, pair), g * KVW:(g + 1) * KVW] = o[g * pair:(g + 1) * pair, :]

    @pl.when(step == pl.num_programs(0) - 1)
    def _():
        y = _dot(a_ref[...].astype(BF16), wo_ref[...])
        o_ref[...] = _layer_norm(ALPHA * x_ref[...] + gt_ref[...] * y, g_ref[...], b_ref[...])


def _attn_sample(x, mods, wq, tables, k_new, v_new, k_buf, v_buf, sinks, wo, ln_g, ln_b, seq):
    t = x.shape[0]
    grp = SAMPLE_GROUP
    buf_spec = pl.BlockSpec((grp,) + k_buf.shape[1:], lambda i: (i, 0, 0))
    return pl.pallas_call(
        functools.partial(_attn_sample_kernel, seq=seq),
        grid=(t // (grp * seq),),
        in_specs=[pl.BlockSpec(memory_space=pltpu.SMEM), _resident(x.shape)]
                 + _mod_specs(1, 3, 3, t, False)
                 + [_resident(wq.shape)] + [_resident(tab.shape) for tab in tables]
                 + [_resident(k_new.shape), _resident(v_new.shape), buf_spec, buf_spec, _resident(wo.shape)]
                 + _ln_specs(4),
        out_specs=pl.BlockSpec((t, D_MODEL), lambda i: (0, 0)),
        out_shape=jax.ShapeDtypeStruct((t, D_MODEL), F32),
        scratch_shapes=[pltpu.VMEM((t, QW), F32), pltpu.VMEM((t, QW), F32)],
        compiler_params=_params(("arbitrary",)),
        name="attn_sample",
    )(sinks, x, mods, mods, mods, wq, *tables, k_new, v_new, k_buf, v_buf, wo, ln_g, ln_b)


def _rope_tables(pos):
    half = ROT_DIM // 2
    inv_freq = ROPE_THETA ** (-jnp.arange(half, dtype=F32) * 2.0 / ROT_DIM)
    ang = pos.astype(F32)[:, None] * inv_freq[None, :]
    cos, sin = jnp.cos(ang), jnp.sin(ang)
    n = pos.shape[0]
    rest = ATT_HD - ROT_DIM
    head = lambda a, b, fill: jnp.concatenate([a, b, jnp.full((n, rest), fill, F32)], axis=1)
    zero = jnp.zeros_like(sin)
    reps = LANES // ATT_HD
    return tuple(jnp.tile(t, (1, reps)) for t in
                 (head(cos, cos, 1.0), head(-sin, zero, 0.0), head(zero, sin, 0.0)))


def kernel(x_prompt, x_sample, c_prompt, c_sample, state_mlstm_C, state_mlstm_n, state_mlstm_m,
           cache_win_k, cache_win_v, ada_w, ada_b, ffn1_up, ffn1_down, ffn2_up, ffn2_down, ln_g, ln_b,
           ml_w_in, ml_b_gates, ml_norm_w, ml_w_out, kv_ada_w, kv_ada_b, kv_w,
           att_w_q, att_sinks, att_w_o):
    bp, sp, _ = x_prompt.shape
    bs, ss, _ = x_sample.shape
    n_buf = cache_win_k.shape[1]
    ts = bs * ss

    c_all = jnp.concatenate([jnp.repeat(c_sample, ss, axis=0), c_prompt,
                             jnp.zeros((-bp % SUBLANES, D_MODEL), F32)], axis=0)
    mods = _ada(c_all, ada_w, ada_b[:, None, :])
    kv_mods = _ada(c_all, kv_ada_w[None], kv_ada_b[None, None, :])

    bf = lambda w: w.astype(BF16)
    up1, down1, up2, down2 = bf(ffn1_up), bf(ffn1_down), bf(ffn2_up), bf(ffn2_down)
    w_in = bf(ml_w_in)
    gate_pad = LANES - 2 * ML_HEADS
    w_gate = bf(jnp.pad(ml_w_in[0, :, 2 * HQK + 2 * HV:], ((0, 0), (0, gate_pad))))
    b_gate = jnp.pad(ml_b_gates[0], (0, gate_pad))[None, :]
    norm_w = ml_norm_w[0][None, :]
    w_out = bf(ml_w_out[0])
    w_kv = bf(kv_w)
    order = np.arange(QW).reshape(ATT_KV_HEADS, ATT_GROUP, ATT_HD).transpose(1, 0, 2).reshape(-1)
    w_q = bf(att_w_q[0][:, order])
    w_o = bf(att_w_o[0][order, :])
    sinks = att_sinks[0]
    lng = ln_g.reshape(DEPTH * 3, 1, D_MODEL)
    lnb = ln_b.reshape(DEPTH * 3, 1, D_MODEL)

    tab_p = _rope_tables(jnp.arange(sp, dtype=jnp.int32))
    tab_s = _rope_tables(jnp.tile(PAST_LEN + jnp.arange(ss, dtype=jnp.int32), bs))

    xp = x_prompt.reshape(bp * sp, D_MODEL)
    xp = _ffn(xp, mods, 0, 0, up1, down1, lng, lnb, 0, ts, sp)
    xp, state_p, m_p = _mlstm_prompt(xp.reshape(bp, sp, D_MODEL), mods, w_in, w_gate, b_gate,
                                     norm_w, w_out, lng, lnb, ts)
    xp, k_p, v_p = _ffn(xp.reshape(bp * sp, D_MODEL), mods, 0, 6, up2, down2, lng, lnb, 2, ts, sp,
                        kv=(kv_mods, w_kv, tab_p))
    k_p = k_p.reshape(bp, sp, KVW)
    v_p = v_p.reshape(bp, sp, KVW)
    xp = _ffn(xp, mods, 1, 0, up1, down1, lng, lnb, 3, ts, sp)
    xp = _attn_prompt(xp.reshape(bp, sp, D_MODEL), mods, w_q, tab_p, k_p, v_p, sinks, w_o, lng, lnb, ts)
    xp = _ffn(xp.reshape(bp * sp, D_MODEL), mods, 1, 6, up2, down2, lng, lnb, 5, ts, sp)
    y_prompt = xp.reshape(bp, sp, D_MODEL)
    keep = min(WINDOW, sp)
    win_k_p = k_p[:, sp - keep:].reshape(bp, keep, ATT_KV_HEADS, ATT_HD)
    win_v_p = v_p[:, sp - keep:].reshape(bp, keep, ATT_KV_HEADS, ATT_HD)

    xs = x_sample.reshape(ts, D_MODEL)
    xs = _ffn(xs, mods, 0, 0, up1, down1, lng, lnb, 0, ts, None)
    m0 = jnp.pad(state_mlstm_m[0], ((0, 0), (0, LANES - ML_HEADS)))
    xs, c_s, n_s, m_s = _mlstm_sample(xs, mods, w_in, w_gate, b_gate, norm_w, w_out, lng, lnb,
                                      state_mlstm_C[0], state_mlstm_n[0].reshape(bs, HQK), m0, ss)
    xs, k_s, v_s = _ffn(xs, mods, 0, 6, up2, down2, lng, lnb, 2, ts, None, kv=(kv_mods, w_kv, tab_s))
    xs = _ffn(xs, mods, 1, 0, up1, down1, lng, lnb, 3, ts, None)
    xs = _attn_sample(xs, mods, w_q, tab_s, k_s, v_s,
                      cache_win_k.reshape(bs, n_buf, KVW), cache_win_v.reshape(bs, n_buf, KVW),
                      sinks, w_o, lng, lnb, ss)
    xs = _ffn(xs, mods, 1, 6, up2, down2, lng, lnb, 5, ts, None)
    y_sample = xs.reshape(bs, ss, D_MODEL)
    win_k_s = jnp.concatenate([cache_win_k, k_s.reshape(bs, ss, ATT_KV_HEADS, ATT_HD)], axis=1)[:, -n_buf:]
    win_v_s = jnp.concatenate([cache_win_v, v_s.reshape(bs, ss, ATT_KV_HEADS, ATT_HD)], axis=1)[:, -n_buf:]

    return (y_prompt, y_sample,
            state_p[None, ..., :ML_DV], state_p[None, ..., ML_DV], m_p[None, :, :ML_HEADS, 0],
            win_k_p, win_v_p,
            c_s[None], n_s.reshape(1, bs, ML_HEADS, ML_DK), m_s[None, :, :ML_HEADS],
            win_k_s, win_v_s)
```

```python
import functools

import numpy as np
import jax
import jax.numpy as jnp
from jax import lax
from jax.experimental import pallas as pl
from jax.experimental.pallas import tpu as pltpu

F32 = jnp.float32
BF16 = jnp.bfloat16

D_MODEL = 1024
DEPTH = 2
PAST_LEN = 8192
ML_HEADS = 4
ML_DK = D_MODEL // 8
ML_DV = D_MODEL // 4
ATT_Q_HEADS = 16
ATT_KV_HEADS = 4
ATT_GROUP = ATT_Q_HEADS // ATT_KV_HEADS
ATT_HD = 64
WINDOW = 128
ROT_DIM = ATT_HD // 4
ROPE_THETA = 500000.0
D_FF = 2816
ALPHA = (2 * DEPTH) ** 0.25
LN_EPS = 1e-5
N_MOD = 9
HQK = ML_HEADS * ML_DK
HV = ML_HEADS * ML_DV
KVW = ATT_KV_HEADS * ATT_HD
QW = ATT_Q_HEADS * ATT_HD

LANES = 128
SUBLANES = 8
VMEM_LIMIT_BYTES = 56 * 1024 * 1024

FFN_ROWS = 512
FFN_CHUNK = 512
ML_ROWS = 512
ML_CHUNK = 128
ATT_ROWS = 256
ADA_COLS = 1024
SAMPLE_GROUP = 8

NT_DIMS = (((1,), (1,)), ((), ()))
HIGHEST = lax.Precision.HIGHEST


def _params(semantics):
    return pltpu.CompilerParams(dimension_semantics=semantics, vmem_limit_bytes=VMEM_LIMIT_BYTES)


def _const_spec(block, index):
    return pl.BlockSpec(block, lambda *_: index, pipeline_mode=pl.Buffered(1))


def _resident(shape):
    return _const_spec(shape, (0,) * len(shape))


def _dot(a, b):
    return jnp.dot(a, b, preferred_element_type=F32)


def _dot_exact(a, b):
    return jnp.dot(a, b, preferred_element_type=F32, precision=HIGHEST)


def _nt(a, b):
    return lax.dot_general(a, b, NT_DIMS, preferred_element_type=F32)


def _iota(shape, dim):
    return lax.broadcasted_iota(jnp.int32, shape, dim)


def _silu(x):
    return x * jax.nn.sigmoid(x)


def _log_sigmoid(x):
    return jnp.minimum(x, 0.0) - jnp.log1p(jnp.exp(-jnp.abs(x)))


def _layer_norm(y, g, b):
    mu = jnp.mean(y, axis=-1, keepdims=True)
    d = y - mu
    var = jnp.mean(d * d, axis=-1, keepdims=True)
    return d * lax.rsqrt(var + LN_EPS) * g + b


def _mod(ref, batch):
    return ref[...] if batch is None else ref[pl.ds(batch, 1), :]


def _modulate_bf16(x, shift, scale):
    return (x * (1.0 + scale) + shift).astype(BF16)


def _rope(x, cos, sin_lo, sin_hi):
    width = x.shape[1]
    reps = width // LANES
    tile = lambda t: jnp.concatenate([t] * reps, axis=1) if reps > 1 else t
    x_up = pltpu.roll(x, width - ROT_DIM // 2, 1)
    x_dn = pltpu.roll(x, ROT_DIM // 2, 1)
    return x * tile(cos) + x_up * tile(sin_lo) + x_dn * tile(sin_hi)


def _mod_specs(layer, first, count, n_sample_rows, prompt):
    if prompt:
        blk = n_sample_rows // SUBLANES
        return [_const_spec((None, SUBLANES, D_MODEL), (layer, blk, first + j)) for j in range(count)]
    return [_const_spec((None, n_sample_rows, D_MODEL), (layer, 0, first + j)) for j in range(count)]


def _ln_specs(index):
    return [_const_spec((None, 1, D_MODEL), (index, 0, 0))] * 2


def _ada_kernel(c_ref, w_ref, b_ref, o_ref):
    ca = _silu(c_ref[...]).astype(BF16)
    o_ref[...] = _dot(ca, w_ref[...].astype(BF16)) + b_ref[...]


def _ada(c, w, b):
    n_l, k, n = w.shape
    m = c.shape[0]
    return pl.pallas_call(
        _ada_kernel,
        grid=(n_l, n // ADA_COLS),
        in_specs=[pl.BlockSpec((m, k), lambda l, j: (0, 0)),
                  pl.BlockSpec((None, k, ADA_COLS), lambda l, j: (l, 0, j)),
                  pl.BlockSpec((None, 1, ADA_COLS), lambda l, j: (l, 0, j))],
        out_specs=pl.BlockSpec((None, m, ADA_COLS), lambda l, j: (l, 0, j)),
        out_shape=jax.ShapeDtypeStruct((n_l, m, n), F32),
        compiler_params=_params(("arbitrary", "arbitrary")),
        name="ada",
    )(c, w, b)


def _ffn_kernel(x_ref, sh_ref, sc_ref, gt_ref, wa_ref, wu_ref, wd_ref, g_ref, b_ref, *rest,
                steps_per_seq, has_kv, has_cast):
    rest = list(rest)
    kv_in = [rest.pop(0) for _ in range(6)] if has_kv else None
    cast_in = [rest.pop(0) for _ in range(2)] if has_cast else None
    o_ref = rest.pop(0)
    kv_out = [rest.pop(0) for _ in range(2)] if has_kv else None
    cast_out = [rest.pop(0) for _ in range(2)] if has_cast else None
    acc_ref, = rest

    batch = None if steps_per_seq is None else pl.program_id(0) // steps_per_seq
    x = x_ref[...]
    xm = _modulate_bf16(x, _mod(sh_ref, batch), _mod(sc_ref, batch))
    for c0 in range(0, D_FF, FFN_CHUNK):
        cw = min(FFN_CHUNK, D_FF - c0)
        a = _dot(xm, wa_ref[:, c0:c0 + cw])
        u = _dot(xm, wu_ref[:, c0:c0 + cw])
        p = _dot((_silu(a) * u).astype(BF16), wd_ref[c0:c0 + cw, :])
        if c0 == 0:
            acc_ref[...] = p
        else:
            acc_ref[...] += p
    y = ALPHA * x + (0.5 * _mod(gt_ref, batch)) * acc_ref[...]
    out = _layer_norm(y, g_ref[...], b_ref[...])
    o_ref[...] = out
    if has_kv:
        ksh_ref, ksc_ref, wkv_ref, cos_ref, slo_ref, shi_ref = kv_in
        kv = _dot(_modulate_bf16(out, _mod(ksh_ref, batch), _mod(ksc_ref, batch)), wkv_ref[...])
        kv_out[0][...] = _rope(kv[:, :KVW], cos_ref[...], slo_ref[...], shi_ref[...])
        kv_out[1][...] = kv[:, KVW:]
    if has_cast:
        for src, dst in zip(cast_in, cast_out):
            dst[...] = src[...].astype(BF16)


def _ffn(x, mods, layer, first, w_up, w_down, ln_g, ln_b, ln_index, n_sample_rows, seq, kv=None, cast=None):
    t = x.shape[0]
    rows = min(FFN_ROWS, t)
    steps = t // rows
    x_spec = pl.BlockSpec((rows, D_MODEL), lambda i: (i, 0))
    in_specs = ([x_spec] + _mod_specs(layer, first, 3, n_sample_rows, seq is not None)
                + [_const_spec((D_MODEL, D_FF), (0, 0)), _const_spec((D_MODEL, D_FF), (0, 1)),
                   _resident(w_down.shape)]
                + _ln_specs(ln_index))
    args = [x, mods, mods, mods, w_up, w_up, w_down, ln_g, ln_b]
    out_specs = [x_spec]
    out_shape = [jax.ShapeDtypeStruct((t, D_MODEL), F32)]
    if kv is not None:
        kv_mods, w_kv, tables = kv
        pos_blocks = tables[0].shape[0] // rows
        tab_spec = pl.BlockSpec((rows, LANES), lambda i: (i % pos_blocks, 0))
        kv_spec = pl.BlockSpec((rows, KVW), lambda i: (i, 0))
        in_specs += (_mod_specs(0, 0, 2, n_sample_rows, seq is not None)
                     + [_resident(w_kv.shape), tab_spec, tab_spec, tab_spec])
        args += [kv_mods, kv_mods, w_kv, *tables]
        out_specs += [kv_spec, kv_spec]
        out_shape += [jax.ShapeDtypeStruct((t, KVW), F32)] * 2
    if cast is not None:
        up_f32, down_f32, src_layer = cast
        up_rows = D_MODEL // steps
        down_rows = 2 * D_FF // steps
        assert up_rows % 16 == 0 and down_rows % 16 == 0 and steps % 2 == 0
        in_specs += [pl.BlockSpec((None, up_rows, 2 * D_FF), lambda i: (src_layer, i, 0)),
                     pl.BlockSpec((None, down_rows, D_MODEL), lambda i: (src_layer, i // 2, 0))]
        args += [up_f32, down_f32]
        out_specs += [pl.BlockSpec((up_rows, 2 * D_FF), lambda i: (i, 0)),
                      pl.BlockSpec((down_rows, D_MODEL), lambda i: (i // 2, 0))]
        out_shape += [jax.ShapeDtypeStruct((D_MODEL, 2 * D_FF), BF16),
                      jax.ShapeDtypeStruct((D_FF, D_MODEL), BF16)]
    outs = pl.pallas_call(
        functools.partial(_ffn_kernel, steps_per_seq=None if seq is None else seq // rows,
                          has_kv=kv is not None, has_cast=cast is not None),
        grid=(steps,),
        in_specs=in_specs,
        out_specs=out_specs,
        out_shape=out_shape,
        scratch_shapes=[pltpu.VMEM((rows, D_MODEL), F32)],
        compiler_params=_params(("arbitrary",)),
        name="ffn" + ("_kv" if kv is not None else "") + ("_cast" if cast is not None else ""),
    )(*args)
    return outs[0] if len(outs) == 1 else outs


def _head_norm(hh, w_row):
    mu = jnp.mean(hh, axis=-1, keepdims=True)
    d = hh - mu
    var = jnp.mean(d * d, axis=-1, keepdims=True)
    return d * lax.rsqrt(var + LN_EPS) * w_row


def _split3(x, axis):
    hi = x.astype(BF16)
    r1 = x - hi.astype(F32)
    mid = r1.astype(BF16)
    lo = (r1 - mid.astype(F32)).astype(BF16)
    return jnp.concatenate([hi, mid, lo], axis=axis)


def _tile_lanes(t, reps):
    return jnp.concatenate([t] * reps, axis=1)


def _mlstm_prompt_kernel(x_ref, sh_ref, sc_ref, gt_ref, wp_ref, wg_ref, bg_ref, nw_ref, wo_ref, g_ref, b_ref,
                         o_ref, c_out, m_out, proj_ref, rep_ref, y_ref, c_ref, mm_ref):
    batch = pl.program_id(0)
    step = pl.program_id(1)
    size = ML_CHUNK
    assert size == LANES and ML_DK == LANES

    @pl.when(step == 0)
    def _():
        c_ref[...] = jnp.zeros_like(c_ref)
        mm_ref[...] = jnp.zeros_like(mm_ref)

    x = x_ref[...]
    rows = x.shape[0]
    n_chunks = rows // size
    h = _modulate_bf16(x, _mod(sh_ref, batch), _mod(sc_ref, batch))
    proj_ref[...] = _dot(h, wp_ref[...])
    gates = _dot(h, wg_ref[...]) + bg_ref[...]

    r = _iota((size, size), 0)
    c = _iota((size, size), 1)
    causal = c <= r
    eye = c == r
    tri3 = _tile_lanes(jnp.where(causal, 1.0, 0.0).astype(BF16), 3)
    n_rep = 2 * ML_HEADS
    spread = jnp.where(_iota((LANES, n_rep * LANES), 1) // LANES == _iota((LANES, n_rep * LANES), 0),
                       1.0, 0.0).astype(BF16)
    spread3 = jnp.concatenate([spread] * 3, axis=0)
    ones3 = jnp.ones((3 * size, LANES), BF16)
    ones_v = jnp.ones((size, LANES), BF16)
    sr = _iota((2 * SUBLANES, LANES), 0)
    sl = _iota((2 * SUBLANES, LANES), 1)
    pick = (jnp.where((sr < ML_HEADS) & (sl == sr), 1.0, 0.0)
            - jnp.where((sr < ML_HEADS) & (sl == sr + ML_HEADS), 1.0, 0.0)
            + jnp.where((sr >= SUBLANES) & (sr < SUBLANES + ML_HEADS) & (sl == sr - SUBLANES + ML_HEADS),
                        1.0, 0.0))
    pick3 = _tile_lanes(pick.astype(BF16), 3)

    log_f = _log_sigmoid(gates)
    bwide = jnp.concatenate([_dot(tri3, _split3(log_f[ci * size:(ci + 1) * size, :], 0))
                             for ci in range(n_chunks)], axis=0)
    src3 = _split3(jnp.where(_iota((rows, LANES), 1) < ML_HEADS, gates, bwide), 1)
    rep_ref[...] = _dot(src3, spread3)
    ab_rows = _nt(pick3, src3)
    a_rows = ab_rows[:SUBLANES, :]
    b_rows = ab_rows[SUBLANES:, :]
    lane_in_chunk = _iota((SUBLANES, rows), 1) % size
    run = a_rows
    shift = 1
    while shift < size:
        run = jnp.maximum(run, jnp.where(lane_in_chunk >= shift, pltpu.roll(run, shift, 1), -jnp.inf))
        shift *= 2
    m_rows = [mm_ref[...]]
    g_rows = []
    for ci in range(n_chunks):
        run_c = run[:, ci * size:(ci + 1) * size]
        g_rows.append(jnp.maximum(run_c, m_rows[-1]))
        end = (ci + 1) * size - 1
        m_end = b_rows[:, end:end + 1] + jnp.maximum(run[:, end:end + 1], m_rows[-1][:, :1])
        m_rows.append(jnp.broadcast_to(m_end, (SUBLANES, LANES)))
    mm_ref[...] = m_rows[-1]

    states = [c_ref[hd] for hd in range(ML_HEADS)]
    for ci in range(n_chunks):
        lo, hi = ci * size, (ci + 1) * size
        for hd in range(ML_HEADS):
            i_rep = rep_ref[lo:hi, hd * LANES:(hd + 1) * LANES]
            b_rep = rep_ref[lo:hi, (ML_HEADS + hd) * LANES:(ML_HEADS + hd + 1) * LANES]
            m_prev = m_rows[ci][hd:hd + 1, :]
            m_new = m_rows[ci + 1][hd:hd + 1, :]
            a_row = a_rows[hd:hd + 1, lo:hi]
            g_rep = _dot(_split3(jnp.where(eye, g_rows[ci][hd:hd + 1, :], 0.0), 1), ones3)
            w = jnp.where(causal, jnp.exp(a_row - g_rep), 0.0)
            w_init = jnp.exp(m_prev - g_rep)
            floor = jnp.exp(-(b_rep + g_rep))
            b_last = b_rep[size - 1:size, :]
            w_end = jnp.exp(b_last - b_rep + i_rep - m_new)
            decay = jnp.exp(b_last + m_prev - m_new)

            q = proj_ref[lo:hi, hd * ML_DK:(hd + 1) * ML_DK]
            k = proj_ref[lo:hi, HQK + hd * ML_DK:HQK + (hd + 1) * ML_DK] * (ML_DK ** -0.5)
            v = proj_ref[lo:hi, 2 * HQK + hd * ML_DV:2 * HQK + (hd + 1) * ML_DV]
            qb = q.astype(BF16)
            v_ext = jnp.concatenate([v.astype(BF16), ones_v], axis=1)
            s = _nt(qb, k.astype(BF16)) * w
            num = _dot(s.astype(BF16), v_ext) + _tile_lanes(w_init, 3) * _dot(qb, states[hd].astype(BF16))
            den = jnp.maximum(jnp.abs(num[:, ML_DV:]), floor)
            hh = num[:, :ML_DV] / _tile_lanes(den, ML_DV // LANES)
            states[hd] = _tile_lanes(decay, 3) * states[hd] + _dot((k * w_end).T.astype(BF16), v_ext)
            y_ref[lo:hi, hd * ML_DV:(hd + 1) * ML_DV] = _head_norm(hh, nw_ref[:, hd * ML_DV:(hd + 1) * ML_DV])
    for hd in range(ML_HEADS):
        c_ref[hd] = states[hd]

    o_gate = jax.nn.sigmoid(proj_ref[:, 2 * HQK + HV:2 * HQK + 2 * HV])
    y = _dot((y_ref[...] * o_gate).astype(BF16), wo_ref[...])
    o_ref[...] = _layer_norm(ALPHA * x + _mod(gt_ref, batch) * y, g_ref[...], b_ref[...])

    @pl.when(step == pl.num_programs(1) - 1)
    def _():
        c_out[...] = c_ref[...]
        m_out[...] = mm_ref[...]


def _mlstm_weight_specs(wg, bg, nw, wo):
    return [_const_spec((None, D_MODEL, 2 * HQK + 2 * HV), (0, 0, 0)), _resident(wg.shape),
            _resident(bg.shape), _resident(nw.shape), _resident(wo.shape)] + _ln_specs(1)


def _mlstm_prompt(x, mods, w_in, wg, bg, nw, wo, ln_g, ln_b, n_sample_rows):
    bsz, seq, _ = x.shape
    rows = min(ML_ROWS, seq)
    x_spec = pl.BlockSpec((None, rows, D_MODEL), lambda bi, i: (bi, i, 0))
    state_shape = (ML_HEADS, ML_DK, ML_DV + LANES)
    return pl.pallas_call(
        _mlstm_prompt_kernel,
        grid=(bsz, seq // rows),
        in_specs=[x_spec] + _mod_specs(0, 3, 3, n_sample_rows, True) + _mlstm_weight_specs(wg, bg, nw, wo),
        out_specs=[x_spec, pl.BlockSpec((None,) + state_shape, lambda bi, i: (bi, 0, 0, 0)),
                   pl.BlockSpec((None, SUBLANES, LANES), lambda bi, i: (bi, 0, 0))],
        out_shape=[jax.ShapeDtypeStruct((bsz, seq, D_MODEL), F32),
                   jax.ShapeDtypeStruct((bsz,) + state_shape, F32),
                   jax.ShapeDtypeStruct((bsz, SUBLANES, LANES), F32)],
        scratch_shapes=[pltpu.VMEM((rows, 2 * HQK + 2 * HV), F32),
                        pltpu.VMEM((rows, 2 * ML_HEADS * LANES), F32),
                        pltpu.VMEM((rows, HV), F32),
                        pltpu.VMEM(state_shape, F32),
                        pltpu.VMEM((SUBLANES, LANES), F32)],
        compiler_params=_params(("arbitrary", "arbitrary")),
        name="mlstm_prompt",
    )(x, mods, mods, mods, w_in, wg, bg, nw, wo, ln_g, ln_b)


def _mlstm_sample_kernel(x_ref, sh_ref, sc_ref, gt_ref, wp_ref, wg_ref, bg_ref, nw_ref, wo_ref, g_ref, b_ref,
                         c0_ref, n0_ref, m0_ref, o_ref, c_out, n_out, m_out,
                         proj_ref, gate_ref, y_ref, *, seq):
    step = pl.program_id(0)
    grp = c0_ref.shape[0]
    rows = grp * seq

    @pl.when(step == 0)
    def _():
        h = _modulate_bf16(x_ref[...], sh_ref[...], sc_ref[...])
        proj_ref[...] = _dot(h, wp_ref[...])
        gate_ref[...] = _dot(h, wg_ref[...]) + bg_ref[...]

    r = _iota((rows, rows), 0)
    c = _iota((rows, rows), 1)
    same = (r // seq) == (c // seq)
    causal = same & (c <= r)
    eye = r == c
    last_of_row = c == (r // seq) * seq + (seq - 1)
    same_f = same.astype(F32)
    expand = (_iota((rows, grp), 0) // seq == _iota((rows, grp), 1)).astype(F32)
    pick_last = (_iota((grp, rows), 1) == _iota((grp, rows), 0) * seq + (seq - 1)).astype(F32)
    ident = (_iota((ML_DK, ML_DK), 0) == _iota((ML_DK, ML_DK), 1)).astype(F32)
    seq_of_row = _iota((rows, ML_DV), 0) // seq
    col_seq = _iota((ML_DK, rows), 1) // seq
    lane = _iota((rows, LANES), 1)
    to_row = lambda col: jnp.sum(jnp.where(eye, col, 0.0), axis=0, keepdims=True)

    r0 = pl.multiple_of(step * rows, rows)
    gates = gate_ref[pl.ds(r0, rows), :]
    log_f = _log_sigmoid(gates)
    bwide = _dot_exact(causal.astype(F32), log_f)
    bsum = _dot_exact(same_f, log_f)
    m_prev_wide = _dot_exact(expand, m0_ref[...])
    n_rows_wide = _dot_exact(expand, n0_ref[...])
    n_pieces = []
    m_wide = jnp.zeros((rows, LANES), F32)
    for hd in range(ML_HEADS):
        q = proj_ref[pl.ds(r0, rows), hd * ML_DK:(hd + 1) * ML_DK]
        k = proj_ref[pl.ds(r0, rows), HQK + hd * ML_DK:HQK + (hd + 1) * ML_DK] * (ML_DK ** -0.5)
        v = proj_ref[pl.ds(r0, rows), 2 * HQK + hd * ML_DV:2 * HQK + (hd + 1) * ML_DV]
        bcol = bwide[:, ML_HEADS + hd:ML_HEADS + hd + 1]
        b_last = bsum[:, ML_HEADS + hd:ML_HEADS + hd + 1]
        icol = gates[:, hd:hd + 1]
        m_prev = m_prev_wide[:, hd:hd + 1]
        n_rows = n_rows_wide[:, hd * ML_DK:(hd + 1) * ML_DK]
        log_w = jnp.where(causal, bcol - to_row(bcol) + to_row(icol), -jnp.inf)
        log_init = bcol + m_prev
        m_t = jnp.maximum(log_init, jnp.max(log_w, axis=-1, keepdims=True))
        w = jnp.exp(log_w - m_t)
        w_init = jnp.exp(log_init - m_t)
        qb = q.astype(BF16)
        s = _nt(qb, k.astype(BF16)) * w
        inter = None
        for bi in range(grp):
            cand = _dot(qb, c0_ref[bi, hd].astype(BF16))
            inter = cand if inter is None else jnp.where(seq_of_row == bi, cand, inter)
        num = _dot(s.astype(BF16), v.astype(BF16)) + w_init * inter
        den = jnp.sum(s, axis=-1, keepdims=True) + w_init * jnp.sum(q * n_rows, axis=-1, keepdims=True)
        hh = num / jnp.maximum(jnp.abs(den), jnp.exp(-m_t))
        m_new = jnp.sum(jnp.where(last_of_row, to_row(m_t), 0.0), axis=1, keepdims=True)
        w_end = jnp.exp(b_last - bcol + icol - m_new)
        decay = jnp.exp(b_last + m_prev - m_new)
        kw = k * w_end
        kw_t = lax.dot_general(ident, kw, NT_DIMS, preferred_element_type=F32, precision=HIGHEST)
        lhs = jnp.concatenate([jnp.where(col_seq == bi, kw_t, 0.0) for bi in range(grp)], axis=0)
        upd = _dot(lhs.astype(BF16), v.astype(BF16))
        for bi in range(grp):
            c_out[bi, hd] = (decay[bi * seq:bi * seq + 1, :] * c0_ref[bi, hd]
                             + upd[bi * ML_DK:(bi + 1) * ML_DK, :])
        n_pieces.append(decay * n_rows + _dot_exact(same_f, kw))
        m_wide = jnp.where(lane == hd, m_new, m_wide)
        y_ref[pl.ds(r0, rows), hd * ML_DV:(hd + 1) * ML_DV] = _head_norm(
            hh, nw_ref[:, hd * ML_DV:(hd + 1) * ML_DV])
    n_out[...] = _dot_exact(pick_last, jnp.concatenate(n_pieces, axis=1))
    m_out[...] = _dot_exact(pick_last, m_wide)

    @pl.when(step == pl.num_programs(0) - 1)
    def _():
        o_gate = jax.nn.sigmoid(proj_ref[:, 2 * HQK + HV:2 * HQK + 2 * HV])
        y = _dot((y_ref[...] * o_gate).astype(BF16), wo_ref[...])
        o_ref[...] = _layer_norm(ALPHA * x_ref[...] + gt_ref[...] * y, g_ref[...], b_ref[...])


def _mlstm_sample(x, mods, w_in, wg, bg, nw, wo, ln_g, ln_b, c0, n0, m0, seq):
    t = x.shape[0]
    grp = SAMPLE_GROUP
    state_specs = [pl.BlockSpec((grp, ML_HEADS, ML_DK, ML_DV), lambda i: (i, 0, 0, 0)),
                   pl.BlockSpec((grp, HQK), lambda i: (i, 0)),
                   pl.BlockSpec((grp, LANES), lambda i: (i, 0))]
    return pl.pallas_call(
        functools.partial(_mlstm_sample_kernel, seq=seq),
        grid=(t // (grp * seq),),
        in_specs=[_resident(x.shape)] + _mod_specs(0, 3, 3, t, False)
                 + _mlstm_weight_specs(wg, bg, nw, wo) + state_specs,
        out_specs=[pl.BlockSpec((t, D_MODEL), lambda i: (0, 0))] + state_specs,
        out_shape=[jax.ShapeDtypeStruct((t, D_MODEL), F32),
                   jax.ShapeDtypeStruct(c0.shape, F32),
                   jax.ShapeDtypeStruct(n0.shape, F32),
                   jax.ShapeDtypeStruct(m0.shape, F32)],
        scratch_shapes=[pltpu.VMEM((t, 2 * HQK + 2 * HV), F32),
                        pltpu.VMEM((t, LANES), F32),
                        pltpu.VMEM((t, HV), F32)],
        compiler_params=_params(("arbitrary",)),
        name="mlstm_sample",
    )(x, mods, mods, mods, w_in, wg, bg, nw, wo, ln_g, ln_b, c0, n0, m0)


def _sink_column(sink_ref, kh, rows_per_head):
    rid = _iota((ATT_GROUP * rows_per_head, 1), 0)
    col = jnp.full((ATT_GROUP * rows_per_head, 1), sink_ref[kh * ATT_GROUP], F32)
    for gq in range(1, ATT_GROUP):
        col = jnp.where(rid >= gq * rows_per_head, sink_ref[kh * ATT_GROUP + gq], col)
    return col


def _head_lanes(x, kh):
    return jnp.where(_iota(x.shape, 1) // ATT_HD == kh, x, 0.0).astype(BF16)


def _stack_groups(x):
    return jnp.concatenate([x[:, g * KVW:(g + 1) * KVW] for g in range(ATT_GROUP)], axis=0)


def _attn_prompt_kernel(sink_ref, x_ref, sh_ref, sc_ref, gt_ref, wq_ref, cos_ref, slo_ref, shi_ref,
                        kp_ref, kc_ref, vp_ref, vc_ref, wo_ref, g_ref, b_ref, o_ref, a_ref):
    batch = pl.program_id(0)
    step = pl.program_id(1)
    x = x_ref[...]
    rows = x.shape[0]
    n_blk = rows // WINDOW
    q = _rope(_dot(_modulate_bf16(x, _mod(sh_ref, batch), _mod(sc_ref, batch)), wq_ref[...]),
              cos_ref[...], slo_ref[...], shi_ref[...])
    q = (q * (ATT_HD ** -0.5)).astype(BF16)

    keys = jnp.concatenate([kp_ref[...], kc_ref[...]], axis=0)
    vals = jnp.concatenate([vp_ref[...], vc_ref[...]], axis=0)
    k_heads = [_head_lanes(keys, kh) for kh in range(ATT_KV_HEADS)]
    v_heads = [_head_lanes(vals, kh) for kh in range(ATT_KV_HEADS)]

    rid = _iota((ATT_GROUP * WINDOW, WINDOW), 0) % WINDOW
    cid = _iota((ATT_GROUP * WINDOW, WINDOW), 1)
    from_prev = cid > rid
    no_prev = jnp.where(step > 0, 0.0, -jnp.inf)
    sinks = [_sink_column(sink_ref, kh, WINDOW) for kh in range(ATT_KV_HEADS)]

    scores = [[_nt(_stack_groups(q[n * WINDOW:(n + 1) * WINDOW, :]),
                   k_heads[kh][n * WINDOW:(n + 2) * WINDOW, :])
               for kh in range(ATT_KV_HEADS)] for n in range(n_blk)]
    probs = []
    for n in range(n_blk):
        row = []
        for kh in range(ATT_KV_HEADS):
            s_prev = scores[n][kh][:, :WINDOW]
            if n == 0:
                s_prev = s_prev + no_prev
            s = jnp.where(from_prev, s_prev, scores[n][kh][:, WINDOW:])
            mx = jnp.maximum(jnp.max(s, axis=-1, keepdims=True), sinks[kh])
            p = jnp.exp(s - mx)
            den = jnp.sum(p, axis=-1, keepdims=True) + jnp.exp(sinks[kh] - mx)
            pn = p / den
            row.append(jnp.where(from_prev, pn, 0.0).astype(BF16))
            row.append(jnp.where(from_prev, 0.0, pn).astype(BF16))
        probs.append(jnp.concatenate(row, axis=1))
    for n in range(n_blk):
        v_all = jnp.concatenate([v_heads[kh][n * WINDOW:(n + 2) * WINDOW, :]
                                 for kh in range(ATT_KV_HEADS)], axis=0)
        o = _dot(probs[n], v_all)
        for g in range(ATT_GROUP):
            a_ref[n * WINDOW:(n + 1) * WINDOW, g * KVW:(g + 1) * KVW] = o[g * WINDOW:(g + 1) * WINDOW, :]

    y = _dot(a_ref[...].astype(BF16), wo_ref[...])
    o_ref[...] = _layer_norm(ALPHA * x + _mod(gt_ref, batch) * y, g_ref[...], b_ref[...])


def _attn_prompt(x, mods, wq, tables, k, v, sinks, wo, ln_g, ln_b, n_sample_rows):
    bsz, seq, _ = x.shape
    rows = min(ATT_ROWS, seq)
    per = rows // WINDOW
    x_spec = pl.BlockSpec((None, rows, D_MODEL), lambda bi, i: (bi, i, 0))
    cur = pl.BlockSpec((None, rows, KVW), lambda bi, i: (bi, i, 0))
    prev = pl.BlockSpec((None, WINDOW, KVW), lambda bi, i: (bi, jnp.maximum(i * per - 1, 0), 0))
    tab = pl.BlockSpec((rows, LANES), lambda bi, i: (i, 0))
    return pl.pallas_call(
        _attn_prompt_kernel,
        grid=(bsz, seq // rows),
        in_specs=[pl.BlockSpec(memory_space=pltpu.SMEM), x_spec]
                 + _mod_specs(1, 3, 3, n_sample_rows, True)
                 + [_resident(wq.shape), tab, tab, tab, prev, cur, prev, cur, _resident(wo.shape)]
                 + _ln_specs(4),
        out_specs=x_spec,
        out_shape=jax.ShapeDtypeStruct((bsz, seq, D_MODEL), F32),
        scratch_shapes=[pltpu.VMEM((rows, QW), F32)],
        compiler_params=_params(("arbitrary", "arbitrary")),
        name="attn_prompt",
    )(sinks, x, mods, mods, mods, wq, *tables, k, k, v, v, wo, ln_g, ln_b)


def _attn_sample_kernel(sink_ref, x_ref, sh_ref, sc_ref, gt_ref, wq_ref, cos_ref, slo_ref, shi_ref,
                        kn_ref, vn_ref, kb_ref, vb_ref, wo_ref, g_ref, b_ref, o_ref,
                        q_ref, a_ref, *, seq):
    step = pl.program_id(0)
    pair = 2 * seq
    assert pair == SUBLANES
    grp, n_buf, _ = kb_ref.shape

    @pl.when(step == 0)
    def _():
        q = _rope(_dot(_modulate_bf16(x_ref[...], sh_ref[...], sc_ref[...]), wq_ref[...]),
                  cos_ref[...], slo_ref[...], shi_ref[...])
        q_ref[...] = q * (ATT_HD ** -0.5)

    n_rows = ATT_GROUP * pair
    n_new = ATT_KV_HEADS * pair
    q_pos = _iota((n_rows, n_buf), 0) % seq + PAST_LEN
    k_pos = _iota((n_rows, n_buf), 1) + (PAST_LEN - n_buf)
    buf_mask = (k_pos <= q_pos) & (k_pos > q_pos - WINDOW) & (k_pos >= 0)
    q_tok = _iota((n_rows, n_new), 0) % seq
    new_col = _iota((n_rows, n_new), 1)
    new_row_head = _iota((n_new, KVW), 0) // pair
    new_lane_head = _iota((n_new, KVW), 1) // ATT_HD
    low = _iota((n_rows, KVW), 0) % pair < seq
    sinks = [_sink_column(sink_ref, kh, pair) for kh in range(ATT_KV_HEADS)]
    for p in range(grp // 2):
        r0 = pl.multiple_of(step * (grp * seq) + p * pair, pair)
        q_all = _stack_groups(q_ref[pl.ds(r0, pair), :].astype(BF16))
        stack = lambda a: jnp.where(new_row_head == new_lane_head,
                                    jnp.concatenate([a] * ATT_KV_HEADS, axis=0), 0.0).astype(BF16)
        k_new = stack(kn_ref[pl.ds(r0, pair), :])
        v_new = stack(vn_ref[pl.ds(r0, pair), :])
        s_new_all = _nt(q_all, k_new)
        halves = []
        for half in range(2):
            bi = 2 * p + half
            k_buf, v_buf = kb_ref[bi], vb_ref[bi]
            tok = new_col % pair - half * seq
            own_new = (tok >= 0) & (tok < seq) & (tok <= q_tok)
            p_buf, p_new = [], None
            for kh in range(ATT_KV_HEADS):
                s_buf = jnp.where(buf_mask, _nt(q_all, _head_lanes(k_buf, kh)), -jnp.inf)
                s_new = jnp.where(own_new & (new_col // pair == kh), s_new_all, -jnp.inf)
                mx = jnp.maximum(jnp.maximum(jnp.max(s_buf, axis=-1, keepdims=True),
                                             jnp.max(s_new, axis=-1, keepdims=True)), sinks[kh])
                e_buf = jnp.exp(s_buf - mx)
                e_new = jnp.exp(s_new - mx)
                den = (jnp.sum(e_buf, axis=-1, keepdims=True) + jnp.sum(e_new, axis=-1, keepdims=True)
                       + jnp.exp(sinks[kh] - mx))
                p_buf.append((e_buf / den).astype(BF16))
                p_new = e_new / den if p_new is None else p_new + e_new / den
            v_all = jnp.concatenate([_head_lanes(v_buf, kh) for kh in range(ATT_KV_HEADS)], axis=0)
            halves.append(_dot(jnp.concatenate(p_buf, axis=1), v_all) + _dot(p_new.astype(BF16), v_new))
        o = jnp.where(low, halves[0], halves[1])
        for g in range(ATT_GROUP):
            a_ref[pl.ds(r0, pair), g * KVW:(g + 1) * KVW] = o[g * pair:(g + 1) * pair, :]

    @pl.when(step == pl.num_programs(0) - 1)
    def _():
        y = _dot(a_ref[...].astype(BF16), wo_ref[...])
        o_ref[...] = _layer_norm(ALPHA * x_ref[...] + gt_ref[...] * y, g_ref[...], b_ref[...])


def _attn_sample(x, mods, wq, tables, k_new, v_new, k_buf, v_buf, sinks, wo, ln_g, ln_b, seq):
    t = x.shape[0]
    grp = SAMPLE_GROUP
    buf_spec = pl.BlockSpec((grp,) + k_buf.shape[1:], lambda i: (i, 0, 0))
    return pl.pallas_call(
        functools.partial(_attn_sample_kernel, seq=seq),
        grid=(t // (grp * seq),),
        in_specs=[pl.BlockSpec(memory_space=pltpu.SMEM), _resident(x.shape)]
                 + _mod_specs(1, 3, 3, t, False)
                 + [_resident(wq.shape)] + [_resident(tab.shape) for tab in tables]
                 + [_resident(k_new.shape), _resident(v_new.shape), buf_spec, buf_spec, _resident(wo.shape)]
                 + _ln_specs(4),
        out_specs=pl.BlockSpec((t, D_MODEL), lambda i: (0, 0)),
        out_shape=jax.ShapeDtypeStruct((t, D_MODEL), F32),
        scratch_shapes=[pltpu.VMEM((t, QW), F32), pltpu.VMEM((t, QW), F32)],
        compiler_params=_params(("arbitrary",)),
        name="attn_sample",
    )(sinks, x, mods, mods, mods, wq, *tables, k_new, v_new, k_buf, v_buf, wo, ln_g, ln_b)


def _rope_tables(pos):
    half = ROT_DIM // 2
    inv_freq = ROPE_THETA ** (-jnp.arange(half, dtype=F32) * 2.0 / ROT_DIM)
    ang = pos.astype(F32)[:, None] * inv_freq[None, :]
    cos, sin = jnp.cos(ang), jnp.sin(ang)
    n = pos.shape[0]
    rest = ATT_HD - ROT_DIM
    head = lambda a, b, fill: jnp.concatenate([a, b, jnp.full((n, rest), fill, F32)], axis=1)
    zero = jnp.zeros_like(sin)
    reps = LANES // ATT_HD
    return tuple(jnp.tile(t, (1, reps)) for t in
                 (head(cos, cos, 1.0), head(-sin, zero, 0.0), head(zero, sin, 0.0)))


def kernel(x_prompt, x_sample, c_prompt, c_sample, state_mlstm_C, state_mlstm_n, state_mlstm_m,
           cache_win_k, cache_win_v, ada_w, ada_b, ffn1_up, ffn1_down, ffn2_up, ffn2_down, ln_g, ln_b,
           ml_w_in, ml_b_gates, ml_norm_w, ml_w_out, kv_ada_w, kv_ada_b, kv_w,
           att_w_q, att_sinks, att_w_o):
    bp, sp, _ = x_prompt.shape
    bs, ss, _ = x_sample.shape
    n_buf = cache_win_k.shape[1]
    ts = bs * ss

    c_all = jnp.concatenate([jnp.repeat(c_sample, ss, axis=0), c_prompt,
                             jnp.zeros((-bp % SUBLANES, D_MODEL), F32)], axis=0)
    mods = _ada(c_all, ada_w, ada_b[:, None, :])
    kv_mods = _ada(c_all, kv_ada_w[None], kv_ada_b[None, None, :])

    bf = lambda w: w.astype(BF16)
    ffn_f32 = [(ffn1_up, ffn1_down, 0), (ffn2_up, ffn2_down, 0), (ffn1_up, ffn1_down, 1), (ffn2_up, ffn2_down, 1)]
    ffn_w = [(bf(ffn1_up[0]), bf(ffn1_down[0]))]
    w_in = bf(ml_w_in)
    gate_pad = LANES - 2 * ML_HEADS
    w_gate = bf(jnp.pad(ml_w_in[0, :, 2 * HQK + 2 * HV:], ((0, 0), (0, gate_pad))))
    b_gate = jnp.pad(ml_b_gates[0], (0, gate_pad))[None, :]
    norm_w = ml_norm_w[0][None, :]
    w_out = bf(ml_w_out[0])
    w_kv = bf(kv_w)
    order = np.arange(QW).reshape(ATT_KV_HEADS, ATT_GROUP, ATT_HD).transpose(1, 0, 2).reshape(-1)
    w_q = bf(att_w_q[0][:, order])
    w_o = bf(att_w_o[0][order, :])
    sinks = att_sinks[0]
    lng = ln_g.reshape(DEPTH * 3, 1, D_MODEL)
    lnb = ln_b.reshape(DEPTH * 3, 1, D_MODEL)

    tab_p = _rope_tables(jnp.arange(sp, dtype=jnp.int32))
    tab_s = _rope_tables(jnp.tile(PAST_LEN + jnp.arange(ss, dtype=jnp.int32), bs))

    xp = x_prompt.reshape(bp * sp, D_MODEL)
    xp, *cast = _ffn(xp, mods, 0, 0, *ffn_w[0], lng, lnb, 0, ts, sp, cast=ffn_f32[1])
    ffn_w.append(tuple(cast))
    xp, state_p, m_p = _mlstm_prompt(xp.reshape(bp, sp, D_MODEL), mods, w_in, w_gate, b_gate,
                                     norm_w, w_out, lng, lnb, ts)
    xp, k_p, v_p, *cast = _ffn(xp.reshape(bp * sp, D_MODEL), mods, 0, 6, *ffn_w[1], lng, lnb, 2, ts, sp,
                               kv=(kv_mods, w_kv, tab_p), cast=ffn_f32[2])
    ffn_w.append(tuple(cast))
    k_p = k_p.reshape(bp, sp, KVW)
    v_p = v_p.reshape(bp, sp, KVW)
    xp, *cast = _ffn(xp, mods, 1, 0, *ffn_w[2], lng, lnb, 3, ts, sp, cast=ffn_f32[3])
    ffn_w.append(tuple(cast))
    xp = _attn_prompt(xp.reshape(bp, sp, D_MODEL), mods, w_q, tab_p, k_p, v_p, sinks, w_o, lng, lnb, ts)
    xp = _ffn(xp.reshape(bp * sp, D_MODEL), mods, 1, 6, *ffn_w[3], lng, lnb, 5, ts, sp)
    y_prompt = xp.reshape(bp, sp, D_MODEL)
    keep = min(WINDOW, sp)
    win_k_p = k_p[:, sp - keep:].reshape(bp, keep, ATT_KV_HEADS, ATT_HD)
    win_v_p = v_p[:, sp - keep:].reshape(bp, keep, ATT_KV_HEADS, ATT_HD)

    xs = x_sample.reshape(ts, D_MODEL)
    xs = _ffn(xs, mods, 0, 0, *ffn_w[0], lng, lnb, 0, ts, None)
    m0 = jnp.pad(state_mlstm_m[0], ((0, 0), (0, LANES - ML_HEADS)))
    xs, c_s, n_s, m_s = _mlstm_sample(xs, mods, w_in, w_gate, b_gate, norm_w, w_out, lng, lnb,
                                      state_mlstm_C[0], state_mlstm_n[0].reshape(bs, HQK), m0, ss)
    xs, k_s, v_s = _ffn(xs, mods, 0, 6, *ffn_w[1], lng, lnb, 2, ts, None, kv=(kv_mods, w_kv, tab_s))
    xs = _ffn(xs, mods, 1, 0, *ffn_w[2], lng, lnb, 3, ts, None)
    xs = _attn_sample(xs, mods, w_q, tab_s, k_s, v_s,
                      cache_win_k.reshape(bs, n_buf, KVW), cache_win_v.reshape(bs, n_buf, KVW),
                      sinks, w_o, lng, lnb, ss)
    xs = _ffn(xs, mods, 1, 6, *ffn_w[3], lng, lnb, 5, ts, None)
    y_sample = xs.reshape(bs, ss, D_MODEL)
    win_k_s = jnp.concatenate([cache_win_k, k_s.reshape(bs, ss, ATT_KV_HEADS, ATT_HD)], axis=1)[:, -n_buf:]
    win_v_s = jnp.concatenate([cache_win_v, v_s.reshape(bs, ss, ATT_KV_HEADS, ATT_HD)], axis=1)[:, -n_buf:]

    return (y_prompt, y_sample,
            state_p[None, ..., :ML_DV], state_p[None, ..., ML_DV], m_p[None, :, :ML_HEADS, 0],
            win_k_p, win_v_p,
            c_s[None], n_s.reshape(1, bs, ML_HEADS, ML_DK), m_s[None, :, :ML_HEADS],
            win_k_s, win_v_s)
```

```python
import functools

import numpy as np
import jax
import jax.numpy as jnp
from jax import lax
from jax.experimental import pallas as pl
from jax.experimental.pallas import tpu as pltpu

F32 = jnp.float32
BF16 = jnp.bfloat16

D_MODEL = 1024
DEPTH = 2
PAST_LEN = 8192
ML_HEADS = 4
ML_DK = D_MODEL // 8
ML_DV = D_MODEL // 4
ATT_Q_HEADS = 16
ATT_KV_HEADS = 4
ATT_GROUP = ATT_Q_HEADS // ATT_KV_HEADS
ATT_HD = 64
WINDOW = 128
ROT_DIM = ATT_HD // 4
ROPE_THETA = 500000.0
D_FF = 2816
ALPHA = (2 * DEPTH) ** 0.25
LN_EPS = 1e-5
N_MOD = 9
HQK = ML_HEADS * ML_DK
HV = ML_HEADS * ML_DV
KVW = ATT_KV_HEADS * ATT_HD
QW = ATT_Q_HEADS * ATT_HD

LANES = 128
SUBLANES = 8
VMEM_LIMIT_BYTES = 56 * 1024 * 1024

FFN_ROWS = 512
FFN_CHUNK = 512
ML_ROWS = 512
ML_CHUNK = 128
ATT_ROWS = 512
ADA_COLS = 1024
SAMPLE_GROUP = 8

NT_DIMS = (((1,), (1,)), ((), ()))
HIGHEST = lax.Precision.HIGHEST


def _params(semantics):
    return pltpu.CompilerParams(dimension_semantics=semantics, vmem_limit_bytes=VMEM_LIMIT_BYTES)


def _const_spec(block, index):
    return pl.BlockSpec(block, lambda *_: index, pipeline_mode=pl.Buffered(1))


def _resident(shape):
    return _const_spec(shape, (0,) * len(shape))


def _dot(a, b):
    return jnp.dot(a, b, preferred_element_type=F32)


def _dot_exact(a, b):
    return jnp.dot(a, b, preferred_element_type=F32, precision=HIGHEST)


def _nt(a, b):
    return lax.dot_general(a, b, NT_DIMS, preferred_element_type=F32)


def _iota(shape, dim):
    return lax.broadcasted_iota(jnp.int32, shape, dim)


def _silu(x):
    return x * jax.nn.sigmoid(x)


def _log_sigmoid(x):
    return jnp.minimum(x, 0.0) - jnp.log1p(jnp.exp(-jnp.abs(x)))


def _layer_norm(y, g, b):
    mu = jnp.mean(y, axis=-1, keepdims=True)
    d = y - mu
    var = jnp.mean(d * d, axis=-1, keepdims=True)
    return d * lax.rsqrt(var + LN_EPS) * g + b


def _mod(ref, batch):
    return ref[...] if batch is None else ref[pl.ds(batch, 1), :]


def _modulate_bf16(x, shift, scale):
    return (x * (1.0 + scale) + shift).astype(BF16)


def _rope(x, cos, sin_lo, sin_hi):
    width = x.shape[1]
    reps = width // LANES
    tile = lambda t: jnp.concatenate([t] * reps, axis=1) if reps > 1 else t
    x_up = pltpu.roll(x, width - ROT_DIM // 2, 1)
    x_dn = pltpu.roll(x, ROT_DIM // 2, 1)
    return x * tile(cos) + x_up * tile(sin_lo) + x_dn * tile(sin_hi)


def _mod_specs(layer, first, count, n_sample_rows, prompt):
    if prompt:
        blk = n_sample_rows // SUBLANES
        return [_const_spec((None, SUBLANES, D_MODEL), (layer, blk, first + j)) for j in range(count)]
    return [_const_spec((None, n_sample_rows, D_MODEL), (layer, 0, first + j)) for j in range(count)]


def _ln_specs(index):
    return [_const_spec((None, 1, D_MODEL), (index, 0, 0))] * 2


def _ada_kernel(c_ref, w_ref, b_ref, o_ref):
    ca = _silu(c_ref[...]).astype(BF16)
    o_ref[...] = _dot(ca, w_ref[...].astype(BF16)) + b_ref[...]


def _ada(c, w, b):
    n_l, k, n = w.shape
    m = c.shape[0]
    return pl.pallas_call(
        _ada_kernel,
        grid=(n_l, n // ADA_COLS),
        in_specs=[pl.BlockSpec((m, k), lambda l, j: (0, 0)),
                  pl.BlockSpec((None, k, ADA_COLS), lambda l, j: (l, 0, j)),
                  pl.BlockSpec((None, 1, ADA_COLS), lambda l, j: (l, 0, j))],
        out_specs=pl.BlockSpec((None, m, ADA_COLS), lambda l, j: (l, 0, j)),
        out_shape=jax.ShapeDtypeStruct((n_l, m, n), F32),
        compiler_params=_params(("arbitrary", "arbitrary")),
        name="ada",
    )(c, w, b)


def _ffn_kernel(x_ref, sh_ref, sc_ref, gt_ref, wa_ref, wu_ref, wd_ref, g_ref, b_ref, *rest,
                steps_per_seq, has_kv, has_cast):
    rest = list(rest)
    kv_in = [rest.pop(0) for _ in range(6)] if has_kv else None
    cast_in = [rest.pop(0) for _ in range(2)] if has_cast else None
    o_ref = rest.pop(0)
    kv_out = [rest.pop(0) for _ in range(2)] if has_kv else None
    cast_out = [rest.pop(0) for _ in range(2)] if has_cast else None
    acc_ref, = rest

    batch = None if steps_per_seq is None else pl.program_id(0) // steps_per_seq
    x = x_ref[...]
    xm = _modulate_bf16(x, _mod(sh_ref, batch), _mod(sc_ref, batch))
    for c0 in range(0, D_FF, FFN_CHUNK):
        cw = min(FFN_CHUNK, D_FF - c0)
        a = _dot(xm, wa_ref[:, c0:c0 + cw])
        u = _dot(xm, wu_ref[:, c0:c0 + cw])
        p = _dot((_silu(a) * u).astype(BF16), wd_ref[c0:c0 + cw, :])
        if c0 == 0:
            acc_ref[...] = p
        else:
            acc_ref[...] += p
    y = ALPHA * x + (0.5 * _mod(gt_ref, batch)) * acc_ref[...]
    out = _layer_norm(y, g_ref[...], b_ref[...])
    o_ref[...] = out
    if has_kv:
        ksh_ref, ksc_ref, wkv_ref, cos_ref, slo_ref, shi_ref = kv_in
        kv = _dot(_modulate_bf16(out, _mod(ksh_ref, batch), _mod(ksc_ref, batch)), wkv_ref[...])
        kv_out[0][...] = _rope(kv[:, :KVW], cos_ref[...], slo_ref[...], shi_ref[...])
        kv_out[1][...] = kv[:, KVW:]
    if has_cast:
        for src, dst in zip(cast_in, cast_out):
            dst[...] = src[...].astype(BF16)


def _ffn(x, mods, layer, first, w_up, w_down, ln_g, ln_b, ln_index, n_sample_rows, seq, kv=None, cast=None):
    t = x.shape[0]
    rows = min(FFN_ROWS, t)
    steps = t // rows
    x_spec = pl.BlockSpec((rows, D_MODEL), lambda i: (i, 0))
    in_specs = ([x_spec] + _mod_specs(layer, first, 3, n_sample_rows, seq is not None)
                + [_const_spec((D_MODEL, D_FF), (0, 0)), _const_spec((D_MODEL, D_FF), (0, 1)),
                   _resident(w_down.shape)]
                + _ln_specs(ln_index))
    args = [x, mods, mods, mods, w_up, w_up, w_down, ln_g, ln_b]
    out_specs = [x_spec]
    out_shape = [jax.ShapeDtypeStruct((t, D_MODEL), F32)]
    if kv is not None:
        kv_mods, w_kv, tables = kv
        pos_blocks = tables[0].shape[0] // rows
        tab_spec = pl.BlockSpec((rows, LANES), lambda i: (i % pos_blocks, 0))
        kv_spec = pl.BlockSpec((rows, KVW), lambda i: (i, 0))
        in_specs += (_mod_specs(0, 0, 2, n_sample_rows, seq is not None)
                     + [_resident(w_kv.shape), tab_spec, tab_spec, tab_spec])
        args += [kv_mods, kv_mods, w_kv, *tables]
        out_specs += [kv_spec, kv_spec]
        out_shape += [jax.ShapeDtypeStruct((t, KVW), F32)] * 2
    if cast is not None:
        up_f32, down_f32, src_layer = cast
        up_rows = D_MODEL // steps
        down_rows = 2 * D_FF // steps
        assert up_rows % 16 == 0 and down_rows % 16 == 0 and steps % 2 == 0
        in_specs += [pl.BlockSpec((None, up_rows, 2 * D_FF), lambda i: (src_layer, i, 0)),
                     pl.BlockSpec((None, down_rows, D_MODEL), lambda i: (src_layer, i // 2, 0))]
        args += [up_f32, down_f32]
        out_specs += [pl.BlockSpec((up_rows, 2 * D_FF), lambda i: (i, 0)),
                      pl.BlockSpec((down_rows, D_MODEL), lambda i: (i // 2, 0))]
        out_shape += [jax.ShapeDtypeStruct((D_MODEL, 2 * D_FF), BF16),
                      jax.ShapeDtypeStruct((D_FF, D_MODEL), BF16)]
    outs = pl.pallas_call(
        functools.partial(_ffn_kernel, steps_per_seq=None if seq is None else seq // rows,
                          has_kv=kv is not None, has_cast=cast is not None),
        grid=(steps,),
        in_specs=in_specs,
        out_specs=out_specs,
        out_shape=out_shape,
        scratch_shapes=[pltpu.VMEM((rows, D_MODEL), F32)],
        compiler_params=_params(("arbitrary",)),
        name="ffn" + ("_kv" if kv is not None else "") + ("_cast" if cast is not None else ""),
    )(*args)
    return outs[0] if len(outs) == 1 else outs


def _head_norm(hh, w_row):
    mu = jnp.mean(hh, axis=-1, keepdims=True)
    d = hh - mu
    var = jnp.mean(d * d, axis=-1, keepdims=True)
    return d * lax.rsqrt(var + LN_EPS) * w_row


def _split3(x, axis):
    hi = x.astype(BF16)
    r1 = x - hi.astype(F32)
    mid = r1.astype(BF16)
    lo = (r1 - mid.astype(F32)).astype(BF16)
    return jnp.concatenate([hi, mid, lo], axis=axis)


def _tile_lanes(t, reps):
    return jnp.concatenate([t] * reps, axis=1)


def _mlstm_prompt_kernel(x_ref, sh_ref, sc_ref, gt_ref, wp_ref, wg_ref, bg_ref, nw_ref, wo_ref, g_ref, b_ref,
                         o_ref, c_out, m_out, proj_ref, rep_ref, y_ref, c_ref, mm_ref):
    batch = pl.program_id(0)
    step = pl.program_id(1)
    size = ML_CHUNK
    assert size == LANES and ML_DK == LANES

    @pl.when(step == 0)
    def _():
        c_ref[...] = jnp.zeros_like(c_ref)
        mm_ref[...] = jnp.zeros_like(mm_ref)

    x = x_ref[...]
    rows = x.shape[0]
    n_chunks = rows // size
    h = _modulate_bf16(x, _mod(sh_ref, batch), _mod(sc_ref, batch))
    proj_ref[...] = _dot(h, wp_ref[...])
    gates = _dot(h, wg_ref[...]) + bg_ref[...]

    r = _iota((size, size), 0)
    c = _iota((size, size), 1)
    causal = c <= r
    eye = c == r
    tri3 = _tile_lanes(jnp.where(causal, 1.0, 0.0).astype(BF16), 3)
    n_rep = 2 * ML_HEADS
    spread = jnp.where(_iota((LANES, n_rep * LANES), 1) // LANES == _iota((LANES, n_rep * LANES), 0),
                       1.0, 0.0).astype(BF16)
    spread3 = jnp.concatenate([spread] * 3, axis=0)
    ones3 = jnp.ones((3 * size, LANES), BF16)
    ones_v = jnp.ones((size, LANES), BF16)
    sr = _iota((2 * SUBLANES, LANES), 0)
    sl = _iota((2 * SUBLANES, LANES), 1)
    pick = (jnp.where((sr < ML_HEADS) & (sl == sr), 1.0, 0.0)
            - jnp.where((sr < ML_HEADS) & (sl == sr + ML_HEADS), 1.0, 0.0)
            + jnp.where((sr >= SUBLANES) & (sr < SUBLANES + ML_HEADS) & (sl == sr - SUBLANES + ML_HEADS),
                        1.0, 0.0))
    pick3 = _tile_lanes(pick.astype(BF16), 3)

    log_f = _log_sigmoid(gates)
    bwide = jnp.concatenate([_dot(tri3, _split3(log_f[ci * size:(ci + 1) * size, :], 0))
                             for ci in range(n_chunks)], axis=0)
    src3 = _split3(jnp.where(_iota((rows, LANES), 1) < ML_HEADS, gates, bwide), 1)
    rep_ref[...] = _dot(src3, spread3)
    ab_rows = _nt(pick3, src3)
    a_rows = ab_rows[:SUBLANES, :]
    b_rows = ab_rows[SUBLANES:, :]
    lane_in_chunk = _iota((SUBLANES, rows), 1) % size
    run = a_rows
    shift = 1
    while shift < size:
        run = jnp.maximum(run, jnp.where(lane_in_chunk >= shift, pltpu.roll(run, shift, 1), -jnp.inf))
        shift *= 2
    m_rows = [mm_ref[...]]
    g_rows = []
    for ci in range(n_chunks):
        run_c = run[:, ci * size:(ci + 1) * size]
        g_rows.append(jnp.maximum(run_c, m_rows[-1]))
        end = (ci + 1) * size - 1
        m_end = b_rows[:, end:end + 1] + jnp.maximum(run[:, end:end + 1], m_rows[-1][:, :1])
        m_rows.append(jnp.broadcast_to(m_end, (SUBLANES, LANES)))
    mm_ref[...] = m_rows[-1]

    states = [c_ref[hd] for hd in range(ML_HEADS)]
    for ci in range(n_chunks):
        lo, hi = ci * size, (ci + 1) * size
        for hd in range(ML_HEADS):
            i_rep = rep_ref[lo:hi, hd * LANES:(hd + 1) * LANES]
            b_rep = rep_ref[lo:hi, (ML_HEADS + hd) * LANES:(ML_HEADS + hd + 1) * LANES]
            m_prev = m_rows[ci][hd:hd + 1, :]
            m_new = m_rows[ci + 1][hd:hd + 1, :]
            a_row = a_rows[hd:hd + 1, lo:hi]
            g_rep = _dot(_split3(jnp.where(eye, g_rows[ci][hd:hd + 1, :], 0.0), 1), ones3)
            w = jnp.where(causal, jnp.exp(a_row - g_rep), 0.0)
            w_init = jnp.exp(m_prev - g_rep)
            floor = jnp.exp(-(b_rep + g_rep))
            b_last = b_rep[size - 1:size, :]
            w_end = jnp.exp(b_last - b_rep + i_rep - m_new)
            decay = jnp.exp(b_last + m_prev - m_new)

            q = proj_ref[lo:hi, hd * ML_DK:(hd + 1) * ML_DK]
            k = proj_ref[lo:hi, HQK + hd * ML_DK:HQK + (hd + 1) * ML_DK] * (ML_DK ** -0.5)
            v = proj_ref[lo:hi, 2 * HQK + hd * ML_DV:2 * HQK + (hd + 1) * ML_DV]
            qb = q.astype(BF16)
            v_ext = jnp.concatenate([v.astype(BF16), ones_v], axis=1)
            s = _nt(qb, k.astype(BF16)) * w
            num = _dot(s.astype(BF16), v_ext) + _tile_lanes(w_init, 3) * _dot(qb, states[hd].astype(BF16))
            den = jnp.maximum(jnp.abs(num[:, ML_DV:]), floor)
            hh = num[:, :ML_DV] / _tile_lanes(den, ML_DV // LANES)
            states[hd] = _tile_lanes(decay, 3) * states[hd] + _dot((k * w_end).T.astype(BF16), v_ext)
            y_ref[lo:hi, hd * ML_DV:(hd + 1) * ML_DV] = _head_norm(hh, nw_ref[:, hd * ML_DV:(hd + 1) * ML_DV])
    for hd in range(ML_HEADS):
        c_ref[hd] = states[hd]

    o_gate = jax.nn.sigmoid(proj_ref[:, 2 * HQK + HV:2 * HQK + 2 * HV])
    y = _dot((y_ref[...] * o_gate).astype(BF16), wo_ref[...])
    o_ref[...] = _layer_norm(ALPHA * x + _mod(gt_ref, batch) * y, g_ref[...], b_ref[...])

    @pl.when(step == pl.num_programs(1) - 1)
    def _():
        c_out[...] = c_ref[...]
        m_out[...] = mm_ref[...]


def _mlstm_weight_specs(wg, bg, nw, wo):
    return [_const_spec((None, D_MODEL, 2 * HQK + 2 * HV), (0, 0, 0)), _resident(wg.shape),
            _resident(bg.shape), _resident(nw.shape), _resident(wo.shape)] + _ln_specs(1)


def _mlstm_prompt(x, mods, w_in, wg, bg, nw, wo, ln_g, ln_b, n_sample_rows):
    bsz, seq, _ = x.shape
    rows = min(ML_ROWS, seq)
    x_spec = pl.BlockSpec((None, rows, D_MODEL), lambda bi, i: (bi, i, 0))
    state_shape = (ML_HEADS, ML_DK, ML_DV + LANES)
    return pl.pallas_call(
        _mlstm_prompt_kernel,
        grid=(bsz, seq // rows),
        in_specs=[x_spec] + _mod_specs(0, 3, 3, n_sample_rows, True) + _mlstm_weight_specs(wg, bg, nw, wo),
        out_specs=[x_spec, pl.BlockSpec((None,) + state_shape, lambda bi, i: (bi, 0, 0, 0)),
                   pl.BlockSpec((None, SUBLANES, LANES), lambda bi, i: (bi, 0, 0))],
        out_shape=[jax.ShapeDtypeStruct((bsz, seq, D_MODEL), F32),
                   jax.ShapeDtypeStruct((bsz,) + state_shape, F32),
                   jax.ShapeDtypeStruct((bsz, SUBLANES, LANES), F32)],
        scratch_shapes=[pltpu.VMEM((rows, 2 * HQK + 2 * HV), F32),
                        pltpu.VMEM((rows, 2 * ML_HEADS * LANES), F32),
                        pltpu.VMEM((rows, HV), F32),
                        pltpu.VMEM(state_shape, F32),
                        pltpu.VMEM((SUBLANES, LANES), F32)],
        compiler_params=_params(("arbitrary", "arbitrary")),
        name="mlstm_prompt",
    )(x, mods, mods, mods, w_in, wg, bg, nw, wo, ln_g, ln_b)


def _mlstm_sample_kernel(x_ref, sh_ref, sc_ref, gt_ref, wp_ref, wg_ref, bg_ref, nw_ref, wo_ref, g_ref, b_ref,
                         c0_ref, n0_ref, m0_ref, o_ref, c_out, n_out, m_out,
                         proj_ref, gate_ref, y_ref, *, seq):
    step = pl.program_id(0)
    grp = c0_ref.shape[0]
    rows = grp * seq

    @pl.when(step == 0)
    def _():
        h = _modulate_bf16(x_ref[...], sh_ref[...], sc_ref[...])
        proj_ref[...] = _dot(h, wp_ref[...])
        gate_ref[...] = _dot(h, wg_ref[...]) + bg_ref[...]

    r = _iota((rows, rows), 0)
    c = _iota((rows, rows), 1)
    same = (r // seq) == (c // seq)
    causal = same & (c <= r)
    eye = r == c
    last_of_row = c == (r // seq) * seq + (seq - 1)
    same_f = same.astype(F32)
    expand = (_iota((rows, grp), 0) // seq == _iota((rows, grp), 1)).astype(F32)
    pick_last = (_iota((grp, rows), 1) == _iota((grp, rows), 0) * seq + (seq - 1)).astype(F32)
    ident = (_iota((ML_DK, ML_DK), 0) == _iota((ML_DK, ML_DK), 1)).astype(F32)
    seq_of_row = _iota((rows, ML_DV), 0) // seq
    col_seq = _iota((ML_DK, rows), 1) // seq
    lane = _iota((rows, LANES), 1)
    to_row = lambda col: jnp.sum(jnp.where(eye, col, 0.0), axis=0, keepdims=True)

    r0 = pl.multiple_of(step * rows, rows)
    gates = gate_ref[pl.ds(r0, rows), :]
    log_f = _log_sigmoid(gates)
    bwide = _dot_exact(causal.astype(F32), log_f)
    bsum = _dot_exact(same_f, log_f)
    m_prev_wide = _dot_exact(expand, m0_ref[...])
    n_rows_wide = _dot_exact(expand, n0_ref[...])
    n_pieces = []
    m_wide = jnp.zeros((rows, LANES), F32)
    for hd in range(ML_HEADS):
        q = proj_ref[pl.ds(r0, rows), hd * ML_DK:(hd + 1) * ML_DK]
        k = proj_ref[pl.ds(r0, rows), HQK + hd * ML_DK:HQK + (hd + 1) * ML_DK] * (ML_DK ** -0.5)
        v = proj_ref[pl.ds(r0, rows), 2 * HQK + hd * ML_DV:2 * HQK + (hd + 1) * ML_DV]
        bcol = bwide[:, ML_HEADS + hd:ML_HEADS + hd + 1]
        b_last = bsum[:, ML_HEADS + hd:ML_HEADS + hd + 1]
        icol = gates[:, hd:hd + 1]
        m_prev = m_prev_wide[:, hd:hd + 1]
        n_rows = n_rows_wide[:, hd * ML_DK:(hd + 1) * ML_DK]
        log_w = jnp.where(causal, bcol - to_row(bcol) + to_row(icol), -jnp.inf)
        log_init = bcol + m_prev
        m_t = jnp.maximum(log_init, jnp.max(log_w, axis=-1, keepdims=True))
        w = jnp.exp(log_w - m_t)
        w_init = jnp.exp(log_init - m_t)
        qb = q.astype(BF16)
        s = _nt(qb, k.astype(BF16)) * w
        inter = None
        for bi in range(grp):
            cand = _dot(qb, c0_ref[bi, hd].astype(BF16))
            inter = cand if inter is None else jnp.where(seq_of_row == bi, cand, inter)
        num = _dot(s.astype(BF16), v.astype(BF16)) + w_init * inter
        den = jnp.sum(s, axis=-1, keepdims=True) + w_init * jnp.sum(q * n_rows, axis=-1, keepdims=True)
        hh = num / jnp.maximum(jnp.abs(den), jnp.exp(-m_t))
        m_new = jnp.sum(jnp.where(last_of_row, to_row(m_t), 0.0), axis=1, keepdims=True)
        w_end = jnp.exp(b_last - bcol + icol - m_new)
        decay = jnp.exp(b_last + m_prev - m_new)
        kw = k * w_end
        kw_t = lax.dot_general(ident, kw, NT_DIMS, preferred_element_type=F32, precision=HIGHEST)
        lhs = jnp.concatenate([jnp.where(col_seq == bi, kw_t, 0.0) for bi in range(grp)], axis=0)
        upd = _dot(lhs.astype(BF16), v.astype(BF16))
        for bi in range(grp):
            c_out[bi, hd] = (decay[bi * seq:bi * seq + 1, :] * c0_ref[bi, hd]
                             + upd[bi * ML_DK:(bi + 1) * ML_DK, :])
        n_pieces.append(decay * n_rows + _dot_exact(same_f, kw))
        m_wide = jnp.where(lane == hd, m_new, m_wide)
        y_ref[pl.ds(r0, rows), hd * ML_DV:(hd + 1) * ML_DV] = _head_norm(
            hh, nw_ref[:, hd * ML_DV:(hd + 1) * ML_DV])
    n_out[...] = _dot_exact(pick_last, jnp.concatenate(n_pieces, axis=1))
    m_out[...] = _dot_exact(pick_last, m_wide)

    @pl.when(step == pl.num_programs(0) - 1)
    def _():
        o_gate = jax.nn.sigmoid(proj_ref[:, 2 * HQK + HV:2 * HQK + 2 * HV])
        y = _dot((y_ref[...] * o_gate).astype(BF16), wo_ref[...])
        o_ref[...] = _layer_norm(ALPHA * x_ref[...] + gt_ref[...] * y, g_ref[...], b_ref[...])


def _mlstm_sample(x, mods, w_in, wg, bg, nw, wo, ln_g, ln_b, c0, n0, m0, seq):
    t = x.shape[0]
    grp = SAMPLE_GROUP
    state_specs = [pl.BlockSpec((grp, ML_HEADS, ML_DK, ML_DV), lambda i: (i, 0, 0, 0)),
                   pl.BlockSpec((grp, HQK), lambda i: (i, 0)),
                   pl.BlockSpec((grp, LANES), lambda i: (i, 0))]
    return pl.pallas_call(
        functools.partial(_mlstm_sample_kernel, seq=seq),
        grid=(t // (grp * seq),),
        in_specs=[_resident(x.shape)] + _mod_specs(0, 3, 3, t, False)
                 + _mlstm_weight_specs(wg, bg, nw, wo) + state_specs,
        out_specs=[pl.BlockSpec((t, D_MODEL), lambda i: (0, 0))] + state_specs,
        out_shape=[jax.ShapeDtypeStruct((t, D_MODEL), F32),
                   jax.ShapeDtypeStruct(c0.shape, F32),
                   jax.ShapeDtypeStruct(n0.shape, F32),
                   jax.ShapeDtypeStruct(m0.shape, F32)],
        scratch_shapes=[pltpu.VMEM((t, 2 * HQK + 2 * HV), F32),
                        pltpu.VMEM((t, LANES), F32),
                        pltpu.VMEM((t, HV), F32)],
        compiler_params=_params(("arbitrary",)),
        name="mlstm_sample",
    )(x, mods, mods, mods, w_in, wg, bg, nw, wo, ln_g, ln_b, c0, n0, m0)


def _sink_column(sink_ref, kh, rows_per_head):
    rid = _iota((ATT_GROUP * rows_per_head, 1), 0)
    col = jnp.full((ATT_GROUP * rows_per_head, 1), sink_ref[kh * ATT_GROUP], F32)
    for gq in range(1, ATT_GROUP):
        col = jnp.where(rid >= gq * rows_per_head, sink_ref[kh * ATT_GROUP + gq], col)
    return col


def _head_lanes(x, kh):
    return jnp.where(_iota(x.shape, 1) // ATT_HD == kh, x, 0.0).astype(BF16)


def _stack_groups(x):
    return jnp.concatenate([x[:, g * KVW:(g + 1) * KVW] for g in range(ATT_GROUP)], axis=0)


def _attn_prompt_kernel(sink_ref, x_ref, sh_ref, sc_ref, gt_ref, wq_ref, cos_ref, slo_ref, shi_ref,
                        kp_ref, kc_ref, vp_ref, vc_ref, wo_ref, g_ref, b_ref, o_ref, a_ref):
    batch = pl.program_id(0)
    step = pl.program_id(1)
    x = x_ref[...]
    rows = x.shape[0]
    n_blk = rows // WINDOW
    q = _rope(_dot(_modulate_bf16(x, _mod(sh_ref, batch), _mod(sc_ref, batch)), wq_ref[...]),
              cos_ref[...], slo_ref[...], shi_ref[...])
    q = (q * (ATT_HD ** -0.5)).astype(BF16)

    keys = jnp.concatenate([kp_ref[...], kc_ref[...]], axis=0)
    vals = jnp.concatenate([vp_ref[...], vc_ref[...]], axis=0)
    k_heads = [_head_lanes(keys, kh) for kh in range(ATT_KV_HEADS)]
    v_heads = [_head_lanes(vals, kh) for kh in range(ATT_KV_HEADS)]

    rid = _iota((ATT_GROUP * WINDOW, WINDOW), 0) % WINDOW
    cid = _iota((ATT_GROUP * WINDOW, WINDOW), 1)
    from_prev = cid > rid
    no_prev = jnp.where(step > 0, 0.0, -jnp.inf)
    sinks = [_sink_column(sink_ref, kh, WINDOW) for kh in range(ATT_KV_HEADS)]

    scores = [[_nt(_stack_groups(q[n * WINDOW:(n + 1) * WINDOW, :]),
                   k_heads[kh][n * WINDOW:(n + 2) * WINDOW, :])
               for kh in range(ATT_KV_HEADS)] for n in range(n_blk)]
    probs = []
    for n in range(n_blk):
        row = []
        for kh in range(ATT_KV_HEADS):
            s_prev = scores[n][kh][:, :WINDOW]
            if n == 0:
                s_prev = s_prev + no_prev
            s = jnp.where(from_prev, s_prev, scores[n][kh][:, WINDOW:])
            mx = jnp.maximum(jnp.max(s, axis=-1, keepdims=True), sinks[kh])
            p = jnp.exp(s - mx)
            den = jnp.sum(p, axis=-1, keepdims=True) + jnp.exp(sinks[kh] - mx)
            pn = p / den
            row.append(jnp.where(from_prev, pn, 0.0).astype(BF16))
            row.append(jnp.where(from_prev, 0.0, pn).astype(BF16))
        probs.append(jnp.concatenate(row, axis=1))
    for n in range(n_blk):
        v_all = jnp.concatenate([v_heads[kh][n * WINDOW:(n + 2) * WINDOW, :]
                                 for kh in range(ATT_KV_HEADS)], axis=0)
        o = _dot(probs[n], v_all)
        for g in range(ATT_GROUP):
            a_ref[n * WINDOW:(n + 1) * WINDOW, g * KVW:(g + 1) * KVW] = o[g * WINDOW:(g + 1) * WINDOW, :]

    y = _dot(a_ref[...].astype(BF16), wo_ref[...])
    o_ref[...] = _layer_norm(ALPHA * x + _mod(gt_ref, batch) * y, g_ref[...], b_ref[...])


def _attn_prompt(x, mods, wq, tables, k, v, sinks, wo, ln_g, ln_b, n_sample_rows):
    bsz, seq, _ = x.shape
    rows = min(ATT_ROWS, seq)
    per = rows // WINDOW
    x_spec = pl.BlockSpec((None, rows, D_MODEL), lambda bi, i: (bi, i, 0))
    cur = pl.BlockSpec((None, rows, KVW), lambda bi, i: (bi, i, 0))
    prev = pl.BlockSpec((None, WINDOW, KVW), lambda bi, i: (bi, jnp.maximum(i * per - 1, 0), 0))
    tab = pl.BlockSpec((rows, LANES), lambda bi, i: (i, 0))
    return pl.pallas_call(
        _attn_prompt_kernel,
        grid=(bsz, seq // rows),
        in_specs=[pl.BlockSpec(memory_space=pltpu.SMEM), x_spec]
                 + _mod_specs(1, 3, 3, n_sample_rows, True)
                 + [_resident(wq.shape), tab, tab, tab, prev, cur, prev, cur, _resident(wo.shape)]
                 + _ln_specs(4),
        out_specs=x_spec,
        out_shape=jax.ShapeDtypeStruct((bsz, seq, D_MODEL), F32),
        scratch_shapes=[pltpu.VMEM((rows, QW), F32)],
        compiler_params=_params(("arbitrary", "arbitrary")),
        name="attn_prompt",
    )(sinks, x, mods, mods, mods, wq, *tables, k, k, v, v, wo, ln_g, ln_b)


def _attn_sample_kernel(sink_ref, x_ref, sh_ref, sc_ref, gt_ref, wq_ref, cos_ref, slo_ref, shi_ref,
                        kn_ref, vn_ref, kb_ref, vb_ref, wo_ref, g_ref, b_ref, o_ref,
                        q_ref, a_ref, *, seq):
    step = pl.program_id(0)
    pair = 2 * seq
    assert pair == SUBLANES
    grp, n_buf, _ = kb_ref.shape

    @pl.when(step == 0)
    def _():
        q = _rope(_dot(_modulate_bf16(x_ref[...], sh_ref[...], sc_ref[...]), wq_ref[...]),
                  cos_ref[...], slo_ref[...], shi_ref[...])
        q_ref[...] = q * (ATT_HD ** -0.5)

    n_rows = ATT_GROUP * pair
    n_all = ATT_KV_HEADS * n_rows
    n_new = 2 * pair
    q_pos = _iota((n_all, n_buf), 0) % seq + PAST_LEN
    k_pos = _iota((n_all, n_buf), 1) + (PAST_LEN - n_buf)
    buf_mask = (k_pos <= q_pos) & (k_pos > q_pos - WINDOW) & (k_pos >= 0)
    q_tok = _iota((n_all, n_new), 0) % seq
    new_col = _iota((n_all, n_new), 1)
    out_lane_head = _iota((n_rows, KVW), 1) // ATT_HD
    low = _iota((n_rows, KVW), 0) % pair < seq
    head_of_row = _iota((n_all, 1), 0) // pair
    sink_col = jnp.full((n_all, 1), sink_ref[0], F32)
    for hq in range(1, ATT_Q_HEADS):
        sink_col = jnp.where(head_of_row >= hq, sink_ref[hq], sink_col)
    pad_rows = jnp.zeros((n_new - pair, KVW), BF16)
    for p in range(grp // 2):
        r0 = pl.multiple_of(step * (grp * seq) + p * pair, pair)
        q_all = _stack_groups(q_ref[pl.ds(r0, pair), :].astype(BF16))
        q_bd = jnp.concatenate([_head_lanes(q_all, kh) for kh in range(ATT_KV_HEADS)], axis=0)
        k_new = jnp.concatenate([kn_ref[pl.ds(r0, pair), :].astype(BF16), pad_rows], axis=0)
        v_new = jnp.concatenate([vn_ref[pl.ds(r0, pair), :].astype(BF16), pad_rows], axis=0)
        s_new_all = _nt(q_bd, k_new)
        halves = []
        for half in range(2):
            bi = 2 * p + half
            tok = new_col - half * seq
            own_new = (tok >= 0) & (tok < seq) & (tok <= q_tok)
            s_buf = jnp.where(buf_mask, _nt(q_bd, kb_ref[bi].astype(BF16)), -jnp.inf)
            s_new = jnp.where(own_new, s_new_all, -jnp.inf)
            mx = jnp.maximum(jnp.maximum(jnp.max(s_buf, axis=-1, keepdims=True),
                                         jnp.max(s_new, axis=-1, keepdims=True)), sink_col)
            e_buf = jnp.exp(s_buf - mx)
            e_new = jnp.exp(s_new - mx)
            den = (jnp.sum(e_buf, axis=-1, keepdims=True) + jnp.sum(e_new, axis=-1, keepdims=True)
                   + jnp.exp(sink_col - mx))
            o_all = (_dot((e_buf / den).astype(BF16), vb_ref[bi].astype(BF16))
                     + _dot((e_new / den).astype(BF16), v_new))
            o_half = None
            for kh in range(ATT_KV_HEADS):
                part = jnp.where(out_lane_head == kh, o_all[kh * n_rows:(kh + 1) * n_rows, :], 0.0)
                o_half = part if o_half is None else o_half + part
            halves.append(o_half)
        o = jnp.where(low, halves[0], halves[1])
        for g in range(ATT_GROUP):
            a_ref[pl.ds(r0, pair), g * KVW:(g + 1) * KVW] = o[g * pair:(g + 1) * pair, :]

    @pl.when(step == pl.num_programs(0) - 1)
    def _():
        y = _dot(a_ref[...].astype(BF16), wo_ref[...])
        o_ref[...] = _layer_norm(ALPHA * x_ref[...] + gt_ref[...] * y, g_ref[...], b_ref[...])


def _attn_sample(x, mods, wq, tables, k_new, v_new, k_buf, v_buf, sinks, wo, ln_g, ln_b, seq):
    t = x.shape[0]
    grp = SAMPLE_GROUP
    buf_spec = pl.BlockSpec((grp,) + k_buf.shape[1:], lambda i: (i, 0, 0))
    return pl.pallas_call(
        functools.partial(_attn_sample_kernel, seq=seq),
        grid=(t // (grp * seq),),
        in_specs=[pl.BlockSpec(memory_space=pltpu.SMEM), _resident(x.shape)]
                 + _mod_specs(1, 3, 3, t, False)
                 + [_resident(wq.shape)] + [_resident(tab.shape) for tab in tables]
                 + [_resident(k_new.shape), _resident(v_new.shape), buf_spec, buf_spec, _resident(wo.shape)]
                 + _ln_specs(4),
        out_specs=pl.BlockSpec((t, D_MODEL), lambda i: (0, 0)),
        out_shape=jax.ShapeDtypeStruct((t, D_MODEL), F32),
        scratch_shapes=[pltpu.VMEM((t, QW), F32), pltpu.VMEM((t, QW), F32)],
        compiler_params=_params(("arbitrary",)),
        name="attn_sample",
    )(sinks, x, mods, mods, mods, wq, *tables, k_new, v_new, k_buf, v_buf, wo, ln_g, ln_b)


def _rope_tables(pos):
    half = ROT_DIM // 2
    inv_freq = ROPE_THETA ** (-jnp.arange(half, dtype=F32) * 2.0 / ROT_DIM)
    ang = pos.astype(F32)[:, None] * inv_freq[None, :]
    cos, sin = jnp.cos(ang), jnp.sin(ang)
    n = pos.shape[0]
    rest = ATT_HD - ROT_DIM
    head = lambda a, b, fill: jnp.concatenate([a, b, jnp.full((n, rest), fill, F32)], axis=1)
    zero = jnp.zeros_like(sin)
    reps = LANES // ATT_HD
    return tuple(jnp.tile(t, (1, reps)) for t in
                 (head(cos, cos, 1.0), head(-sin, zero, 0.0), head(zero, sin, 0.0)))


def kernel(x_prompt, x_sample, c_prompt, c_sample, state_mlstm_C, state_mlstm_n, state_mlstm_m,
           cache_win_k, cache_win_v, ada_w, ada_b, ffn1_up, ffn1_down, ffn2_up, ffn2_down, ln_g, ln_b,
           ml_w_in, ml_b_gates, ml_norm_w, ml_w_out, kv_ada_w, kv_ada_b, kv_w,
           att_w_q, att_sinks, att_w_o):
    bp, sp, _ = x_prompt.shape
    bs, ss, _ = x_sample.shape
    n_buf = cache_win_k.shape[1]
    ts = bs * ss

    c_all = jnp.concatenate([jnp.repeat(c_sample, ss, axis=0), c_prompt,
                             jnp.zeros((-bp % SUBLANES, D_MODEL), F32)], axis=0)
    mods = _ada(c_all, ada_w, ada_b[:, None, :])
    kv_mods = _ada(c_all, kv_ada_w[None], kv_ada_b[None, None, :])

    bf = lambda w: w.astype(BF16)
    ffn_f32 = [(ffn1_up, ffn1_down, 0), (ffn2_up, ffn2_down, 0), (ffn1_up, ffn1_down, 1), (ffn2_up, ffn2_down, 1)]
    ffn_w = [(bf(ffn1_up[0]), bf(ffn1_down[0]))]
    w_in = bf(ml_w_in)
    gate_pad = LANES - 2 * ML_HEADS
    w_gate = bf(jnp.pad(ml_w_in[0, :, 2 * HQK + 2 * HV:], ((0, 0), (0, gate_pad))))
    b_gate = jnp.pad(ml_b_gates[0], (0, gate_pad))[None, :]
    norm_w = ml_norm_w[0][None, :]
    w_out = bf(ml_w_out[0])
    w_kv = bf(kv_w)
    order = np.arange(QW).reshape(ATT_KV_HEADS, ATT_GROUP, ATT_HD).transpose(1, 0, 2).reshape(-1)
    w_q = bf(att_w_q[0][:, order])
    w_o = bf(att_w_o[0][order, :])
    sinks = att_sinks[0]
    lng = ln_g.reshape(DEPTH * 3, 1, D_MODEL)
    lnb = ln_b.reshape(DEPTH * 3, 1, D_MODEL)

    tab_p = _rope_tables(jnp.arange(sp, dtype=jnp.int32))
    tab_s = _rope_tables(jnp.tile(PAST_LEN + jnp.arange(ss, dtype=jnp.int32), bs))

    xp = x_prompt.reshape(bp * sp, D_MODEL)
    xp, *cast = _ffn(xp, mods, 0, 0, *ffn_w[0], lng, lnb, 0, ts, sp, cast=ffn_f32[1])
    ffn_w.append(tuple(cast))
    xp, state_p, m_p = _mlstm_prompt(xp.reshape(bp, sp, D_MODEL), mods, w_in, w_gate, b_gate,
                                     norm_w, w_out, lng, lnb, ts)
    xp, k_p, v_p, *cast = _ffn(xp.reshape(bp * sp, D_MODEL), mods, 0, 6, *ffn_w[1], lng, lnb, 2, ts, sp,
                               kv=(kv_mods, w_kv, tab_p), cast=ffn_f32[2])
    ffn_w.append(tuple(cast))
    k_p = k_p.reshape(bp, sp, KVW)
    v_p = v_p.reshape(bp, sp, KVW)
    xp, *cast = _ffn(xp, mods, 1, 0, *ffn_w[2], lng, lnb, 3, ts, sp, cast=ffn_f32[3])
    ffn_w.append(tuple(cast))
    xp = _attn_prompt(xp.reshape(bp, sp, D_MODEL), mods, w_q, tab_p, k_p, v_p, sinks, w_o, lng, lnb, ts)
    xp = _ffn(xp.reshape(bp * sp, D_MODEL), mods, 1, 6, *ffn_w[3], lng, lnb, 5, ts, sp)
    y_prompt = xp.reshape(bp, sp, D_MODEL)
    keep = min(WINDOW, sp)
    win_k_p = k_p[:, sp - keep:].reshape(bp, keep, ATT_KV_HEADS, ATT_HD)
    win_v_p = v_p[:, sp - keep:].reshape(bp, keep, ATT_KV_HEADS, ATT_HD)

    xs = x_sample.reshape(ts, D_MODEL)
    xs = _ffn(xs, mods, 0, 0, *ffn_w[0], lng, lnb, 0, ts, None)
    m0 = jnp.pad(state_mlstm_m[0], ((0, 0), (0, LANES - ML_HEADS)))
    xs, c_s, n_s, m_s = _mlstm_sample(xs, mods, w_in, w_gate, b_gate, norm_w, w_out, lng, lnb,
                                      state_mlstm_C[0], state_mlstm_n[0].reshape(bs, HQK), m0, ss)
    xs, k_s, v_s = _ffn(xs, mods, 0, 6, *ffn_w[1], lng, lnb, 2, ts, None, kv=(kv_mods, w_kv, tab_s))
    xs = _ffn(xs, mods, 1, 0, *ffn_w[2], lng, lnb, 3, ts, None)
    xs = _attn_sample(xs, mods, w_q, tab_s, k_s, v_s,
                      cache_win_k.reshape(bs, n_buf, KVW), cache_win_v.reshape(bs, n_buf, KVW),
                      sinks, w_o, lng, lnb, ss)
    xs = _ffn(xs, mods, 1, 6, *ffn_w[3], lng, lnb, 5, ts, None)
    y_sample = xs.reshape(bs, ss, D_MODEL)
    win_k_s = jnp.concatenate([cache_win_k, k_s.reshape(bs, ss, ATT_KV_HEADS, ATT_HD)], axis=1)[:, -n_buf:]
    win_v_s = jnp.concatenate([cache_win_v, v_s.reshape(bs, ss, ATT_KV_HEADS, ATT_HD)], axis=1)[:, -n_buf:]

    return (y_prompt, y_sample,
            state_p[None, ..., :ML_DV], state_p[None, ..., ML_DV], m_p[None, :, :ML_HEADS, 0],
            win_k_p, win_v_p,
            c_s[None], n_s.reshape(1, bs, ML_HEADS, ML_DK), m_s[None, :, :ML_HEADS],
            win_k_s, win_v_s)
```

```python
import functools

import numpy as np
import jax
import jax.numpy as jnp
from jax import lax
from jax.experimental import pallas as pl
from jax.experimental.pallas import tpu as pltpu

F32 = jnp.float32
BF16 = jnp.bfloat16

D_MODEL = 1024
DEPTH = 2
PAST_LEN = 8192
ML_HEADS = 4
ML_DK = D_MODEL // 8
ML_DV = D_MODEL // 4
ATT_Q_HEADS = 16
ATT_KV_HEADS = 4
ATT_GROUP = ATT_Q_HEADS // ATT_KV_HEADS
ATT_HD = 64
WINDOW = 128
ROT_DIM = ATT_HD // 4
ROPE_THETA = 500000.0
D_FF = 2816
ALPHA = (2 * DEPTH) ** 0.25
LN_EPS = 1e-5
N_MOD = 9
HQK = ML_HEADS * ML_DK
HV = ML_HEADS * ML_DV
KVW = ATT_KV_HEADS * ATT_HD
QW = ATT_Q_HEADS * ATT_HD

LANES = 128
SUBLANES = 8
VMEM_LIMIT_BYTES = 56 * 1024 * 1024

FFN_ROWS = 512
FFN_CHUNK = 256
ML_ROWS = 512
ML_CHUNK = 128
ATT_ROWS = 512
ADA_COLS = 1024
SAMPLE_GROUP = 8

NT_DIMS = (((1,), (1,)), ((), ()))
HIGHEST = lax.Precision.HIGHEST


def _params(semantics):
    return pltpu.CompilerParams(dimension_semantics=semantics, vmem_limit_bytes=VMEM_LIMIT_BYTES)


def _const_spec(block, index):
    return pl.BlockSpec(block, lambda *_: index, pipeline_mode=pl.Buffered(1))


def _resident(shape):
    return _const_spec(shape, (0,) * len(shape))


def _dot(a, b):
    return jnp.dot(a, b, preferred_element_type=F32)


def _dot_exact(a, b):
    return jnp.dot(a, b, preferred_element_type=F32, precision=HIGHEST)


def _nt(a, b):
    return lax.dot_general(a, b, NT_DIMS, preferred_element_type=F32)


def _iota(shape, dim):
    return lax.broadcasted_iota(jnp.int32, shape, dim)


def _silu(x):
    return x * jax.nn.sigmoid(x)


def _log_sigmoid(x):
    return jnp.minimum(x, 0.0) - jnp.log1p(jnp.exp(-jnp.abs(x)))


def _layer_norm(y, g, b):
    mu = jnp.mean(y, axis=-1, keepdims=True)
    d = y - mu
    var = jnp.mean(d * d, axis=-1, keepdims=True)
    return d * lax.rsqrt(var + LN_EPS) * g + b


def _mod(ref, batch):
    return ref[...] if batch is None else ref[pl.ds(batch, 1), :]


def _modulate_bf16(x, shift, scale):
    return (x * (1.0 + scale) + shift).astype(BF16)


def _rope(x, cos, sin_lo, sin_hi):
    width = x.shape[1]
    reps = width // LANES
    tile = lambda t: jnp.concatenate([t] * reps, axis=1) if reps > 1 else t
    x_up = pltpu.roll(x, width - ROT_DIM // 2, 1)
    x_dn = pltpu.roll(x, ROT_DIM // 2, 1)
    return x * tile(cos) + x_up * tile(sin_lo) + x_dn * tile(sin_hi)


def _mod_specs(layer, first, count, n_sample_rows, prompt):
    if prompt:
        blk = n_sample_rows // SUBLANES
        return [_const_spec((None, SUBLANES, D_MODEL), (layer, blk, first + j)) for j in range(count)]
    return [_const_spec((None, n_sample_rows, D_MODEL), (layer, 0, first + j)) for j in range(count)]


def _ln_specs(index):
    return [_const_spec((None, 1, D_MODEL), (index, 0, 0))] * 2


def _ada_kernel(c_ref, w_ref, b_ref, o_ref):
    ca = _silu(c_ref[...]).astype(BF16)
    o_ref[...] = _dot(ca, w_ref[...].astype(BF16)) + b_ref[...]


def _ada(c, w, b):
    n_l, k, n = w.shape
    m = c.shape[0]
    return pl.pallas_call(
        _ada_kernel,
        grid=(n_l, n // ADA_COLS),
        in_specs=[pl.BlockSpec((m, k), lambda l, j: (0, 0)),
                  pl.BlockSpec((None, k, ADA_COLS), lambda l, j: (l, 0, j)),
                  pl.BlockSpec((None, 1, ADA_COLS), lambda l, j: (l, 0, j))],
        out_specs=pl.BlockSpec((None, m, ADA_COLS), lambda l, j: (l, 0, j)),
        out_shape=jax.ShapeDtypeStruct((n_l, m, n), F32),
        compiler_params=_params(("arbitrary", "arbitrary")),
        name="ada",
    )(c, w, b)


def _ffn_kernel(x_ref, sh_ref, sc_ref, gt_ref, wa_ref, wu_ref, wd_ref, g_ref, b_ref, *rest,
                steps_per_seq, has_kv, has_cast):
    rest = list(rest)
    kv_in = [rest.pop(0) for _ in range(6)] if has_kv else None
    cast_in = [rest.pop(0) for _ in range(2)] if has_cast else None
    o_ref = rest.pop(0)
    kv_out = [rest.pop(0) for _ in range(2)] if has_kv else None
    cast_out = [rest.pop(0) for _ in range(2)] if has_cast else None
    h_ref, = rest

    batch = None if steps_per_seq is None else pl.program_id(0) // steps_per_seq
    x = x_ref[...]
    xm = _modulate_bf16(x, _mod(sh_ref, batch), _mod(sc_ref, batch))
    for c0 in range(0, D_FF, FFN_CHUNK):
        cw = min(FFN_CHUNK, D_FF - c0)
        a = _dot(xm, wa_ref[:, c0:c0 + cw])
        u = _dot(xm, wu_ref[:, c0:c0 + cw])
        h_ref[:, c0:c0 + cw] = (_silu(a) * u).astype(BF16)
    y = ALPHA * x + (0.5 * _mod(gt_ref, batch)) * _dot(h_ref[...], wd_ref[...])
    out = _layer_norm(y, g_ref[...], b_ref[...])
    o_ref[...] = out
    if has_kv:
        ksh_ref, ksc_ref, wkv_ref, cos_ref, slo_ref, shi_ref = kv_in
        kv = _dot(_modulate_bf16(out, _mod(ksh_ref, batch), _mod(ksc_ref, batch)), wkv_ref[...])
        kv_out[0][...] = _rope(kv[:, :KVW], cos_ref[...], slo_ref[...], shi_ref[...])
        kv_out[1][...] = kv[:, KVW:]
    if has_cast:
        for src, dst in zip(cast_in, cast_out):
            dst[...] = src[...].astype(BF16)


def _ffn(x, mods, layer, first, w_up, w_down, ln_g, ln_b, ln_index, n_sample_rows, seq, kv=None, cast=None):
    t = x.shape[0]
    rows = min(FFN_ROWS, t)
    steps = t // rows
    x_spec = pl.BlockSpec((rows, D_MODEL), lambda i: (i, 0))
    in_specs = ([x_spec] + _mod_specs(layer, first, 3, n_sample_rows, seq is not None)
                + [_const_spec((D_MODEL, D_FF), (0, 0)), _const_spec((D_MODEL, D_FF), (0, 1)),
                   _resident(w_down.shape)]
                + _ln_specs(ln_index))
    args = [x, mods, mods, mods, w_up, w_up, w_down, ln_g, ln_b]
    out_specs = [x_spec]
    out_shape = [jax.ShapeDtypeStruct((t, D_MODEL), F32)]
    if kv is not None:
        kv_mods, w_kv, tables = kv
        pos_blocks = tables[0].shape[0] // rows
        tab_spec = pl.BlockSpec((rows, LANES), lambda i: (i % pos_blocks, 0))
        kv_spec = pl.BlockSpec((rows, KVW), lambda i: (i, 0))
        in_specs += (_mod_specs(0, 0, 2, n_sample_rows, seq is not None)
                     + [_resident(w_kv.shape), tab_spec, tab_spec, tab_spec])
        args += [kv_mods, kv_mods, w_kv, *tables]
        out_specs += [kv_spec, kv_spec]
        out_shape += [jax.ShapeDtypeStruct((t, KVW), F32)] * 2
    if cast is not None:
        up_f32, down_f32, src_layer = cast
        up_rows = D_MODEL // steps
        down_rows = 2 * D_FF // steps
        assert up_rows % 16 == 0 and down_rows % 16 == 0 and steps % 2 == 0
        in_specs += [pl.BlockSpec((None, up_rows, 2 * D_FF), lambda i: (src_layer, i, 0)),
                     pl.BlockSpec((None, down_rows, D_MODEL), lambda i: (src_layer, i // 2, 0))]
        args += [up_f32, down_f32]
        out_specs += [pl.BlockSpec((up_rows, 2 * D_FF), lambda i: (i, 0)),
                      pl.BlockSpec((down_rows, D_MODEL), lambda i: (i // 2, 0))]
        out_shape += [jax.ShapeDtypeStruct((D_MODEL, 2 * D_FF), BF16),
                      jax.ShapeDtypeStruct((D_FF, D_MODEL), BF16)]
    outs = pl.pallas_call(
        functools.partial(_ffn_kernel, steps_per_seq=None if seq is None else seq // rows,
                          has_kv=kv is not None, has_cast=cast is not None),
        grid=(steps,),
        in_specs=in_specs,
        out_specs=out_specs,
        out_shape=out_shape,
        scratch_shapes=[pltpu.VMEM((rows, D_FF), BF16)],
        compiler_params=_params(("arbitrary",)),
        name="ffn" + ("_kv" if kv is not None else "") + ("_cast" if cast is not None else ""),
    )(*args)
    return outs[0] if len(outs) == 1 else outs


def _head_norm(hh, w_row):
    mu = jnp.mean(hh, axis=-1, keepdims=True)
    d = hh - mu
    var = jnp.mean(d * d, axis=-1, keepdims=True)
    return d * lax.rsqrt(var + LN_EPS) * w_row


def _split3(x, axis):
    hi = x.astype(BF16)
    r1 = x - hi.astype(F32)
    mid = r1.astype(BF16)
    lo = (r1 - mid.astype(F32)).astype(BF16)
    return jnp.concatenate([hi, mid, lo], axis=axis)


def _tile_lanes(t, reps):
    return jnp.concatenate([t] * reps, axis=1)


def _mlstm_prompt_kernel(x_ref, sh_ref, sc_ref, gt_ref, wp_ref, wg_ref, bg_ref, nw_ref, wo_ref, g_ref, b_ref,
                         o_ref, c_out, m_out, proj_ref, rep_ref, y_ref, c_ref, mm_ref):
    batch = pl.program_id(0)
    step = pl.program_id(1)
    size = ML_CHUNK
    assert size == LANES and ML_DK == LANES

    @pl.when(step == 0)
    def _():
        c_ref[...] = jnp.zeros_like(c_ref)
        mm_ref[...] = jnp.zeros_like(mm_ref)

    x = x_ref[...]
    rows = x.shape[0]
    n_chunks = rows // size
    h = _modulate_bf16(x, _mod(sh_ref, batch), _mod(sc_ref, batch))
    gates = _dot(h, wg_ref[...]) + bg_ref[...]

    r = _iota((size, size), 0)
    c = _iota((size, size), 1)
    causal = c <= r
    eye = c == r
    tri3 = _tile_lanes(jnp.where(causal, 1.0, 0.0).astype(BF16), 3)
    n_rep = 2 * ML_HEADS
    spread = jnp.where(_iota((LANES, n_rep * LANES), 1) // LANES == _iota((LANES, n_rep * LANES), 0),
                       1.0, 0.0).astype(BF16)
    spread3 = jnp.concatenate([spread] * 3, axis=0)
    ones3 = jnp.ones((3 * size, LANES), BF16)
    ones_v = jnp.ones((size, LANES), BF16)
    sr = _iota((2 * SUBLANES, LANES), 0)
    sl = _iota((2 * SUBLANES, LANES), 1)
    pick = (jnp.where((sr < ML_HEADS) & (sl == sr), 1.0, 0.0)
            - jnp.where((sr < ML_HEADS) & (sl == sr + ML_HEADS), 1.0, 0.0)
            + jnp.where((sr >= SUBLANES) & (sr < SUBLANES + ML_HEADS) & (sl == sr - SUBLANES + ML_HEADS),
                        1.0, 0.0))
    pick3 = _tile_lanes(pick.astype(BF16), 3)

    log_f = _log_sigmoid(gates)
    bwide = jnp.concatenate([_dot(tri3, _split3(log_f[ci * size:(ci + 1) * size, :], 0))
                             for ci in range(n_chunks)], axis=0)
    src3 = _split3(jnp.where(_iota((rows, LANES), 1) < ML_HEADS, gates, bwide), 1)
    rep_ref[...] = _dot(src3, spread3)
    ab_rows = _nt(pick3, src3)
    a_rows = ab_rows[:SUBLANES, :]
    b_rows = ab_rows[SUBLANES:, :]
    lane_in_chunk = _iota((SUBLANES, rows), 1) % size
    run = a_rows
    shift = 1
    while shift < size:
        run = jnp.maximum(run, jnp.where(lane_in_chunk >= shift, pltpu.roll(run, shift, 1), -jnp.inf))
        shift *= 2
    m_rows = [mm_ref[...]]
    g_rows = []
    for ci in range(n_chunks):
        run_c = run[:, ci * size:(ci + 1) * size]
        g_rows.append(jnp.maximum(run_c, m_rows[-1]))
        end = (ci + 1) * size - 1
        m_end = b_rows[:, end:end + 1] + jnp.maximum(run[:, end:end + 1], m_rows[-1][:, :1])
        m_rows.append(jnp.broadcast_to(m_end, (SUBLANES, LANES)))
    mm_ref[...] = m_rows[-1]

    proj_ref[...] = _dot(h, wp_ref[...])

    states = [c_ref[hd] for hd in range(ML_HEADS)]
    for ci in range(n_chunks):
        lo, hi = ci * size, (ci + 1) * size
        for hd in range(ML_HEADS):
            i_rep = rep_ref[lo:hi, hd * LANES:(hd + 1) * LANES]
            b_rep = rep_ref[lo:hi, (ML_HEADS + hd) * LANES:(ML_HEADS + hd + 1) * LANES]
            m_prev = m_rows[ci][hd:hd + 1, :]
            m_new = m_rows[ci + 1][hd:hd + 1, :]
            a_row = a_rows[hd:hd + 1, lo:hi]
            g_rep = _dot(_split3(jnp.where(eye, g_rows[ci][hd:hd + 1, :], 0.0), 1), ones3)
            w = jnp.where(causal, jnp.exp(a_row - g_rep), 0.0)
            w_init = jnp.exp(m_prev - g_rep)
            floor = jnp.exp(-(b_rep + g_rep))
            b_last = b_rep[size - 1:size, :]
            w_end = jnp.exp(b_last - b_rep + i_rep - m_new)
            decay = jnp.exp(b_last + m_prev - m_new)

            q = proj_ref[lo:hi, hd * ML_DK:(hd + 1) * ML_DK]
            k = proj_ref[lo:hi, HQK + hd * ML_DK:HQK + (hd + 1) * ML_DK] * (ML_DK ** -0.5)
            v = proj_ref[lo:hi, 2 * HQK + hd * ML_DV:2 * HQK + (hd + 1) * ML_DV]
            qb = q.astype(BF16)
            v_ext = jnp.concatenate([v.astype(BF16), ones_v], axis=1)
            s = _nt(qb, k.astype(BF16)) * w
            num = _dot(s.astype(BF16), v_ext) + _tile_lanes(w_init, 3) * _dot(qb, states[hd].astype(BF16))
            den = jnp.maximum(jnp.abs(num[:, ML_DV:]), floor)
            hh = num[:, :ML_DV] / _tile_lanes(den, ML_DV // LANES)
            states[hd] = _tile_lanes(decay, 3) * states[hd] + _dot((k * w_end).T.astype(BF16), v_ext)
            y_ref[lo:hi, hd * ML_DV:(hd + 1) * ML_DV] = _head_norm(hh, nw_ref[:, hd * ML_DV:(hd + 1) * ML_DV])
    for hd in range(ML_HEADS):
        c_ref[hd] = states[hd]

    o_gate = jax.nn.sigmoid(proj_ref[:, 2 * HQK + HV:2 * HQK + 2 * HV])
    y = _dot((y_ref[...] * o_gate).astype(BF16), wo_ref[...])
    o_ref[...] = _layer_norm(ALPHA * x + _mod(gt_ref, batch) * y, g_ref[...], b_ref[...])

    @pl.when(step == pl.num_programs(1) - 1)
    def _():
        c_out[...] = c_ref[...]
        m_out[...] = mm_ref[...]


def _mlstm_weight_specs(wg, bg, nw, wo):
    return [_const_spec((None, D_MODEL, 2 * HQK + 2 * HV), (0, 0, 0)), _resident(wg.shape),
            _resident(bg.shape), _resident(nw.shape), _resident(wo.shape)] + _ln_specs(1)


def _mlstm_prompt(x, mods, w_in, wg, bg, nw, wo, ln_g, ln_b, n_sample_rows):
    bsz, seq, _ = x.shape
    rows = min(ML_ROWS, seq)
    x_spec = pl.BlockSpec((None, rows, D_MODEL), lambda bi, i: (bi, i, 0))
    state_shape = (ML_HEADS, ML_DK, ML_DV + LANES)
    return pl.pallas_call(
        _mlstm_prompt_kernel,
        grid=(bsz, seq // rows),
        in_specs=[x_spec] + _mod_specs(0, 3, 3, n_sample_rows, True) + _mlstm_weight_specs(wg, bg, nw, wo),
        out_specs=[x_spec, pl.BlockSpec((None,) + state_shape, lambda bi, i: (bi, 0, 0, 0)),
                   pl.BlockSpec((None, SUBLANES, LANES), lambda bi, i: (bi, 0, 0))],
        out_shape=[jax.ShapeDtypeStruct((bsz, seq, D_MODEL), F32),
                   jax.ShapeDtypeStruct((bsz,) + state_shape, F32),
                   jax.ShapeDtypeStruct((bsz, SUBLANES, LANES), F32)],
        scratch_shapes=[pltpu.VMEM((rows, 2 * HQK + 2 * HV), F32),
                        pltpu.VMEM((rows, 2 * ML_HEADS * LANES), F32),
                        pltpu.VMEM((rows, HV), F32),
                        pltpu.VMEM(state_shape, F32),
                        pltpu.VMEM((SUBLANES, LANES), F32)],
        compiler_params=_params(("arbitrary", "arbitrary")),
        name="mlstm_prompt",
    )(x, mods, mods, mods, w_in, wg, bg, nw, wo, ln_g, ln_b)


def _mlstm_sample_kernel(x_ref, sh_ref, sc_ref, gt_ref, wp_ref, wg_ref, bg_ref, nw_ref, wo_ref, g_ref, b_ref,
                         c0_ref, n0_ref, m0_ref, o_ref, c_out, n_out, m_out,
                         proj_ref, gate_ref, y_ref, *, seq):
    step = pl.program_id(0)
    grp = c0_ref.shape[0]
    rows = grp * seq

    @pl.when(step == 0)
    def _():
        h = _modulate_bf16(x_ref[...], sh_ref[...], sc_ref[...])
        proj_ref[...] = _dot(h, wp_ref[...])
        gate_ref[...] = _dot(h, wg_ref[...]) + bg_ref[...]

    r = _iota((rows, rows), 0)
    c = _iota((rows, rows), 1)
    same = (r // seq) == (c // seq)
    causal = same & (c <= r)
    eye = r == c
    last_of_row = c == (r // seq) * seq + (seq - 1)
    same_f = same.astype(F32)
    expand = (_iota((rows, grp), 0) // seq == _iota((rows, grp), 1)).astype(F32)
    pick_last = (_iota((grp, rows), 1) == _iota((grp, rows), 0) * seq + (seq - 1)).astype(F32)
    ident = (_iota((ML_DK, ML_DK), 0) == _iota((ML_DK, ML_DK), 1)).astype(F32)
    seq_of_row = _iota((rows, ML_DV), 0) // seq
    col_seq = _iota((ML_DK, rows), 1) // seq
    lane = _iota((rows, LANES), 1)
    to_row = lambda col: jnp.sum(jnp.where(eye, col, 0.0), axis=0, keepdims=True)

    r0 = pl.multiple_of(step * rows, rows)
    gates = gate_ref[pl.ds(r0, rows), :]
    log_f = _log_sigmoid(gates)
    bwide = _dot_exact(causal.astype(F32), log_f)
    bsum = _dot_exact(same_f, log_f)
    m_prev_wide = _dot_exact(expand, m0_ref[...])
    n_rows_wide = _dot_exact(expand, n0_ref[...])
    n_pieces = []
    m_wide = jnp.zeros((rows, LANES), F32)
    for hd in range(ML_HEADS):
        q = proj_ref[pl.ds(r0, rows), hd * ML_DK:(hd + 1) * ML_DK]
        k = proj_ref[pl.ds(r0, rows), HQK + hd * ML_DK:HQK + (hd + 1) * ML_DK] * (ML_DK ** -0.5)
        v = proj_ref[pl.ds(r0, rows), 2 * HQK + hd * ML_DV:2 * HQK + (hd + 1) * ML_DV]
        bcol = bwide[:, ML_HEADS + hd:ML_HEADS + hd + 1]
        b_last = bsum[:, ML_HEADS + hd:ML_HEADS + hd + 1]
        icol = gates[:, hd:hd + 1]
        m_prev = m_prev_wide[:, hd:hd + 1]
        n_rows = n_rows_wide[:, hd * ML_DK:(hd + 1) * ML_DK]
        log_w = jnp.where(causal, bcol - to_row(bcol) + to_row(icol), -jnp.inf)
        log_init = bcol + m_prev
        m_t = jnp.maximum(log_init, jnp.max(log_w, axis=-1, keepdims=True))
        w = jnp.exp(log_w - m_t)
        w_init = jnp.exp(log_init - m_t)
        qb = q.astype(BF16)
        s = _nt(qb, k.astype(BF16)) * w
        inter = None
        for bi in range(grp):
            cand = _dot(qb, c0_ref[bi, hd].astype(BF16))
            inter = cand if inter is None else jnp.where(seq_of_row == bi, cand, inter)
        num = _dot(s.astype(BF16), v.astype(BF16)) + w_init * inter
        den = jnp.sum(s, axis=-1, keepdims=True) + w_init * jnp.sum(q * n_rows, axis=-1, keepdims=True)
        hh = num / jnp.maximum(jnp.abs(den), jnp.exp(-m_t))
        m_new = jnp.sum(jnp.where(last_of_row, to_row(m_t), 0.0), axis=1, keepdims=True)
        w_end = jnp.exp(b_last - bcol + icol - m_new)
        decay = jnp.exp(b_last + m_prev - m_new)
        kw = k * w_end
        kw_t = lax.dot_general(ident, kw, NT_DIMS, preferred_element_type=F32, precision=HIGHEST)
        lhs = jnp.concatenate([jnp.where(col_seq == bi, kw_t, 0.0) for bi in range(grp)], axis=0)
        upd = _dot(lhs.astype(BF16), v.astype(BF16))
        for bi in range(grp):
            c_out[bi, hd] = (decay[bi * seq:bi * seq + 1, :] * c0_ref[bi, hd]
                             + upd[bi * ML_DK:(bi + 1) * ML_DK, :])
        n_pieces.append(decay * n_rows + _dot_exact(same_f, kw))
        m_wide = jnp.where(lane == hd, m_new, m_wide)
        y_ref[pl.ds(r0, rows), hd * ML_DV:(hd + 1) * ML_DV] = _head_norm(
            hh, nw_ref[:, hd * ML_DV:(hd + 1) * ML_DV])
    n_out[...] = _dot_exact(pick_last, jnp.concatenate(n_pieces, axis=1))
    m_out[...] = _dot_exact(pick_last, m_wide)

    @pl.when(step == pl.num_programs(0) - 1)
    def _():
        o_gate = jax.nn.sigmoid(proj_ref[:, 2 * HQK + HV:2 * HQK + 2 * HV])
        y = _dot((y_ref[...] * o_gate).astype(BF16), wo_ref[...])
        o_ref[...] = _layer_norm(ALPHA * x_ref[...] + gt_ref[...] * y, g_ref[...], b_ref[...])


def _mlstm_sample(x, mods, w_in, wg, bg, nw, wo, ln_g, ln_b, c0, n0, m0, seq):
    t = x.shape[0]
    grp = SAMPLE_GROUP
    state_specs = [pl.BlockSpec((grp, ML_HEADS, ML_DK, ML_DV), lambda i: (i, 0, 0, 0)),
                   pl.BlockSpec((grp, HQK), lambda i: (i, 0)),
                   pl.BlockSpec((grp, LANES), lambda i: (i, 0))]
    return pl.pallas_call(
        functools.partial(_mlstm_sample_kernel, seq=seq),
        grid=(t // (grp * seq),),
        in_specs=[_resident(x.shape)] + _mod_specs(0, 3, 3, t, False)
                 + _mlstm_weight_specs(wg, bg, nw, wo) + state_specs,
        out_specs=[pl.BlockSpec((t, D_MODEL), lambda i: (0, 0))] + state_specs,
        out_shape=[jax.ShapeDtypeStruct((t, D_MODEL), F32),
                   jax.ShapeDtypeStruct(c0.shape, F32),
                   jax.ShapeDtypeStruct(n0.shape, F32),
                   jax.ShapeDtypeStruct(m0.shape, F32)],
        scratch_shapes=[pltpu.VMEM((t, 2 * HQK + 2 * HV), F32),
                        pltpu.VMEM((t, LANES), F32),
                        pltpu.VMEM((t, HV), F32)],
        compiler_params=_params(("arbitrary",)),
        name="mlstm_sample",
    )(x, mods, mods, mods, w_in, wg, bg, nw, wo, ln_g, ln_b, c0, n0, m0)


def _sink_column(sink_ref, kh, rows_per_head):
    rid = _iota((ATT_GROUP * rows_per_head, 1), 0)
    col = jnp.full((ATT_GROUP * rows_per_head, 1), sink_ref[kh * ATT_GROUP], F32)
    for gq in range(1, ATT_GROUP):
        col = jnp.where(rid >= gq * rows_per_head, sink_ref[kh * ATT_GROUP + gq], col)
    return col


def _head_lanes(x, kh):
    return jnp.where(_iota(x.shape, 1) // ATT_HD == kh, x, 0.0).astype(BF16)


def _stack_groups(x):
    return jnp.concatenate([x[:, g * KVW:(g + 1) * KVW] for g in range(ATT_GROUP)], axis=0)


def _attn_prompt_kernel(sink_ref, x_ref, sh_ref, sc_ref, gt_ref, wq_ref, cos_ref, slo_ref, shi_ref,
                        kp_ref, kc_ref, vp_ref, vc_ref, wo_ref, g_ref, b_ref, o_ref, a_ref):
    batch = pl.program_id(0)
    step = pl.program_id(1)
    x = x_ref[...]
    rows = x.shape[0]
    n_blk = rows // WINDOW
    q = _rope(_dot(_modulate_bf16(x, _mod(sh_ref, batch), _mod(sc_ref, batch)), wq_ref[...]),
              cos_ref[...], slo_ref[...], shi_ref[...])
    q = (q * (ATT_HD ** -0.5)).astype(BF16)

    keys = jnp.concatenate([kp_ref[...], kc_ref[...]], axis=0)
    vals = jnp.concatenate([vp_ref[...], vc_ref[...]], axis=0)
    k_heads = [_head_lanes(keys, kh) for kh in range(ATT_KV_HEADS)]
    v_heads = [_head_lanes(vals, kh) for kh in range(ATT_KV_HEADS)]

    rid = _iota((ATT_GROUP * WINDOW, WINDOW), 0) % WINDOW
    cid = _iota((ATT_GROUP * WINDOW, WINDOW), 1)
    from_prev = cid > rid
    no_prev = jnp.where(step > 0, 0.0, -jnp.inf)
    sinks = [_sink_column(sink_ref, kh, WINDOW) for kh in range(ATT_KV_HEADS)]

    scores = [[_nt(_stack_groups(q[n * WINDOW:(n + 1) * WINDOW, :]),
                   k_heads[kh][n * WINDOW:(n + 2) * WINDOW, :])
               for kh in range(ATT_KV_HEADS)] for n in range(n_blk)]
    probs = []
    for n in range(n_blk):
        row = []
        for kh in range(ATT_KV_HEADS):
            s_prev = scores[n][kh][:, :WINDOW]
            if n == 0:
                s_prev = s_prev + no_prev
            s = jnp.where(from_prev, s_prev, scores[n][kh][:, WINDOW:])
            mx = jnp.maximum(jnp.max(s, axis=-1, keepdims=True), sinks[kh])
            p = jnp.exp(s - mx)
            den = jnp.sum(p, axis=-1, keepdims=True) + jnp.exp(sinks[kh] - mx)
            pn = p / den
            row.append(jnp.where(from_prev, pn, 0.0).astype(BF16))
            row.append(jnp.where(from_prev, 0.0, pn).astype(BF16))
        probs.append(jnp.concatenate(row, axis=1))
    for n in range(n_blk):
        v_all = jnp.concatenate([v_heads[kh][n * WINDOW:(n + 2) * WINDOW, :]
                                 for kh in range(ATT_KV_HEADS)], axis=0)
        o = _dot(probs[n], v_all)
        for g in range(ATT_GROUP):
            a_ref[n * WINDOW:(n + 1) * WINDOW, g * KVW:(g + 1) * KVW] = o[g * WINDOW:(g + 1) * WINDOW, :]

    y = _dot(a_ref[...].astype(BF16), wo_ref[...])
    o_ref[...] = _layer_norm(ALPHA * x + _mod(gt_ref, batch) * y, g_ref[...], b_ref[...])


def _attn_prompt(x, mods, wq, tables, k, v, sinks, wo, ln_g, ln_b, n_sample_rows):
    bsz, seq, _ = x.shape
    rows = min(ATT_ROWS, seq)
    per = rows // WINDOW
    x_spec = pl.BlockSpec((None, rows, D_MODEL), lambda bi, i: (bi, i, 0))
    cur = pl.BlockSpec((None, rows, KVW), lambda bi, i: (bi, i, 0))
    prev = pl.BlockSpec((None, WINDOW, KVW), lambda bi, i: (bi, jnp.maximum(i * per - 1, 0), 0))
    tab = pl.BlockSpec((rows, LANES), lambda bi, i: (i, 0))
    return pl.pallas_call(
        _attn_prompt_kernel,
        grid=(bsz, seq // rows),
        in_specs=[pl.BlockSpec(memory_space=pltpu.SMEM), x_spec]
                 + _mod_specs(1, 3, 3, n_sample_rows, True)
                 + [_resident(wq.shape), tab, tab, tab, prev, cur, prev, cur, _resident(wo.shape)]
                 + _ln_specs(4),
        out_specs=x_spec,
        out_shape=jax.ShapeDtypeStruct((bsz, seq, D_MODEL), F32),
        scratch_shapes=[pltpu.VMEM((rows, QW), F32)],
        compiler_params=_params(("arbitrary", "arbitrary")),
        name="attn_prompt",
    )(sinks, x, mods, mods, mods, wq, *tables, k, k, v, v, wo, ln_g, ln_b)


def _attn_sample_kernel(sink_ref, x_ref, sh_ref, sc_ref, gt_ref, wq_ref, cos_ref, slo_ref, shi_ref,
                        kn_ref, vn_ref, kb_ref, vb_ref, wo_ref, g_ref, b_ref, o_ref,
                        q_ref, a_ref, *, seq):
    step = pl.program_id(0)
    pair = 2 * seq
    assert pair == SUBLANES
    grp, n_buf, _ = kb_ref.shape

    @pl.when(step == 0)
    def _():
        q = _rope(_dot(_modulate_bf16(x_ref[...], sh_ref[...], sc_ref[...]), wq_ref[...]),
                  cos_ref[...], slo_ref[...], shi_ref[...])
        q_ref[...] = q * (ATT_HD ** -0.5)

    n_rows = ATT_GROUP * pair
    n_all = ATT_KV_HEADS * n_rows
    n_new = 2 * pair
    q_pos = _iota((n_all, n_buf), 0) % seq + PAST_LEN
    k_pos = _iota((n_all, n_buf), 1) + (PAST_LEN - n_buf)
    buf_mask = (k_pos <= q_pos) & (k_pos > q_pos - WINDOW) & (k_pos >= 0)
    q_tok = _iota((n_all, n_new), 0) % seq
    new_col = _iota((n_all, n_new), 1)
    out_lane_head = _iota((n_rows, KVW), 1) // ATT_HD
    low = _iota((n_rows, KVW), 0) % pair < seq
    head_of_row = _iota((n_all, 1), 0) // pair
    sink_col = jnp.full((n_all, 1), sink_ref[0], F32)
    for hq in range(1, ATT_Q_HEADS):
        sink_col = jnp.where(head_of_row >= hq, sink_ref[hq], sink_col)
    pad_rows = jnp.zeros((n_new - pair, KVW), BF16)
    for p in range(grp // 2):
        r0 = pl.multiple_of(step * (grp * seq) + p * pair, pair)
        q_all = _stack_groups(q_ref[pl.ds(r0, pair), :].astype(BF16))
        q_bd = jnp.concatenate([_head_lanes(q_all, kh) for kh in range(ATT_KV_HEADS)], axis=0)
        k_new = jnp.concatenate([kn_ref[pl.ds(r0, pair), :].astype(BF16), pad_rows], axis=0)
        v_new = jnp.concatenate([vn_ref[pl.ds(r0, pair), :].astype(BF16), pad_rows], axis=0)
        s_new_all = _nt(q_bd, k_new)
        halves = []
        for half in range(2):
            bi = 2 * p + half
            tok = new_col - half * seq
            own_new = (tok >= 0) & (tok < seq) & (tok <= q_tok)
            s_buf = jnp.where(buf_mask, _nt(q_bd, kb_ref[bi].astype(BF16)), -jnp.inf)
            s_new = jnp.where(own_new, s_new_all, -jnp.inf)
            mx = jnp.maximum(jnp.maximum(jnp.max(s_buf, axis=-1, keepdims=True),
                                         jnp.max(s_new, axis=-1, keepdims=True)), sink_col)
            e_buf = jnp.exp(s_buf - mx)
            e_new = jnp.exp(s_new - mx)
            den = (jnp.sum(e_buf, axis=-1, keepdims=True) + jnp.sum(e_new, axis=-1, keepdims=True)
                   + jnp.exp(sink_col - mx))
            o_all = (_dot((e_buf / den).astype(BF16), vb_ref[bi].astype(BF16))
                     + _dot((e_new / den).astype(BF16), v_new))
            o_half = None
            for kh in range(ATT_KV_HEADS):
                part = jnp.where(out_lane_head == kh, o_all[kh * n_rows:(kh + 1) * n_rows, :], 0.0)
                o_half = part if o_half is None else o_half + part
            halves.append(o_half)
        o = jnp.where(low, halves[0], halves[1])
        for g in range(ATT_GROUP):
            a_ref[pl.ds(r0, pair), g * KVW:(g + 1) * KVW] = o[g * pair:(g + 1) * pair, :]

    @pl.when(step == pl.num_programs(0) - 1)
    def _():
        y = _dot(a_ref[...].astype(BF16), wo_ref[...])
        o_ref[...] = _layer_norm(ALPHA * x_ref[...] + gt_ref[...] * y, g_ref[...], b_ref[...])


def _attn_sample(x, mods, wq, tables, k_new, v_new, k_buf, v_buf, sinks, wo, ln_g, ln_b, seq):
    t = x.shape[0]
    grp = SAMPLE_GROUP
    buf_spec = pl.BlockSpec((grp,) + k_buf.shape[1:], lambda i: (i, 0, 0))
    return pl.pallas_call(
        functools.partial(_attn_sample_kernel, seq=seq),
        grid=(t // (grp * seq),),
        in_specs=[pl.BlockSpec(memory_space=pltpu.SMEM), _resident(x.shape)]
                 + _mod_specs(1, 3, 3, t, False)
                 + [_resident(wq.shape)] + [_resident(tab.shape) for tab in tables]
                 + [_resident(k_new.shape), _resident(v_new.shape), buf_spec, buf_spec, _resident(wo.shape)]
                 + _ln_specs(4),
        out_specs=pl.BlockSpec((t, D_MODEL), lambda i: (0, 0)),
        out_shape=jax.ShapeDtypeStruct((t, D_MODEL), F32),
        scratch_shapes=[pltpu.VMEM((t, QW), F32), pltpu.VMEM((t, QW), F32)],
        compiler_params=_params(("arbitrary",)),
        name="attn_sample",
    )(sinks, x, mods, mods, mods, wq, *tables, k_new, v_new, k_buf, v_buf, wo, ln_g, ln_b)


def _rope_tables(pos):
    half = ROT_DIM // 2
    inv_freq = ROPE_THETA ** (-jnp.arange(half, dtype=F32) * 2.0 / ROT_DIM)
    ang = pos.astype(F32)[:, None] * inv_freq[None, :]
    cos, sin = jnp.cos(ang), jnp.sin(ang)
    dim = np.arange(LANES) % ATT_HD
    freq = np.arange(half)[:, None]
    lo = (dim[None, :] == freq).astype(np.float32)
    hi = (dim[None, :] == freq + half).astype(np.float32)
    spread = lambda a, m: jnp.dot(a, jnp.asarray(m), precision=HIGHEST)
    return (spread(cos, lo + hi) + jnp.asarray((dim >= ROT_DIM).astype(np.float32)),
            spread(sin, -lo), spread(sin, hi))


def kernel(x_prompt, x_sample, c_prompt, c_sample, state_mlstm_C, state_mlstm_n, state_mlstm_m,
           cache_win_k, cache_win_v, ada_w, ada_b, ffn1_up, ffn1_down, ffn2_up, ffn2_down, ln_g, ln_b,
           ml_w_in, ml_b_gates, ml_norm_w, ml_w_out, kv_ada_w, kv_ada_b, kv_w,
           att_w_q, att_sinks, att_w_o):
    bp, sp, _ = x_prompt.shape
    bs, ss, _ = x_sample.shape
    n_buf = cache_win_k.shape[1]
    ts = bs * ss

    c_all = jnp.concatenate([jnp.repeat(c_sample, ss, axis=0), c_prompt,
                             jnp.zeros((-bp % SUBLANES, D_MODEL), F32)], axis=0)
    mods = _ada(c_all, ada_w, ada_b[:, None, :])
    kv_mods = _ada(c_all, kv_ada_w[None], kv_ada_b[None, None, :])

    bf = lambda w: w.astype(BF16)
    ffn_f32 = [(ffn1_up, ffn1_down, 0), (ffn2_up, ffn2_down, 0), (ffn1_up, ffn1_down, 1), (ffn2_up, ffn2_down, 1)]
    ffn_w = [(bf(ffn1_up[0]), bf(ffn1_down[0]))]
    w_in = bf(ml_w_in)
    gate_pad = LANES - 2 * ML_HEADS
    w_gate = bf(jnp.pad(ml_w_in[0, :, 2 * HQK + 2 * HV:], ((0, 0), (0, gate_pad))))
    b_gate = jnp.pad(ml_b_gates[0], (0, gate_pad))[None, :]
    norm_w = ml_norm_w[0][None, :]
    w_out = bf(ml_w_out[0])
    w_kv = bf(kv_w)
    w_q = bf(att_w_q[0].reshape(D_MODEL, ATT_KV_HEADS, ATT_GROUP, ATT_HD)
             .transpose(0, 2, 1, 3).reshape(D_MODEL, QW))
    w_o = bf(att_w_o[0].reshape(ATT_KV_HEADS, ATT_GROUP, ATT_HD, D_MODEL)
             .transpose(1, 0, 2, 3).reshape(QW, D_MODEL))
    sinks = att_sinks[0]
    lng = ln_g.reshape(DEPTH * 3, 1, D_MODEL)
    lnb = ln_b.reshape(DEPTH * 3, 1, D_MODEL)

    tab_p = _rope_tables(jnp.arange(sp, dtype=jnp.int32))
    tab_s = _rope_tables(jnp.tile(PAST_LEN + jnp.arange(ss, dtype=jnp.int32), bs))

    xp = x_prompt.reshape(bp * sp, D_MODEL)
    xp, *cast = _ffn(xp, mods, 0, 0, *ffn_w[0], lng, lnb, 0, ts, sp, cast=ffn_f32[1])
    ffn_w.append(tuple(cast))
    xp, state_p, m_p = _mlstm_prompt(xp.reshape(bp, sp, D_MODEL), mods, w_in, w_gate, b_gate,
                                     norm_w, w_out, lng, lnb, ts)
    xp, k_p, v_p, *cast = _ffn(xp.reshape(bp * sp, D_MODEL), mods, 0, 6, *ffn_w[1], lng, lnb, 2, ts, sp,
                               kv=(kv_mods, w_kv, tab_p), cast=ffn_f32[2])
    ffn_w.append(tuple(cast))
    k_p = k_p.reshape(bp, sp, KVW)
    v_p = v_p.reshape(bp, sp, KVW)
    xp, *cast = _ffn(xp, mods, 1, 0, *ffn_w[2], lng, lnb, 3, ts, sp, cast=ffn_f32[3])
    ffn_w.append(tuple(cast))
    xp = _attn_prompt(xp.reshape(bp, sp, D_MODEL), mods, w_q, tab_p, k_p, v_p, sinks, w_o, lng, lnb, ts)
    xp = _ffn(xp.reshape(bp * sp, D_MODEL), mods, 1, 6, *ffn_w[3], lng, lnb, 5, ts, sp)
    y_prompt = xp.reshape(bp, sp, D_MODEL)
    keep = min(WINDOW, sp)
    win_k_p = k_p[:, sp - keep:].reshape(bp, keep, ATT_KV_HEADS, ATT_HD)
    win_v_p = v_p[:, sp - keep:].reshape(bp, keep, ATT_KV_HEADS, ATT_HD)

    xs = x_sample.reshape(ts, D_MODEL)
    xs = _ffn(xs, mods, 0, 0, *ffn_w[0], lng, lnb, 0, ts, None)
    m0 = jnp.pad(state_mlstm_m[0], ((0, 0), (0, LANES - ML_HEADS)))
    xs, c_s, n_s, m_s = _mlstm_sample(xs, mods, w_in, w_gate, b_gate, norm_w, w_out, lng, lnb,
                                      state_mlstm_C[0], state_mlstm_n[0].reshape(bs, HQK), m0, ss)
    xs, k_s, v_s = _ffn(xs, mods, 0, 6, *ffn_w[1], lng, lnb, 2, ts, None, kv=(kv_mods, w_kv, tab_s))
    xs = _ffn(xs, mods, 1, 0, *ffn_w[2], lng, lnb, 3, ts, None)
    xs = _attn_sample(xs, mods, w_q, tab_s, k_s, v_s,
                      cache_win_k.reshape(bs, n_buf, KVW), cache_win_v.reshape(bs, n_buf, KVW),
                      sinks, w_o, lng, lnb, ss)
    xs = _ffn(xs, mods, 1, 6, *ffn_w[3], lng, lnb, 5, ts, None)
    y_sample = xs.reshape(bs, ss, D_MODEL)
    win_k_s = jnp.concatenate([cache_win_k, k_s.reshape(bs, ss, ATT_KV_HEADS, ATT_HD)], axis=1)[:, -n_buf:]
    win_v_s = jnp.concatenate([cache_win_v, v_s.reshape(bs, ss, ATT_KV_HEADS, ATT_HD)], axis=1)[:, -n_buf:]

    return (y_prompt, y_sample,
            state_p[None, ..., :ML_DV], state_p[None, ..., ML_DV], m_p[None, :, :ML_HEADS, 0],
            win_k_p, win_v_p,
            c_s[None], n_s.reshape(1, bs, ML_HEADS, ML_DK), m_s[None, :, :ML_HEADS],
            win_k_s, win_v_s)
```

```python
import functools

import numpy as np
import jax
import jax.numpy as jnp
from jax import lax
from jax.experimental import pallas as pl
from jax.experimental.pallas import tpu as pltpu

F32 = jnp.float32
BF16 = jnp.bfloat16

D_MODEL = 1024
DEPTH = 2
PAST_LEN = 8192
ML_HEADS = 4
ML_DK = D_MODEL // 8
ML_DV = D_MODEL // 4
ATT_Q_HEADS = 16
ATT_KV_HEADS = 4
ATT_GROUP = ATT_Q_HEADS // ATT_KV_HEADS
ATT_HD = 64
WINDOW = 128
ROT_DIM = ATT_HD // 4
ROPE_THETA = 500000.0
D_FF = 2816
ALPHA = (2 * DEPTH) ** 0.25
LN_EPS = 1e-5
N_MOD = 9
HQK = ML_HEADS * ML_DK
HV = ML_HEADS * ML_DV
KVW = ATT_KV_HEADS * ATT_HD
QW = ATT_Q_HEADS * ATT_HD

LANES = 128
SUBLANES = 8
VMEM_LIMIT_BYTES = 56 * 1024 * 1024

FFN_ROWS = 1024
FFN_ROWS_KV = 512
FFN_CHUNK = 256
ML_ROWS = 1024
ML_CHUNK = 128
ATT_ROWS = 1024
ADA_COLS = 1024
SAMPLE_GROUP = 8

NT_DIMS = (((1,), (1,)), ((), ()))
HIGHEST = lax.Precision.HIGHEST


def _params(semantics):
    return pltpu.CompilerParams(dimension_semantics=semantics, vmem_limit_bytes=VMEM_LIMIT_BYTES)


def _const_spec(block, index):
    return pl.BlockSpec(block, lambda *_: index, pipeline_mode=pl.Buffered(1))


def _resident(shape):
    return _const_spec(shape, (0,) * len(shape))


def _dot(a, b):
    return jnp.dot(a, b, preferred_element_type=F32)


def _dot_exact(a, b):
    return jnp.dot(a, b, preferred_element_type=F32, precision=HIGHEST)


def _nt(a, b):
    return lax.dot_general(a, b, NT_DIMS, preferred_element_type=F32)


def _iota(shape, dim):
    return lax.broadcasted_iota(jnp.int32, shape, dim)


def _silu(x):
    return x * jax.nn.sigmoid(x)


def _log_sigmoid(x):
    return jnp.minimum(x, 0.0) - jnp.log1p(jnp.exp(-jnp.abs(x)))


def _layer_norm(y, g, b):
    mu = jnp.mean(y, axis=-1, keepdims=True)
    d = y - mu
    var = jnp.mean(d * d, axis=-1, keepdims=True)
    return d * lax.rsqrt(var + LN_EPS) * g + b


def _mod(ref, batch):
    return ref[...] if batch is None else ref[pl.ds(batch, 1), :]


def _modulate_bf16(x, shift, scale):
    return (x * (1.0 + scale) + shift).astype(BF16)


def _rope(x, cos, sin_lo, sin_hi):
    width = x.shape[1]
    reps = width // LANES
    tile = lambda t: jnp.concatenate([t] * reps, axis=1) if reps > 1 else t
    x_up = pltpu.roll(x, width - ROT_DIM // 2, 1)
    x_dn = pltpu.roll(x, ROT_DIM // 2, 1)
    return x * tile(cos) + x_up * tile(sin_lo) + x_dn * tile(sin_hi)


def _mod_specs(layer, first, count, n_sample_rows, prompt):
    if prompt:
        blk = n_sample_rows // SUBLANES
        return [_const_spec((None, SUBLANES, D_MODEL), (layer, blk, first + j)) for j in range(count)]
    return [_const_spec((None, n_sample_rows, D_MODEL), (layer, 0, first + j)) for j in range(count)]


def _ln_specs(index):
    return [_const_spec((None, 1, D_MODEL), (index, 0, 0))] * 2


def _ada_kernel(c_ref, w_ref, b_ref, o_ref):
    ca = _silu(c_ref[...]).astype(BF16)
    o_ref[...] = _dot(ca, w_ref[...].astype(BF16)) + b_ref[...]


def _ada(c, w, b):
    n_l, k, n = w.shape
    m = c.shape[0]
    return pl.pallas_call(
        _ada_kernel,
        grid=(n_l, n // ADA_COLS),
        in_specs=[pl.BlockSpec((m, k), lambda l, j: (0, 0)),
                  pl.BlockSpec((None, k, ADA_COLS), lambda l, j: (l, 0, j)),
                  pl.BlockSpec((None, 1, ADA_COLS), lambda l, j: (l, 0, j))],
        out_specs=pl.BlockSpec((None, m, ADA_COLS), lambda l, j: (l, 0, j)),
        out_shape=jax.ShapeDtypeStruct((n_l, m, n), F32),
        compiler_params=_params(("arbitrary", "arbitrary")),
        name="ada",
    )(c, w, b)


def _ffn_kernel(x_ref, sh_ref, sc_ref, gt_ref, wa_ref, wu_ref, wd_ref, g_ref, b_ref, *rest,
                steps_per_seq, has_kv, has_cast):
    rest = list(rest)
    kv_in = [rest.pop(0) for _ in range(6)] if has_kv else None
    cast_in = [rest.pop(0) for _ in range(2)] if has_cast else None
    o_ref = rest.pop(0)
    kv_out = [rest.pop(0) for _ in range(2)] if has_kv else None
    cast_out = [rest.pop(0) for _ in range(2)] if has_cast else None
    h_ref, = rest

    batch = None if steps_per_seq is None else pl.program_id(0) // steps_per_seq
    x = x_ref[...]
    xm = _modulate_bf16(x, _mod(sh_ref, batch), _mod(sc_ref, batch))
    for c0 in range(0, D_FF, FFN_CHUNK):
        cw = min(FFN_CHUNK, D_FF - c0)
        a = _dot(xm, wa_ref[:, c0:c0 + cw])
        u = _dot(xm, wu_ref[:, c0:c0 + cw])
        h_ref[:, c0:c0 + cw] = (_silu(a) * u).astype(BF16)
    y = ALPHA * x + (0.5 * _mod(gt_ref, batch)) * _dot(h_ref[...], wd_ref[...])
    out = _layer_norm(y, g_ref[...], b_ref[...])
    o_ref[...] = out
    if has_kv:
        ksh_ref, ksc_ref, wkv_ref, cos_ref, slo_ref, shi_ref = kv_in
        kv = _dot(_modulate_bf16(out, _mod(ksh_ref, batch), _mod(ksc_ref, batch)), wkv_ref[...])
        kv_out[0][...] = _rope(kv[:, :KVW], cos_ref[...], slo_ref[...], shi_ref[...])
        kv_out[1][...] = kv[:, KVW:]
    if has_cast:
        for src, dst in zip(cast_in, cast_out):
            dst[...] = src[...].astype(BF16)


def _ffn(x, mods, layer, first, w_up, w_down, ln_g, ln_b, ln_index, n_sample_rows, seq, kv=None, cast=None):
    t = x.shape[0]
    rows = min(FFN_ROWS if kv is None else FFN_ROWS_KV, t)
    steps = t // rows
    x_spec = pl.BlockSpec((rows, D_MODEL), lambda i: (i, 0))
    in_specs = ([x_spec] + _mod_specs(layer, first, 3, n_sample_rows, seq is not None)
                + [_const_spec((D_MODEL, D_FF), (0, 0)), _const_spec((D_MODEL, D_FF), (0, 1)),
                   _resident(w_down.shape)]
                + _ln_specs(ln_index))
    args = [x, mods, mods, mods, w_up, w_up, w_down, ln_g, ln_b]
    out_specs = [x_spec]
    out_shape = [jax.ShapeDtypeStruct((t, D_MODEL), F32)]
    if kv is not None:
        kv_mods, w_kv, tables = kv
        pos_blocks = tables[0].shape[0] // rows
        tab_spec = pl.BlockSpec((rows, LANES), lambda i: (i % pos_blocks, 0))
        kv_spec = pl.BlockSpec((rows, KVW), lambda i: (i, 0))
        in_specs += (_mod_specs(0, 0, 2, n_sample_rows, seq is not None)
                     + [_resident(w_kv.shape), tab_spec, tab_spec, tab_spec])
        args += [kv_mods, kv_mods, w_kv, *tables]
        out_specs += [kv_spec, kv_spec]
        out_shape += [jax.ShapeDtypeStruct((t, KVW), F32)] * 2
    if cast is not None:
        up_f32, down_f32, src_layer = cast
        up_rows = D_MODEL // steps
        down_rows = 2 * D_FF // steps
        assert up_rows % 16 == 0 and down_rows % 16 == 0 and steps % 2 == 0
        in_specs += [pl.BlockSpec((None, up_rows, 2 * D_FF), lambda i: (src_layer, i, 0)),
                     pl.BlockSpec((None, down_rows, D_MODEL), lambda i: (src_layer, i // 2, 0))]
        args += [up_f32, down_f32]
        out_specs += [pl.BlockSpec((up_rows, 2 * D_FF), lambda i: (i, 0)),
                      pl.BlockSpec((down_rows, D_MODEL), lambda i: (i // 2, 0))]
        out_shape += [jax.ShapeDtypeStruct((D_MODEL, 2 * D_FF), BF16),
                      jax.ShapeDtypeStruct((D_FF, D_MODEL), BF16)]
    outs = pl.pallas_call(
        functools.partial(_ffn_kernel, steps_per_seq=None if seq is None else seq // rows,
                          has_kv=kv is not None, has_cast=cast is not None),
        grid=(steps,),
        in_specs=in_specs,
        out_specs=out_specs,
        out_shape=out_shape,
        scratch_shapes=[pltpu.VMEM((rows, D_FF), BF16)],
        compiler_params=_params(("arbitrary",)),
        name="ffn" + ("_kv" if kv is not None else "") + ("_cast" if cast is not None else ""),
    )(*args)
    return outs[0] if len(outs) == 1 else outs


def _head_norm(hh, w_row):
    mu = jnp.mean(hh, axis=-1, keepdims=True)
    d = hh - mu
    var = jnp.mean(d * d, axis=-1, keepdims=True)
    return d * lax.rsqrt(var + LN_EPS) * w_row


def _split3(x, axis):
    hi = x.astype(BF16)
    r1 = x - hi.astype(F32)
    mid = r1.astype(BF16)
    lo = (r1 - mid.astype(F32)).astype(BF16)
    return jnp.concatenate([hi, mid, lo], axis=axis)


def _tile_lanes(t, reps):
    return jnp.concatenate([t] * reps, axis=1)


def _mlstm_prompt_kernel(x_ref, sh_ref, sc_ref, gt_ref, wp_ref, wg_ref, bg_ref, nw_ref, wo_ref, g_ref, b_ref,
                         o_ref, c_out, m_out, proj_ref, rep_ref, y_ref, c_ref, mm_ref):
    batch = pl.program_id(0)
    step = pl.program_id(1)
    size = ML_CHUNK
    assert size == LANES and ML_DK == LANES

    @pl.when(step == 0)
    def _():
        c_ref[...] = jnp.zeros_like(c_ref)
        mm_ref[...] = jnp.zeros_like(mm_ref)

    x = x_ref[...]
    rows = x.shape[0]
    n_chunks = rows // size
    h = _modulate_bf16(x, _mod(sh_ref, batch), _mod(sc_ref, batch))
    gates = _dot(h, wg_ref[...]) + bg_ref[...]

    r = _iota((size, size), 0)
    c = _iota((size, size), 1)
    causal = c <= r
    eye = c == r
    tri3 = _tile_lanes(jnp.where(causal, 1.0, 0.0).astype(BF16), 3)
    n_rep = 2 * ML_HEADS
    spread = jnp.where(_iota((LANES, n_rep * LANES), 1) // LANES == _iota((LANES, n_rep * LANES), 0),
                       1.0, 0.0).astype(BF16)
    spread3 = jnp.concatenate([spread] * 3, axis=0)
    ones3 = jnp.ones((3 * size, LANES), BF16)
    ones_v = jnp.ones((size, LANES), BF16)
    sr = _iota((2 * SUBLANES, LANES), 0)
    sl = _iota((2 * SUBLANES, LANES), 1)
    pick = (jnp.where((sr < ML_HEADS) & (sl == sr), 1.0, 0.0)
            - jnp.where((sr < ML_HEADS) & (sl == sr + ML_HEADS), 1.0, 0.0)
            + jnp.where((sr >= SUBLANES) & (sr < SUBLANES + ML_HEADS) & (sl == sr - SUBLANES + ML_HEADS),
                        1.0, 0.0))
    pick3 = _tile_lanes(pick.astype(BF16), 3)

    log_f = _log_sigmoid(gates)
    bwide = jnp.concatenate([_dot(tri3, _split3(log_f[ci * size:(ci + 1) * size, :], 0))
                             for ci in range(n_chunks)], axis=0)
    src3 = _split3(jnp.where(_iota((rows, LANES), 1) < ML_HEADS, gates, bwide), 1)
    rep_ref[...] = _dot(src3, spread3)
    ab_rows = _nt(pick3, src3)
    a_rows = ab_rows[:SUBLANES, :]
    b_rows = ab_rows[SUBLANES:, :]
    lane_in_chunk = _iota((SUBLANES, rows), 1) % size
    run = a_rows
    shift = 1
    while shift < size:
        run = jnp.maximum(run, jnp.where(lane_in_chunk >= shift, pltpu.roll(run, shift, 1), -jnp.inf))
        shift *= 2
    m_rows = [mm_ref[...]]
    g_rows = []
    for ci in range(n_chunks):
        run_c = run[:, ci * size:(ci + 1) * size]
        g_rows.append(jnp.maximum(run_c, m_rows[-1]))
        end = (ci + 1) * size - 1
        m_end = b_rows[:, end:end + 1] + jnp.maximum(run[:, end:end + 1], m_rows[-1][:, :1])
        m_rows.append(jnp.broadcast_to(m_end, (SUBLANES, LANES)))
    mm_ref[...] = m_rows[-1]

    proj_ref[...] = _dot(h, wp_ref[...])

    states = [c_ref[hd] for hd in range(ML_HEADS)]
    for ci in range(n_chunks):
        lo, hi = ci * size, (ci + 1) * size
        for hd in range(ML_HEADS):
            i_rep = rep_ref[lo:hi, hd * LANES:(hd + 1) * LANES]
            b_rep = rep_ref[lo:hi, (ML_HEADS + hd) * LANES:(ML_HEADS + hd + 1) * LANES]
            m_prev = m_rows[ci][hd:hd + 1, :]
            m_new = m_rows[ci + 1][hd:hd + 1, :]
            a_row = a_rows[hd:hd + 1, lo:hi]
            g_rep = _dot(_split3(jnp.where(eye, g_rows[ci][hd:hd + 1, :], 0.0), 1), ones3)
            w = jnp.where(causal, jnp.exp(a_row - g_rep), 0.0)
            w_init = jnp.exp(m_prev - g_rep)
            floor = jnp.exp(-(b_rep + g_rep))
            b_last = b_rep[size - 1:size, :]
            w_end = jnp.exp(b_last - b_rep + i_rep - m_new)
            decay = jnp.exp(b_last + m_prev - m_new)

            q = proj_ref[lo:hi, hd * ML_DK:(hd + 1) * ML_DK]
            k = proj_ref[lo:hi, HQK + hd * ML_DK:HQK + (hd + 1) * ML_DK] * (ML_DK ** -0.5)
            v = proj_ref[lo:hi, 2 * HQK + hd * ML_DV:2 * HQK + (hd + 1) * ML_DV]
            qb = q.astype(BF16)
            v_ext = jnp.concatenate([v.astype(BF16), ones_v], axis=1)
            s = _nt(qb, k.astype(BF16)) * w
            num = _dot(s.astype(BF16), v_ext) + _tile_lanes(w_init, 3) * _dot(qb, states[hd].astype(BF16))
            den = jnp.maximum(jnp.abs(num[:, ML_DV:]), floor)
            hh = num[:, :ML_DV] / _tile_lanes(den, ML_DV // LANES)
            states[hd] = _tile_lanes(decay, 3) * states[hd] + _dot((k * w_end).T.astype(BF16), v_ext)
            y_ref[lo:hi, hd * ML_DV:(hd + 1) * ML_DV] = _head_norm(hh, nw_ref[:, hd * ML_DV:(hd + 1) * ML_DV])
    for hd in range(ML_HEADS):
        c_ref[hd] = states[hd]

    o_gate = jax.nn.sigmoid(proj_ref[:, 2 * HQK + HV:2 * HQK + 2 * HV])
    y = _dot((y_ref[...] * o_gate).astype(BF16), wo_ref[...])
    o_ref[...] = _layer_norm(ALPHA * x + _mod(gt_ref, batch) * y, g_ref[...], b_ref[...])

    @pl.when(step == pl.num_programs(1) - 1)
    def _():
        c_out[...] = c_ref[...]
        m_out[...] = mm_ref[...]


def _mlstm_weight_specs(wg, bg, nw, wo):
    return [_const_spec((None, D_MODEL, 2 * HQK + 2 * HV), (0, 0, 0)), _resident(wg.shape),
            _resident(bg.shape), _resident(nw.shape), _resident(wo.shape)] + _ln_specs(1)


def _mlstm_prompt(x, mods, w_in, wg, bg, nw, wo, ln_g, ln_b, n_sample_rows):
    bsz, seq, _ = x.shape
    rows = min(ML_ROWS, seq)
    x_spec = pl.BlockSpec((None, rows, D_MODEL), lambda bi, i: (bi, i, 0))
    state_shape = (ML_HEADS, ML_DK, ML_DV + LANES)
    return pl.pallas_call(
        _mlstm_prompt_kernel,
        grid=(bsz, seq // rows),
        in_specs=[x_spec] + _mod_specs(0, 3, 3, n_sample_rows, True) + _mlstm_weight_specs(wg, bg, nw, wo),
        out_specs=[x_spec, pl.BlockSpec((None,) + state_shape, lambda bi, i: (bi, 0, 0, 0)),
                   pl.BlockSpec((None, SUBLANES, LANES), lambda bi, i: (bi, 0, 0))],
        out_shape=[jax.ShapeDtypeStruct((bsz, seq, D_MODEL), F32),
                   jax.ShapeDtypeStruct((bsz,) + state_shape, F32),
                   jax.ShapeDtypeStruct((bsz, SUBLANES, LANES), F32)],
        scratch_shapes=[pltpu.VMEM((rows, 2 * HQK + 2 * HV), F32),
                        pltpu.VMEM((rows, 2 * ML_HEADS * LANES), F32),
                        pltpu.VMEM((rows, HV), F32),
                        pltpu.VMEM(state_shape, F32),
                        pltpu.VMEM((SUBLANES, LANES), F32)],
        compiler_params=_params(("arbitrary", "arbitrary")),
        name="mlstm_prompt",
    )(x, mods, mods, mods, w_in, wg, bg, nw, wo, ln_g, ln_b)


def _mlstm_sample_kernel(x_ref, sh_ref, sc_ref, gt_ref, wp_ref, wg_ref, bg_ref, nw_ref, wo_ref, g_ref, b_ref,
                         c0_ref, n0_ref, m0_ref, o_ref, c_out, n_out, m_out,
                         proj_ref, gate_ref, y_ref, *, seq):
    step = pl.program_id(0)
    grp = c0_ref.shape[0]
    rows = grp * seq

    @pl.when(step == 0)
    def _():
        h = _modulate_bf16(x_ref[...], sh_ref[...], sc_ref[...])
        proj_ref[...] = _dot(h, wp_ref[...])
        gate_ref[...] = _dot(h, wg_ref[...]) + bg_ref[...]

    r = _iota((rows, rows), 0)
    c = _iota((rows, rows), 1)
    same = (r // seq) == (c // seq)
    causal = same & (c <= r)
    eye = r == c
    last_of_row = c == (r // seq) * seq + (seq - 1)
    same_f = same.astype(F32)
    expand = (_iota((rows, grp), 0) // seq == _iota((rows, grp), 1)).astype(F32)
    pick_last = (_iota((grp, rows), 1) == _iota((grp, rows), 0) * seq + (seq - 1)).astype(F32)
    ident = (_iota((ML_DK, ML_DK), 0) == _iota((ML_DK, ML_DK), 1)).astype(F32)
    seq_of_row = _iota((rows, ML_DV), 0) // seq
    col_seq = _iota((ML_DK, rows), 1) // seq
    lane = _iota((rows, LANES), 1)
    to_row = lambda col: jnp.sum(jnp.where(eye, col, 0.0), axis=0, keepdims=True)

    r0 = pl.multiple_of(step * rows, rows)
    gates = gate_ref[pl.ds(r0, rows), :]
    log_f = _log_sigmoid(gates)
    bwide = _dot_exact(causal.astype(F32), log_f)
    bsum = _dot_exact(same_f, log_f)
    m_prev_wide = _dot_exact(expand, m0_ref[...])
    n_rows_wide = _dot_exact(expand, n0_ref[...])
    n_pieces = []
    m_wide = jnp.zeros((rows, LANES), F32)
    for hd in range(ML_HEADS):
        q = proj_ref[pl.ds(r0, rows), hd * ML_DK:(hd + 1) * ML_DK]
        k = proj_ref[pl.ds(r0, rows), HQK + hd * ML_DK:HQK + (hd + 1) * ML_DK] * (ML_DK ** -0.5)
        v = proj_ref[pl.ds(r0, rows), 2 * HQK + hd * ML_DV:2 * HQK + (hd + 1) * ML_DV]
        bcol = bwide[:, ML_HEADS + hd:ML_HEADS + hd + 1]
        b_last = bsum[:, ML_HEADS + hd:ML_HEADS + hd + 1]
        icol = gates[:, hd:hd + 1]
        m_prev = m_prev_wide[:, hd:hd + 1]
        n_rows = n_rows_wide[:, hd * ML_DK:(hd + 1) * ML_DK]
        log_w = jnp.where(causal, bcol - to_row(bcol) + to_row(icol), -jnp.inf)
        log_init = bcol + m_prev
        m_t = jnp.maximum(log_init, jnp.max(log_w, axis=-1, keepdims=True))
        w = jnp.exp(log_w - m_t)
        w_init = jnp.exp(log_init - m_t)
        qb = q.astype(BF16)
        s = _nt(qb, k.astype(BF16)) * w
        inter = None
        for bi in range(grp):
            cand = _dot(qb, c0_ref[bi, hd].astype(BF16))
            inter = cand if inter is None else jnp.where(seq_of_row == bi, cand, inter)
        num = _dot(s.astype(BF16), v.astype(BF16)) + w_init * inter
        den = jnp.sum(s, axis=-1, keepdims=True) + w_init * jnp.sum(q * n_rows, axis=-1, keepdims=True)
        hh = num / jnp.maximum(jnp.abs(den), jnp.exp(-m_t))
        m_new = jnp.sum(jnp.where(last_of_row, to_row(m_t), 0.0), axis=1, keepdims=True)
        w_end = jnp.exp(b_last - bcol + icol - m_new)
        decay = jnp.exp(b_last + m_prev - m_new)
        kw = k * w_end
        kw_t = lax.dot_general(ident, kw, NT_DIMS, preferred_element_type=F32, precision=HIGHEST)
        lhs = jnp.concatenate([jnp.where(col_seq == bi, kw_t, 0.0) for bi in range(grp)], axis=0)
        upd = _dot(lhs.astype(BF16), v.astype(BF16))
        for bi in range(grp):
            c_out[bi, hd] = (decay[bi * seq:bi * seq + 1, :] * c0_ref[bi, hd]
                             + upd[bi * ML_DK:(bi + 1) * ML_DK, :])
        n_pieces.append(decay * n_rows + _dot_exact(same_f, kw))
        m_wide = jnp.where(lane == hd, m_new, m_wide)
        y_ref[pl.ds(r0, rows), hd * ML_DV:(hd + 1) * ML_DV] = _head_norm(
            hh, nw_ref[:, hd * ML_DV:(hd + 1) * ML_DV])
    n_out[...] = _dot_exact(pick_last, jnp.concatenate(n_pieces, axis=1))
    m_out[...] = _dot_exact(pick_last, m_wide)

    @pl.when(step == pl.num_programs(0) - 1)
    def _():
        o_gate = jax.nn.sigmoid(proj_ref[:, 2 * HQK + HV:2 * HQK + 2 * HV])
        y = _dot((y_ref[...] * o_gate).astype(BF16), wo_ref[...])
        o_ref[...] = _layer_norm(ALPHA * x_ref[...] + gt_ref[...] * y, g_ref[...], b_ref[...])


def _mlstm_sample(x, mods, w_in, wg, bg, nw, wo, ln_g, ln_b, c0, n0, m0, seq):
    t = x.shape[0]
    grp = SAMPLE_GROUP
    state_specs = [pl.BlockSpec((grp, ML_HEADS, ML_DK, ML_DV), lambda i: (i, 0, 0, 0)),
                   pl.BlockSpec((grp, HQK), lambda i: (i, 0)),
                   pl.BlockSpec((grp, LANES), lambda i: (i, 0))]
    return pl.pallas_call(
        functools.partial(_mlstm_sample_kernel, seq=seq),
        grid=(t // (grp * seq),),
        in_specs=[_resident(x.shape)] + _mod_specs(0, 3, 3, t, False)
                 + _mlstm_weight_specs(wg, bg, nw, wo) + state_specs,
        out_specs=[pl.BlockSpec((t, D_MODEL), lambda i: (0, 0))] + state_specs,
        out_shape=[jax.ShapeDtypeStruct((t, D_MODEL), F32),
                   jax.ShapeDtypeStruct(c0.shape, F32),
                   jax.ShapeDtypeStruct(n0.shape, F32),
                   jax.ShapeDtypeStruct(m0.shape, F32)],
        scratch_shapes=[pltpu.VMEM((t, 2 * HQK + 2 * HV), F32),
                        pltpu.VMEM((t, LANES), F32),
                        pltpu.VMEM((t, HV), F32)],
        compiler_params=_params(("arbitrary",)),
        name="mlstm_sample",
    )(x, mods, mods, mods, w_in, wg, bg, nw, wo, ln_g, ln_b, c0, n0, m0)


def _sink_column(sink_ref, kh, rows_per_head):
    rid = _iota((ATT_GROUP * rows_per_head, 1), 0)
    col = jnp.full((ATT_GROUP * rows_per_head, 1), sink_ref[kh * ATT_GROUP], F32)
    for gq in range(1, ATT_GROUP):
        col = jnp.where(rid >= gq * rows_per_head, sink_ref[kh * ATT_GROUP + gq], col)
    return col


def _head_lanes(x, kh):
    return jnp.where(_iota(x.shape, 1) // ATT_HD == kh, x, 0.0).astype(BF16)


def _stack_groups(x):
    return jnp.concatenate([x[:, g * KVW:(g + 1) * KVW] for g in range(ATT_GROUP)], axis=0)


def _attn_prompt_kernel(sink_ref, x_ref, sh_ref, sc_ref, gt_ref, wq_ref, cos_ref, slo_ref, shi_ref,
                        kp_ref, kc_ref, vp_ref, vc_ref, wo_ref, g_ref, b_ref, o_ref, a_ref):
    batch = pl.program_id(0)
    step = pl.program_id(1)
    x = x_ref[...]
    rows = x.shape[0]
    n_blk = rows // WINDOW
    q = _rope(_dot(_modulate_bf16(x, _mod(sh_ref, batch), _mod(sc_ref, batch)), wq_ref[...]),
              cos_ref[...], slo_ref[...], shi_ref[...])
    q = (q * (ATT_HD ** -0.5)).astype(BF16)

    keys = jnp.concatenate([kp_ref[...], kc_ref[...]], axis=0)
    vals = jnp.concatenate([vp_ref[...], vc_ref[...]], axis=0)
    k_heads = [_head_lanes(keys, kh) for kh in range(ATT_KV_HEADS)]
    v_heads = [_head_lanes(vals, kh) for kh in range(ATT_KV_HEADS)]

    rid = _iota((ATT_GROUP * WINDOW, WINDOW), 0) % WINDOW
    cid = _iota((ATT_GROUP * WINDOW, WINDOW), 1)
    from_prev = cid > rid
    no_prev = jnp.where(step > 0, 0.0, -jnp.inf)
    sinks = [_sink_column(sink_ref, kh, WINDOW) for kh in range(ATT_KV_HEADS)]

    scores = [[_nt(_stack_groups(q[n * WINDOW:(n + 1) * WINDOW, :]),
                   k_heads[kh][n * WINDOW:(n + 2) * WINDOW, :])
               for kh in range(ATT_KV_HEADS)] for n in range(n_blk)]
    probs = []
    for n in range(n_blk):
        row = []
        for kh in range(ATT_KV_HEADS):
            s_prev = scores[n][kh][:, :WINDOW]
            if n == 0:
                s_prev = s_prev + no_prev
            s = jnp.where(from_prev, s_prev, scores[n][kh][:, WINDOW:])
            mx = jnp.maximum(jnp.max(s, axis=-1, keepdims=True), sinks[kh])
            p = jnp.exp(s - mx)
            den = jnp.sum(p, axis=-1, keepdims=True) + jnp.exp(sinks[kh] - mx)
            pn = p / den
            row.append(jnp.where(from_prev, pn, 0.0).astype(BF16))
            row.append(jnp.where(from_prev, 0.0, pn).astype(BF16))
        probs.append(jnp.concatenate(row, axis=1))
    for n in range(n_blk):
        v_all = jnp.concatenate([v_heads[kh][n * WINDOW:(n + 2) * WINDOW, :]
                                 for kh in range(ATT_KV_HEADS)], axis=0)
        o = _dot(probs[n], v_all)
        for g in range(ATT_GROUP):
            a_ref[n * WINDOW:(n + 1) * WINDOW, g * KVW:(g + 1) * KVW] = o[g * WINDOW:(g + 1) * WINDOW, :]

    y = _dot(a_ref[...].astype(BF16), wo_ref[...])
    o_ref[...] = _layer_norm(ALPHA * x + _mod(gt_ref, batch) * y, g_ref[...], b_ref[...])


def _attn_prompt(x, mods, wq, tables, k, v, sinks, wo, ln_g, ln_b, n_sample_rows):
    bsz, seq, _ = x.shape
    rows = min(ATT_ROWS, seq)
    per = rows // WINDOW
    x_spec = pl.BlockSpec((None, rows, D_MODEL), lambda bi, i: (bi, i, 0))
    cur = pl.BlockSpec((None, rows, KVW), lambda bi, i: (bi, i, 0))
    prev = pl.BlockSpec((None, WINDOW, KVW), lambda bi, i: (bi, jnp.maximum(i * per - 1, 0), 0))
    tab = pl.BlockSpec((rows, LANES), lambda bi, i: (i, 0))
    return pl.pallas_call(
        _attn_prompt_kernel,
        grid=(bsz, seq // rows),
        in_specs=[pl.BlockSpec(memory_space=pltpu.SMEM), x_spec]
                 + _mod_specs(1, 3, 3, n_sample_rows, True)
                 + [_resident(wq.shape), tab, tab, tab, prev, cur, prev, cur, _resident(wo.shape)]
                 + _ln_specs(4),
        out_specs=x_spec,
        out_shape=jax.ShapeDtypeStruct((bsz, seq, D_MODEL), F32),
        scratch_shapes=[pltpu.VMEM((rows, QW), F32)],
        compiler_params=_params(("arbitrary", "arbitrary")),
        name="attn_prompt",
    )(sinks, x, mods, mods, mods, wq, *tables, k, k, v, v, wo, ln_g, ln_b)


def _attn_sample_kernel(sink_ref, x_ref, sh_ref, sc_ref, gt_ref, wq_ref, cos_ref, slo_ref, shi_ref,
                        kn_ref, vn_ref, kb_ref, vb_ref, wo_ref, g_ref, b_ref, o_ref,
                        q_ref, a_ref, *, seq):
    step = pl.program_id(0)
    pair = 2 * seq
    assert pair == SUBLANES
    grp, n_buf, _ = kb_ref.shape

    @pl.when(step == 0)
    def _():
        q = _rope(_dot(_modulate_bf16(x_ref[...], sh_ref[...], sc_ref[...]), wq_ref[...]),
                  cos_ref[...], slo_ref[...], shi_ref[...])
        q_ref[...] = q * (ATT_HD ** -0.5)

    n_rows = ATT_GROUP * pair
    n_all = ATT_KV_HEADS * n_rows
    n_new = 2 * pair
    q_pos = _iota((n_all, n_buf), 0) % seq + PAST_LEN
    k_pos = _iota((n_all, n_buf), 1) + (PAST_LEN - n_buf)
    buf_mask = (k_pos <= q_pos) & (k_pos > q_pos - WINDOW) & (k_pos >= 0)
    q_tok = _iota((n_all, n_new), 0) % seq
    new_col = _iota((n_all, n_new), 1)
    out_lane_head = _iota((n_rows, KVW), 1) // ATT_HD
    low = _iota((n_rows, KVW), 0) % pair < seq
    head_of_row = _iota((n_all, 1), 0) // pair
    sink_col = jnp.full((n_all, 1), sink_ref[0], F32)
    for hq in range(1, ATT_Q_HEADS):
        sink_col = jnp.where(head_of_row >= hq, sink_ref[hq], sink_col)
    pad_rows = jnp.zeros((n_new - pair, KVW), BF16)
    for p in range(grp // 2):
        r0 = pl.multiple_of(step * (grp * seq) + p * pair, pair)
        q_all = _stack_groups(q_ref[pl.ds(r0, pair), :].astype(BF16))
        q_bd = jnp.concatenate([_head_lanes(q_all, kh) for kh in range(ATT_KV_HEADS)], axis=0)
        k_new = jnp.concatenate([kn_ref[pl.ds(r0, pair), :].astype(BF16), pad_rows], axis=0)
        v_new = jnp.concatenate([vn_ref[pl.ds(r0, pair), :].astype(BF16), pad_rows], axis=0)
        s_new_all = _nt(q_bd, k_new)
        halves = []
        for half in range(2):
            bi = 2 * p + half
            tok = new_col - half * seq
            own_new = (tok >= 0) & (tok < seq) & (tok <= q_tok)
            s_buf = jnp.where(buf_mask, _nt(q_bd, kb_ref[bi].astype(BF16)), -jnp.inf)
            s_new = jnp.where(own_new, s_new_all, -jnp.inf)
            mx = jnp.maximum(jnp.maximum(jnp.max(s_buf, axis=-1, keepdims=True),
                                         jnp.max(s_new, axis=-1, keepdims=True)), sink_col)
            e_buf = jnp.exp(s_buf - mx)
            e_new = jnp.exp(s_new - mx)
            den = (jnp.sum(e_buf, axis=-1, keepdims=True) + jnp.sum(e_new, axis=-1, keepdims=True)
                   + jnp.exp(sink_col - mx))
            o_all = (_dot((e_buf / den).astype(BF16), vb_ref[bi].astype(BF16))
                     + _dot((e_new / den).astype(BF16), v_new))
            o_half = None
            for kh in range(ATT_KV_HEADS):
                part = jnp.where(out_lane_head == kh, o_all[kh * n_rows:(kh + 1) * n_rows, :], 0.0)
                o_half = part if o_half is None else o_half + part
            halves.append(o_half)
        o = jnp.where(low, halves[0], halves[1])
        for g in range(ATT_GROUP):
            a_ref[pl.ds(r0, pair), g * KVW:(g + 1) * KVW] = o[g * pair:(g + 1) * pair, :]

    @pl.when(step == pl.num_programs(0) - 1)
    def _():
        y = _dot(a_ref[...].astype(BF16), wo_ref[...])
        o_ref[...] = _layer_norm(ALPHA * x_ref[...] + gt_ref[...] * y, g_ref[...], b_ref[...])


def _attn_sample(x, mods, wq, tables, k_new, v_new, k_buf, v_buf, sinks, wo, ln_g, ln_b, seq):
    t = x.shape[0]
    grp = SAMPLE_GROUP
    buf_spec = pl.BlockSpec((grp,) + k_buf.shape[1:], lambda i: (i, 0, 0))
    return pl.pallas_call(
        functools.partial(_attn_sample_kernel, seq=seq),
        grid=(t // (grp * seq),),
        in_specs=[pl.BlockSpec(memory_space=pltpu.SMEM), _resident(x.shape)]
                 + _mod_specs(1, 3, 3, t, False)
                 + [_resident(wq.shape)] + [_resident(tab.shape) for tab in tables]
                 + [_resident(k_new.shape), _resident(v_new.shape), buf_spec, buf_spec, _resident(wo.shape)]
                 + _ln_specs(4),
        out_specs=pl.BlockSpec((t, D_MODEL), lambda i: (0, 0)),
        out_shape=jax.ShapeDtypeStruct((t, D_MODEL), F32),
        scratch_shapes=[pltpu.VMEM((t, QW), F32), pltpu.VMEM((t, QW), F32)],
        compiler_params=_params(("arbitrary",)),
        name="attn_sample",
    )(sinks, x, mods, mods, mods, wq, *tables, k_new, v_new, k_buf, v_buf, wo, ln_g, ln_b)


def _rope_tables(pos):
    half = ROT_DIM // 2
    inv_freq = ROPE_THETA ** (-jnp.arange(half, dtype=F32) * 2.0 / ROT_DIM)
    ang = pos.astype(F32)[:, None] * inv_freq[None, :]
    cos, sin = jnp.cos(ang), jnp.sin(ang)
    dim = np.arange(LANES) % ATT_HD
    freq = np.arange(half)[:, None]
    lo = (dim[None, :] == freq).astype(np.float32)
    hi = (dim[None, :] == freq + half).astype(np.float32)
    spread = lambda a, m: jnp.dot(a, jnp.asarray(m), precision=HIGHEST)
    return (spread(cos, lo + hi) + jnp.asarray((dim >= ROT_DIM).astype(np.float32)),
            spread(sin, -lo), spread(sin, hi))


def kernel(x_prompt, x_sample, c_prompt, c_sample, state_mlstm_C, state_mlstm_n, state_mlstm_m,
           cache_win_k, cache_win_v, ada_w, ada_b, ffn1_up, ffn1_down, ffn2_up, ffn2_down, ln_g, ln_b,
           ml_w_in, ml_b_gates, ml_norm_w, ml_w_out, kv_ada_w, kv_ada_b, kv_w,
           att_w_q, att_sinks, att_w_o):
    bp, sp, _ = x_prompt.shape
    bs, ss, _ = x_sample.shape
    n_buf = cache_win_k.shape[1]
    ts = bs * ss

    c_all = jnp.concatenate([jnp.repeat(c_sample, ss, axis=0), c_prompt,
                             jnp.zeros((-bp % SUBLANES, D_MODEL), F32)], axis=0)
    mods = _ada(c_all, ada_w, ada_b[:, None, :])
    kv_mods = _ada(c_all, kv_ada_w[None], kv_ada_b[None, None, :])

    bf = lambda w: w.astype(BF16)
    ffn_f32 = [(ffn1_up, ffn1_down, 0), (ffn2_up, ffn2_down, 0), (ffn1_up, ffn1_down, 1), (ffn2_up, ffn2_down, 1)]
    ffn_w = [(bf(ffn1_up[0]), bf(ffn1_down[0]))]
    w_in = bf(ml_w_in)
    gate_pad = LANES - 2 * ML_HEADS
    w_gate = bf(jnp.pad(ml_w_in[0, :, 2 * HQK + 2 * HV:], ((0, 0), (0, gate_pad))))
    b_gate = jnp.pad(ml_b_gates[0], (0, gate_pad))[None, :]
    norm_w = ml_norm_w[0][None, :]
    w_out = bf(ml_w_out[0])
    w_kv = bf(kv_w)
    w_q = bf(att_w_q[0].reshape(D_MODEL, ATT_KV_HEADS, ATT_GROUP, ATT_HD)
             .transpose(0, 2, 1, 3).reshape(D_MODEL, QW))
    w_o = bf(att_w_o[0].reshape(ATT_KV_HEADS, ATT_GROUP, ATT_HD, D_MODEL)
             .transpose(1, 0, 2, 3).reshape(QW, D_MODEL))
    sinks = att_sinks[0]
    lng = ln_g.reshape(DEPTH * 3, 1, D_MODEL)
    lnb = ln_b.reshape(DEPTH * 3, 1, D_MODEL)

    tab_p = _rope_tables(jnp.arange(sp, dtype=jnp.int32))
    tab_s = _rope_tables(jnp.tile(PAST_LEN + jnp.arange(ss, dtype=jnp.int32), bs))

    xp = x_prompt.reshape(bp * sp, D_MODEL)
    xp, *cast = _ffn(xp, mods, 0, 0, *ffn_w[0], lng, lnb, 0, ts, sp, cast=ffn_f32[1])
    ffn_w.append(tuple(cast))
    xp, state_p, m_p = _mlstm_prompt(xp.reshape(bp, sp, D_MODEL), mods, w_in, w_gate, b_gate,
                                     norm_w, w_out, lng, lnb, ts)
    xp, k_p, v_p, *cast = _ffn(xp.reshape(bp * sp, D_MODEL), mods, 0, 6, *ffn_w[1], lng, lnb, 2, ts, sp,
                               kv=(kv_mods, w_kv, tab_p), cast=ffn_f32[2])
    ffn_w.append(tuple(cast))
    k_p = k_p.reshape(bp, sp, KVW)
    v_p = v_p.reshape(bp, sp, KVW)
    xp, *cast = _ffn(xp, mods, 1, 0, *ffn_w[2], lng, lnb, 3, ts, sp, cast=ffn_f32[3])
    ffn_w.append(tuple(cast))
    xp = _attn_prompt(xp.reshape(bp, sp, D_MODEL), mods, w_q, tab_p, k_p, v_p, sinks, w_o, lng, lnb, ts)
    xp = _ffn(xp.reshape(bp * sp, D_MODEL), mods, 1, 6, *ffn_w[3], lng, lnb, 5, ts, sp)
    y_prompt = xp.reshape(bp, sp, D_MODEL)
    keep = min(WINDOW, sp)
    win_k_p = k_p[:, sp - keep:].reshape(bp, keep, ATT_KV_HEADS, ATT_HD)
    win_v_p = v_p[:, sp - keep:].reshape(bp, keep, ATT_KV_HEADS, ATT_HD)

    xs = x_sample.reshape(ts, D_MODEL)
    xs = _ffn(xs, mods, 0, 0, *ffn_w[0], lng, lnb, 0, ts, None)
    m0 = jnp.pad(state_mlstm_m[0], ((0, 0), (0, LANES - ML_HEADS)))
    xs, c_s, n_s, m_s = _mlstm_sample(xs, mods, w_in, w_gate, b_gate, norm_w, w_out, lng, lnb,
                                      state_mlstm_C[0], state_mlstm_n[0].reshape(bs, HQK), m0, ss)
    xs, k_s, v_s = _ffn(xs, mods, 0, 6, *ffn_w[1], lng, lnb, 2, ts, None, kv=(kv_mods, w_kv, tab_s))
    xs = _ffn(xs, mods, 1, 0, *ffn_w[2], lng, lnb, 3, ts, None)
    xs = _attn_sample(xs, mods, w_q, tab_s, k_s, v_s,
                      cache_win_k.reshape(bs, n_buf, KVW), cache_win_v.reshape(bs, n_buf, KVW),
                      sinks, w_o, lng, lnb, ss)
    xs = _ffn(xs, mods, 1, 6, *ffn_w[3], lng, lnb, 5, ts, None)
    y_sample = xs.reshape(bs, ss, D_MODEL)
    win_k_s = jnp.concatenate([cache_win_k, k_s.reshape(bs, ss, ATT_KV_HEADS, ATT_HD)], axis=1)[:, -n_buf:]
    win_v_s = jnp.concatenate([cache_win_v, v_s.reshape(bs, ss, ATT_KV_HEADS, ATT_HD)], axis=1)[:, -n_buf:]

    return (y_prompt, y_sample,
            state_p[None, ..., :ML_DV], state_p[None, ..., ML_DV], m_p[None, :, :ML_HEADS, 0],
            win_k_p, win_v_p,
            c_s[None], n_s.reshape(1, bs, ML_HEADS, ML_DK), m_s[None, :, :ML_HEADS],
            win_k_s, win_v_s)
```

```python
import functools

import numpy as np
import jax
import jax.numpy as jnp
from jax import lax
from jax.experimental import pallas as pl
from jax.experimental.pallas import tpu as pltpu

F32 = jnp.float32
BF16 = jnp.bfloat16

D_MODEL = 1024
DEPTH = 2
PAST_LEN = 8192
ML_HEADS = 4
ML_DK = D_MODEL // 8
ML_DV = D_MODEL // 4
ATT_Q_HEADS = 16
ATT_KV_HEADS = 4
ATT_GROUP = ATT_Q_HEADS // ATT_KV_HEADS
ATT_HD = 64
WINDOW = 128
ROT_DIM = ATT_HD // 4
ROPE_THETA = 500000.0
D_FF = 2816
ALPHA = (2 * DEPTH) ** 0.25
LN_EPS = 1e-5
N_MOD = 9
HQK = ML_HEADS * ML_DK
HV = ML_HEADS * ML_DV
KVW = ATT_KV_HEADS * ATT_HD
QW = ATT_Q_HEADS * ATT_HD

LANES = 128
SUBLANES = 8
VMEM_LIMIT_BYTES = 56 * 1024 * 1024

FFN_ROWS = 1024
FFN_ROWS_KV = 512
FFN_CHUNK = 256
ML_ROWS = 1024
ML_CHUNK = 128
ATT_ROWS = 1024
ADA_COLS = 1024
SAMPLE_GROUP = 8

NT_DIMS = (((1,), (1,)), ((), ()))
HIGHEST = lax.Precision.HIGHEST


def _params(semantics):
    return pltpu.CompilerParams(dimension_semantics=semantics, vmem_limit_bytes=VMEM_LIMIT_BYTES)


def _const_spec(block, index):
    return pl.BlockSpec(block, lambda *_: index, pipeline_mode=pl.Buffered(1))


def _resident(shape):
    return _const_spec(shape, (0,) * len(shape))


def _dot(a, b):
    return jnp.dot(a, b, preferred_element_type=F32)


def _dot_exact(a, b):
    return jnp.dot(a, b, preferred_element_type=F32, precision=HIGHEST)


def _nt(a, b):
    return lax.dot_general(a, b, NT_DIMS, preferred_element_type=F32)


def _iota(shape, dim):
    return lax.broadcasted_iota(jnp.int32, shape, dim)


def _silu(x):
    return x * jax.nn.sigmoid(x)


def _log_sigmoid(x):
    return jnp.minimum(x, 0.0) - jnp.log1p(jnp.exp(-jnp.abs(x)))


def _layer_norm(y, g, b):
    mu = jnp.mean(y, axis=-1, keepdims=True)
    d = y - mu
    var = jnp.mean(d * d, axis=-1, keepdims=True)
    return d * lax.rsqrt(var + LN_EPS) * g + b


def _mod(ref, batch):
    return ref[...] if batch is None else ref[pl.ds(batch, 1), :]


def _modulate_bf16(x, shift, scale):
    return (x * (1.0 + scale) + shift).astype(BF16)


def _rope(x, cos, sin_lo, sin_hi):
    width = x.shape[1]
    reps = width // LANES
    tile = lambda t: jnp.concatenate([t] * reps, axis=1) if reps > 1 else t
    x_up = pltpu.roll(x, width - ROT_DIM // 2, 1)
    x_dn = pltpu.roll(x, ROT_DIM // 2, 1)
    return x * tile(cos) + x_up * tile(sin_lo) + x_dn * tile(sin_hi)


def _mod_specs(layer, first, count, n_sample_rows, prompt):
    if prompt:
        blk = n_sample_rows // SUBLANES
        return [_const_spec((None, SUBLANES, D_MODEL), (layer, blk, first + j)) for j in range(count)]
    return [_const_spec((None, n_sample_rows, D_MODEL), (layer, 0, first + j)) for j in range(count)]


def _ln_specs(index):
    return [_const_spec((None, 1, D_MODEL), (index, 0, 0))] * 2


def _ada_kernel(c_ref, w_ref, b_ref, o_ref):
    ca = _silu(c_ref[...]).astype(BF16)
    o_ref[...] = _dot(ca, w_ref[...].astype(BF16)) + b_ref[...]


def _ada(c, w, b):
    n_l, k, n = w.shape
    m = c.shape[0]
    return pl.pallas_call(
        _ada_kernel,
        grid=(n_l, n // ADA_COLS),
        in_specs=[pl.BlockSpec((m, k), lambda l, j: (0, 0)),
                  pl.BlockSpec((None, k, ADA_COLS), lambda l, j: (l, 0, j)),
                  pl.BlockSpec((None, 1, ADA_COLS), lambda l, j: (l, 0, j))],
        out_specs=pl.BlockSpec((None, m, ADA_COLS), lambda l, j: (l, 0, j)),
        out_shape=jax.ShapeDtypeStruct((n_l, m, n), F32),
        compiler_params=_params(("arbitrary", "arbitrary")),
        name="ada",
    )(c, w, b)


def _ffn_kernel(x_ref, sh_ref, sc_ref, gt_ref, wa_ref, wu_ref, wd_ref, g_ref, b_ref, *rest,
                steps_per_seq, has_kv, has_cast):
    rest = list(rest)
    kv_in = [rest.pop(0) for _ in range(6)] if has_kv else None
    cast_in = [rest.pop(0) for _ in range(2)] if has_cast else None
    o_ref = rest.pop(0)
    kv_out = [rest.pop(0) for _ in range(2)] if has_kv else None
    cast_out = [rest.pop(0) for _ in range(2)] if has_cast else None
    h_ref, = rest

    batch = None if steps_per_seq is None else pl.program_id(0) // steps_per_seq
    x = x_ref[...]
    xm = _modulate_bf16(x, _mod(sh_ref, batch), _mod(sc_ref, batch))
    for c0 in range(0, D_FF, FFN_CHUNK):
        cw = min(FFN_CHUNK, D_FF - c0)
        a = _dot(xm, wa_ref[:, c0:c0 + cw])
        u = _dot(xm, wu_ref[:, c0:c0 + cw])
        h_ref[:, c0:c0 + cw] = (_silu(a) * u).astype(BF16)
    y = ALPHA * x + (0.5 * _mod(gt_ref, batch)) * _dot(h_ref[...], wd_ref[...])
    out = _layer_norm(y, g_ref[...], b_ref[...])
    o_ref[...] = out
    if has_kv:
        ksh_ref, ksc_ref, wkv_ref, cos_ref, slo_ref, shi_ref = kv_in
        kv = _dot(_modulate_bf16(out, _mod(ksh_ref, batch), _mod(ksc_ref, batch)), wkv_ref[...])
        kv_out[0][...] = _rope(kv[:, :KVW], cos_ref[...], slo_ref[...], shi_ref[...])
        kv_out[1][...] = kv[:, KVW:]
    if has_cast:
        for src, dst in zip(cast_in, cast_out):
            dst[...] = src[...].astype(BF16)


def _ffn(x, mods, layer, first, w_up, w_down, ln_g, ln_b, ln_index, n_sample_rows, seq, kv=None, cast=None):
    t = x.shape[0]
    rows = min(FFN_ROWS if kv is None else FFN_ROWS_KV, t)
    steps = t // rows
    x_spec = pl.BlockSpec((rows, D_MODEL), lambda i: (i, 0))
    in_specs = ([x_spec] + _mod_specs(layer, first, 3, n_sample_rows, seq is not None)
                + [_const_spec((D_MODEL, D_FF), (0, 0)), _const_spec((D_MODEL, D_FF), (0, 1)),
                   _resident(w_down.shape)]
                + _ln_specs(ln_index))
    args = [x, mods, mods, mods, w_up, w_up, w_down, ln_g, ln_b]
    out_specs = [x_spec]
    out_shape = [jax.ShapeDtypeStruct((t, D_MODEL), F32)]
    if kv is not None:
        kv_mods, w_kv, tables = kv
        pos_blocks = tables[0].shape[0] // rows
        tab_spec = pl.BlockSpec((rows, LANES), lambda i: (i % pos_blocks, 0))
        kv_spec = pl.BlockSpec((rows, KVW), lambda i: (i, 0))
        in_specs += (_mod_specs(0, 0, 2, n_sample_rows, seq is not None)
                     + [_resident(w_kv.shape), tab_spec, tab_spec, tab_spec])
        args += [kv_mods, kv_mods, w_kv, *tables]
        out_specs += [kv_spec, kv_spec]
        out_shape += [jax.ShapeDtypeStruct((t, KVW), F32)] * 2
    if cast is not None:
        up_f32, down_f32, src_layer = cast
        up_rows = D_MODEL // steps
        down_rows = 2 * D_FF // steps
        assert up_rows % 16 == 0 and down_rows % 16 == 0 and steps % 2 == 0
        in_specs += [pl.BlockSpec((None, up_rows, 2 * D_FF), lambda i: (src_layer, i, 0)),
                     pl.BlockSpec((None, down_rows, D_MODEL), lambda i: (src_layer, i // 2, 0))]
        args += [up_f32, down_f32]
        out_specs += [pl.BlockSpec((up_rows, 2 * D_FF), lambda i: (i, 0)),
                      pl.BlockSpec((down_rows, D_MODEL), lambda i: (i // 2, 0))]
        out_shape += [jax.ShapeDtypeStruct((D_MODEL, 2 * D_FF), BF16),
                      jax.ShapeDtypeStruct((D_FF, D_MODEL), BF16)]
    outs = pl.pallas_call(
        functools.partial(_ffn_kernel, steps_per_seq=None if seq is None else seq // rows,
                          has_kv=kv is not None, has_cast=cast is not None),
        grid=(steps,),
        in_specs=in_specs,
        out_specs=out_specs,
        out_shape=out_shape,
        scratch_shapes=[pltpu.VMEM((rows, D_FF), BF16)],
        compiler_params=_params(("arbitrary",)),
        name="ffn" + ("_kv" if kv is not None else "") + ("_cast" if cast is not None else ""),
    )(*args)
    return outs[0] if len(outs) == 1 else outs


def _head_norm(hh, w_row):
    mu = jnp.mean(hh, axis=-1, keepdims=True)
    d = hh - mu
    var = jnp.mean(d * d, axis=-1, keepdims=True)
    return d * lax.rsqrt(var + LN_EPS) * w_row


def _split3(x, axis):
    hi = x.astype(BF16)
    r1 = x - hi.astype(F32)
    mid = r1.astype(BF16)
    lo = (r1 - mid.astype(F32)).astype(BF16)
    return jnp.concatenate([hi, mid, lo], axis=axis)


def _tile_lanes(t, reps):
    return jnp.concatenate([t] * reps, axis=1)


def _mlstm_prompt_kernel(x_ref, sh_ref, sc_ref, gt_ref, wp_ref, wg_ref, bg_ref, nw_ref, wo_ref, g_ref, b_ref,
                         o_ref, c_out, m_out, proj_ref, rep_ref, y_ref, c_ref, mm_ref):
    batch = pl.program_id(0)
    step = pl.program_id(1)
    size = ML_CHUNK
    assert size == LANES and ML_DK == LANES

    @pl.when(step == 0)
    def _():
        c_ref[...] = jnp.zeros_like(c_ref)
        mm_ref[...] = jnp.zeros_like(mm_ref)

    x = x_ref[...]
    rows = x.shape[0]
    n_chunks = rows // size
    h = _modulate_bf16(x, _mod(sh_ref, batch), _mod(sc_ref, batch))
    gates = _dot(h, wg_ref[...]) + bg_ref[...]

    r = _iota((size, size), 0)
    c = _iota((size, size), 1)
    causal = c <= r
    eye = c == r
    tri3 = _tile_lanes(jnp.where(causal, 1.0, 0.0).astype(BF16), 3)
    n_rep = 2 * ML_HEADS
    spread = jnp.where(_iota((LANES, n_rep * LANES), 1) // LANES == _iota((LANES, n_rep * LANES), 0),
                       1.0, 0.0).astype(BF16)
    spread3 = jnp.concatenate([spread] * 3, axis=0)
    ones3 = jnp.ones((3 * size, LANES), BF16)
    sr = _iota((2 * SUBLANES, LANES), 0)
    sl = _iota((2 * SUBLANES, LANES), 1)
    pick = (jnp.where((sr < ML_HEADS) & (sl == sr), 1.0, 0.0)
            - jnp.where((sr < ML_HEADS) & (sl == sr + ML_HEADS), 1.0, 0.0)
            + jnp.where((sr >= SUBLANES) & (sr < SUBLANES + ML_HEADS) & (sl == sr - SUBLANES + ML_HEADS),
                        1.0, 0.0))
    pick3 = _tile_lanes(pick.astype(BF16), 3)

    log_f = _log_sigmoid(gates)
    bwide = jnp.concatenate([_dot(tri3, _split3(log_f[ci * size:(ci + 1) * size, :], 0))
                             for ci in range(n_chunks)], axis=0)
    src3 = _split3(jnp.where(_iota((rows, LANES), 1) < ML_HEADS, gates, bwide), 1)
    rep_ref[...] = _dot(src3, spread3)
    ab_rows = _nt(pick3, src3)
    a_rows = ab_rows[:SUBLANES, :]
    b_rows = ab_rows[SUBLANES:, :]
    lane_in_chunk = _iota((SUBLANES, rows), 1) % size
    run = a_rows
    shift = 1
    while shift < size:
        run = jnp.maximum(run, jnp.where(lane_in_chunk >= shift, pltpu.roll(run, shift, 1), -jnp.inf))
        shift *= 2
    m_rows = [mm_ref[...]]
    g_rows = []
    for ci in range(n_chunks):
        run_c = run[:, ci * size:(ci + 1) * size]
        g_rows.append(jnp.maximum(run_c, m_rows[-1]))
        end = (ci + 1) * size - 1
        m_end = b_rows[:, end:end + 1] + jnp.maximum(run[:, end:end + 1], m_rows[-1][:, :1])
        m_rows.append(jnp.broadcast_to(m_end, (SUBLANES, LANES)))
    mm_ref[...] = m_rows[-1]

    proj_ref[...] = _dot(h, wp_ref[...])

    states = [c_ref[hd] for hd in range(ML_HEADS)]
    for ci in range(n_chunks):
        lo, hi = ci * size, (ci + 1) * size
        for hd in range(ML_HEADS):
            i_rep = rep_ref[lo:hi, hd * LANES:(hd + 1) * LANES]
            b_rep = rep_ref[lo:hi, (ML_HEADS + hd) * LANES:(ML_HEADS + hd + 1) * LANES]
            m_prev = m_rows[ci][hd:hd + 1, :]
            m_new = m_rows[ci + 1][hd:hd + 1, :]
            a_row = a_rows[hd:hd + 1, lo:hi]
            g_rep = _dot(_split3(jnp.where(eye, g_rows[ci][hd:hd + 1, :], 0.0), 1), ones3)
            w = jnp.where(causal, jnp.exp(a_row - g_rep), 0.0)
            w_init = jnp.exp(m_prev - g_rep)
            floor = jnp.exp(-(b_rep + g_rep))
            b_last = b_rep[size - 1:size, :]
            w_end = jnp.exp(b_last - b_rep + i_rep - m_new)
            decay = jnp.exp(b_last + m_prev - m_new)

            q = proj_ref[lo:hi, hd * ML_DK:(hd + 1) * ML_DK]
            k = proj_ref[lo:hi, HQK + hd * ML_DK:HQK + (hd + 1) * ML_DK] * (ML_DK ** -0.5)
            v = proj_ref[lo:hi, 2 * HQK + hd * ML_DV:2 * HQK + (hd + 1) * ML_DV]
            qb = q.astype(BF16)
            vb = v.astype(BF16)
            c_mat = states[hd][:, :ML_DV]
            n_row = states[hd][0:1, ML_DV:]
            s = _nt(qb, k.astype(BF16)) * w
            num = _dot(s.astype(BF16), vb) + _tile_lanes(w_init, 2) * _dot(qb, c_mat.astype(BF16))
            den = (jnp.sum(s, axis=-1, keepdims=True)
                   + w_init[:, :1] * jnp.sum(q * n_row, axis=-1, keepdims=True))
            hh = num / jnp.maximum(jnp.abs(den), floor[:, :1])
            kw = k * w_end
            c_new = _tile_lanes(decay, 2) * c_mat + _dot(kw.T.astype(BF16), vb)
            n_new = decay * n_row + jnp.sum(kw, axis=0, keepdims=True)
            states[hd] = jnp.concatenate([c_new, jnp.broadcast_to(n_new, (ML_DK, LANES))], axis=1)
            y_ref[lo:hi, hd * ML_DV:(hd + 1) * ML_DV] = _head_norm(hh, nw_ref[:, hd * ML_DV:(hd + 1) * ML_DV])
    for hd in range(ML_HEADS):
        c_ref[hd] = states[hd]

    o_gate = jax.nn.sigmoid(proj_ref[:, 2 * HQK + HV:2 * HQK + 2 * HV])
    y = _dot((y_ref[...] * o_gate).astype(BF16), wo_ref[...])
    o_ref[...] = _layer_norm(ALPHA * x + _mod(gt_ref, batch) * y, g_ref[...], b_ref[...])

    @pl.when(step == pl.num_programs(1) - 1)
    def _():
        c_out[...] = c_ref[...]
        m_out[...] = mm_ref[...]


def _mlstm_weight_specs(wg, bg, nw, wo):
    return [_const_spec((None, D_MODEL, 2 * HQK + 2 * HV), (0, 0, 0)), _resident(wg.shape),
            _resident(bg.shape), _resident(nw.shape), _resident(wo.shape)] + _ln_specs(1)


def _mlstm_prompt(x, mods, w_in, wg, bg, nw, wo, ln_g, ln_b, n_sample_rows):
    bsz, seq, _ = x.shape
    rows = min(ML_ROWS, seq)
    x_spec = pl.BlockSpec((None, rows, D_MODEL), lambda bi, i: (bi, i, 0))
    state_shape = (ML_HEADS, ML_DK, ML_DV + LANES)
    return pl.pallas_call(
        _mlstm_prompt_kernel,
        grid=(bsz, seq // rows),
        in_specs=[x_spec] + _mod_specs(0, 3, 3, n_sample_rows, True) + _mlstm_weight_specs(wg, bg, nw, wo),
        out_specs=[x_spec, pl.BlockSpec((None,) + state_shape, lambda bi, i: (bi, 0, 0, 0)),
                   pl.BlockSpec((None, SUBLANES, LANES), lambda bi, i: (bi, 0, 0))],
        out_shape=[jax.ShapeDtypeStruct((bsz, seq, D_MODEL), F32),
                   jax.ShapeDtypeStruct((bsz,) + state_shape, F32),
                   jax.ShapeDtypeStruct((bsz, SUBLANES, LANES), F32)],
        scratch_shapes=[pltpu.VMEM((rows, 2 * HQK + 2 * HV), F32),
                        pltpu.VMEM((rows, 2 * ML_HEADS * LANES), F32),
                        pltpu.VMEM((rows, HV), F32),
                        pltpu.VMEM(state_shape, F32),
                        pltpu.VMEM((SUBLANES, LANES), F32)],
        compiler_params=_params(("arbitrary", "arbitrary")),
        name="mlstm_prompt",
    )(x, mods, mods, mods, w_in, wg, bg, nw, wo, ln_g, ln_b)


def _mlstm_sample_kernel(x_ref, sh_ref, sc_ref, gt_ref, wp_ref, wg_ref, bg_ref, nw_ref, wo_ref, g_ref, b_ref,
                         c0_ref, n0_ref, m0_ref, o_ref, c_out, n_out, m_out,
                         proj_ref, gate_ref, y_ref, *, seq):
    step = pl.program_id(0)
    grp = c0_ref.shape[0]
    rows = grp * seq

    @pl.when(step == 0)
    def _():
        h = _modulate_bf16(x_ref[...], sh_ref[...], sc_ref[...])
        proj_ref[...] = _dot(h, wp_ref[...])
        gate_ref[...] = _dot(h, wg_ref[...]) + bg_ref[...]

    r = _iota((rows, rows), 0)
    c = _iota((rows, rows), 1)
    same = (r // seq) == (c // seq)
    causal = same & (c <= r)
    eye = r == c
    last_of_row = c == (r // seq) * seq + (seq - 1)
    same_f = same.astype(F32)
    expand = (_iota((rows, grp), 0) // seq == _iota((rows, grp), 1)).astype(F32)
    pick_last = (_iota((grp, rows), 1) == _iota((grp, rows), 0) * seq + (seq - 1)).astype(F32)
    ident = (_iota((ML_DK, ML_DK), 0) == _iota((ML_DK, ML_DK), 1)).astype(F32)
    seq_of_row = _iota((rows, ML_DV), 0) // seq
    col_seq = _iota((ML_DK, rows), 1) // seq
    lane = _iota((rows, LANES), 1)
    to_row = lambda col: jnp.sum(jnp.where(eye, col, 0.0), axis=0, keepdims=True)

    r0 = pl.multiple_of(step * rows, rows)
    gates = gate_ref[pl.ds(r0, rows), :]
    log_f = _log_sigmoid(gates)
    bwide = _dot_exact(causal.astype(F32), log_f)
    bsum = _dot_exact(same_f, log_f)
    m_prev_wide = _dot_exact(expand, m0_ref[...])
    n_rows_wide = _dot_exact(expand, n0_ref[...])
    n_pieces = []
    m_wide = jnp.zeros((rows, LANES), F32)
    for hd in range(ML_HEADS):
        q = proj_ref[pl.ds(r0, rows), hd * ML_DK:(hd + 1) * ML_DK]
        k = proj_ref[pl.ds(r0, rows), HQK + hd * ML_DK:HQK + (hd + 1) * ML_DK] * (ML_DK ** -0.5)
        v = proj_ref[pl.ds(r0, rows), 2 * HQK + hd * ML_DV:2 * HQK + (hd + 1) * ML_DV]
        bcol = bwide[:, ML_HEADS + hd:ML_HEADS + hd + 1]
        b_last = bsum[:, ML_HEADS + hd:ML_HEADS + hd + 1]
        icol = gates[:, hd:hd + 1]
        m_prev = m_prev_wide[:, hd:hd + 1]
        n_rows = n_rows_wide[:, hd * ML_DK:(hd + 1) * ML_DK]
        log_w = jnp.where(causal, bcol - to_row(bcol) + to_row(icol), -jnp.inf)
        log_init = bcol + m_prev
        m_t = jnp.maximum(log_init, jnp.max(log_w, axis=-1, keepdims=True))
        w = jnp.exp(log_w - m_t)
        w_init = jnp.exp(log_init - m_t)
        qb = q.astype(BF16)
        s = _nt(qb, k.astype(BF16)) * w
        inter = None
        for bi in range(grp):
            cand = _dot(qb, c0_ref[bi, hd].astype(BF16))
            inter = cand if inter is None else jnp.where(seq_of_row == bi, cand, inter)
        num = _dot(s.astype(BF16), v.astype(BF16)) + w_init * inter
        den = jnp.sum(s, axis=-1, keepdims=True) + w_init * jnp.sum(q * n_rows, axis=-1, keepdims=True)
        hh = num / jnp.maximum(jnp.abs(den), jnp.exp(-m_t))
        m_new = jnp.sum(jnp.where(last_of_row, to_row(m_t), 0.0), axis=1, keepdims=True)
        w_end = jnp.exp(b_last - bcol + icol - m_new)
        decay = jnp.exp(b_last + m_prev - m_new)
        kw = k * w_end
        kw_t = lax.dot_general(ident, kw, NT_DIMS, preferred_element_type=F32, precision=HIGHEST)
        lhs = jnp.concatenate([jnp.where(col_seq == bi, kw_t, 0.0) for bi in range(grp)], axis=0)
        upd = _dot(lhs.astype(BF16), v.astype(BF16))
        for bi in range(grp):
            c_out[bi, hd] = (decay[bi * seq:bi * seq + 1, :] * c0_ref[bi, hd]
                             + upd[bi * ML_DK:(bi + 1) * ML_DK, :])
        n_pieces.append(decay * n_rows + _dot_exact(same_f, kw))
        m_wide = jnp.where(lane == hd, m_new, m_wide)
        y_ref[pl.ds(r0, rows), hd * ML_DV:(hd + 1) * ML_DV] = _head_norm(
            hh, nw_ref[:, hd * ML_DV:(hd + 1) * ML_DV])
    n_out[...] = _dot_exact(pick_last, jnp.concatenate(n_pieces, axis=1))
    m_out[...] = _dot_exact(pick_last, m_wide)

    @pl.when(step == pl.num_programs(0) - 1)
    def _():
        o_gate = jax.nn.sigmoid(proj_ref[:, 2 * HQK + HV:2 * HQK + 2 * HV])
        y = _dot((y_ref[...] * o_gate).astype(BF16), wo_ref[...])
        o_ref[...] = _layer_norm(ALPHA * x_ref[...] + gt_ref[...] * y, g_ref[...], b_ref[...])


def _mlstm_sample(x, mods, w_in, wg, bg, nw, wo, ln_g, ln_b, c0, n0, m0, seq):
    t = x.shape[0]
    grp = SAMPLE_GROUP
    state_specs = [pl.BlockSpec((grp, ML_HEADS, ML_DK, ML_DV), lambda i: (i, 0, 0, 0)),
                   pl.BlockSpec((grp, HQK), lambda i: (i, 0)),
                   pl.BlockSpec((grp, LANES), lambda i: (i, 0))]
    return pl.pallas_call(
        functools.partial(_mlstm_sample_kernel, seq=seq),
        grid=(t // (grp * seq),),
        in_specs=[_resident(x.shape)] + _mod_specs(0, 3, 3, t, False)
                 + _mlstm_weight_specs(wg, bg, nw, wo) + state_specs,
        out_specs=[pl.BlockSpec((t, D_MODEL), lambda i: (0, 0))] + state_specs,
        out_shape=[jax.ShapeDtypeStruct((t, D_MODEL), F32),
                   jax.ShapeDtypeStruct(c0.shape, F32),
                   jax.ShapeDtypeStruct(n0.shape, F32),
                   jax.ShapeDtypeStruct(m0.shape, F32)],
        scratch_shapes=[pltpu.VMEM((t, 2 * HQK + 2 * HV), F32),
                        pltpu.VMEM((t, LANES), F32),
                        pltpu.VMEM((t, HV), F32)],
        compiler_params=_params(("arbitrary",)),
        name="mlstm_sample",
    )(x, mods, mods, mods, w_in, wg, bg, nw, wo, ln_g, ln_b, c0, n0, m0)


def _sink_column(sink_ref, kh, rows_per_head):
    rid = _iota((ATT_GROUP * rows_per_head, 1), 0)
    col = jnp.full((ATT_GROUP * rows_per_head, 1), sink_ref[kh * ATT_GROUP], F32)
    for gq in range(1, ATT_GROUP):
        col = jnp.where(rid >= gq * rows_per_head, sink_ref[kh * ATT_GROUP + gq], col)
    return col


def _head_lanes(x, kh):
    return jnp.where(_iota(x.shape, 1) // ATT_HD == kh, x, 0.0).astype(BF16)


def _stack_groups(x):
    return jnp.concatenate([x[:, g * KVW:(g + 1) * KVW] for g in range(ATT_GROUP)], axis=0)


def _attn_prompt_kernel(sink_ref, x_ref, sh_ref, sc_ref, gt_ref, wq_ref, cos_ref, slo_ref, shi_ref,
                        kp_ref, kc_ref, vp_ref, vc_ref, wo_ref, g_ref, b_ref, o_ref, a_ref):
    batch = pl.program_id(0)
    step = pl.program_id(1)
    x = x_ref[...]
    rows = x.shape[0]
    n_blk = rows // WINDOW
    q = _rope(_dot(_modulate_bf16(x, _mod(sh_ref, batch), _mod(sc_ref, batch)), wq_ref[...]),
              cos_ref[...], slo_ref[...], shi_ref[...])
    q = (q * (ATT_HD ** -0.5)).astype(BF16)

    keys = jnp.concatenate([kp_ref[...], kc_ref[...]], axis=0)
    vals = jnp.concatenate([vp_ref[...], vc_ref[...]], axis=0)
    k_heads = [_head_lanes(keys, kh) for kh in range(ATT_KV_HEADS)]
    v_heads = [_head_lanes(vals, kh) for kh in range(ATT_KV_HEADS)]

    rid = _iota((ATT_GROUP * WINDOW, WINDOW), 0) % WINDOW
    cid = _iota((ATT_GROUP * WINDOW, WINDOW), 1)
    from_prev = cid > rid
    no_prev = jnp.where(step > 0, 0.0, -jnp.inf)
    sinks = [_sink_column(sink_ref, kh, WINDOW) for kh in range(ATT_KV_HEADS)]

    scores = [[_nt(_stack_groups(q[n * WINDOW:(n + 1) * WINDOW, :]),
                   k_heads[kh][n * WINDOW:(n + 2) * WINDOW, :])
               for kh in range(ATT_KV_HEADS)] for n in range(n_blk)]
    probs = []
    for n in range(n_blk):
        row = []
        for kh in range(ATT_KV_HEADS):
            s_prev = scores[n][kh][:, :WINDOW]
            if n == 0:
                s_prev = s_prev + no_prev
            s = jnp.where(from_prev, s_prev, scores[n][kh][:, WINDOW:])
            mx = jnp.maximum(jnp.max(s, axis=-1, keepdims=True), sinks[kh])
            p = jnp.exp(s - mx)
            den = jnp.sum(p, axis=-1, keepdims=True) + jnp.exp(sinks[kh] - mx)
            pn = p / den
            row.append(jnp.where(from_prev, pn, 0.0).astype(BF16))
            row.append(jnp.where(from_prev, 0.0, pn).astype(BF16))
        probs.append(jnp.concatenate(row, axis=1))
    for n in range(n_blk):
        v_all = jnp.concatenate([v_heads[kh][n * WINDOW:(n + 2) * WINDOW, :]
                                 for kh in range(ATT_KV_HEADS)], axis=0)
        o = _dot(probs[n], v_all)
        for g in range(ATT_GROUP):
            a_ref[n * WINDOW:(n + 1) * WINDOW, g * KVW:(g + 1) * KVW] = o[g * WINDOW:(g + 1) * WINDOW, :]

    y = _dot(a_ref[...].astype(BF16), wo_ref[...])
    o_ref[...] = _layer_norm(ALPHA * x + _mod(gt_ref, batch) * y, g_ref[...], b_ref[...])


def _attn_prompt(x, mods, wq, tables, k, v, sinks, wo, ln_g, ln_b, n_sample_rows):
    bsz, seq, _ = x.shape
    rows = min(ATT_ROWS, seq)
    per = rows // WINDOW
    x_spec = pl.BlockSpec((None, rows, D_MODEL), lambda bi, i: (bi, i, 0))
    cur = pl.BlockSpec((None, rows, KVW), lambda bi, i: (bi, i, 0))
    prev = pl.BlockSpec((None, WINDOW, KVW), lambda bi, i: (bi, jnp.maximum(i * per - 1, 0), 0))
    tab = pl.BlockSpec((rows, LANES), lambda bi, i: (i, 0))
    return pl.pallas_call(
        _attn_prompt_kernel,
        grid=(bsz, seq // rows),
        in_specs=[pl.BlockSpec(memory_space=pltpu.SMEM), x_spec]
                 + _mod_specs(1, 3, 3, n_sample_rows, True)
                 + [_resident(wq.shape), tab, tab, tab, prev, cur, prev, cur, _resident(wo.shape)]
                 + _ln_specs(4),
        out_specs=x_spec,
        out_shape=jax.ShapeDtypeStruct((bsz, seq, D_MODEL), F32),
        scratch_shapes=[pltpu.VMEM((rows, QW), F32)],
        compiler_params=_params(("arbitrary", "arbitrary")),
        name="attn_prompt",
    )(sinks, x, mods, mods, mods, wq, *tables, k, k, v, v, wo, ln_g, ln_b)


def _attn_sample_kernel(sink_ref, x_ref, sh_ref, sc_ref, gt_ref, wq_ref, cos_ref, slo_ref, shi_ref,
                        kn_ref, vn_ref, kb_ref, vb_ref, wo_ref, g_ref, b_ref, o_ref,
                        q_ref, a_ref, *, seq):
    step = pl.program_id(0)
    pair = 2 * seq
    assert pair == SUBLANES
    grp, n_buf, _ = kb_ref.shape

    @pl.when(step == 0)
    def _():
        q = _rope(_dot(_modulate_bf16(x_ref[...], sh_ref[...], sc_ref[...]), wq_ref[...]),
                  cos_ref[...], slo_ref[...], shi_ref[...])
        q_ref[...] = q * (ATT_HD ** -0.5)

    n_rows = ATT_GROUP * pair
    n_all = ATT_KV_HEADS * n_rows
    n_new = 2 * pair
    q_pos = _iota((n_all, n_buf), 0) % seq + PAST_LEN
    k_pos = _iota((n_all, n_buf), 1) + (PAST_LEN - n_buf)
    buf_mask = (k_pos <= q_pos) & (k_pos > q_pos - WINDOW) & (k_pos >= 0)
    q_tok = _iota((n_all, n_new), 0) % seq
    new_col = _iota((n_all, n_new), 1)
    out_lane_head = _iota((n_rows, KVW), 1) // ATT_HD
    low = _iota((n_rows, KVW), 0) % pair < seq
    head_of_row = _iota((n_all, 1), 0) // pair
    sink_col = jnp.full((n_all, 1), sink_ref[0], F32)
    for hq in range(1, ATT_Q_HEADS):
        sink_col = jnp.where(head_of_row >= hq, sink_ref[hq], sink_col)
    pad_rows = jnp.zeros((n_new - pair, KVW), BF16)
    for p in range(grp // 2):
        r0 = pl.multiple_of(step * (grp * seq) + p * pair, pair)
        q_all = _stack_groups(q_ref[pl.ds(r0, pair), :].astype(BF16))
        q_bd = jnp.concatenate([_head_lanes(q_all, kh) for kh in range(ATT_KV_HEADS)], axis=0)
        k_new = jnp.concatenate([kn_ref[pl.ds(r0, pair), :].astype(BF16), pad_rows], axis=0)
        v_new = jnp.concatenate([vn_ref[pl.ds(r0, pair), :].astype(BF16), pad_rows], axis=0)
        s_new_all = _nt(q_bd, k_new)
        halves = []
        for half in range(2):
            bi = 2 * p + half
            tok = new_col - half * seq
            own_new = (tok >= 0) & (tok < seq) & (tok <= q_tok)
            s_buf = jnp.where(buf_mask, _nt(q_bd, kb_ref[bi].astype(BF16)), -jnp.inf)
            s_new = jnp.where(own_new, s_new_all, -jnp.inf)
            mx = jnp.maximum(jnp.maximum(jnp.max(s_buf, axis=-1, keepdims=True),
                                         jnp.max(s_new, axis=-1, keepdims=True)), sink_col)
            e_buf = jnp.exp(s_buf - mx)
            e_new = jnp.exp(s_new - mx)
            den = (jnp.sum(e_buf, axis=-1, keepdims=True) + jnp.sum(e_new, axis=-1, keepdims=True)
                   + jnp.exp(sink_col - mx))
            o_all = (_dot((e_buf / den).astype(BF16), vb_ref[bi].astype(BF16))
                     + _dot((e_new / den).astype(BF16), v_new))
            o_half = None
            for kh in range(ATT_KV_HEADS):
                part = jnp.where(out_lane_head == kh, o_all[kh * n_rows:(kh + 1) * n_rows, :], 0.0)
                o_half = part if o_half is None else o_half + part
            halves.append(o_half)
        o = jnp.where(low, halves[0], halves[1])
        for g in range(ATT_GROUP):
            a_ref[pl.ds(r0, pair), g * KVW:(g + 1) * KVW] = o[g * pair:(g + 1) * pair, :]

    @pl.when(step == pl.num_programs(0) - 1)
    def _():
        y = _dot(a_ref[...].astype(BF16), wo_ref[...])
        o_ref[...] = _layer_norm(ALPHA * x_ref[...] + gt_ref[...] * y, g_ref[...], b_ref[...])


def _attn_sample(x, mods, wq, tables, k_new, v_new, k_buf, v_buf, sinks, wo, ln_g, ln_b, seq):
    t = x.shape[0]
    grp = SAMPLE_GROUP
    buf_spec = pl.BlockSpec((grp,) + k_buf.shape[1:], lambda i: (i, 0, 0))
    return pl.pallas_call(
        functools.partial(_attn_sample_kernel, seq=seq),
        grid=(t // (grp * seq),),
        in_specs=[pl.BlockSpec(memory_space=pltpu.SMEM), _resident(x.shape)]
                 + _mod_specs(1, 3, 3, t, False)
                 + [_resident(wq.shape)] + [_resident(tab.shape) for tab in tables]
                 + [_resident(k_new.shape), _resident(v_new.shape), buf_spec, buf_spec, _resident(wo.shape)]
                 + _ln_specs(4),
        out_specs=pl.BlockSpec((t, D_MODEL), lambda i: (0, 0)),
        out_shape=jax.ShapeDtypeStruct((t, D_MODEL), F32),
        scratch_shapes=[pltpu.VMEM((t, QW), F32), pltpu.VMEM((t, QW), F32)],
        compiler_params=_params(("arbitrary",)),
        name="attn_sample",
    )(sinks, x, mods, mods, mods, wq, *tables, k_new, v_new, k_buf, v_buf, wo, ln_g, ln_b)


def _rope_tables(pos):
    half = ROT_DIM // 2
    inv_freq = ROPE_THETA ** (-jnp.arange(half, dtype=F32) * 2.0 / ROT_DIM)
    ang = pos.astype(F32)[:, None] * inv_freq[None, :]
    cos, sin = jnp.cos(ang), jnp.sin(ang)
    dim = np.arange(LANES) % ATT_HD
    freq = np.arange(half)[:, None]
    lo = (dim[None, :] == freq).astype(np.float32)
    hi = (dim[None, :] == freq + half).astype(np.float32)
    spread = lambda a, m: jnp.dot(a, jnp.asarray(m), precision=HIGHEST)
    return (spread(cos, lo + hi) + jnp.asarray((dim >= ROT_DIM).astype(np.float32)),
            spread(sin, -lo), spread(sin, hi))


def kernel(x_prompt, x_sample, c_prompt, c_sample, state_mlstm_C, state_mlstm_n, state_mlstm_m,
           cache_win_k, cache_win_v, ada_w, ada_b, ffn1_up, ffn1_down, ffn2_up, ffn2_down, ln_g, ln_b,
           ml_w_in, ml_b_gates, ml_norm_w, ml_w_out, kv_ada_w, kv_ada_b, kv_w,
           att_w_q, att_sinks, att_w_o):
    bp, sp, _ = x_prompt.shape
    bs, ss, _ = x_sample.shape
    n_buf = cache_win_k.shape[1]
    ts = bs * ss

    c_all = jnp.concatenate([jnp.repeat(c_sample, ss, axis=0), c_prompt,
                             jnp.zeros((-bp % SUBLANES, D_MODEL), F32)], axis=0)
    mods = _ada(c_all, ada_w, ada_b[:, None, :])
    kv_mods = _ada(c_all, kv_ada_w[None], kv_ada_b[None, None, :])

    bf = lambda w: w.astype(BF16)
    ffn_f32 = [(ffn1_up, ffn1_down, 0), (ffn2_up, ffn2_down, 0), (ffn1_up, ffn1_down, 1), (ffn2_up, ffn2_down, 1)]
    ffn_w = [(bf(ffn1_up[0]), bf(ffn1_down[0]))]
    w_in = bf(ml_w_in)
    gate_pad = LANES - 2 * ML_HEADS
    w_gate = bf(jnp.pad(ml_w_in[0, :, 2 * HQK + 2 * HV:], ((0, 0), (0, gate_pad))))
    b_gate = jnp.pad(ml_b_gates[0], (0, gate_pad))[None, :]
    norm_w = ml_norm_w[0][None, :]
    w_out = bf(ml_w_out[0])
    w_kv = bf(kv_w)
    w_q = bf(att_w_q[0].reshape(D_MODEL, ATT_KV_HEADS, ATT_GROUP, ATT_HD)
             .transpose(0, 2, 1, 3).reshape(D_MODEL, QW))
    w_o = bf(att_w_o[0].reshape(ATT_KV_HEADS, ATT_GROUP, ATT_HD, D_MODEL)
             .transpose(1, 0, 2, 3).reshape(QW, D_MODEL))
    sinks = att_sinks[0]
    lng = ln_g.reshape(DEPTH * 3, 1, D_MODEL)
    lnb = ln_b.reshape(DEPTH * 3, 1, D_MODEL)

    tab_p = _rope_tables(jnp.arange(sp, dtype=jnp.int32))
    tab_s = _rope_tables(jnp.tile(PAST_LEN + jnp.arange(ss, dtype=jnp.int32), bs))

    xp = x_prompt.reshape(bp * sp, D_MODEL)
    xp, *cast = _ffn(xp, mods, 0, 0, *ffn_w[0], lng, lnb, 0, ts, sp, cast=ffn_f32[1])
    ffn_w.append(tuple(cast))
    xp, state_p, m_p = _mlstm_prompt(xp.reshape(bp, sp, D_MODEL), mods, w_in, w_gate, b_gate,
                                     norm_w, w_out, lng, lnb, ts)
    xp, k_p, v_p, *cast = _ffn(xp.reshape(bp * sp, D_MODEL), mods, 0, 6, *ffn_w[1], lng, lnb, 2, ts, sp,
                               kv=(kv_mods, w_kv, tab_p), cast=ffn_f32[2])
    ffn_w.append(tuple(cast))
    k_p = k_p.reshape(bp, sp, KVW)
    v_p = v_p.reshape(bp, sp, KVW)
    xp, *cast = _ffn(xp, mods, 1, 0, *ffn_w[2], lng, lnb, 3, ts, sp, cast=ffn_f32[3])
    ffn_w.append(tuple(cast))
    xp = _attn_prompt(xp.reshape(bp, sp, D_MODEL), mods, w_q, tab_p, k_p, v_p, sinks, w_o, lng, lnb, ts)
    xp = _ffn(xp.reshape(bp * sp, D_MODEL), mods, 1, 6, *ffn_w[3], lng, lnb, 5, ts, sp)
    y_prompt = xp.reshape(bp, sp, D_MODEL)
    keep = min(WINDOW, sp)
    win_k_p = k_p[:, sp - keep:].reshape(bp, keep, ATT_KV_HEADS, ATT_HD)
    win_v_p = v_p[:, sp - keep:].reshape(bp, keep, ATT_KV_HEADS, ATT_HD)

    xs = x_sample.reshape(ts, D_MODEL)
    xs = _ffn(xs, mods, 0, 0, *ffn_w[0], lng, lnb, 0, ts, None)
    m0 = jnp.pad(state_mlstm_m[0], ((0, 0), (0, LANES - ML_HEADS)))
    xs, c_s, n_s, m_s = _mlstm_sample(xs, mods, w_in, w_gate, b_gate, norm_w, w_out, lng, lnb,
                                      state_mlstm_C[0], state_mlstm_n[0].reshape(bs, HQK), m0, ss)
    xs, k_s, v_s = _ffn(xs, mods, 0, 6, *ffn_w[1], lng, lnb, 2, ts, None, kv=(kv_mods, w_kv, tab_s))
    xs = _ffn(xs, mods, 1, 0, *ffn_w[2], lng, lnb, 3, ts, None)
    xs = _attn_sample(xs, mods, w_q, tab_s, k_s, v_s,
                      cache_win_k.reshape(bs, n_buf, KVW), cache_win_v.reshape(bs, n_buf, KVW),
                      sinks, w_o, lng, lnb, ss)
    xs = _ffn(xs, mods, 1, 6, *ffn_w[3], lng, lnb, 5, ts, None)
    y_sample = xs.reshape(bs, ss, D_MODEL)
    win_k_s = jnp.concatenate([cache_win_k, k_s.reshape(bs, ss, ATT_KV_HEADS, ATT_HD)], axis=1)[:, -n_buf:]
    win_v_s = jnp.concatenate([cache_win_v, v_s.reshape(bs, ss, ATT_KV_HEADS, ATT_HD)], axis=1)[:, -n_buf:]

    return (y_prompt, y_sample,
            state_p[None, ..., :ML_DV], state_p[None, :, :, 0, ML_DV:], m_p[None, :, :ML_HEADS, 0],
            win_k_p, win_v_p,
            c_s[None], n_s.reshape(1, bs, ML_HEADS, ML_DK), m_s[None, :, :ML_HEADS],
            win_k_s, win_v_s)
```

```python
import functools

import numpy as np
import jax
import jax.numpy as jnp
from jax import lax
from jax.experimental import pallas as pl
from jax.experimental.pallas import tpu as pltpu

F32 = jnp.float32
BF16 = jnp.bfloat16

D_MODEL = 1024
DEPTH = 2
PAST_LEN = 8192
ML_HEADS = 4
ML_DK = D_MODEL // 8
ML_DV = D_MODEL // 4
ATT_Q_HEADS = 16
ATT_KV_HEADS = 4
ATT_GROUP = ATT_Q_HEADS // ATT_KV_HEADS
ATT_HD = 64
WINDOW = 128
ROT_DIM = ATT_HD // 4
ROPE_THETA = 500000.0
D_FF = 2816
ALPHA = (2 * DEPTH) ** 0.25
LN_EPS = 1e-5
N_MOD = 9
HQK = ML_HEADS * ML_DK
HV = ML_HEADS * ML_DV
KVW = ATT_KV_HEADS * ATT_HD
QW = ATT_Q_HEADS * ATT_HD

LANES = 128
SUBLANES = 8
VMEM_LIMIT_BYTES = 56 * 1024 * 1024

FFN_ROWS = 1024
FFN_ROWS_KV = 512
FFN_CHUNK = 256
ML_ROWS = 1024
ML_CHUNK = 128
ATT_ROWS = 1024
ADA_COLS = 1024
SAMPLE_GROUP = 8

NT_DIMS = (((1,), (1,)), ((), ()))
HIGHEST = lax.Precision.HIGHEST


def _params(semantics):
    return pltpu.CompilerParams(dimension_semantics=semantics, vmem_limit_bytes=VMEM_LIMIT_BYTES)


def _const_spec(block, index):
    return pl.BlockSpec(block, lambda *_: index, pipeline_mode=pl.Buffered(1))


def _resident(shape):
    return _const_spec(shape, (0,) * len(shape))


def _dot(a, b):
    return jnp.dot(a, b, preferred_element_type=F32)


def _dot_exact(a, b):
    return jnp.dot(a, b, preferred_element_type=F32, precision=HIGHEST)


def _nt(a, b):
    return lax.dot_general(a, b, NT_DIMS, preferred_element_type=F32)


def _iota(shape, dim):
    return lax.broadcasted_iota(jnp.int32, shape, dim)


def _silu(x):
    return x * jax.nn.sigmoid(x)


def _log_sigmoid(x):
    return jnp.minimum(x, 0.0) - jnp.log1p(jnp.exp(-jnp.abs(x)))


def _layer_norm(y, g, b):
    mu = jnp.mean(y, axis=-1, keepdims=True)
    d = y - mu
    var = jnp.mean(d * d, axis=-1, keepdims=True)
    return d * lax.rsqrt(var + LN_EPS) * g + b


def _mod(ref, batch):
    return ref[...] if batch is None else ref[pl.ds(batch, 1), :]


def _modulate_bf16(x, shift, scale):
    return (x * (1.0 + scale) + shift).astype(BF16)


def _rope(x, cos, sin_lo, sin_hi):
    width = x.shape[1]
    reps = width // LANES
    tile = lambda t: jnp.concatenate([t] * reps, axis=1) if reps > 1 else t
    x_up = pltpu.roll(x, width - ROT_DIM // 2, 1)
    x_dn = pltpu.roll(x, ROT_DIM // 2, 1)
    return x * tile(cos) + x_up * tile(sin_lo) + x_dn * tile(sin_hi)


def _mod_specs(layer, first, count, n_sample_rows, prompt):
    if prompt:
        blk = n_sample_rows // SUBLANES
        return [_const_spec((None, SUBLANES, D_MODEL), (layer, blk, first + j)) for j in range(count)]
    return [_const_spec((None, n_sample_rows, D_MODEL), (layer, 0, first + j)) for j in range(count)]


def _ln_specs(index):
    return [_const_spec((None, 1, D_MODEL), (index, 0, 0))] * 2


def _ada_kernel(c_ref, w_ref, b_ref, o_ref):
    ca = _silu(c_ref[...]).astype(BF16)
    o_ref[...] = _dot(ca, w_ref[...].astype(BF16)) + b_ref[...]


def _ada(c, w, b):
    n_l, k, n = w.shape
    m = c.shape[0]
    return pl.pallas_call(
        _ada_kernel,
        grid=(n_l, n // ADA_COLS),
        in_specs=[pl.BlockSpec((m, k), lambda l, j: (0, 0)),
                  pl.BlockSpec((None, k, ADA_COLS), lambda l, j: (l, 0, j)),
                  pl.BlockSpec((None, 1, ADA_COLS), lambda l, j: (l, 0, j))],
        out_specs=pl.BlockSpec((None, m, ADA_COLS), lambda l, j: (l, 0, j)),
        out_shape=jax.ShapeDtypeStruct((n_l, m, n), F32),
        compiler_params=_params(("arbitrary", "arbitrary")),
        name="ada",
    )(c, w, b)


def _ffn_kernel(x_ref, sh_ref, sc_ref, gt_ref, wa_ref, wu_ref, wd_ref, g_ref, b_ref, *rest,
                steps_per_seq, has_kv, n_cast):
    rest = list(rest)
    kv_in = [rest.pop(0) for _ in range(6)] if has_kv else None
    cast_in = [rest.pop(0) for _ in range(n_cast)]
    o_ref = rest.pop(0)
    kv_out = [rest.pop(0) for _ in range(2)] if has_kv else None
    cast_out = [rest.pop(0) for _ in range(n_cast)]
    h_ref, = rest

    batch = None if steps_per_seq is None else pl.program_id(0) // steps_per_seq
    x = x_ref[...]
    xm = _modulate_bf16(x, _mod(sh_ref, batch), _mod(sc_ref, batch))
    for c0 in range(0, D_FF, FFN_CHUNK):
        cw = min(FFN_CHUNK, D_FF - c0)
        a = _dot(xm, wa_ref[:, c0:c0 + cw])
        u = _dot(xm, wu_ref[:, c0:c0 + cw])
        h_ref[:, c0:c0 + cw] = (_silu(a) * u).astype(BF16)
    y = ALPHA * x + (0.5 * _mod(gt_ref, batch)) * _dot(h_ref[...], wd_ref[...])
    out = _layer_norm(y, g_ref[...], b_ref[...])
    o_ref[...] = out
    if has_kv:
        ksh_ref, ksc_ref, wkv_ref, cos_ref, slo_ref, shi_ref = kv_in
        kv = _dot(_modulate_bf16(out, _mod(ksh_ref, batch), _mod(ksc_ref, batch)), wkv_ref[...])
        kv_out[0][...] = _rope(kv[:, :KVW], cos_ref[...], slo_ref[...], shi_ref[...])
        kv_out[1][...] = kv[:, KVW:]
    for src, dst in zip(cast_in, cast_out):
        dst[...] = src[...].astype(BF16)


def _ffn(x, mods, layer, first, w_up, w_down, ln_g, ln_b, ln_index, n_sample_rows, seq, kv=None, cast=None):
    t = x.shape[0]
    rows = min(FFN_ROWS if kv is None else FFN_ROWS_KV, t)
    steps = t // rows
    x_spec = pl.BlockSpec((rows, D_MODEL), lambda i: (i, 0))
    in_specs = ([x_spec] + _mod_specs(layer, first, 3, n_sample_rows, seq is not None)
                + [_const_spec((D_MODEL, D_FF), (0, 0)), _const_spec((D_MODEL, D_FF), (0, 1)),
                   _resident(w_down.shape)]
                + _ln_specs(ln_index))
    args = [x, mods, mods, mods, w_up, w_up, w_down, ln_g, ln_b]
    out_specs = [x_spec]
    out_shape = [jax.ShapeDtypeStruct((t, D_MODEL), F32)]
    if kv is not None:
        kv_mods, w_kv, tables = kv
        pos_blocks = tables[0].shape[0] // rows
        tab_spec = pl.BlockSpec((rows, LANES), lambda i: (i % pos_blocks, 0))
        kv_spec = pl.BlockSpec((rows, KVW), lambda i: (i, 0))
        in_specs += (_mod_specs(0, 0, 2, n_sample_rows, seq is not None)
                     + [_resident(w_kv.shape), tab_spec, tab_spec, tab_spec])
        args += [kv_mods, kv_mods, w_kv, *tables]
        out_specs += [kv_spec, kv_spec]
        out_shape += [jax.ShapeDtypeStruct((t, KVW), F32)] * 2
    n_cast = 0
    if cast is not None:
        up_f32, down_f32, src_layer, *extra = cast
        up_rows = D_MODEL // steps
        down_rows = 2 * D_FF // steps
        assert up_rows % 16 == 0 and down_rows % 16 == 0 and steps % 2 == 0
        in_specs += [pl.BlockSpec((None, up_rows, 2 * D_FF), lambda i: (src_layer, i, 0)),
                     pl.BlockSpec((None, down_rows, D_MODEL), lambda i: (src_layer, i // 2, 0))]
        args += [up_f32, down_f32]
        out_specs += [pl.BlockSpec((up_rows, 2 * D_FF), lambda i: (i, 0)),
                      pl.BlockSpec((down_rows, D_MODEL), lambda i: (i // 2, 0))]
        out_shape += [jax.ShapeDtypeStruct((D_MODEL, 2 * D_FF), BF16),
                      jax.ShapeDtypeStruct((D_FF, D_MODEL), BF16)]
        for w in extra:
            spec = pl.BlockSpec((None, up_rows, w.shape[2]), lambda i: (0, i, 0))
            in_specs.append(spec)
            args.append(w)
            out_specs.append(spec)
            out_shape.append(jax.ShapeDtypeStruct(w.shape, BF16))
        n_cast = 2 + len(extra)
    outs = pl.pallas_call(
        functools.partial(_ffn_kernel, steps_per_seq=None if seq is None else seq // rows,
                          has_kv=kv is not None, n_cast=n_cast),
        grid=(steps,),
        in_specs=in_specs,
        out_specs=out_specs,
        out_shape=out_shape,
        scratch_shapes=[pltpu.VMEM((rows, D_FF), BF16)],
        compiler_params=_params(("arbitrary",)),
        name="ffn" + ("_kv" if kv is not None else "") + ("_cast" if cast is not None else ""),
    )(*args)
    return outs[0] if len(outs) == 1 else outs


def _head_norm(hh, w_row):
    mu = jnp.mean(hh, axis=-1, keepdims=True)
    d = hh - mu
    var = jnp.mean(d * d, axis=-1, keepdims=True)
    return d * lax.rsqrt(var + LN_EPS) * w_row


def _split3(x, axis):
    hi = x.astype(BF16)
    r1 = x - hi.astype(F32)
    mid = r1.astype(BF16)
    lo = (r1 - mid.astype(F32)).astype(BF16)
    return jnp.concatenate([hi, mid, lo], axis=axis)


def _tile_lanes(t, reps):
    return jnp.concatenate([t] * reps, axis=1)


def _mlstm_prompt_kernel(x_ref, sh_ref, sc_ref, gt_ref, wp_ref, wg_ref, bg_ref, nw_ref, wo_ref, g_ref, b_ref,
                         o_ref, c_out, m_out, proj_ref, rep_ref, y_ref, c_ref, mm_ref):
    batch = pl.program_id(0)
    step = pl.program_id(1)
    size = ML_CHUNK
    assert size == LANES and ML_DK == LANES

    @pl.when(step == 0)
    def _():
        c_ref[...] = jnp.zeros_like(c_ref)
        mm_ref[...] = jnp.zeros_like(mm_ref)

    x = x_ref[...]
    rows = x.shape[0]
    n_chunks = rows // size
    h = _modulate_bf16(x, _mod(sh_ref, batch), _mod(sc_ref, batch))
    gates = _dot(h, wg_ref[...]) + bg_ref[...]

    r = _iota((size, size), 0)
    c = _iota((size, size), 1)
    causal = c <= r
    eye = c == r
    tri3 = _tile_lanes(jnp.where(causal, 1.0, 0.0).astype(BF16), 3)
    n_rep = 2 * ML_HEADS
    spread = jnp.where(_iota((LANES, n_rep * LANES), 1) // LANES == _iota((LANES, n_rep * LANES), 0),
                       1.0, 0.0).astype(BF16)
    spread3 = jnp.concatenate([spread] * 3, axis=0)
    ones3 = jnp.ones((3 * size, LANES), BF16)
    sr = _iota((2 * SUBLANES, LANES), 0)
    sl = _iota((2 * SUBLANES, LANES), 1)
    pick = (jnp.where((sr < ML_HEADS) & (sl == sr), 1.0, 0.0)
            - jnp.where((sr < ML_HEADS) & (sl == sr + ML_HEADS), 1.0, 0.0)
            + jnp.where((sr >= SUBLANES) & (sr < SUBLANES + ML_HEADS) & (sl == sr - SUBLANES + ML_HEADS),
                        1.0, 0.0))
    pick3 = _tile_lanes(pick.astype(BF16), 3)

    log_f = _log_sigmoid(gates)
    bwide = jnp.concatenate([_dot(tri3, _split3(log_f[ci * size:(ci + 1) * size, :], 0))
                             for ci in range(n_chunks)], axis=0)
    src3 = _split3(jnp.where(_iota((rows, LANES), 1) < ML_HEADS, gates, bwide), 1)
    rep_ref[...] = _dot(src3, spread3)
    ab_rows = _nt(pick3, src3)
    a_rows = ab_rows[:SUBLANES, :]
    b_rows = ab_rows[SUBLANES:, :]
    lane_in_chunk = _iota((SUBLANES, rows), 1) % size
    run = a_rows
    shift = 1
    while shift < size:
        run = jnp.maximum(run, jnp.where(lane_in_chunk >= shift, pltpu.roll(run, shift, 1), -jnp.inf))
        shift *= 2
    m_rows = [mm_ref[...]]
    g_rows = []
    for ci in range(n_chunks):
        run_c = run[:, ci * size:(ci + 1) * size]
        g_rows.append(jnp.maximum(run_c, m_rows[-1]))
        end = (ci + 1) * size - 1
        m_end = b_rows[:, end:end + 1] + jnp.maximum(run[:, end:end + 1], m_rows[-1][:, :1])
        m_rows.append(jnp.broadcast_to(m_end, (SUBLANES, LANES)))
    mm_ref[...] = m_rows[-1]

    proj_ref[...] = _dot(h, wp_ref[...])

    states = [c_ref[hd] for hd in range(ML_HEADS)]
    for ci in range(n_chunks):
        lo, hi = ci * size, (ci + 1) * size
        for hd in range(ML_HEADS):
            i_rep = rep_ref[lo:hi, hd * LANES:(hd + 1) * LANES]
            b_rep = rep_ref[lo:hi, (ML_HEADS + hd) * LANES:(ML_HEADS + hd + 1) * LANES]
            m_prev = m_rows[ci][hd:hd + 1, :]
            m_new = m_rows[ci + 1][hd:hd + 1, :]
            a_row = a_rows[hd:hd + 1, lo:hi]
            g_rep = _dot(_split3(jnp.where(eye, g_rows[ci][hd:hd + 1, :], 0.0), 1), ones3)
            w = jnp.where(causal, jnp.exp(a_row - g_rep), 0.0)
            w_init = jnp.exp(m_prev - g_rep)
            floor = jnp.exp(-(b_rep + g_rep))
            b_last = b_rep[size - 1:size, :]
            w_end = jnp.exp(b_last - b_rep + i_rep - m_new)
            decay = jnp.exp(b_last + m_prev - m_new)

            q = proj_ref[lo:hi, hd * ML_DK:(hd + 1) * ML_DK]
            k = proj_ref[lo:hi, HQK + hd * ML_DK:HQK + (hd + 1) * ML_DK] * (ML_DK ** -0.5)
            v = proj_ref[lo:hi, 2 * HQK + hd * ML_DV:2 * HQK + (hd + 1) * ML_DV]
            qb = q.astype(BF16)
            vb = v.astype(BF16)
            c_mat = states[hd][:, :ML_DV]
            n_row = states[hd][0:1, ML_DV:]
            s = _nt(qb, k.astype(BF16)) * w
            num = _dot(s.astype(BF16), vb) + _tile_lanes(w_init, 2) * _dot(qb, c_mat.astype(BF16))
            den = (jnp.sum(s, axis=-1, keepdims=True)
                   + w_init[:, :1] * jnp.sum(q * n_row, axis=-1, keepdims=True))
            hh = num / jnp.maximum(jnp.abs(den), floor[:, :1])
            kw = k * w_end
            c_new = _tile_lanes(decay, 2) * c_mat + _dot(kw.T.astype(BF16), vb)
            n_new = decay * n_row + jnp.sum(kw, axis=0, keepdims=True)
            states[hd] = jnp.concatenate([c_new, jnp.broadcast_to(n_new, (ML_DK, LANES))], axis=1)
            y_ref[lo:hi, hd * ML_DV:(hd + 1) * ML_DV] = _head_norm(hh, nw_ref[:, hd * ML_DV:(hd + 1) * ML_DV])
    for hd in range(ML_HEADS):
        c_ref[hd] = states[hd]

    o_gate = jax.nn.sigmoid(proj_ref[:, 2 * HQK + HV:2 * HQK + 2 * HV])
    y = _dot((y_ref[...] * o_gate).astype(BF16), wo_ref[...])
    o_ref[...] = _layer_norm(ALPHA * x + _mod(gt_ref, batch) * y, g_ref[...], b_ref[...])

    @pl.when(step == pl.num_programs(1) - 1)
    def _():
        c_out[...] = c_ref[...]
        m_out[...] = mm_ref[...]


def _mlstm_weight_specs(wg, bg, nw, wo):
    return [_const_spec((None, D_MODEL, 2 * HQK + 2 * HV), (0, 0, 0)), _resident(wg.shape),
            _resident(bg.shape), _resident(nw.shape), _resident(wo.shape)] + _ln_specs(1)


def _mlstm_prompt(x, mods, w_in, wg, bg, nw, wo, ln_g, ln_b, n_sample_rows):
    bsz, seq, _ = x.shape
    rows = min(ML_ROWS, seq)
    x_spec = pl.BlockSpec((None, rows, D_MODEL), lambda bi, i: (bi, i, 0))
    state_shape = (ML_HEADS, ML_DK, ML_DV + LANES)
    return pl.pallas_call(
        _mlstm_prompt_kernel,
        grid=(bsz, seq // rows),
        in_specs=[x_spec] + _mod_specs(0, 3, 3, n_sample_rows, True) + _mlstm_weight_specs(wg, bg, nw, wo),
        out_specs=[x_spec, pl.BlockSpec((None,) + state_shape, lambda bi, i: (bi, 0, 0, 0)),
                   pl.BlockSpec((None, SUBLANES, LANES), lambda bi, i: (bi, 0, 0))],
        out_shape=[jax.ShapeDtypeStruct((bsz, seq, D_MODEL), F32),
                   jax.ShapeDtypeStruct((bsz,) + state_shape, F32),
                   jax.ShapeDtypeStruct((bsz, SUBLANES, LANES), F32)],
        scratch_shapes=[pltpu.VMEM((rows, 2 * HQK + 2 * HV), F32),
                        pltpu.VMEM((rows, 2 * ML_HEADS * LANES), F32),
                        pltpu.VMEM((rows, HV), F32),
                        pltpu.VMEM(state_shape, F32),
                        pltpu.VMEM((SUBLANES, LANES), F32)],
        compiler_params=_params(("arbitrary", "arbitrary")),
        name="mlstm_prompt",
    )(x, mods, mods, mods, w_in, wg, bg, nw, wo, ln_g, ln_b)


def _mlstm_sample_kernel(x_ref, sh_ref, sc_ref, gt_ref, wp_ref, wg_ref, bg_ref, nw_ref, wo_ref, g_ref, b_ref,
                         c0_ref, n0_ref, m0_ref, o_ref, c_out, n_out, m_out,
                         proj_ref, gate_ref, y_ref, *, seq):
    step = pl.program_id(0)
    grp = c0_ref.shape[0]
    rows = grp * seq

    @pl.when(step == 0)
    def _():
        h = _modulate_bf16(x_ref[...], sh_ref[...], sc_ref[...])
        proj_ref[...] = _dot(h, wp_ref[...])
        gate_ref[...] = _dot(h, wg_ref[...]) + bg_ref[...]

    r = _iota((rows, rows), 0)
    c = _iota((rows, rows), 1)
    same = (r // seq) == (c // seq)
    causal = same & (c <= r)
    eye = r == c
    last_of_row = c == (r // seq) * seq + (seq - 1)
    same_f = same.astype(F32)
    expand = (_iota((rows, grp), 0) // seq == _iota((rows, grp), 1)).astype(F32)
    pick_last = (_iota((grp, rows), 1) == _iota((grp, rows), 0) * seq + (seq - 1)).astype(F32)
    ident = (_iota((ML_DK, ML_DK), 0) == _iota((ML_DK, ML_DK), 1)).astype(F32)
    seq_of_row = _iota((rows, ML_DV), 0) // seq
    col_seq = _iota((ML_DK, rows), 1) // seq
    lane = _iota((rows, LANES), 1)
    to_row = lambda col: jnp.sum(jnp.where(eye, col, 0.0), axis=0, keepdims=True)

    r0 = pl.multiple_of(step * rows, rows)
    gates = gate_ref[pl.ds(r0, rows), :]
    log_f = _log_sigmoid(gates)
    bwide = _dot_exact(causal.astype(F32), log_f)
    bsum = _dot_exact(same_f, log_f)
    m_prev_wide = _dot_exact(expand, m0_ref[...])
    n_rows_wide = _dot_exact(expand, n0_ref[...])
    n_pieces = []
    m_wide = jnp.zeros((rows, LANES), F32)
    for hd in range(ML_HEADS):
        q = proj_ref[pl.ds(r0, rows), hd * ML_DK:(hd + 1) * ML_DK]
        k = proj_ref[pl.ds(r0, rows), HQK + hd * ML_DK:HQK + (hd + 1) * ML_DK] * (ML_DK ** -0.5)
        v = proj_ref[pl.ds(r0, rows), 2 * HQK + hd * ML_DV:2 * HQK + (hd + 1) * ML_DV]
        bcol = bwide[:, ML_HEADS + hd:ML_HEADS + hd + 1]
        b_last = bsum[:, ML_HEADS + hd:ML_HEADS + hd + 1]
        icol = gates[:, hd:hd + 1]
        m_prev = m_prev_wide[:, hd:hd + 1]
        n_rows = n_rows_wide[:, hd * ML_DK:(hd + 1) * ML_DK]
        log_w = jnp.where(causal, bcol - to_row(bcol) + to_row(icol), -jnp.inf)
        log_init = bcol + m_prev
        m_t = jnp.maximum(log_init, jnp.max(log_w, axis=-1, keepdims=True))
        w = jnp.exp(log_w - m_t)
        w_init = jnp.exp(log_init - m_t)
        qb = q.astype(BF16)
        s = _nt(qb, k.astype(BF16)) * w
        inter = None
        for bi in range(grp):
            cand = _dot(qb, c0_ref[bi, hd].astype(BF16))
            inter = cand if inter is None else jnp.where(seq_of_row == bi, cand, inter)
        num = _dot(s.astype(BF16), v.astype(BF16)) + w_init * inter
        den = jnp.sum(s, axis=-1, keepdims=True) + w_init * jnp.sum(q * n_rows, axis=-1, keepdims=True)
        hh = num / jnp.maximum(jnp.abs(den), jnp.exp(-m_t))
        m_new = jnp.sum(jnp.where(last_of_row, to_row(m_t), 0.0), axis=1, keepdims=True)
        w_end = jnp.exp(b_last - bcol + icol - m_new)
        decay = jnp.exp(b_last + m_prev - m_new)
        kw = k * w_end
        kw_t = lax.dot_general(ident, kw, NT_DIMS, preferred_element_type=F32, precision=HIGHEST)
        lhs = jnp.concatenate([jnp.where(col_seq == bi, kw_t, 0.0) for bi in range(grp)], axis=0)
        upd = _dot(lhs.astype(BF16), v.astype(BF16))
        for bi in range(grp):
            c_out[bi, hd] = (decay[bi * seq:bi * seq + 1, :] * c0_ref[bi, hd]
                             + upd[bi * ML_DK:(bi + 1) * ML_DK, :])
        n_pieces.append(decay * n_rows + _dot_exact(same_f, kw))
        m_wide = jnp.where(lane == hd, m_new, m_wide)
        y_ref[pl.ds(r0, rows), hd * ML_DV:(hd + 1) * ML_DV] = _head_norm(
            hh, nw_ref[:, hd * ML_DV:(hd + 1) * ML_DV])
    n_out[...] = _dot_exact(pick_last, jnp.concatenate(n_pieces, axis=1))
    m_out[...] = _dot_exact(pick_last, m_wide)

    @pl.when(step == pl.num_programs(0) - 1)
    def _():
        o_gate = jax.nn.sigmoid(proj_ref[:, 2 * HQK + HV:2 * HQK + 2 * HV])
        y = _dot((y_ref[...] * o_gate).astype(BF16), wo_ref[...])
        o_ref[...] = _layer_norm(ALPHA * x_ref[...] + gt_ref[...] * y, g_ref[...], b_ref[...])


def _mlstm_sample(x, mods, w_in, wg, bg, nw, wo, ln_g, ln_b, c0, n0, m0, seq):
    t = x.shape[0]
    grp = SAMPLE_GROUP
    state_specs = [pl.BlockSpec((grp, ML_HEADS, ML_DK, ML_DV), lambda i: (i, 0, 0, 0)),
                   pl.BlockSpec((grp, HQK), lambda i: (i, 0)),
                   pl.BlockSpec((grp, LANES), lambda i: (i, 0))]
    return pl.pallas_call(
        functools.partial(_mlstm_sample_kernel, seq=seq),
        grid=(t // (grp * seq),),
        in_specs=[_resident(x.shape)] + _mod_specs(0, 3, 3, t, False)
                 + _mlstm_weight_specs(wg, bg, nw, wo) + state_specs,
        out_specs=[pl.BlockSpec((t, D_MODEL), lambda i: (0, 0))] + state_specs,
        out_shape=[jax.ShapeDtypeStruct((t, D_MODEL), F32),
                   jax.ShapeDtypeStruct(c0.shape, F32),
                   jax.ShapeDtypeStruct(n0.shape, F32),
                   jax.ShapeDtypeStruct(m0.shape, F32)],
        scratch_shapes=[pltpu.VMEM((t, 2 * HQK + 2 * HV), F32),
                        pltpu.VMEM((t, LANES), F32),
                        pltpu.VMEM((t, HV), F32)],
        compiler_params=_params(("arbitrary",)),
        name="mlstm_sample",
    )(x, mods, mods, mods, w_in, wg, bg, nw, wo, ln_g, ln_b, c0, n0, m0)


def _sink_column(sink_ref, kh, rows_per_head):
    rid = _iota((ATT_GROUP * rows_per_head, 1), 0)
    col = jnp.full((ATT_GROUP * rows_per_head, 1), sink_ref[kh * ATT_GROUP], F32)
    for gq in range(1, ATT_GROUP):
        col = jnp.where(rid >= gq * rows_per_head, sink_ref[kh * ATT_GROUP + gq], col)
    return col


def _head_lanes(x, kh):
    return jnp.where(_iota(x.shape, 1) // ATT_HD == kh, x, 0.0).astype(BF16)


def _stack_groups(x):
    return jnp.concatenate([x[:, g * KVW:(g + 1) * KVW] for g in range(ATT_GROUP)], axis=0)


def _attn_prompt_kernel(sink_ref, x_ref, sh_ref, sc_ref, gt_ref, wq_ref, cos_ref, slo_ref, shi_ref,
                        kp_ref, kc_ref, vp_ref, vc_ref, wo_ref, g_ref, b_ref, o_ref, a_ref):
    batch = pl.program_id(0)
    step = pl.program_id(1)
    x = x_ref[...]
    rows = x.shape[0]
    n_blk = rows // WINDOW
    q = _rope(_dot(_modulate_bf16(x, _mod(sh_ref, batch), _mod(sc_ref, batch)), wq_ref[...]),
              cos_ref[...], slo_ref[...], shi_ref[...])
    q = (q * (ATT_HD ** -0.5)).astype(BF16)

    keys = jnp.concatenate([kp_ref[...], kc_ref[...]], axis=0)
    vals = jnp.concatenate([vp_ref[...], vc_ref[...]], axis=0)
    k_heads = [_head_lanes(keys, kh) for kh in range(ATT_KV_HEADS)]
    v_heads = [_head_lanes(vals, kh) for kh in range(ATT_KV_HEADS)]

    rid = _iota((ATT_GROUP * WINDOW, WINDOW), 0) % WINDOW
    cid = _iota((ATT_GROUP * WINDOW, WINDOW), 1)
    from_prev = cid > rid
    no_prev = jnp.where(step > 0, 0.0, -jnp.inf)
    sinks = [_sink_column(sink_ref, kh, WINDOW) for kh in range(ATT_KV_HEADS)]

    scores = [[_nt(_stack_groups(q[n * WINDOW:(n + 1) * WINDOW, :]),
                   k_heads[kh][n * WINDOW:(n + 2) * WINDOW, :])
               for kh in range(ATT_KV_HEADS)] for n in range(n_blk)]
    probs = []
    for n in range(n_blk):
        row = []
        for kh in range(ATT_KV_HEADS):
            s_prev = scores[n][kh][:, :WINDOW]
            if n == 0:
                s_prev = s_prev + no_prev
            s = jnp.where(from_prev, s_prev, scores[n][kh][:, WINDOW:])
            mx = jnp.maximum(jnp.max(s, axis=-1, keepdims=True), sinks[kh])
            p = jnp.exp(s - mx)
            den = jnp.sum(p, axis=-1, keepdims=True) + jnp.exp(sinks[kh] - mx)
            pn = p / den
            row.append(jnp.where(from_prev, pn, 0.0).astype(BF16))
            row.append(jnp.where(from_prev, 0.0, pn).astype(BF16))
        probs.append(jnp.concatenate(row, axis=1))
    for n in range(n_blk):
        v_all = jnp.concatenate([v_heads[kh][n * WINDOW:(n + 2) * WINDOW, :]
                                 for kh in range(ATT_KV_HEADS)], axis=0)
        o = _dot(probs[n], v_all)
        for g in range(ATT_GROUP):
            a_ref[n * WINDOW:(n + 1) * WINDOW, g * KVW:(g + 1) * KVW] = o[g * WINDOW:(g + 1) * WINDOW, :]

    y = _dot(a_ref[...].astype(BF16), wo_ref[...])
    o_ref[...] = _layer_norm(ALPHA * x + _mod(gt_ref, batch) * y, g_ref[...], b_ref[...])


def _attn_prompt(x, mods, wq, tables, k, v, sinks, wo, ln_g, ln_b, n_sample_rows):
    bsz, seq, _ = x.shape
    rows = min(ATT_ROWS, seq)
    per = rows // WINDOW
    x_spec = pl.BlockSpec((None, rows, D_MODEL), lambda bi, i: (bi, i, 0))
    cur = pl.BlockSpec((None, rows, KVW), lambda bi, i: (bi, i, 0))
    prev = pl.BlockSpec((None, WINDOW, KVW), lambda bi, i: (bi, jnp.maximum(i * per - 1, 0), 0))
    tab = pl.BlockSpec((rows, LANES), lambda bi, i: (i, 0))
    return pl.pallas_call(
        _attn_prompt_kernel,
        grid=(bsz, seq // rows),
        in_specs=[pl.BlockSpec(memory_space=pltpu.SMEM), x_spec]
                 + _mod_specs(1, 3, 3, n_sample_rows, True)
                 + [_resident(wq.shape), tab, tab, tab, prev, cur, prev, cur, _resident(wo.shape)]
                 + _ln_specs(4),
        out_specs=x_spec,
        out_shape=jax.ShapeDtypeStruct((bsz, seq, D_MODEL), F32),
        scratch_shapes=[pltpu.VMEM((rows, QW), F32)],
        compiler_params=_params(("arbitrary", "arbitrary")),
        name="attn_prompt",
    )(sinks, x, mods, mods, mods, wq, *tables, k, k, v, v, wo, ln_g, ln_b)


def _attn_sample_kernel(sink_ref, x_ref, sh_ref, sc_ref, gt_ref, wq_ref, cos_ref, slo_ref, shi_ref,
                        kn_ref, vn_ref, kb_ref, vb_ref, wo_ref, g_ref, b_ref, o_ref,
                        q_ref, a_ref, *, seq):
    step = pl.program_id(0)
    pair = 2 * seq
    assert pair == SUBLANES
    grp, n_buf, _ = kb_ref.shape

    @pl.when(step == 0)
    def _():
        q = _rope(_dot(_modulate_bf16(x_ref[...], sh_ref[...], sc_ref[...]), wq_ref[...]),
                  cos_ref[...], slo_ref[...], shi_ref[...])
        q_ref[...] = q * (ATT_HD ** -0.5)

    n_rows = ATT_GROUP * pair
    n_all = ATT_KV_HEADS * n_rows
    n_new = 2 * pair
    q_pos = _iota((n_all, n_buf), 0) % seq + PAST_LEN
    k_pos = _iota((n_all, n_buf), 1) + (PAST_LEN - n_buf)
    buf_mask = (k_pos <= q_pos) & (k_pos > q_pos - WINDOW) & (k_pos >= 0)
    q_tok = _iota((n_all, n_new), 0) % seq
    new_col = _iota((n_all, n_new), 1)
    out_lane_head = _iota((n_rows, KVW), 1) // ATT_HD
    low = _iota((n_rows, KVW), 0) % pair < seq
    head_of_row = _iota((n_all, 1), 0) // pair
    sink_col = jnp.full((n_all, 1), sink_ref[0], F32)
    for hq in range(1, ATT_Q_HEADS):
        sink_col = jnp.where(head_of_row >= hq, sink_ref[hq], sink_col)
    pad_rows = jnp.zeros((n_new - pair, KVW), BF16)
    for p in range(grp // 2):
        r0 = pl.multiple_of(step * (grp * seq) + p * pair, pair)
        q_all = _stack_groups(q_ref[pl.ds(r0, pair), :].astype(BF16))
        q_bd = jnp.concatenate([_head_lanes(q_all, kh) for kh in range(ATT_KV_HEADS)], axis=0)
        k_new = jnp.concatenate([kn_ref[pl.ds(r0, pair), :].astype(BF16), pad_rows], axis=0)
        v_new = jnp.concatenate([vn_ref[pl.ds(r0, pair), :].astype(BF16), pad_rows], axis=0)
        s_new_all = _nt(q_bd, k_new)
        halves = []
        for half in range(2):
            bi = 2 * p + half
            tok = new_col - half * seq
            own_new = (tok >= 0) & (tok < seq) & (tok <= q_tok)
            s_buf = jnp.where(buf_mask, _nt(q_bd, kb_ref[bi].astype(BF16)), -jnp.inf)
            s_new = jnp.where(own_new, s_new_all, -jnp.inf)
            mx = jnp.maximum(jnp.maximum(jnp.max(s_buf, axis=-1, keepdims=True),
                                         jnp.max(s_new, axis=-1, keepdims=True)), sink_col)
            e_buf = jnp.exp(s_buf - mx)
            e_new = jnp.exp(s_new - mx)
            den = (jnp.sum(e_buf, axis=-1, keepdims=True) + jnp.sum(e_new, axis=-1, keepdims=True)
                   + jnp.exp(sink_col - mx))
            o_all = (_dot((e_buf / den).astype(BF16), vb_ref[bi].astype(BF16))
                     + _dot((e_new / den).astype(BF16), v_new))
            o_half = None
            for kh in range(ATT_KV_HEADS):
                part = jnp.where(out_lane_head == kh, o_all[kh * n_rows:(kh + 1) * n_rows, :], 0.0)
                o_half = part if o_half is None else o_half + part
            halves.append(o_half)
        o = jnp.where(low, halves[0], halves[1])
        for g in range(ATT_GROUP):
            a_ref[pl.ds(r0, pair), g * KVW:(g + 1) * KVW] = o[g * pair:(g + 1) * pair, :]

    @pl.when(step == pl.num_programs(0) - 1)
    def _():
        y = _dot(a_ref[...].astype(BF16), wo_ref[...])
        o_ref[...] = _layer_norm(ALPHA * x_ref[...] + gt_ref[...] * y, g_ref[...], b_ref[...])


def _attn_sample(x, mods, wq, tables, k_new, v_new, k_buf, v_buf, sinks, wo, ln_g, ln_b, seq):
    t = x.shape[0]
    grp = SAMPLE_GROUP
    buf_spec = pl.BlockSpec((grp,) + k_buf.shape[1:], lambda i: (i, 0, 0))
    return pl.pallas_call(
        functools.partial(_attn_sample_kernel, seq=seq),
        grid=(t // (grp * seq),),
        in_specs=[pl.BlockSpec(memory_space=pltpu.SMEM), _resident(x.shape)]
                 + _mod_specs(1, 3, 3, t, False)
                 + [_resident(wq.shape)] + [_resident(tab.shape) for tab in tables]
                 + [_resident(k_new.shape), _resident(v_new.shape), buf_spec, buf_spec, _resident(wo.shape)]
                 + _ln_specs(4),
        out_specs=pl.BlockSpec((t, D_MODEL), lambda i: (0, 0)),
        out_shape=jax.ShapeDtypeStruct((t, D_MODEL), F32),
        scratch_shapes=[pltpu.VMEM((t, QW), F32), pltpu.VMEM((t, QW), F32)],
        compiler_params=_params(("arbitrary",)),
        name="attn_sample",
    )(sinks, x, mods, mods, mods, wq, *tables, k_new, v_new, k_buf, v_buf, wo, ln_g, ln_b)


def _rope_tables(pos):
    half = ROT_DIM // 2
    inv_freq = ROPE_THETA ** (-jnp.arange(half, dtype=F32) * 2.0 / ROT_DIM)
    ang = pos.astype(F32)[:, None] * inv_freq[None, :]
    cos, sin = jnp.cos(ang), jnp.sin(ang)
    dim = np.arange(LANES) % ATT_HD
    freq = np.arange(half)[:, None]
    lo = (dim[None, :] == freq).astype(np.float32)
    hi = (dim[None, :] == freq + half).astype(np.float32)
    spread = lambda a, m: jnp.dot(a, jnp.asarray(m), precision=HIGHEST)
    return (spread(cos, lo + hi) + jnp.asarray((dim >= ROT_DIM).astype(np.float32)),
            spread(sin, -lo), spread(sin, hi))


def kernel(x_prompt, x_sample, c_prompt, c_sample, state_mlstm_C, state_mlstm_n, state_mlstm_m,
           cache_win_k, cache_win_v, ada_w, ada_b, ffn1_up, ffn1_down, ffn2_up, ffn2_down, ln_g, ln_b,
           ml_w_in, ml_b_gates, ml_norm_w, ml_w_out, kv_ada_w, kv_ada_b, kv_w,
           att_w_q, att_sinks, att_w_o):
    bp, sp, _ = x_prompt.shape
    bs, ss, _ = x_sample.shape
    n_buf = cache_win_k.shape[1]
    ts = bs * ss

    c_all = jnp.concatenate([jnp.repeat(c_sample, ss, axis=0), c_prompt,
                             jnp.zeros((-bp % SUBLANES, D_MODEL), F32)], axis=0)
    mods = _ada(c_all, ada_w, ada_b[:, None, :])
    kv_mods = _ada(c_all, kv_ada_w[None], kv_ada_b[None, None, :])

    bf = lambda w: w.astype(BF16)
    ffn_f32 = [(ffn1_up, ffn1_down, 0), (ffn2_up, ffn2_down, 0), (ffn1_up, ffn1_down, 1), (ffn2_up, ffn2_down, 1)]
    ffn_w = [(bf(ffn1_up[0]), bf(ffn1_down[0]))]
    gate_pad = LANES - 2 * ML_HEADS
    w_gate = bf(jnp.pad(ml_w_in[0, :, 2 * HQK + 2 * HV:], ((0, 0), (0, gate_pad))))
    b_gate = jnp.pad(ml_b_gates[0], (0, gate_pad))[None, :]
    norm_w = ml_norm_w[0][None, :]
    w_q = bf(att_w_q[0].reshape(D_MODEL, ATT_KV_HEADS, ATT_GROUP, ATT_HD)
             .transpose(0, 2, 1, 3).reshape(D_MODEL, QW))
    w_o = bf(att_w_o[0].reshape(ATT_KV_HEADS, ATT_GROUP, ATT_HD, D_MODEL)
             .transpose(1, 0, 2, 3).reshape(QW, D_MODEL))
    sinks = att_sinks[0]
    lng = ln_g.reshape(DEPTH * 3, 1, D_MODEL)
    lnb = ln_b.reshape(DEPTH * 3, 1, D_MODEL)

    tab_p = _rope_tables(jnp.arange(sp, dtype=jnp.int32))
    tab_s = _rope_tables(jnp.tile(PAST_LEN + jnp.arange(ss, dtype=jnp.int32), bs))

    xp = x_prompt.reshape(bp * sp, D_MODEL)
    xp, *cast, w_in, w_out, w_kv = _ffn(xp, mods, 0, 0, *ffn_w[0], lng, lnb, 0, ts, sp,
                                        cast=ffn_f32[1] + (ml_w_in, ml_w_out, kv_w[None]))
    ffn_w.append(tuple(cast))
    w_out, w_kv = w_out[0], w_kv[0]
    xp, state_p, m_p = _mlstm_prompt(xp.reshape(bp, sp, D_MODEL), mods, w_in, w_gate, b_gate,
                                     norm_w, w_out, lng, lnb, ts)
    xp, k_p, v_p, *cast = _ffn(xp.reshape(bp * sp, D_MODEL), mods, 0, 6, *ffn_w[1], lng, lnb, 2, ts, sp,
                               kv=(kv_mods, w_kv, tab_p), cast=ffn_f32[2])
    ffn_w.append(tuple(cast))
    k_p = k_p.reshape(bp, sp, KVW)
    v_p = v_p.reshape(bp, sp, KVW)
    xp, *cast = _ffn(xp, mods, 1, 0, *ffn_w[2], lng, lnb, 3, ts, sp, cast=ffn_f32[3])
    ffn_w.append(tuple(cast))
    xp = _attn_prompt(xp.reshape(bp, sp, D_MODEL), mods, w_q, tab_p, k_p, v_p, sinks, w_o, lng, lnb, ts)
    xp = _ffn(xp.reshape(bp * sp, D_MODEL), mods, 1, 6, *ffn_w[3], lng, lnb, 5, ts, sp)
    y_prompt = xp.reshape(bp, sp, D_MODEL)
    keep = min(WINDOW, sp)
    win_k_p = k_p[:, sp - keep:].reshape(bp, keep, ATT_KV_HEADS, ATT_HD)
    win_v_p = v_p[:, sp - keep:].reshape(bp, keep, ATT_KV_HEADS, ATT_HD)

    xs = x_sample.reshape(ts, D_MODEL)
    xs = _ffn(xs, mods, 0, 0, *ffn_w[0], lng, lnb, 0, ts, None)
    m0 = jnp.pad(state_mlstm_m[0], ((0, 0), (0, LANES - ML_HEADS)))
    xs, c_s, n_s, m_s = _mlstm_sample(xs, mods, w_in, w_gate, b_gate, norm_w, w_out, lng, lnb,
                                      state_mlstm_C[0], state_mlstm_n[0].reshape(bs, HQK), m0, ss)
    xs, k_s, v_s = _ffn(xs, mods, 0, 6, *ffn_w[1], lng, lnb, 2, ts, None, kv=(kv_mods, w_kv, tab_s))
    xs = _ffn(xs, mods, 1, 0, *ffn_w[2], lng, lnb, 3, ts, None)
    xs = _attn_sample(xs, mods, w_q, tab_s, k_s, v_s,
                      cache_win_k.reshape(bs, n_buf, KVW), cache_win_v.reshape(bs, n_buf, KVW),
                      sinks, w_o, lng, lnb, ss)
    xs = _ffn(xs, mods, 1, 6, *ffn_w[3], lng, lnb, 5, ts, None)
    y_sample = xs.reshape(bs, ss, D_MODEL)
    win_k_s = jnp.concatenate([cache_win_k, k_s.reshape(bs, ss, ATT_KV_HEADS, ATT_HD)], axis=1)[:, -n_buf:]
    win_v_s = jnp.concatenate([cache_win_v, v_s.reshape(bs, ss, ATT_KV_HEADS, ATT_HD)], axis=1)[:, -n_buf:]

    return (y_prompt, y_sample,
            state_p[None, ..., :ML_DV], state_p[None, :, :, 0, ML_DV:], m_p[None, :, :ML_HEADS, 0],
            win_k_p, win_v_p,
            c_s[None], n_s.reshape(1, bs, ML_HEADS, ML_DK), m_s[None, :, :ML_HEADS],
            win_k_s, win_v_s)
```

```python
import functools

import numpy as np
import jax
import jax.numpy as jnp
from jax import lax
from jax.experimental import pallas as pl
from jax.experimental.pallas import tpu as pltpu

F32 = jnp.float32
BF16 = jnp.bfloat16

D_MODEL = 1024
DEPTH = 2
PAST_LEN = 8192
ML_HEADS = 4
ML_DK = D_MODEL // 8
ML_DV = D_MODEL // 4
ATT_Q_HEADS = 16
ATT_KV_HEADS = 4
ATT_GROUP = ATT_Q_HEADS // ATT_KV_HEADS
ATT_HD = 64
WINDOW = 128
ROT_DIM = ATT_HD // 4
ROPE_THETA = 500000.0
D_FF = 2816
ALPHA = (2 * DEPTH) ** 0.25
LN_EPS = 1e-5
N_MOD = 9
HQK = ML_HEADS * ML_DK
HV = ML_HEADS * ML_DV
KVW = ATT_KV_HEADS * ATT_HD
QW = ATT_Q_HEADS * ATT_HD

LANES = 128
SUBLANES = 8
VMEM_LIMIT_BYTES = 56 * 1024 * 1024

FFN_ROWS = 1024
FFN_ROWS_KV = 512
FFN_CHUNK = 256
ML_ROWS = 1024
ML_CHUNK = 128
ATT_ROWS = 1024
ADA_COLS = 1024
SAMPLE_GROUP = 8

NT_DIMS = (((1,), (1,)), ((), ()))
HIGHEST = lax.Precision.HIGHEST


def _params(semantics):
    return pltpu.CompilerParams(dimension_semantics=semantics, vmem_limit_bytes=VMEM_LIMIT_BYTES)


def _const_spec(block, index):
    return pl.BlockSpec(block, lambda *_: index, pipeline_mode=pl.Buffered(1))


def _resident(shape):
    return _const_spec(shape, (0,) * len(shape))


def _dot(a, b):
    return jnp.dot(a, b, preferred_element_type=F32)


def _dot_exact(a, b):
    return jnp.dot(a, b, preferred_element_type=F32, precision=HIGHEST)


def _nt(a, b):
    return lax.dot_general(a, b, NT_DIMS, preferred_element_type=F32)


def _iota(shape, dim):
    return lax.broadcasted_iota(jnp.int32, shape, dim)


def _silu(x):
    return x * jax.nn.sigmoid(x)


def _log_sigmoid(x):
    return jnp.minimum(x, 0.0) - jnp.log1p(jnp.exp(-jnp.abs(x)))


def _layer_norm(y, g, b):
    mu = jnp.mean(y, axis=-1, keepdims=True)
    d = y - mu
    var = jnp.mean(d * d, axis=-1, keepdims=True)
    return d * lax.rsqrt(var + LN_EPS) * g + b


def _mod(ref, batch):
    return ref[...] if batch is None else ref[pl.ds(batch, 1), :]


def _modulate_bf16(x, shift, scale):
    return (x * (1.0 + scale) + shift).astype(BF16)


def _rope(x, cos, sin_lo, sin_hi):
    width = x.shape[1]
    reps = width // LANES
    tile = lambda t: jnp.concatenate([t] * reps, axis=1) if reps > 1 else t
    x_up = pltpu.roll(x, width - ROT_DIM // 2, 1)
    x_dn = pltpu.roll(x, ROT_DIM // 2, 1)
    return x * tile(cos) + x_up * tile(sin_lo) + x_dn * tile(sin_hi)


def _mod_specs(layer, first, count, n_sample_rows, prompt):
    if prompt:
        blk = n_sample_rows // SUBLANES
        return [_const_spec((None, SUBLANES, D_MODEL), (layer, blk, first + j)) for j in range(count)]
    return [_const_spec((None, n_sample_rows, D_MODEL), (layer, 0, first + j)) for j in range(count)]


def _ln_specs(index):
    return [_const_spec((None, 1, D_MODEL), (index, 0, 0))] * 2


def _ada_kernel(c_ref, w_ref, b_ref, o_ref):
    ca = _silu(c_ref[...]).astype(BF16)
    o_ref[...] = _dot(ca, w_ref[...].astype(BF16)) + b_ref[...]


def _ada(c, w, b):
    n_l, k, n = w.shape
    m = c.shape[0]
    return pl.pallas_call(
        _ada_kernel,
        grid=(n_l, n // ADA_COLS),
        in_specs=[pl.BlockSpec((m, k), lambda l, j: (0, 0)),
                  pl.BlockSpec((None, k, ADA_COLS), lambda l, j: (l, 0, j)),
                  pl.BlockSpec((None, 1, ADA_COLS), lambda l, j: (l, 0, j))],
        out_specs=pl.BlockSpec((None, m, ADA_COLS), lambda l, j: (l, 0, j)),
        out_shape=jax.ShapeDtypeStruct((n_l, m, n), F32),
        compiler_params=_params(("arbitrary", "arbitrary")),
        name="ada",
    )(c, w, b)


def _ffn_kernel(x_ref, sh_ref, sc_ref, gt_ref, wup_hbm, wdn_hbm, g_ref, b_ref, *rest,
                steps_per_seq, has_kv, n_cast):
    rest = list(rest)
    kv_in = [rest.pop(0) for _ in range(6)] if has_kv else None
    cast_in = [rest.pop(0) for _ in range(n_cast)]
    o_ref = rest.pop(0)
    kv_out = [rest.pop(0) for _ in range(2)] if has_kv else None
    cast_out = [rest.pop(0) for _ in range(n_cast)]
    h_ref, wa_ref, wu_ref, wd_ref, sem = rest

    chunks = [(c0, min(FFN_CHUNK, D_FF - c0)) for c0 in range(0, D_FF, FFN_CHUNK)]

    def up_copies(ci):
        c0, cw = chunks[ci]
        return (pltpu.make_async_copy(wup_hbm.at[:, c0:c0 + cw], wa_ref.at[:, c0:c0 + cw], sem.at[2 * ci]),
                pltpu.make_async_copy(wup_hbm.at[:, D_FF + c0:D_FF + c0 + cw], wu_ref.at[:, c0:c0 + cw],
                                      sem.at[2 * ci + 1]))

    down_copy = pltpu.make_async_copy(wdn_hbm, wd_ref, sem.at[2 * len(chunks)])

    def body(fetch):
        if fetch:
            for ci in range(len(chunks)):
                for cp in up_copies(ci):
                    cp.start()
            down_copy.start()
        batch = None if steps_per_seq is None else pl.program_id(0) // steps_per_seq
        x = x_ref[...]
        xm = _modulate_bf16(x, _mod(sh_ref, batch), _mod(sc_ref, batch))
        for ci, (c0, cw) in enumerate(chunks):
            if fetch:
                for cp in up_copies(ci):
                    cp.wait()
            a = _dot(xm, wa_ref[:, c0:c0 + cw])
            u = _dot(xm, wu_ref[:, c0:c0 + cw])
            h_ref[:, c0:c0 + cw] = (_silu(a) * u).astype(BF16)
        if fetch:
            down_copy.wait()
        y = ALPHA * x + (0.5 * _mod(gt_ref, batch)) * _dot(h_ref[...], wd_ref[...])
        out = _layer_norm(y, g_ref[...], b_ref[...])
        o_ref[...] = out
        if has_kv:
            ksh_ref, ksc_ref, wkv_ref, cos_ref, slo_ref, shi_ref = kv_in
            kv = _dot(_modulate_bf16(out, _mod(ksh_ref, batch), _mod(ksc_ref, batch)), wkv_ref[...])
            kv_out[0][...] = _rope(kv[:, :KVW], cos_ref[...], slo_ref[...], shi_ref[...])
            kv_out[1][...] = kv[:, KVW:]
        for src, dst in zip(cast_in, cast_out):
            dst[...] = src[...].astype(BF16)

    @pl.when(pl.program_id(0) == 0)
    def _():
        body(True)

    @pl.when(pl.program_id(0) > 0)
    def _():
        body(False)


def _ffn(x, mods, layer, first, w_up, w_down, ln_g, ln_b, ln_index, n_sample_rows, seq, kv=None, cast=None):
    t = x.shape[0]
    rows = min(FFN_ROWS if kv is None else FFN_ROWS_KV, t)
    steps = t // rows
    x_spec = pl.BlockSpec((rows, D_MODEL), lambda i: (i, 0))
    in_specs = ([x_spec] + _mod_specs(layer, first, 3, n_sample_rows, seq is not None)
                + [pl.BlockSpec(memory_space=pl.ANY), pl.BlockSpec(memory_space=pl.ANY)]
                + _ln_specs(ln_index))
    args = [x, mods, mods, mods, w_up, w_down, ln_g, ln_b]
    out_specs = [x_spec]
    out_shape = [jax.ShapeDtypeStruct((t, D_MODEL), F32)]
    if kv is not None:
        kv_mods, w_kv, tables = kv
        pos_blocks = tables[0].shape[0] // rows
        tab_spec = pl.BlockSpec((rows, LANES), lambda i: (i % pos_blocks, 0))
        kv_spec = pl.BlockSpec((rows, KVW), lambda i: (i, 0))
        in_specs += (_mod_specs(0, 0, 2, n_sample_rows, seq is not None)
                     + [_resident(w_kv.shape), tab_spec, tab_spec, tab_spec])
        args += [kv_mods, kv_mods, w_kv, *tables]
        out_specs += [kv_spec, kv_spec]
        out_shape += [jax.ShapeDtypeStruct((t, KVW), F32)] * 2
    n_cast = 0
    if cast is not None:
        up_f32, down_f32, src_layer, *extra = cast
        up_rows = D_MODEL // steps
        down_rows = 2 * D_FF // steps
        assert up_rows % 16 == 0 and down_rows % 16 == 0 and steps % 2 == 0
        in_specs += [pl.BlockSpec((None, up_rows, 2 * D_FF), lambda i: (src_layer, i, 0)),
                     pl.BlockSpec((None, down_rows, D_MODEL), lambda i: (src_layer, i // 2, 0))]
        args += [up_f32, down_f32]
        out_specs += [pl.BlockSpec((up_rows, 2 * D_FF), lambda i: (i, 0)),
                      pl.BlockSpec((down_rows, D_MODEL), lambda i: (i // 2, 0))]
        out_shape += [jax.ShapeDtypeStruct((D_MODEL, 2 * D_FF), BF16),
                      jax.ShapeDtypeStruct((D_FF, D_MODEL), BF16)]
        for w in extra:
            spec = pl.BlockSpec((None, up_rows, w.shape[2]), lambda i: (0, i, 0))
            in_specs.append(spec)
            args.append(w)
            out_specs.append(spec)
            out_shape.append(jax.ShapeDtypeStruct(w.shape, BF16))
        n_cast = 2 + len(extra)
    outs = pl.pallas_call(
        functools.partial(_ffn_kernel, steps_per_seq=None if seq is None else seq // rows,
                          has_kv=kv is not None, n_cast=n_cast),
        grid=(steps,),
        in_specs=in_specs,
        out_specs=out_specs,
        out_shape=out_shape,
        scratch_shapes=[pltpu.VMEM((rows, D_FF), BF16),
                        pltpu.VMEM((D_MODEL, D_FF), BF16), pltpu.VMEM((D_MODEL, D_FF), BF16),
                        pltpu.VMEM((D_FF, D_MODEL), BF16),
                        pltpu.SemaphoreType.DMA((2 * pl.cdiv(D_FF, FFN_CHUNK) + 1,))],
        compiler_params=_params(("arbitrary",)),
        name="ffn" +("_kv" if kv is not None else "") + ("_cast" if cast is not None else ""),
    )(*args)
    return outs[0] if len(outs) == 1 else outs


def _head_norm(hh, w_row):
    mu = jnp.mean(hh, axis=-1, keepdims=True)
    d = hh - mu
    var = jnp.mean(d * d, axis=-1, keepdims=True)
    return d * lax.rsqrt(var + LN_EPS) * w_row


def _split3(x, axis):
    hi = x.astype(BF16)
    r1 = x - hi.astype(F32)
    mid = r1.astype(BF16)
    lo = (r1 - mid.astype(F32)).astype(BF16)
    return jnp.concatenate([hi, mid, lo], axis=axis)


def _tile_lanes(t, reps):
    return jnp.concatenate([t] * reps, axis=1)


def _mlstm_prompt_kernel(x_ref, sh_ref, sc_ref, gt_ref, wp_ref, wg_ref, bg_ref, nw_ref, wo_ref, g_ref, b_ref,
                         o_ref, c_out, m_out, proj_ref, rep_ref, y_ref, c_ref, mm_ref):
    batch = pl.program_id(0)
    step = pl.program_id(1)
    size = ML_CHUNK
    assert size == LANES and ML_DK == LANES

    @pl.when(step == 0)
    def _():
        c_ref[...] = jnp.zeros_like(c_ref)
        mm_ref[...] = jnp.zeros_like(mm_ref)

    x = x_ref[...]
    rows = x.shape[0]
    n_chunks = rows // size
    h = _modulate_bf16(x, _mod(sh_ref, batch), _mod(sc_ref, batch))
    gates = _dot(h, wg_ref[...]) + bg_ref[...]

    r = _iota((size, size), 0)
    c = _iota((size, size), 1)
    causal = c <= r
    eye = c == r
    tri3 = _tile_lanes(jnp.where(causal, 1.0, 0.0).astype(BF16), 3)
    n_rep = 2 * ML_HEADS
    spread = jnp.where(_iota((LANES, n_rep * LANES), 1) // LANES == _iota((LANES, n_rep * LANES), 0),
                       1.0, 0.0).astype(BF16)
    spread3 = jnp.concatenate([spread] * 3, axis=0)
    ones3 = jnp.ones((3 * size, LANES), BF16)
    sr = _iota((2 * SUBLANES, LANES), 0)
    sl = _iota((2 * SUBLANES, LANES), 1)
    pick = (jnp.where((sr < ML_HEADS) & (sl == sr), 1.0, 0.0)
            - jnp.where((sr < ML_HEADS) & (sl == sr + ML_HEADS), 1.0, 0.0)
            + jnp.where((sr >= SUBLANES) & (sr < SUBLANES + ML_HEADS) & (sl == sr - SUBLANES + ML_HEADS),
                        1.0, 0.0))
    pick3 = _tile_lanes(pick.astype(BF16), 3)

    log_f = _log_sigmoid(gates)
    bwide = jnp.concatenate([_dot(tri3, _split3(log_f[ci * size:(ci + 1) * size, :], 0))
                             for ci in range(n_chunks)], axis=0)
    src3 = _split3(jnp.where(_iota((rows, LANES), 1) < ML_HEADS, gates, bwide), 1)
    rep_ref[...] = _dot(src3, spread3)
    ab_rows = _nt(pick3, src3)
    a_rows = ab_rows[:SUBLANES, :]
    b_rows = ab_rows[SUBLANES:, :]
    lane_in_chunk = _iota((SUBLANES, rows), 1) % size
    run = a_rows
    shift = 1
    while shift < size:
        run = jnp.maximum(run, jnp.where(lane_in_chunk >= shift, pltpu.roll(run, shift, 1), -jnp.inf))
        shift *= 2
    m_rows = [mm_ref[...]]
    g_rows = []
    for ci in range(n_chunks):
        run_c = run[:, ci * size:(ci + 1) * size]
        g_rows.append(jnp.maximum(run_c, m_rows[-1]))
        end = (ci + 1) * size - 1
        m_end = b_rows[:, end:end + 1] + jnp.maximum(run[:, end:end + 1], m_rows[-1][:, :1])
        m_rows.append(jnp.broadcast_to(m_end, (SUBLANES, LANES)))
    mm_ref[...] = m_rows[-1]

    proj_ref[...] = _dot(h, wp_ref[...])

    states = [c_ref[hd] for hd in range(ML_HEADS)]
    for ci in range(n_chunks):
        lo, hi = ci * size, (ci + 1) * size
        for hd in range(ML_HEADS):
            i_rep = rep_ref[lo:hi, hd * LANES:(hd + 1) * LANES]
            b_rep = rep_ref[lo:hi, (ML_HEADS + hd) * LANES:(ML_HEADS + hd + 1) * LANES]
            m_prev = m_rows[ci][hd:hd + 1, :]
            m_new = m_rows[ci + 1][hd:hd + 1, :]
            a_row = a_rows[hd:hd + 1, lo:hi]
            g_rep = _dot(_split3(jnp.where(eye, g_rows[ci][hd:hd + 1, :], 0.0), 1), ones3)
            w = jnp.where(causal, jnp.exp(a_row - g_rep), 0.0)
            w_init = jnp.exp(m_prev - g_rep)
            floor = jnp.exp(-(b_rep + g_rep))
            b_last = b_rep[size - 1:size, :]
            w_end = jnp.exp(b_last - b_rep + i_rep - m_new)
            decay = jnp.exp(b_last + m_prev - m_new)

            q = proj_ref[lo:hi, hd * ML_DK:(hd + 1) * ML_DK]
            k = proj_ref[lo:hi, HQK + hd * ML_DK:HQK + (hd + 1) * ML_DK] * (ML_DK ** -0.5)
            v = proj_ref[lo:hi, 2 * HQK + hd * ML_DV:2 * HQK + (hd + 1) * ML_DV]
            qb = q.astype(BF16)
            vb = v.astype(BF16)
            c_mat = states[hd][:, :ML_DV]
            n_row = states[hd][0:1, ML_DV:]
            s = _nt(qb, k.astype(BF16)) * w
            num = _dot(s.astype(BF16), vb) + _tile_lanes(w_init, 2) * _dot(qb, c_mat.astype(BF16))
            den = (jnp.sum(s, axis=-1, keepdims=True)
                   + w_init[:, :1] * jnp.sum(q * n_row, axis=-1, keepdims=True))
            hh = num / jnp.maximum(jnp.abs(den), floor[:, :1])
            kw = k * w_end
            c_new = _tile_lanes(decay, 2) * c_mat + _dot(kw.T.astype(BF16), vb)
            n_new = decay * n_row + jnp.sum(kw, axis=0, keepdims=True)
            states[hd] = jnp.concatenate([c_new, jnp.broadcast_to(n_new, (ML_DK, LANES))], axis=1)
            y_ref[lo:hi, hd * ML_DV:(hd + 1) * ML_DV] = _head_norm(hh, nw_ref[:, hd * ML_DV:(hd + 1) * ML_DV])
    for hd in range(ML_HEADS):
        c_ref[hd] = states[hd]

    o_gate = jax.nn.sigmoid(proj_ref[:, 2 * HQK + HV:2 * HQK + 2 * HV])
    y = _dot((y_ref[...] * o_gate).astype(BF16), wo_ref[...])
    o_ref[...] = _layer_norm(ALPHA * x + _mod(gt_ref, batch) * y, g_ref[...], b_ref[...])

    @pl.when(step == pl.num_programs(1) - 1)
    def _():
        c_out[...] = c_ref[...]
        m_out[...] = mm_ref[...]


def _mlstm_weight_specs(wg, bg, nw, wo):
    return [_const_spec((None, D_MODEL, 2 * HQK + 2 * HV), (0, 0, 0)), _resident(wg.shape),
            _resident(bg.shape), _resident(nw.shape), _resident(wo.shape)] + _ln_specs(1)


def _mlstm_prompt(x, mods, w_in, wg, bg, nw, wo, ln_g, ln_b, n_sample_rows):
    bsz, seq, _ = x.shape
    rows = min(ML_ROWS, seq)
    x_spec = pl.BlockSpec((None, rows, D_MODEL), lambda bi, i: (bi, i, 0))
    state_shape = (ML_HEADS, ML_DK, ML_DV + LANES)
    return pl.pallas_call(
        _mlstm_prompt_kernel,
        grid=(bsz, seq // rows),
        in_specs=[x_spec] + _mod_specs(0, 3, 3, n_sample_rows, True) + _mlstm_weight_specs(wg, bg, nw, wo),
        out_specs=[x_spec, pl.BlockSpec((None,) + state_shape, lambda bi, i: (bi, 0, 0, 0)),
                   pl.BlockSpec((None, SUBLANES, LANES), lambda bi, i: (bi, 0, 0))],
        out_shape=[jax.ShapeDtypeStruct((bsz, seq, D_MODEL), F32),
                   jax.ShapeDtypeStruct((bsz,) + state_shape, F32),
                   jax.ShapeDtypeStruct((bsz, SUBLANES, LANES), F32)],
        scratch_shapes=[pltpu.VMEM((rows, 2 * HQK + 2 * HV), F32),
                        pltpu.VMEM((rows, 2 * ML_HEADS * LANES), F32),
                        pltpu.VMEM((rows, HV), F32),
                        pltpu.VMEM(state_shape, F32),
                        pltpu.VMEM((SUBLANES, LANES), F32)],
        compiler_params=_params(("arbitrary", "arbitrary")),
        name="mlstm_prompt",
    )(x, mods, mods, mods, w_in, wg, bg, nw, wo, ln_g, ln_b)


def _mlstm_sample_kernel(x_ref, sh_ref, sc_ref, gt_ref, wp_ref, wg_ref, bg_ref, nw_ref, wo_ref, g_ref, b_ref,
                         c0_ref, n0_ref, m0_ref, o_ref, c_out, n_out, m_out,
                         proj_ref, gate_ref, y_ref, *, seq):
    step = pl.program_id(0)
    grp = c0_ref.shape[0]
    rows = grp * seq

    @pl.when(step == 0)
    def _():
        h = _modulate_bf16(x_ref[...], sh_ref[...], sc_ref[...])
        proj_ref[...] = _dot(h, wp_ref[...])
        gate_ref[...] = _dot(h, wg_ref[...]) + bg_ref[...]

    r = _iota((rows, rows), 0)
    c = _iota((rows, rows), 1)
    same = (r // seq) == (c // seq)
    causal = same & (c <= r)
    eye = r == c
    last_of_row = c == (r // seq) * seq + (seq - 1)
    same_f = same.astype(F32)
    expand = (_iota((rows, grp), 0) // seq == _iota((rows, grp), 1)).astype(F32)
    pick_last = (_iota((grp, rows), 1) == _iota((grp, rows), 0) * seq + (seq - 1)).astype(F32)
    ident = (_iota((ML_DK, ML_DK), 0) == _iota((ML_DK, ML_DK), 1)).astype(F32)
    seq_of_row = _iota((rows, ML_DV), 0) // seq
    col_seq = _iota((ML_DK, rows), 1) // seq
    lane = _iota((rows, LANES), 1)
    to_row = lambda col: jnp.sum(jnp.where(eye, col, 0.0), axis=0, keepdims=True)

    r0 = pl.multiple_of(step * rows, rows)
    gates = gate_ref[pl.ds(r0, rows), :]
    log_f = _log_sigmoid(gates)
    bwide = _dot_exact(causal.astype(F32), log_f)
    bsum = _dot_exact(same_f, log_f)
    m_prev_wide = _dot_exact(expand, m0_ref[...])
    n_rows_wide = _dot_exact(expand, n0_ref[...])
    n_pieces = []
    m_wide = jnp.zeros((rows, LANES), F32)
    for hd in range(ML_HEADS):
        q = proj_ref[pl.ds(r0, rows), hd * ML_DK:(hd + 1) * ML_DK]
        k = proj_ref[pl.ds(r0, rows), HQK + hd * ML_DK:HQK + (hd + 1) * ML_DK] * (ML_DK ** -0.5)
        v = proj_ref[pl.ds(r0, rows), 2 * HQK + hd * ML_DV:2 * HQK + (hd + 1) * ML_DV]
        bcol = bwide[:, ML_HEADS + hd:ML_HEADS + hd + 1]
        b_last = bsum[:, ML_HEADS + hd:ML_HEADS + hd + 1]
        icol = gates[:, hd:hd + 1]
        m_prev = m_prev_wide[:, hd:hd + 1]
        n_rows = n_rows_wide[:, hd * ML_DK:(hd + 1) * ML_DK]
        log_w = jnp.where(causal, bcol - to_row(bcol) + to_row(icol), -jnp.inf)
        log_init = bcol + m_prev
        m_t = jnp.maximum(log_init, jnp.max(log_w, axis=-1, keepdims=True))
        w = jnp.exp(log_w - m_t)
        w_init = jnp.exp(log_init - m_t)
        qb = q.astype(BF16)
        s = _nt(qb, k.astype(BF16)) * w
        inter = None
        for bi in range(grp):
            cand = _dot(qb, c0_ref[bi, hd].astype(BF16))
            inter = cand if inter is None else jnp.where(seq_of_row == bi, cand, inter)
        num = _dot(s.astype(BF16), v.astype(BF16)) + w_init * inter
        den = jnp.sum(s, axis=-1, keepdims=True) + w_init * jnp.sum(q * n_rows, axis=-1, keepdims=True)
        hh = num / jnp.maximum(jnp.abs(den), jnp.exp(-m_t))
        m_new = jnp.sum(jnp.where(last_of_row, to_row(m_t), 0.0), axis=1, keepdims=True)
        w_end = jnp.exp(b_last - bcol + icol - m_new)
        decay = jnp.exp(b_last + m_prev - m_new)
        kw = k * w_end
        kw_t = lax.dot_general(ident, kw, NT_DIMS, preferred_element_type=F32, precision=HIGHEST)
        lhs = jnp.concatenate([jnp.where(col_seq == bi, kw_t, 0.0) for bi in range(grp)], axis=0)
        upd = _dot(lhs.astype(BF16), v.astype(BF16))
        for bi in range(grp):
            c_out[bi, hd] = (decay[bi * seq:bi * seq + 1, :] * c0_ref[bi, hd]
                             + upd[bi * ML_DK:(bi + 1) * ML_DK, :])
        n_pieces.append(decay * n_rows + _dot_exact(same_f, kw))
        m_wide = jnp.where(lane == hd, m_new, m_wide)
        y_ref[pl.ds(r0, rows), hd * ML_DV:(hd + 1) * ML_DV] = _head_norm(
            hh, nw_ref[:, hd * ML_DV:(hd + 1) * ML_DV])
    n_out[...] = _dot_exact(pick_last, jnp.concatenate(n_pieces, axis=1))
    m_out[...] = _dot_exact(pick_last, m_wide)

    @pl.when(step == pl.num_programs(0) - 1)
    def _():
        o_gate = jax.nn.sigmoid(proj_ref[:, 2 * HQK + HV:2 * HQK + 2 * HV])
        y = _dot((y_ref[...] * o_gate).astype(BF16), wo_ref[...])
        o_ref[...] = _layer_norm(ALPHA * x_ref[...] + gt_ref[...] * y, g_ref[...], b_ref[...])


def _mlstm_sample(x, mods, w_in, wg, bg, nw, wo, ln_g, ln_b, c0, n0, m0, seq):
    t = x.shape[0]
    grp = SAMPLE_GROUP
    state_specs = [pl.BlockSpec((grp, ML_HEADS, ML_DK, ML_DV), lambda i: (i, 0, 0, 0)),
                   pl.BlockSpec((grp, HQK), lambda i: (i, 0)),
                   pl.BlockSpec((grp, LANES), lambda i: (i, 0))]
    return pl.pallas_call(
        functools.partial(_mlstm_sample_kernel, seq=seq),
        grid=(t // (grp * seq),),
        in_specs=[_resident(x.shape)] + _mod_specs(0, 3, 3, t, False)
                 + _mlstm_weight_specs(wg, bg, nw, wo) + state_specs,
        out_specs=[pl.BlockSpec((t, D_MODEL), lambda i: (0, 0))] + state_specs,
        out_shape=[jax.ShapeDtypeStruct((t, D_MODEL), F32),
                   jax.ShapeDtypeStruct(c0.shape, F32),
                   jax.ShapeDtypeStruct(n0.shape, F32),
                   jax.ShapeDtypeStruct(m0.shape, F32)],
        scratch_shapes=[pltpu.VMEM((t, 2 * HQK + 2 * HV), F32),
                        pltpu.VMEM((t, LANES), F32),
                        pltpu.VMEM((t, HV), F32)],
        compiler_params=_params(("arbitrary",)),
        name="mlstm_sample",
    )(x, mods, mods, mods, w_in, wg, bg, nw, wo, ln_g, ln_b, c0, n0, m0)


def _sink_column(sink_ref, kh, rows_per_head):
    rid = _iota((ATT_GROUP * rows_per_head, 1), 0)
    col = jnp.full((ATT_GROUP * rows_per_head, 1), sink_ref[kh * ATT_GROUP], F32)
    for gq in range(1, ATT_GROUP):
        col = jnp.where(rid >= gq * rows_per_head, sink_ref[kh * ATT_GROUP + gq], col)
    return col


def _head_lanes(x, kh):
    return jnp.where(_iota(x.shape, 1) // ATT_HD == kh, x, 0.0).astype(BF16)


def _stack_groups(x):
    return jnp.concatenate([x[:, g * KVW:(g + 1) * KVW] for g in range(ATT_GROUP)], axis=0)


def _attn_prompt_kernel(sink_ref, x_ref, sh_ref, sc_ref, gt_ref, wq_ref, cos_ref, slo_ref, shi_ref,
                        kp_ref, kc_ref, vp_ref, vc_ref, wo_ref, g_ref, b_ref, o_ref, a_ref):
    batch = pl.program_id(0)
    step = pl.program_id(1)
    x = x_ref[...]
    rows = x.shape[0]
    n_blk = rows // WINDOW
    q = _rope(_dot(_modulate_bf16(x, _mod(sh_ref, batch), _mod(sc_ref, batch)), wq_ref[...]),
              cos_ref[...], slo_ref[...], shi_ref[...])
    q = (q * (ATT_HD ** -0.5)).astype(BF16)

    keys = jnp.concatenate([kp_ref[...], kc_ref[...]], axis=0)
    vals = jnp.concatenate([vp_ref[...], vc_ref[...]], axis=0)
    k_heads = [_head_lanes(keys, kh) for kh in range(ATT_KV_HEADS)]
    v_heads = [_head_lanes(vals, kh) for kh in range(ATT_KV_HEADS)]

    rid = _iota((ATT_GROUP * WINDOW, WINDOW), 0) % WINDOW
    cid = _iota((ATT_GROUP * WINDOW, WINDOW), 1)
    from_prev = cid > rid
    no_prev = jnp.where(step > 0, 0.0, -jnp.inf)
    sinks = [_sink_column(sink_ref, kh, WINDOW) for kh in range(ATT_KV_HEADS)]

    scores = [[_nt(_stack_groups(q[n * WINDOW:(n + 1) * WINDOW, :]),
                   k_heads[kh][n * WINDOW:(n + 2) * WINDOW, :])
               for kh in range(ATT_KV_HEADS)] for n in range(n_blk)]
    probs = []
    for n in range(n_blk):
        row = []
        for kh in range(ATT_KV_HEADS):
            s_prev = scores[n][kh][:, :WINDOW]
            if n == 0:
                s_prev = s_prev + no_prev
            s = jnp.where(from_prev, s_prev, scores[n][kh][:, WINDOW:])
            mx = jnp.maximum(jnp.max(s, axis=-1, keepdims=True), sinks[kh])
            p = jnp.exp(s - mx)
            den = jnp.sum(p, axis=-1, keepdims=True) + jnp.exp(sinks[kh] - mx)
            pn = p / den
            row.append(jnp.where(from_prev, pn, 0.0).astype(BF16))
            row.append(jnp.where(from_prev, 0.0, pn).astype(BF16))
        probs.append(jnp.concatenate(row, axis=1))
    for n in range(n_blk):
        v_all = jnp.concatenate([v_heads[kh][n * WINDOW:(n + 2) * WINDOW, :]
                                 for kh in range(ATT_KV_HEADS)], axis=0)
        o = _dot(probs[n], v_all)
        for g in range(ATT_GROUP):
            a_ref[n * WINDOW:(n + 1) * WINDOW, g * KVW:(g + 1) * KVW] = o[g * WINDOW:(g + 1) * WINDOW, :]

    y = _dot(a_ref[...].astype(BF16), wo_ref[...])
    o_ref[...] = _layer_norm(ALPHA * x + _mod(gt_ref, batch) * y, g_ref[...], b_ref[...])


def _attn_prompt(x, mods, wq, tables, k, v, sinks, wo, ln_g, ln_b, n_sample_rows):
    bsz, seq, _ = x.shape
    rows = min(ATT_ROWS, seq)
    per = rows // WINDOW
    x_spec = pl.BlockSpec((None, rows, D_MODEL), lambda bi, i: (bi, i, 0))
    cur = pl.BlockSpec((None, rows, KVW), lambda bi, i: (bi, i, 0))
    prev = pl.BlockSpec((None, WINDOW, KVW), lambda bi, i: (bi, jnp.maximum(i * per - 1, 0), 0))
    tab = pl.BlockSpec((rows, LANES), lambda bi, i: (i, 0))
    return pl.pallas_call(
        _attn_prompt_kernel,
        grid=(bsz, seq // rows),
        in_specs=[pl.BlockSpec(memory_space=pltpu.SMEM), x_spec]
                 + _mod_specs(1, 3, 3, n_sample_rows, True)
                 + [_resident(wq.shape), tab, tab, tab, prev, cur, prev, cur, _resident(wo.shape)]
                 + _ln_specs(4),
        out_specs=x_spec,
        out_shape=jax.ShapeDtypeStruct((bsz, seq, D_MODEL), F32),
        scratch_shapes=[pltpu.VMEM((rows, QW), F32)],
        compiler_params=_params(("arbitrary", "arbitrary")),
        name="attn_prompt",
    )(sinks, x, mods, mods, mods, wq, *tables, k, k, v, v, wo, ln_g, ln_b)


def _attn_sample_kernel(sink_ref, x_ref, sh_ref, sc_ref, gt_ref, wq_ref, cos_ref, slo_ref, shi_ref,
                        kn_ref, vn_ref, kb_ref, vb_ref, wo_ref, g_ref, b_ref, o_ref,
                        q_ref, a_ref, *, seq):
    step = pl.program_id(0)
    pair = 2 * seq
    assert pair == SUBLANES
    grp, n_buf, _ = kb_ref.shape

    @pl.when(step == 0)
    def _():
        q = _rope(_dot(_modulate_bf16(x_ref[...], sh_ref[...], sc_ref[...]), wq_ref[...]),
                  cos_ref[...], slo_ref[...], shi_ref[...])
        q_ref[...] = q * (ATT_HD ** -0.5)

    n_rows = ATT_GROUP * pair
    n_all = ATT_KV_HEADS * n_rows
    n_new = 2 * pair
    q_pos = _iota((n_all, n_buf), 0) % seq + PAST_LEN
    k_pos = _iota((n_all, n_buf), 1) + (PAST_LEN - n_buf)
    buf_mask = (k_pos <= q_pos) & (k_pos > q_pos - WINDOW) & (k_pos >= 0)
    q_tok = _iota((n_all, n_new), 0) % seq
    new_col = _iota((n_all, n_new), 1)
    out_lane_head = _iota((n_rows, KVW), 1) // ATT_HD
    low = _iota((n_rows, KVW), 0) % pair < seq
    head_of_row = _iota((n_all, 1), 0) // pair
    sink_col = jnp.full((n_all, 1), sink_ref[0], F32)
    for hq in range(1, ATT_Q_HEADS):
        sink_col = jnp.where(head_of_row >= hq, sink_ref[hq], sink_col)
    pad_rows = jnp.zeros((n_new - pair, KVW), BF16)
    for p in range(grp // 2):
        r0 = pl.multiple_of(step * (grp * seq) + p * pair, pair)
        q_all = _stack_groups(q_ref[pl.ds(r0, pair), :].astype(BF16))
        q_bd = jnp.concatenate([_head_lanes(q_all, kh) for kh in range(ATT_KV_HEADS)], axis=0)
        k_new = jnp.concatenate([kn_ref[pl.ds(r0, pair), :].astype(BF16), pad_rows], axis=0)
        v_new = jnp.concatenate([vn_ref[pl.ds(r0, pair), :].astype(BF16), pad_rows], axis=0)
        s_new_all = _nt(q_bd, k_new)
        halves = []
        for half in range(2):
            bi = 2 * p + half
            tok = new_col - half * seq
            own_new = (tok >= 0) & (tok < seq) & (tok <= q_tok)
            s_buf = jnp.where(buf_mask, _nt(q_bd, kb_ref[bi].astype(BF16)), -jnp.inf)
            s_new = jnp.where(own_new, s_new_all, -jnp.inf)
            mx = jnp.maximum(jnp.maximum(jnp.max(s_buf, axis=-1, keepdims=True),
                                         jnp.max(s_new, axis=-1, keepdims=True)), sink_col)
            e_buf = jnp.exp(s_buf - mx)
            e_new = jnp.exp(s_new - mx)
            den = (jnp.sum(e_buf, axis=-1, keepdims=True) + jnp.sum(e_new, axis=-1, keepdims=True)
                   + jnp.exp(sink_col - mx))
            o_all = (_dot((e_buf / den).astype(BF16), vb_ref[bi].astype(BF16))
                     + _dot((e_new / den).astype(BF16), v_new))
            o_half = None
            for kh in range(ATT_KV_HEADS):
                part = jnp.where(out_lane_head == kh, o_all[kh * n_rows:(kh + 1) * n_rows, :], 0.0)
                o_half = part if o_half is None else o_half + part
            halves.append(o_half)
        o = jnp.where(low, halves[0], halves[1])
        for g in range(ATT_GROUP):
            a_ref[pl.ds(r0, pair), g * KVW:(g + 1) * KVW] = o[g * pair:(g + 1) * pair, :]

    @pl.when(step == pl.num_programs(0) - 1)
    def _():
        y = _dot(a_ref[...].astype(BF16), wo_ref[...])
        o_ref[...] = _layer_norm(ALPHA * x_ref[...] + gt_ref[...] * y, g_ref[...], b_ref[...])


def _attn_sample(x, mods, wq, tables, k_new, v_new, k_buf, v_buf, sinks, wo, ln_g, ln_b, seq):
    t = x.shape[0]
    grp = SAMPLE_GROUP
    buf_spec = pl.BlockSpec((grp,) + k_buf.shape[1:], lambda i: (i, 0, 0))
    return pl.pallas_call(
        functools.partial(_attn_sample_kernel, seq=seq),
        grid=(t // (grp * seq),),
        in_specs=[pl.BlockSpec(memory_space=pltpu.SMEM), _resident(x.shape)]
                 + _mod_specs(1, 3, 3, t, False)
                 + [_resident(wq.shape)] + [_resident(tab.shape) for tab in tables]
                 + [_resident(k_new.shape), _resident(v_new.shape), buf_spec, buf_spec, _resident(wo.shape)]
                 + _ln_specs(4),
        out_specs=pl.BlockSpec((t, D_MODEL), lambda i: (0, 0)),
        out_shape=jax.ShapeDtypeStruct((t, D_MODEL), F32),
        scratch_shapes=[pltpu.VMEM((t, QW), F32), pltpu.VMEM((t, QW), F32)],
        compiler_params=_params(("arbitrary",)),
        name="attn_sample",
    )(sinks, x, mods, mods, mods, wq, *tables, k_new, v_new, k_buf, v_buf, wo, ln_g, ln_b)


def _rope_tables(pos):
    half = ROT_DIM // 2
    inv_freq = ROPE_THETA ** (-jnp.arange(half, dtype=F32) * 2.0 / ROT_DIM)
    ang = pos.astype(F32)[:, None] * inv_freq[None, :]
    cos, sin = jnp.cos(ang), jnp.sin(ang)
    dim = np.arange(LANES) % ATT_HD
    freq = np.arange(half)[:, None]
    lo = (dim[None, :] == freq).astype(np.float32)
    hi = (dim[None, :] == freq + half).astype(np.float32)
    spread = lambda a, m: jnp.dot(a, jnp.asarray(m), precision=HIGHEST)
    return (spread(cos, lo + hi) + jnp.asarray((dim >= ROT_DIM).astype(np.float32)),
            spread(sin, -lo), spread(sin, hi))


def kernel(x_prompt, x_sample, c_prompt, c_sample, state_mlstm_C, state_mlstm_n, state_mlstm_m,
           cache_win_k, cache_win_v, ada_w, ada_b, ffn1_up, ffn1_down, ffn2_up, ffn2_down, ln_g, ln_b,
           ml_w_in, ml_b_gates, ml_norm_w, ml_w_out, kv_ada_w, kv_ada_b, kv_w,
           att_w_q, att_sinks, att_w_o):
    bp, sp, _ = x_prompt.shape
    bs, ss, _ = x_sample.shape
    n_buf = cache_win_k.shape[1]
    ts = bs * ss

    c_all = jnp.concatenate([jnp.repeat(c_sample, ss, axis=0), c_prompt,
                             jnp.zeros((-bp % SUBLANES, D_MODEL), F32)], axis=0)
    mods = _ada(c_all, ada_w, ada_b[:, None, :])
    kv_mods = _ada(c_all, kv_ada_w[None], kv_ada_b[None, None, :])

    bf = lambda w: w.astype(BF16)
    ffn_f32 = [(ffn1_up, ffn1_down, 0), (ffn2_up, ffn2_down, 0), (ffn1_up, ffn1_down, 1), (ffn2_up, ffn2_down, 1)]
    ffn_w = [(bf(ffn1_up[0]), bf(ffn1_down[0]))]
    w_in = bf(ml_w_in)
    gate_pad = LANES - 2 * ML_HEADS
    w_gate = bf(jnp.pad(ml_w_in[0, :, 2 * HQK + 2 * HV:], ((0, 0), (0, gate_pad))))
    b_gate = jnp.pad(ml_b_gates[0], (0, gate_pad))[None, :]
    norm_w = ml_norm_w[0][None, :]
    w_q = bf(att_w_q[0].reshape(D_MODEL, ATT_KV_HEADS, ATT_GROUP, ATT_HD)
             .transpose(0, 2, 1, 3).reshape(D_MODEL, QW))
    w_o = bf(att_w_o[0].reshape(ATT_KV_HEADS, ATT_GROUP, ATT_HD, D_MODEL)
             .transpose(1, 0, 2, 3).reshape(QW, D_MODEL))
    sinks = att_sinks[0]
    lng = ln_g.reshape(DEPTH * 3, 1, D_MODEL)
    lnb = ln_b.reshape(DEPTH * 3, 1, D_MODEL)

    tab_p = _rope_tables(jnp.arange(sp, dtype=jnp.int32))
    tab_s = _rope_tables(jnp.tile(PAST_LEN + jnp.arange(ss, dtype=jnp.int32), bs))

    xp = x_prompt.reshape(bp * sp, D_MODEL)
    xp, *cast, w_out, w_kv = _ffn(xp, mods, 0, 0, *ffn_w[0], lng, lnb, 0, ts, sp,
                                  cast=ffn_f32[1] + (ml_w_out, kv_w[None]))
    ffn_w.append(tuple(cast))
    w_out, w_kv = w_out[0], w_kv[0]
    xp, state_p, m_p = _mlstm_prompt(xp.reshape(bp, sp, D_MODEL), mods, w_in, w_gate, b_gate,
                                     norm_w, w_out, lng, lnb, ts)
    xp, k_p, v_p, *cast = _ffn(xp.reshape(bp * sp, D_MODEL), mods, 0, 6, *ffn_w[1], lng, lnb, 2, ts, sp,
                               kv=(kv_mods, w_kv, tab_p), cast=ffn_f32[2])
    ffn_w.append(tuple(cast))
    k_p = k_p.reshape(bp, sp, KVW)
    v_p = v_p.reshape(bp, sp, KVW)
    xp, *cast = _ffn(xp, mods, 1, 0, *ffn_w[2], lng, lnb, 3, ts, sp, cast=ffn_f32[3])
    ffn_w.append(tuple(cast))
    xp = _attn_prompt(xp.reshape(bp, sp, D_MODEL), mods, w_q, tab_p, k_p, v_p, sinks, w_o, lng, lnb, ts)
    xp = _ffn(xp.reshape(bp * sp, D_MODEL), mods, 1, 6, *ffn_w[3], lng, lnb, 5, ts, sp)
    y_prompt = xp.reshape(bp, sp, D_MODEL)
    keep = min(WINDOW, sp)
    win_k_p = k_p[:, sp - keep:].reshape(bp, keep, ATT_KV_HEADS, ATT_HD)
    win_v_p = v_p[:, sp - keep:].reshape(bp, keep, ATT_KV_HEADS, ATT_HD)

    xs = x_sample.reshape(ts, D_MODEL)
    xs = _ffn(xs, mods, 0, 0, *ffn_w[0], lng, lnb, 0, ts, None)
    m0 = jnp.pad(state_mlstm_m[0], ((0, 0), (0, LANES - ML_HEADS)))
    xs, c_s, n_s, m_s = _mlstm_sample(xs, mods, w_in, w_gate, b_gate, norm_w, w_out, lng, lnb,
                                      state_mlstm_C[0], state_mlstm_n[0].reshape(bs, HQK), m0, ss)
    xs, k_s, v_s = _ffn(xs, mods, 0, 6, *ffn_w[1], lng, lnb, 2, ts, None, kv=(kv_mods, w_kv, tab_s))
    xs = _ffn(xs, mods, 1, 0, *ffn_w[2], lng, lnb, 3, ts, None)
    xs = _attn_sample(xs, mods, w_q, tab_s, k_s, v_s,
                      cache_win_k.reshape(bs, n_buf, KVW), cache_win_v.reshape(bs, n_buf, KVW),
                      sinks, w_o, lng, lnb, ss)
    xs = _ffn(xs, mods, 1, 6, *ffn_w[3], lng, lnb, 5, ts, None)
    y_sample = xs.reshape(bs, ss, D_MODEL)
    win_k_s = jnp.concatenate([cache_win_k, k_s.reshape(bs, ss, ATT_KV_HEADS, ATT_HD)], axis=1)[:, -n_buf:]
    win_v_s = jnp.concatenate([cache_win_v, v_s.reshape(bs, ss, ATT_KV_HEADS, ATT_HD)], axis=1)[:, -n_buf:]

    return (y_prompt, y_sample,
            state_p[None, ..., :ML_DV], state_p[None, :, :, 0, ML_DV:], m_p[None, :, :ML_HEADS, 0],
            win_k_p, win_v_p,
            c_s[None], n_s.reshape(1, bs, ML_HEADS, ML_DK), m_s[None, :, :ML_HEADS],
            win_k_s, win_v_s)
```

```python
import functools

import numpy as np
import jax
import jax.numpy as jnp
from jax import lax
from jax.experimental import pallas as pl
from jax.experimental.pallas import tpu as pltpu

F32 = jnp.float32
BF16 = jnp.bfloat16

D_MODEL = 1024
DEPTH = 2
PAST_LEN = 8192
ML_HEADS = 4
ML_DK = D_MODEL // 8
ML_DV = D_MODEL // 4
ATT_Q_HEADS = 16
ATT_KV_HEADS = 4
ATT_GROUP = ATT_Q_HEADS // ATT_KV_HEADS
ATT_HD = 64
WINDOW = 128
ROT_DIM = ATT_HD // 4
ROPE_THETA = 500000.0
D_FF = 2816
ALPHA = (2 * DEPTH) ** 0.25
LN_EPS = 1e-5
N_MOD = 9
HQK = ML_HEADS * ML_DK
HV = ML_HEADS * ML_DV
KVW = ATT_KV_HEADS * ATT_HD
QW = ATT_Q_HEADS * ATT_HD

LANES = 128
SUBLANES = 8
VMEM_LIMIT_BYTES = 56 * 1024 * 1024

FFN_ROWS = 1024
FFN_ROWS_KV = 512
FFN_CHUNK = 256
ML_ROWS = 1024
ML_CHUNK = 128
ATT_ROWS = 1024
ADA_COLS = (1536, 1024)
SAMPLE_GROUP = 8

NT_DIMS = (((1,), (1,)), ((), ()))
HIGHEST = lax.Precision.HIGHEST


def _params(semantics):
    return pltpu.CompilerParams(dimension_semantics=semantics, vmem_limit_bytes=VMEM_LIMIT_BYTES)


def _const_spec(block, index):
    return pl.BlockSpec(block, lambda *_: index, pipeline_mode=pl.Buffered(1))


def _resident(shape):
    return _const_spec(shape, (0,) * len(shape))


def _dot(a, b):
    return jnp.dot(a, b, preferred_element_type=F32)


def _dot_exact(a, b):
    return jnp.dot(a, b, preferred_element_type=F32, precision=HIGHEST)


def _nt(a, b):
    return lax.dot_general(a, b, NT_DIMS, preferred_element_type=F32)


def _iota(shape, dim):
    return lax.broadcasted_iota(jnp.int32, shape, dim)


def _silu(x):
    return x * jax.nn.sigmoid(x)


def _log_sigmoid(x):
    return jnp.minimum(x, 0.0) - jnp.log1p(jnp.exp(-jnp.abs(x)))


def _layer_norm(y, g, b):
    mu = jnp.mean(y, axis=-1, keepdims=True)
    d = y - mu
    var = jnp.mean(d * d, axis=-1, keepdims=True)
    return d * lax.rsqrt(var + LN_EPS) * g + b


def _mod(ref, batch):
    return ref[...] if batch is None else ref[pl.ds(batch, 1), :]


def _modulate_bf16(x, shift, scale):
    return (x * (1.0 + scale) + shift).astype(BF16)


def _rope(x, cos, sin_lo, sin_hi):
    width = x.shape[1]
    reps = width // LANES
    tile = lambda t: jnp.concatenate([t] * reps, axis=1) if reps > 1 else t
    x_up = pltpu.roll(x, width - ROT_DIM // 2, 1)
    x_dn = pltpu.roll(x, ROT_DIM // 2, 1)
    return x * tile(cos) + x_up * tile(sin_lo) + x_dn * tile(sin_hi)


def _mod_specs(layer, first, count, n_sample_rows, prompt):
    if prompt:
        blk = n_sample_rows // SUBLANES
        return [_const_spec((None, SUBLANES, D_MODEL), (layer, blk, first + j)) for j in range(count)]
    return [_const_spec((None, n_sample_rows, D_MODEL), (layer, 0, first + j)) for j in range(count)]


def _ln_specs(index):
    return [_const_spec((None, 1, D_MODEL), (index, 0, 0))] * 2


def _ada_kernel(c_ref, w_ref, b_ref, o_ref, ca_ref):
    @pl.when((pl.program_id(0) == 0) & (pl.program_id(1) == 0))
    def _():
        ca_ref[...] = _silu(c_ref[...]).astype(BF16)

    o_ref[...] = _dot(ca_ref[...], w_ref[...].astype(BF16)) + b_ref[...]


def _ada(c, w, b):
    n_l, k, n = w.shape
    m = c.shape[0]
    cols = next(width for width in ADA_COLS if n % width == 0)
    return pl.pallas_call(
        _ada_kernel,
        grid=(n_l, n // cols),
        in_specs=[pl.BlockSpec((m, k), lambda l, j: (0, 0)),
                  pl.BlockSpec((None, k, cols), lambda l, j: (l, 0, j)),
                  pl.BlockSpec((None, 1, cols), lambda l, j: (l, 0, j))],
        out_specs=pl.BlockSpec((None, m, cols), lambda l, j: (l, 0, j)),
        out_shape=jax.ShapeDtypeStruct((n_l, m, n), F32),
        scratch_shapes=[pltpu.VMEM((m, k), BF16)],
        compiler_params=_params(("arbitrary", "arbitrary")),
        name="ada",
    )(c, w, b)


def _ffn_kernel(x_ref, sh_ref, sc_ref, gt_ref, wa_ref, wu_ref, wd_ref, g_ref, b_ref, *rest,
                steps_per_seq, has_kv, n_cast):
    rest = list(rest)
    kv_in = [rest.pop(0) for _ in range(6)] if has_kv else None
    cast_in = [rest.pop(0) for _ in range(n_cast)]
    o_ref = rest.pop(0)
    kv_out = [rest.pop(0) for _ in range(2)] if has_kv else None
    cast_out = [rest.pop(0) for _ in range(n_cast)]
    h_ref, = rest

    batch = None if steps_per_seq is None else pl.program_id(0) // steps_per_seq
    x = x_ref[...]
    xm = _modulate_bf16(x, _mod(sh_ref, batch), _mod(sc_ref, batch))
    for c0 in range(0, D_FF, FFN_CHUNK):
        cw = min(FFN_CHUNK, D_FF - c0)
        a = _dot(xm, wa_ref[:, c0:c0 + cw])
        u = _dot(xm, wu_ref[:, c0:c0 + cw])
        h_ref[:, c0:c0 + cw] = (_silu(a) * u).astype(BF16)
    y = ALPHA * x + (0.5 * _mod(gt_ref, batch)) * _dot(h_ref[...], wd_ref[...])
    out = _layer_norm(y, g_ref[...], b_ref[...])
    o_ref[...] = out
    if has_kv:
        ksh_ref, ksc_ref, wkv_ref, cos_ref, slo_ref, shi_ref = kv_in
        kv = _dot(_modulate_bf16(out, _mod(ksh_ref, batch), _mod(ksc_ref, batch)), wkv_ref[...])
        kv_out[0][...] = _rope(kv[:, :KVW], cos_ref[...], slo_ref[...], shi_ref[...])
        kv_out[1][...] = kv[:, KVW:]
    for src, dst in zip(cast_in, cast_out):
        dst[...] = src[...].astype(BF16)


def _ffn(x, mods, layer, first, w_up, w_down, ln_g, ln_b, ln_index, n_sample_rows, seq, kv=None, cast=None):
    t = x.shape[0]
    rows = min(FFN_ROWS if kv is None else FFN_ROWS_KV, t)
    steps = t // rows
    x_spec = pl.BlockSpec((rows, D_MODEL), lambda i: (i, 0))
    in_specs = ([x_spec] + _mod_specs(layer, first, 3, n_sample_rows, seq is not None)
                + [_const_spec((D_MODEL, D_FF), (0, 0)), _const_spec((D_MODEL, D_FF), (0, 1)),
                   _resident(w_down.shape)]
                + _ln_specs(ln_index))
    args = [x, mods, mods, mods, w_up, w_up, w_down, ln_g, ln_b]
    out_specs = [x_spec]
    out_shape = [jax.ShapeDtypeStruct((t, D_MODEL), F32)]
    if kv is not None:
        kv_mods, w_kv, tables = kv
        pos_blocks = tables[0].shape[0] // rows
        tab_spec = pl.BlockSpec((rows, LANES), lambda i: (i % pos_blocks, 0))
        kv_spec = pl.BlockSpec((rows, KVW), lambda i: (i, 0))
        in_specs += (_mod_specs(0, 0, 2, n_sample_rows, seq is not None)
                     + [_resident(w_kv.shape), tab_spec, tab_spec, tab_spec])
        args += [kv_mods, kv_mods, w_kv, *tables]
        out_specs += [kv_spec, kv_spec]
        out_shape += [jax.ShapeDtypeStruct((t, KVW), F32)] * 2
    n_cast = 0
    if cast is not None:
        up_f32, down_f32, src_layer, *extra = cast
        up_rows = D_MODEL // steps
        down_rows = 2 * D_FF // steps
        assert up_rows % 16 == 0 and down_rows % 16 == 0 and steps % 2 == 0
        in_specs += [pl.BlockSpec((None, up_rows, 2 * D_FF), lambda i: (src_layer, i, 0)),
                     pl.BlockSpec((None, down_rows, D_MODEL), lambda i: (src_layer, i // 2, 0))]
        args += [up_f32, down_f32]
        out_specs += [pl.BlockSpec((up_rows, 2 * D_FF), lambda i: (i, 0)),
                      pl.BlockSpec((down_rows, D_MODEL), lambda i: (i // 2, 0))]
        out_shape += [jax.ShapeDtypeStruct((D_MODEL, 2 * D_FF), BF16),
                      jax.ShapeDtypeStruct((D_FF, D_MODEL), BF16)]
        for w in extra:
            spec = pl.BlockSpec((None, up_rows, w.shape[2]), lambda i: (0, i, 0))
            in_specs.append(spec)
            args.append(w)
            out_specs.append(spec)
            out_shape.append(jax.ShapeDtypeStruct(w.shape, BF16))
        n_cast = 2 + len(extra)
    outs = pl.pallas_call(
        functools.partial(_ffn_kernel, steps_per_seq=None if seq is None else seq // rows,
                          has_kv=kv is not None, n_cast=n_cast),
        grid=(steps,),
        in_specs=in_specs,
        out_specs=out_specs,
        out_shape=out_shape,
        scratch_shapes=[pltpu.VMEM((rows, D_FF), BF16)],
        compiler_params=_params(("arbitrary",)),
        name="ffn" + ("_kv" if kv is not None else "") + ("_cast" if cast is not None else ""),
    )(*args)
    return outs[0] if len(outs) == 1 else outs


def _head_norm(hh, w_row):
    mu = jnp.mean(hh, axis=-1, keepdims=True)
    d = hh - mu
    var = jnp.mean(d * d, axis=-1, keepdims=True)
    return d * lax.rsqrt(var + LN_EPS) * w_row


def _split3(x, axis):
    hi = x.astype(BF16)
    r1 = x - hi.astype(F32)
    mid = r1.astype(BF16)
    lo = (r1 - mid.astype(F32)).astype(BF16)
    return jnp.concatenate([hi, mid, lo], axis=axis)


def _tile_lanes(t, reps):
    return jnp.concatenate([t] * reps, axis=1)


def _mlstm_prompt_kernel(x_ref, sh_ref, sc_ref, gt_ref, wp_ref, wg_ref, bg_ref, nw_ref, wo_ref, g_ref, b_ref,
                         o_ref, c_out, m_out, proj_ref, rep_ref, y_ref, c_ref, mm_ref):
    batch = pl.program_id(0)
    step = pl.program_id(1)
    size = ML_CHUNK
    assert size == LANES and ML_DK == LANES

    @pl.when(step == 0)
    def _():
        c_ref[...] = jnp.zeros_like(c_ref)
        mm_ref[...] = jnp.zeros_like(mm_ref)

    x = x_ref[...]
    rows = x.shape[0]
    n_chunks = rows // size
    h = _modulate_bf16(x, _mod(sh_ref, batch), _mod(sc_ref, batch))
    gates = _dot(h, wg_ref[...]) + bg_ref[...]

    r = _iota((size, size), 0)
    c = _iota((size, size), 1)
    causal = c <= r
    eye = c == r
    tri3 = _tile_lanes(jnp.where(causal, 1.0, 0.0).astype(BF16), 3)
    n_rep = 2 * ML_HEADS
    spread = jnp.where(_iota((LANES, n_rep * LANES), 1) // LANES == _iota((LANES, n_rep * LANES), 0),
                       1.0, 0.0).astype(BF16)
    spread3 = jnp.concatenate([spread] * 3, axis=0)
    ones3 = jnp.ones((3 * size, LANES), BF16)
    sr = _iota((2 * SUBLANES, LANES), 0)
    sl = _iota((2 * SUBLANES, LANES), 1)
    pick = (jnp.where((sr < ML_HEADS) & (sl == sr), 1.0, 0.0)
            - jnp.where((sr < ML_HEADS) & (sl == sr + ML_HEADS), 1.0, 0.0)
            + jnp.where((sr >= SUBLANES) & (sr < SUBLANES + ML_HEADS) & (sl == sr - SUBLANES + ML_HEADS),
                        1.0, 0.0))
    pick3 = _tile_lanes(pick.astype(BF16), 3)

    log_f = _log_sigmoid(gates)
    bwide = jnp.concatenate([_dot(tri3, _split3(log_f[ci * size:(ci + 1) * size, :], 0))
                             for ci in range(n_chunks)], axis=0)
    src3 = _split3(jnp.where(_iota((rows, LANES), 1) < ML_HEADS, gates, bwide), 1)
    rep_ref[...] = _dot(src3, spread3)
    ab_rows = _nt(pick3, src3)
    a_rows = ab_rows[:SUBLANES, :]
    b_rows = ab_rows[SUBLANES:, :]
    lane_in_chunk = _iota((SUBLANES, rows), 1) % size
    run = a_rows
    shift = 1
    while shift < size:
        run = jnp.maximum(run, jnp.where(lane_in_chunk >= shift, pltpu.roll(run, shift, 1), -jnp.inf))
        shift *= 2
    m_rows = [mm_ref[...]]
    g_rows = []
    for ci in range(n_chunks):
        run_c = run[:, ci * size:(ci + 1) * size]
        g_rows.append(jnp.maximum(run_c, m_rows[-1]))
        end = (ci + 1) * size - 1
        m_end = b_rows[:, end:end + 1] + jnp.maximum(run[:, end:end + 1], m_rows[-1][:, :1])
        m_rows.append(jnp.broadcast_to(m_end, (SUBLANES, LANES)))
    mm_ref[...] = m_rows[-1]

    proj_ref[...] = _dot(h, wp_ref[...])

    states = [c_ref[hd] for hd in range(ML_HEADS)]
    for ci in range(n_chunks):
        lo, hi = ci * size, (ci + 1) * size
        for hd in range(ML_HEADS):
            i_rep = rep_ref[lo:hi, hd * LANES:(hd + 1) * LANES]
            b_rep = rep_ref[lo:hi, (ML_HEADS + hd) * LANES:(ML_HEADS + hd + 1) * LANES]
            m_prev = m_rows[ci][hd:hd + 1, :]
            m_new = m_rows[ci + 1][hd:hd + 1, :]
            a_row = a_rows[hd:hd + 1, lo:hi]
            g_rep = _dot(_split3(jnp.where(eye, g_rows[ci][hd:hd + 1, :], 0.0), 1), ones3)
            w = jnp.where(causal, jnp.exp(a_row - g_rep), 0.0)
            w_init = jnp.exp(m_prev - g_rep)
            floor = jnp.exp(-(b_rep + g_rep))
            b_last = b_rep[size - 1:size, :]
            w_end = jnp.exp(b_last - b_rep + i_rep - m_new)
            decay = jnp.exp(b_last + m_prev - m_new)

            q = proj_ref[lo:hi, hd * ML_DK:(hd + 1) * ML_DK]
            k = proj_ref[lo:hi, HQK + hd * ML_DK:HQK + (hd + 1) * ML_DK] * (ML_DK ** -0.5)
            v = proj_ref[lo:hi, 2 * HQK + hd * ML_DV:2 * HQK + (hd + 1) * ML_DV]
            qb = q.astype(BF16)
            vb = v.astype(BF16)
            c_mat = states[hd][:, :ML_DV]
            n_row = states[hd][0:1, ML_DV:]
            s = _nt(qb, k.astype(BF16)) * w
            num = _dot(s.astype(BF16), vb) + _tile_lanes(w_init, 2) * _dot(qb, c_mat.astype(BF16))
            den = (jnp.sum(s, axis=-1, keepdims=True)
                   + w_init[:, :1] * jnp.sum(q * n_row, axis=-1, keepdims=True))
            hh = num / jnp.maximum(jnp.abs(den), floor[:, :1])
            kw = k * w_end
            c_new = _tile_lanes(decay, 2) * c_mat + _dot(kw.T.astype(BF16), vb)
            n_new = decay * n_row + jnp.sum(kw, axis=0, keepdims=True)
            states[hd] = jnp.concatenate([c_new, jnp.broadcast_to(n_new, (ML_DK, LANES))], axis=1)
            y_ref[lo:hi, hd * ML_DV:(hd + 1) * ML_DV] = _head_norm(hh, nw_ref[:, hd * ML_DV:(hd + 1) * ML_DV])
    for hd in range(ML_HEADS):
        c_ref[hd] = states[hd]

    o_gate = jax.nn.sigmoid(proj_ref[:, 2 * HQK + HV:2 * HQK + 2 * HV])
    y = _dot((y_ref[...] * o_gate).astype(BF16), wo_ref[...])
    o_ref[...] = _layer_norm(ALPHA * x + _mod(gt_ref, batch) * y, g_ref[...], b_ref[...])

    @pl.when(step == pl.num_programs(1) - 1)
    def _():
        c_out[...] = c_ref[...]
        m_out[...] = mm_ref[...]


def _mlstm_weight_specs(wg, bg, nw, wo):
    return [_const_spec((None, D_MODEL, 2 * HQK + 2 * HV), (0, 0, 0)), _resident(wg.shape),
            _resident(bg.shape), _resident(nw.shape), _resident(wo.shape)] + _ln_specs(1)


def _mlstm_prompt(x, mods, w_in, wg, bg, nw, wo, ln_g, ln_b, n_sample_rows):
    bsz, seq, _ = x.shape
    rows = min(ML_ROWS, seq)
    x_spec = pl.BlockSpec((None, rows, D_MODEL), lambda bi, i: (bi, i, 0))
    state_shape = (ML_HEADS, ML_DK, ML_DV + LANES)
    return pl.pallas_call(
        _mlstm_prompt_kernel,
        grid=(bsz, seq // rows),
        in_specs=[x_spec] + _mod_specs(0, 3, 3, n_sample_rows, True) + _mlstm_weight_specs(wg, bg, nw, wo),
        out_specs=[x_spec, pl.BlockSpec((None,) + state_shape, lambda bi, i: (bi, 0, 0, 0)),
                   pl.BlockSpec((None, SUBLANES, LANES), lambda bi, i: (bi, 0, 0))],
        out_shape=[jax.ShapeDtypeStruct((bsz, seq, D_MODEL), F32),
                   jax.ShapeDtypeStruct((bsz,) + state_shape, F32),
                   jax.ShapeDtypeStruct((bsz, SUBLANES, LANES), F32)],
        scratch_shapes=[pltpu.VMEM((rows, 2 * HQK + 2 * HV), F32),
                        pltpu.VMEM((rows, 2 * ML_HEADS * LANES), F32),
                        pltpu.VMEM((rows, HV), F32),
                        pltpu.VMEM(state_shape, F32),
                        pltpu.VMEM((SUBLANES, LANES), F32)],
        compiler_params=_params(("arbitrary", "arbitrary")),
        name="mlstm_prompt",
    )(x, mods, mods, mods, w_in, wg, bg, nw, wo, ln_g, ln_b)


def _mlstm_sample_kernel(x_ref, sh_ref, sc_ref, gt_ref, wp_ref, wg_ref, bg_ref, nw_ref, wo_ref, g_ref, b_ref,
                         c0_ref, n0_ref, m0_ref, o_ref, c_out, n_out, m_out,
                         proj_ref, gate_ref, y_ref, *, seq):
    step = pl.program_id(0)
    grp = c0_ref.shape[0]
    rows = grp * seq

    @pl.when(step == 0)
    def _():
        h = _modulate_bf16(x_ref[...], sh_ref[...], sc_ref[...])
        proj_ref[...] = _dot(h, wp_ref[...])
        gate_ref[...] = _dot(h, wg_ref[...]) + bg_ref[...]

    r = _iota((rows, rows), 0)
    c = _iota((rows, rows), 1)
    same = (r // seq) == (c // seq)
    causal = same & (c <= r)
    eye = r == c
    last_of_row = c == (r // seq) * seq + (seq - 1)
    same_f = same.astype(F32)
    expand = (_iota((rows, grp), 0) // seq == _iota((rows, grp), 1)).astype(F32)
    pick_last = (_iota((grp, rows), 1) == _iota((grp, rows), 0) * seq + (seq - 1)).astype(F32)
    ident = (_iota((ML_DK, ML_DK), 0) == _iota((ML_DK, ML_DK), 1)).astype(F32)
    seq_of_row = _iota((rows, ML_DV), 0) // seq
    col_seq = _iota((ML_DK, rows), 1) // seq
    lane = _iota((rows, LANES), 1)
    to_row = lambda col: jnp.sum(jnp.where(eye, col, 0.0), axis=0, keepdims=True)

    r0 = pl.multiple_of(step * rows, rows)
    gates = gate_ref[pl.ds(r0, rows), :]
    log_f = _log_sigmoid(gates)
    bwide = _dot_exact(causal.astype(F32), log_f)
    bsum = _dot_exact(same_f, log_f)
    m_prev_wide = _dot_exact(expand, m0_ref[...])
    n_rows_wide = _dot_exact(expand, n0_ref[...])
    n_pieces = []
    m_wide = jnp.zeros((rows, LANES), F32)
    for hd in range(ML_HEADS):
        q = proj_ref[pl.ds(r0, rows), hd * ML_DK:(hd + 1) * ML_DK]
        k = proj_ref[pl.ds(r0, rows), HQK + hd * ML_DK:HQK + (hd + 1) * ML_DK] * (ML_DK ** -0.5)
        v = proj_ref[pl.ds(r0, rows), 2 * HQK + hd * ML_DV:2 * HQK + (hd + 1) * ML_DV]
        bcol = bwide[:, ML_HEADS + hd:ML_HEADS + hd + 1]
        b_last = bsum[:, ML_HEADS + hd:ML_HEADS + hd + 1]
        icol = gates[:, hd:hd + 1]
        m_prev = m_prev_wide[:, hd:hd + 1]
        n_rows = n_rows_wide[:, hd * ML_DK:(hd + 1) * ML_DK]
        log_w = jnp.where(causal, bcol - to_row(bcol) + to_row(icol), -jnp.inf)
        log_init = bcol + m_prev
        m_t = jnp.maximum(log_init, jnp.max(log_w, axis=-1, keepdims=True))
        w = jnp.exp(log_w - m_t)
        w_init = jnp.exp(log_init - m_t)
        qb = q.astype(BF16)
        s = _nt(qb, k.astype(BF16)) * w
        inter = None
        for bi in range(grp):
            cand = _dot(qb, c0_ref[bi, hd].astype(BF16))
            inter = cand if inter is None else jnp.where(seq_of_row == bi, cand, inter)
        num = _dot(s.astype(BF16), v.astype(BF16)) + w_init * inter
        den = jnp.sum(s, axis=-1, keepdims=True) + w_init * jnp.sum(q * n_rows, axis=-1, keepdims=True)
        hh = num / jnp.maximum(jnp.abs(den), jnp.exp(-m_t))
        m_new = jnp.sum(jnp.where(last_of_row, to_row(m_t), 0.0), axis=1, keepdims=True)
        w_end = jnp.exp(b_last - bcol + icol - m_new)
        decay = jnp.exp(b_last + m_prev - m_new)
        kw = k * w_end
        kw_t = lax.dot_general(ident, kw, NT_DIMS, preferred_element_type=F32, precision=HIGHEST)
        lhs = jnp.concatenate([jnp.where(col_seq == bi, kw_t, 0.0) for bi in range(grp)], axis=0)
        upd = _dot(lhs.astype(BF16), v.astype(BF16))
        for bi in range(grp):
            c_out[bi, hd] = (decay[bi * seq:bi * seq + 1, :] * c0_ref[bi, hd]
                             + upd[bi * ML_DK:(bi + 1) * ML_DK, :])
        n_pieces.append(decay * n_rows + _dot_exact(same_f, kw))
        m_wide = jnp.where(lane == hd, m_new, m_wide)
        y_ref[pl.ds(r0, rows), hd * ML_DV:(hd + 1) * ML_DV] = _head_norm(
            hh, nw_ref[:, hd * ML_DV:(hd + 1) * ML_DV])
    n_out[...] = _dot_exact(pick_last, jnp.concatenate(n_pieces, axis=1))
    m_out[...] = _dot_exact(pick_last, m_wide)

    @pl.when(step == pl.num_programs(0) - 1)
    def _():
        o_gate = jax.nn.sigmoid(proj_ref[:, 2 * HQK + HV:2 * HQK + 2 * HV])
        y = _dot((y_ref[...] * o_gate).astype(BF16), wo_ref[...])
        o_ref[...] = _layer_norm(ALPHA * x_ref[...] + gt_ref[...] * y, g_ref[...], b_ref[...])


def _mlstm_sample(x, mods, w_in, wg, bg, nw, wo, ln_g, ln_b, c0, n0, m0, seq):
    t = x.shape[0]
    grp = SAMPLE_GROUP
    state_specs = [pl.BlockSpec((grp, ML_HEADS, ML_DK, ML_DV), lambda i: (i, 0, 0, 0)),
                   pl.BlockSpec((grp, HQK), lambda i: (i, 0)),
                   pl.BlockSpec((grp, LANES), lambda i: (i, 0))]
    return pl.pallas_call(
        functools.partial(_mlstm_sample_kernel, seq=seq),
        grid=(t // (grp * seq),),
        in_specs=[_resident(x.shape)] + _mod_specs(0, 3, 3, t, False)
                 + _mlstm_weight_specs(wg, bg, nw, wo) + state_specs,
        out_specs=[pl.BlockSpec((t, D_MODEL), lambda i: (0, 0))] + state_specs,
        out_shape=[jax.ShapeDtypeStruct((t, D_MODEL), F32),
                   jax.ShapeDtypeStruct(c0.shape, F32),
                   jax.ShapeDtypeStruct(n0.shape, F32),
                   jax.ShapeDtypeStruct(m0.shape, F32)],
        scratch_shapes=[pltpu.VMEM((t, 2 * HQK + 2 * HV), F32),
                        pltpu.VMEM((t, LANES), F32),
                        pltpu.VMEM((t, HV), F32)],
        compiler_params=_params(("arbitrary",)),
        name="mlstm_sample",
    )(x, mods, mods, mods, w_in, wg, bg, nw, wo, ln_g, ln_b, c0, n0, m0)


def _sink_column(sink_ref, kh, rows_per_head):
    rid = _iota((ATT_GROUP * rows_per_head, 1), 0)
    col = jnp.full((ATT_GROUP * rows_per_head, 1), sink_ref[kh * ATT_GROUP], F32)
    for gq in range(1, ATT_GROUP):
        col = jnp.where(rid >= gq * rows_per_head, sink_ref[kh * ATT_GROUP + gq], col)
    return col


def _head_lanes(x, kh):
    return jnp.where(_iota(x.shape, 1) // ATT_HD == kh, x, 0.0).astype(BF16)


def _stack_groups(x):
    return jnp.concatenate([x[:, g * KVW:(g + 1) * KVW] for g in range(ATT_GROUP)], axis=0)


def _attn_prompt_kernel(sink_ref, x_ref, sh_ref, sc_ref, gt_ref, wq_ref, cos_ref, slo_ref, shi_ref,
                        kp_ref, kc_ref, vp_ref, vc_ref, wo_ref, g_ref, b_ref, o_ref, a_ref):
    batch = pl.program_id(0)
    step = pl.program_id(1)
    x = x_ref[...]
    rows = x.shape[0]
    n_blk = rows // WINDOW
    q = _rope(_dot(_modulate_bf16(x, _mod(sh_ref, batch), _mod(sc_ref, batch)), wq_ref[...]),
              cos_ref[...], slo_ref[...], shi_ref[...])
    q = (q * (ATT_HD ** -0.5)).astype(BF16)

    keys = jnp.concatenate([kp_ref[...], kc_ref[...]], axis=0)
    vals = jnp.concatenate([vp_ref[...], vc_ref[...]], axis=0)
    k_heads = [_head_lanes(keys, kh) for kh in range(ATT_KV_HEADS)]
    v_heads = [_head_lanes(vals, kh) for kh in range(ATT_KV_HEADS)]

    rid = _iota((ATT_GROUP * WINDOW, WINDOW), 0) % WINDOW
    cid = _iota((ATT_GROUP * WINDOW, WINDOW), 1)
    from_prev = cid > rid
    no_prev = jnp.where(step > 0, 0.0, -jnp.inf)
    sinks = [_sink_column(sink_ref, kh, WINDOW) for kh in range(ATT_KV_HEADS)]

    scores = [[_nt(_stack_groups(q[n * WINDOW:(n + 1) * WINDOW, :]),
                   k_heads[kh][n * WINDOW:(n + 2) * WINDOW, :])
               for kh in range(ATT_KV_HEADS)] for n in range(n_blk)]
    probs = []
    for n in range(n_blk):
        row = []
        for kh in range(ATT_KV_HEADS):
            s_prev = scores[n][kh][:, :WINDOW]
            if n == 0:
                s_prev = s_prev + no_prev
            s = jnp.where(from_prev, s_prev, scores[n][kh][:, WINDOW:])
            mx = jnp.maximum(jnp.max(s, axis=-1, keepdims=True), sinks[kh])
            p = jnp.exp(s - mx)
            den = jnp.sum(p, axis=-1, keepdims=True) + jnp.exp(sinks[kh] - mx)
            pn = p / den
            row.append(jnp.where(from_prev, pn, 0.0).astype(BF16))
            row.append(jnp.where(from_prev, 0.0, pn).astype(BF16))
        probs.append(jnp.concatenate(row, axis=1))
    for n in range(n_blk):
        v_all = jnp.concatenate([v_heads[kh][n * WINDOW:(n + 2) * WINDOW, :]
                                 for kh in range(ATT_KV_HEADS)], axis=0)
        o = _dot(probs[n], v_all)
        for g in range(ATT_GROUP):
            a_ref[n * WINDOW:(n + 1) * WINDOW, g * KVW:(g + 1) * KVW] = o[g * WINDOW:(g + 1) * WINDOW, :]

    y = _dot(a_ref[...].astype(BF16), wo_ref[...])
    o_ref[...] = _layer_norm(ALPHA * x + _mod(gt_ref, batch) * y, g_ref[...], b_ref[...])


def _attn_prompt(x, mods, wq, tables, k, v, sinks, wo, ln_g, ln_b, n_sample_rows):
    bsz, seq, _ = x.shape
    rows = min(ATT_ROWS, seq)
    per = rows // WINDOW
    x_spec = pl.BlockSpec((None, rows, D_MODEL), lambda bi, i: (bi, i, 0))
    cur = pl.BlockSpec((None, rows, KVW), lambda bi, i: (bi, i, 0))
    prev = pl.BlockSpec((None, WINDOW, KVW), lambda bi, i: (bi, jnp.maximum(i * per - 1, 0), 0))
    tab = pl.BlockSpec((rows, LANES), lambda bi, i: (i, 0))
    return pl.pallas_call(
        _attn_prompt_kernel,
        grid=(bsz, seq // rows),
        in_specs=[pl.BlockSpec(memory_space=pltpu.SMEM), x_spec]
                 + _mod_specs(1, 3, 3, n_sample_rows, True)
                 + [_resident(wq.shape), tab, tab, tab, prev, cur, prev, cur, _resident(wo.shape)]
                 + _ln_specs(4),
        out_specs=x_spec,
        out_shape=jax.ShapeDtypeStruct((bsz, seq, D_MODEL), F32),
        scratch_shapes=[pltpu.VMEM((rows, QW), F32)],
        compiler_params=_params(("arbitrary", "arbitrary")),
        name="attn_prompt",
    )(sinks, x, mods, mods, mods, wq, *tables, k, k, v, v, wo, ln_g, ln_b)


def _attn_sample_kernel(sink_ref, x_ref, sh_ref, sc_ref, gt_ref, wq_ref, cos_ref, slo_ref, shi_ref,
                        kn_ref, vn_ref, kb_ref, vb_ref, wo_ref, g_ref, b_ref, o_ref,
                        q_ref, a_ref, *, seq):
    step = pl.program_id(0)
    pair = 2 * seq
    assert pair == SUBLANES
    grp, n_buf, _ = kb_ref.shape

    @pl.when(step == 0)
    def _():
        q = _rope(_dot(_modulate_bf16(x_ref[...], sh_ref[...], sc_ref[...]), wq_ref[...]),
                  cos_ref[...], slo_ref[...], shi_ref[...])
        q_ref[...] = q * (ATT_HD ** -0.5)

    n_rows = ATT_GROUP * pair
    n_all = ATT_KV_HEADS * n_rows
    n_new = 2 * pair
    q_pos = _iota((n_all, n_buf), 0) % seq + PAST_LEN
    k_pos = _iota((n_all, n_buf), 1) + (PAST_LEN - n_buf)
    buf_mask = (k_pos <= q_pos) & (k_pos > q_pos - WINDOW) & (k_pos >= 0)
    q_tok = _iota((n_all, n_new), 0) % seq
    new_col = _iota((n_all, n_new), 1)
    out_lane_head = _iota((n_rows, KVW), 1) // ATT_HD
    low = _iota((n_rows, KVW), 0) % pair < seq
    head_of_row = _iota((n_all, 1), 0) // pair
    sink_col = jnp.full((n_all, 1), sink_ref[0], F32)
    for hq in range(1, ATT_Q_HEADS):
        sink_col = jnp.where(head_of_row >= hq, sink_ref[hq], sink_col)
    pad_rows = jnp.zeros((n_new - pair, KVW), BF16)
    for p in range(grp // 2):
        r0 = pl.multiple_of(step * (grp * seq) + p * pair, pair)
        q_all = _stack_groups(q_ref[pl.ds(r0, pair), :].astype(BF16))
        q_bd = jnp.concatenate([_head_lanes(q_all, kh) for kh in range(ATT_KV_HEADS)], axis=0)
        k_new = jnp.concatenate([kn_ref[pl.ds(r0, pair), :].astype(BF16), pad_rows], axis=0)
        v_new = jnp.concatenate([vn_ref[pl.ds(r0, pair), :].astype(BF16), pad_rows], axis=0)
        s_new_all = _nt(q_bd, k_new)
        halves = []
        for half in range(2):
            bi = 2 * p + half
            tok = new_col - half * seq
            own_new = (tok >= 0) & (tok < seq) & (tok <= q_tok)
            s_buf = jnp.where(buf_mask, _nt(q_bd, kb_ref[bi].astype(BF16)), -jnp.inf)
            s_new = jnp.where(own_new, s_new_all, -jnp.inf)
            mx = jnp.maximum(jnp.maximum(jnp.max(s_buf, axis=-1, keepdims=True),
                                         jnp.max(s_new, axis=-1, keepdims=True)), sink_col)
            e_buf = jnp.exp(s_buf - mx)
            e_new = jnp.exp(s_new - mx)
            den = (jnp.sum(e_buf, axis=-1, keepdims=True) + jnp.sum(e_new, axis=-1, keepdims=True)
                   + jnp.exp(sink_col - mx))
            o_all = (_dot((e_buf / den).astype(BF16), vb_ref[bi].astype(BF16))
                     + _dot((e_new / den).astype(BF16), v_new))
            o_half = None
            for kh in range(ATT_KV_HEADS):
                part = jnp.where(out_lane_head == kh, o_all[kh * n_rows:(kh + 1) * n_rows, :], 0.0)
                o_half = part if o_half is None else o_half + part
            halves.append(o_half)
        o = jnp.where(low, halves[0], halves[1])
        for g in range(ATT_GROUP):
            a_ref[pl.ds(r0, pair), g * KVW:(g + 1) * KVW] = o[g * pair:(g + 1) * pair, :]

    @pl.when(step == pl.num_programs(0) - 1)
    def _():
        y = _dot(a_ref[...].astype(BF16), wo_ref[...])
        o_ref[...] = _layer_norm(ALPHA * x_ref[...] + gt_ref[...] * y, g_ref[...], b_ref[...])


def _attn_sample(x, mods, wq, tables, k_new, v_new, k_buf, v_buf, sinks, wo, ln_g, ln_b, seq):
    t = x.shape[0]
    grp = SAMPLE_GROUP
    buf_spec = pl.BlockSpec((grp,) + k_buf.shape[1:], lambda i: (i, 0, 0))
    return pl.pallas_call(
        functools.partial(_attn_sample_kernel, seq=seq),
        grid=(t // (grp * seq),),
        in_specs=[pl.BlockSpec(memory_space=pltpu.SMEM), _resident(x.shape)]
                 + _mod_specs(1, 3, 3, t, False)
                 + [_resident(wq.shape)] + [_resident(tab.shape) for tab in tables]
                 + [_resident(k_new.shape), _resident(v_new.shape), buf_spec, buf_spec, _resident(wo.shape)]
                 + _ln_specs(4),
        out_specs=pl.BlockSpec((t, D_MODEL), lambda i: (0, 0)),
        out_shape=jax.ShapeDtypeStruct((t, D_MODEL), F32),
        scratch_shapes=[pltpu.VMEM((t, QW), F32), pltpu.VMEM((t, QW), F32)],
        compiler_params=_params(("arbitrary",)),
        name="attn_sample",
    )(sinks, x, mods, mods, mods, wq, *tables, k_new, v_new, k_buf, v_buf, wo, ln_g, ln_b)


def _rope_tables(pos):
    half = ROT_DIM // 2
    inv_freq = ROPE_THETA ** (-jnp.arange(half, dtype=F32) * 2.0 / ROT_DIM)
    ang = pos.astype(F32)[:, None] * inv_freq[None, :]
    cos, sin = jnp.cos(ang), jnp.sin(ang)
    dim = np.arange(LANES) % ATT_HD
    freq = np.arange(half)[:, None]
    lo = (dim[None, :] == freq).astype(np.float32)
    hi = (dim[None, :] == freq + half).astype(np.float32)
    spread = lambda a, m: jnp.dot(a, jnp.asarray(m), precision=HIGHEST)
    return (spread(cos, lo + hi) + jnp.asarray((dim >= ROT_DIM).astype(np.float32)),
            spread(sin, -lo), spread(sin, hi))


def kernel(x_prompt, x_sample, c_prompt, c_sample, state_mlstm_C, state_mlstm_n, state_mlstm_m,
           cache_win_k, cache_win_v, ada_w, ada_b, ffn1_up, ffn1_down, ffn2_up, ffn2_down, ln_g, ln_b,
           ml_w_in, ml_b_gates, ml_norm_w, ml_w_out, kv_ada_w, kv_ada_b, kv_w,
           att_w_q, att_sinks, att_w_o):
    bp, sp, _ = x_prompt.shape
    bs, ss, _ = x_sample.shape
    n_buf = cache_win_k.shape[1]
    ts = bs * ss

    c_all = jnp.concatenate([jnp.repeat(c_sample, ss, axis=0), c_prompt,
                             jnp.zeros((-bp % (2 * SUBLANES), D_MODEL), F32)], axis=0)
    mods = _ada(c_all, ada_w, ada_b[:, None, :])
    kv_mods = _ada(c_all, kv_ada_w[None], kv_ada_b[None, None, :])

    bf = lambda w: w.astype(BF16)
    ffn_f32 = [(ffn1_up, ffn1_down, 0), (ffn2_up, ffn2_down, 0), (ffn1_up, ffn1_down, 1), (ffn2_up, ffn2_down, 1)]
    ffn_w = [(bf(ffn1_up[0]), bf(ffn1_down[0]))]
    w_in = bf(ml_w_in)
    gate_pad = LANES - 2 * ML_HEADS
    w_gate = bf(jnp.pad(ml_w_in[0, :, 2 * HQK + 2 * HV:], ((0, 0), (0, gate_pad))))
    b_gate = jnp.pad(ml_b_gates[0], (0, gate_pad))[None, :]
    norm_w = ml_norm_w[0][None, :]
    w_q = bf(att_w_q[0].reshape(D_MODEL, ATT_KV_HEADS, ATT_GROUP, ATT_HD)
             .transpose(0, 2, 1, 3).reshape(D_MODEL, QW))
    w_o = bf(att_w_o[0].reshape(ATT_KV_HEADS, ATT_GROUP, ATT_HD, D_MODEL)
             .transpose(1, 0, 2, 3).reshape(QW, D_MODEL))
    sinks = att_sinks[0]
    lng = ln_g.reshape(DEPTH * 3, 1, D_MODEL)
    lnb = ln_b.reshape(DEPTH * 3, 1, D_MODEL)

    tab_p = _rope_tables(jnp.arange(sp, dtype=jnp.int32))
    tab_s = _rope_tables(jnp.tile(PAST_LEN + jnp.arange(ss, dtype=jnp.int32), bs))

    xp = x_prompt.reshape(bp * sp, D_MODEL)
    xp, *cast, w_out, w_kv = _ffn(xp, mods, 0, 0, *ffn_w[0], lng, lnb, 0, ts, sp,
                                  cast=ffn_f32[1] + (ml_w_out, kv_w[None]))
    ffn_w.append(tuple(cast))
    w_out, w_kv = w_out[0], w_kv[0]
    xp, state_p, m_p = _mlstm_prompt(xp.reshape(bp, sp, D_MODEL), mods, w_in, w_gate, b_gate,
                                     norm_w, w_out, lng, lnb, ts)
    xp, k_p, v_p, *cast = _ffn(xp.reshape(bp * sp, D_MODEL), mods, 0, 6, *ffn_w[1], lng, lnb, 2, ts, sp,
                               kv=(kv_mods, w_kv, tab_p), cast=ffn_f32[2])
    ffn_w.append(tuple(cast))
    k_p = k_p.reshape(bp, sp, KVW)
    v_p = v_p.reshape(bp, sp, KVW)
    xp, *cast = _ffn(xp, mods, 1, 0, *ffn_w[2], lng, lnb, 3, ts, sp, cast=ffn_f32[3])
    ffn_w.append(tuple(cast))
    xp = _attn_prompt(xp.reshape(bp, sp, D_MODEL), mods, w_q, tab_p, k_p, v_p, sinks, w_o, lng, lnb, ts)
    xp = _ffn(xp.reshape(bp * sp, D_MODEL), mods, 1, 6, *ffn_w[3], lng, lnb, 5, ts, sp)
    y_prompt = xp.reshape(bp, sp, D_MODEL)
    keep = min(WINDOW, sp)
    win_k_p = k_p[:, sp - keep:].reshape(bp, keep, ATT_KV_HEADS, ATT_HD)
    win_v_p = v_p[:, sp - keep:].reshape(bp, keep, ATT_KV_HEADS, ATT_HD)

    xs = x_sample.reshape(ts, D_MODEL)
    xs = _ffn(xs, mods, 0, 0, *ffn_w[0], lng, lnb, 0, ts, None)
    m0 = jnp.pad(state_mlstm_m[0], ((0, 0), (0, LANES - ML_HEADS)))
    xs, c_s, n_s, m_s = _mlstm_sample(xs, mods, w_in, w_gate, b_gate, norm_w, w_out, lng, lnb,
                                      state_mlstm_C[0], state_mlstm_n[0].reshape(bs, HQK), m0, ss)
    xs, k_s, v_s = _ffn(xs, mods, 0, 6, *ffn_w[1], lng, lnb, 2, ts, None, kv=(kv_mods, w_kv, tab_s))
    xs = _ffn(xs, mods, 1, 0, *ffn_w[2], lng, lnb, 3, ts, None)
    xs = _attn_sample(xs, mods, w_q, tab_s, k_s, v_s,
                      cache_win_k.reshape(bs, n_buf, KVW), cache_win_v.reshape(bs, n_buf, KVW),
                      sinks, w_o, lng, lnb, ss)
    xs = _ffn(xs, mods, 1, 6, *ffn_w[3], lng, lnb, 5, ts, None)
    y_sample = xs.reshape(bs, ss, D_MODEL)
    win_k_s = jnp.concatenate([cache_win_k, k_s.reshape(bs, ss, ATT_KV_HEADS, ATT_HD)], axis=1)[:, -n_buf:]
    win_v_s = jnp.concatenate([cache_win_v, v_s.reshape(bs, ss, ATT_KV_HEADS, ATT_HD)], axis=1)[:, -n_buf:]

    return (y_prompt, y_sample,
            state_p[None, ..., :ML_DV], state_p[None, :, :, 0, ML_DV:], m_p[None, :, :ML_HEADS, 0],
            win_k_p, win_v_p,
            c_s[None], n_s.reshape(1, bs, ML_HEADS, ML_DK), m_s[None, :, :ML_HEADS],
            win_k_s, win_v_s)
```

```python
import functools

import numpy as np
import jax
import jax.numpy as jnp
from jax import lax
from jax.experimental import pallas as pl
from jax.experimental.pallas import tpu as pltpu

F32 = jnp.float32
BF16 = jnp.bfloat16

D_MODEL = 1024
DEPTH = 2
PAST_LEN = 8192
ML_HEADS = 4
ML_DK = D_MODEL // 8
ML_DV = D_MODEL // 4
ATT_Q_HEADS = 16
ATT_KV_HEADS = 4
ATT_GROUP = ATT_Q_HEADS // ATT_KV_HEADS
ATT_HD = 64
WINDOW = 128
ROT_DIM = ATT_HD // 4
ROPE_THETA = 500000.0
D_FF = 2816
ALPHA = (2 * DEPTH) ** 0.25
LN_EPS = 1e-5
HQK = ML_HEADS * ML_DK
HV = ML_HEADS * ML_DV
KVW = ATT_KV_HEADS * ATT_HD
QW = ATT_Q_HEADS * ATT_HD

LANES = 128
SUBLANES = 8
VMEM_LIMIT_BYTES = 60 * 1024 * 1024

FFN_ROWS = 1024
FFN_ROWS_KV = 1024
FFN_CHUNK = 256
FFN_STREAM_BLOCKS = 2
ML_ROWS = 1024
ML_CHUNK = 128
ATT_ROWS = 1024
ADA_COLS = (1536, 1024)
SAMPLE_GROUP = 8

NT_DIMS = (((1,), (1,)), ((), ()))
HIGHEST = lax.Precision.HIGHEST


def _params(semantics):
    return pltpu.CompilerParams(dimension_semantics=semantics, vmem_limit_bytes=VMEM_LIMIT_BYTES)


def _const_spec(block, index):
    return pl.BlockSpec(block, lambda *_: index, pipeline_mode=pl.Buffered(1))


def _resident(shape):
    return _const_spec(shape, (0,) * len(shape))


def _dot(a, b):
    return jnp.dot(a, b, preferred_element_type=F32)


def _dot_exact(a, b):
    return jnp.dot(a, b, preferred_element_type=F32, precision=HIGHEST)


def _nt(a, b):
    return lax.dot_general(a, b, NT_DIMS, preferred_element_type=F32)


def _iota(shape, dim):
    return lax.broadcasted_iota(jnp.int32, shape, dim)


def _silu(x):
    return x * jax.nn.sigmoid(x)


def _log_sigmoid(x):
    return jnp.minimum(x, 0.0) - jnp.log1p(jnp.exp(-jnp.abs(x)))


def _layer_norm(y, g, b):
    mu = jnp.mean(y, axis=-1, keepdims=True)
    d = y - mu
    var = jnp.mean(d * d, axis=-1, keepdims=True)
    return d * lax.rsqrt(var + LN_EPS) * g + b


def _mod(ref, batch):
    return ref[...] if batch is None else ref[pl.ds(batch, 1), :]


def _modulate_bf16(x, shift, scale):
    return (x * (1.0 + scale) + shift).astype(BF16)


def _rope(x, cos, sin_lo, sin_hi):
    width = x.shape[1]
    reps = width // LANES
    tile = lambda t: jnp.concatenate([t] * reps, axis=1) if reps > 1 else t
    x_up = pltpu.roll(x, width - ROT_DIM // 2, 1)
    x_dn = pltpu.roll(x, ROT_DIM // 2, 1)
    return x * tile(cos) + x_up * tile(sin_lo) + x_dn * tile(sin_hi)


def _mod_specs(layer, first, count, n_sample_rows, prompt):
    if prompt:
        blk = n_sample_rows // SUBLANES
        return [_const_spec((None, SUBLANES, D_MODEL), (layer, blk, first + j)) for j in range(count)]
    return [_const_spec((None, n_sample_rows, D_MODEL), (layer, 0, first + j)) for j in range(count)]


def _ln_specs(index):
    return [_const_spec((None, 1, D_MODEL), (index, 0, 0))] * 2


def _ada_kernel(c_ref, w_ref, b_ref, o_ref, ca_ref):
    @pl.when((pl.program_id(0) == 0) & (pl.program_id(1) == 0))
    def _():
        ca_ref[...] = _silu(c_ref[...]).astype(BF16)

    o_ref[...] = _dot(ca_ref[...], w_ref[...].astype(BF16)) + b_ref[...]


def _ada(c, w, b):
    n_l, k, n = w.shape
    m = c.shape[0]
    cols = next(width for width in ADA_COLS if n % width == 0)
    return pl.pallas_call(
        _ada_kernel,
        grid=(n_l, n // cols),
        in_specs=[pl.BlockSpec((m, k), lambda l, j: (0, 0)),
                  pl.BlockSpec((None, k, cols), lambda l, j: (l, 0, j)),
                  pl.BlockSpec((None, 1, cols), lambda l, j: (l, 0, j))],
        out_specs=pl.BlockSpec((None, m, cols), lambda l, j: (l, 0, j)),
        out_shape=jax.ShapeDtypeStruct((n_l, m, n), F32),
        scratch_shapes=[pltpu.VMEM((m, k), BF16)],
        compiler_params=_params(("arbitrary", "arbitrary")),
        name="ada",
    )(c, w, b)


def _ffn_kernel(x_ref, sh_ref, sc_ref, gt_ref, wa_ref, wu_ref, wd_ref, g_ref, b_ref, *rest,
                steps_per_seq, has_kv, n_cast):
    rest = list(rest)
    kv_in = [rest.pop(0) for _ in range(6)] if has_kv else None
    cast_in = [rest.pop(0) for _ in range(n_cast)]
    o_ref = rest.pop(0)
    kv_out = [rest.pop(0) for _ in range(2)] if has_kv else None
    cast_out = [rest.pop(0) for _ in range(n_cast)]
    h_ref, = rest

    batch = None if steps_per_seq is None else pl.program_id(0) // steps_per_seq
    x = x_ref[...]
    xm = _modulate_bf16(x, _mod(sh_ref, batch), _mod(sc_ref, batch))
    for c0 in range(0, D_FF, FFN_CHUNK):
        cw = min(FFN_CHUNK, D_FF - c0)
        a = _dot(xm, wa_ref[:, c0:c0 + cw])
        u = _dot(xm, wu_ref[:, c0:c0 + cw])
        h_ref[:, c0:c0 + cw] = (_silu(a) * u).astype(BF16)
    y = ALPHA * x + (0.5 * _mod(gt_ref, batch)) * _dot(h_ref[...], wd_ref[...])
    out = _layer_norm(y, g_ref[...], b_ref[...])
    o_ref[...] = out
    if has_kv:
        ksh_ref, ksc_ref, wkv_ref, cos_ref, slo_ref, shi_ref = kv_in
        kv = _dot(_modulate_bf16(out, _mod(ksh_ref, batch), _mod(ksc_ref, batch)), wkv_ref[...])
        kv_out[0][...] = _rope(kv[:, :KVW], cos_ref[...], slo_ref[...], shi_ref[...])
        kv_out[1][...] = kv[:, KVW:]
    for src, dst in zip(cast_in, cast_out):
        dst[...] = src[...].astype(BF16)


def _ffn(x, mods, layer, first, w_up, w_down, ln_g, ln_b, ln_index, n_sample_rows, seq, kv=None, cast=None):
    t = x.shape[0]
    rows = min(FFN_ROWS if kv is None else FFN_ROWS_KV, t)
    steps = t // rows
    x_spec = pl.BlockSpec((rows, D_MODEL), lambda i: (i, 0))
    in_specs = ([x_spec] + _mod_specs(layer, first, 3, n_sample_rows, seq is not None)
                + [_const_spec((D_MODEL, D_FF), (0, 0)), _const_spec((D_MODEL, D_FF), (0, 1)),
                   _resident(w_down.shape)]
                + _ln_specs(ln_index))
    args = [x, mods, mods, mods, w_up, w_up, w_down, ln_g, ln_b]
    out_specs = [x_spec]
    out_shape = [jax.ShapeDtypeStruct((t, D_MODEL), F32)]
    if kv is not None:
        kv_mods, w_kv, tables = kv
        pos_blocks = tables[0].shape[0] // rows
        tab_spec = pl.BlockSpec((rows, LANES), lambda i: (i % pos_blocks, 0))
        kv_spec = pl.BlockSpec((rows, KVW), lambda i: (i, 0))
        in_specs += (_mod_specs(0, 0, 2, n_sample_rows, seq is not None)
                     + [_resident(w_kv.shape), tab_spec, tab_spec, tab_spec])
        args += [kv_mods, kv_mods, w_kv, *tables]
        out_specs += [kv_spec, kv_spec]
        out_shape += [jax.ShapeDtypeStruct((t, KVW), F32)] * 2
    n_cast = 0
    if cast is not None:
        up_f32, down_f32, src_layer, *extra = cast
        up_rows = D_MODEL // steps
        down_rows = 2 * D_FF // steps
        assert up_rows % 16 == 0 and down_rows % 16 == 0 and steps % 2 == 0
        in_specs += [pl.BlockSpec((None, up_rows, 2 * D_FF), lambda i: (src_layer, i, 0)),
                     pl.BlockSpec((None, down_rows, D_MODEL), lambda i: (src_layer, i // 2, 0))]
        args += [up_f32, down_f32]
        out_specs += [pl.BlockSpec((up_rows, 2 * D_FF), lambda i: (i, 0)),
                      pl.BlockSpec((down_rows, D_MODEL), lambda i: (i // 2, 0))]
        out_shape += [jax.ShapeDtypeStruct((D_MODEL, 2 * D_FF), BF16),
                      jax.ShapeDtypeStruct((D_FF, D_MODEL), BF16)]
        for w in extra:
            spec = pl.BlockSpec((None, up_rows, w.shape[2]), lambda i: (0, i, 0))
            in_specs.append(spec)
            args.append(w)
            out_specs.append(spec)
            out_shape.append(jax.ShapeDtypeStruct(w.shape, BF16))
        n_cast = 2 + len(extra)
    outs = pl.pallas_call(
        functools.partial(_ffn_kernel, steps_per_seq=None if seq is None else seq // rows,
                          has_kv=kv is not None, n_cast=n_cast),
        grid=(steps,),
        in_specs=in_specs,
        out_specs=out_specs,
        out_shape=out_shape,
        scratch_shapes=[pltpu.VMEM((rows, D_FF), BF16)],
        compiler_params=_params(("arbitrary",)),
        name="ffn" + ("_kv" if kv is not None else "") + ("_cast" if cast is not None else ""),
    )(*args)
    return outs[0] if len(outs) == 1 else outs


def _ffn_stream_kernel(x_ref, sh_ref, sc_ref, gt_ref, wa_ref, wu_ref, wd_ref, g_ref, b_ref, *rest, has_kv):
    rest = list(rest)
    kv_in = [rest.pop(0) for _ in range(6)] if has_kv else None
    o_ref = rest.pop(0)
    kv_out = [rest.pop(0) for _ in range(2)] if has_kv else None
    xm_ref, h_ref, acc_ref = rest
    j = pl.program_id(0)

    @pl.when(j == 0)
    def _():
        xm_ref[...] = _modulate_bf16(x_ref[...], sh_ref[...], sc_ref[...])

    xm = xm_ref[...]
    width = wa_ref.shape[1]
    for c0 in range(0, width, FFN_CHUNK):
        cw = min(FFN_CHUNK, width - c0)
        a = _dot(xm, wa_ref[:, c0:c0 + cw])
        u = _dot(xm, wu_ref[:, c0:c0 + cw])
        h_ref[:, c0:c0 + cw] = (_silu(a) * u).astype(BF16)
    part = _dot(h_ref[...], wd_ref[...])

    @pl.when(j == 0)
    def _():
        acc_ref[...] = part

    @pl.when(j > 0)
    def _():
        acc_ref[...] += part

    @pl.when(j == pl.num_programs(0) - 1)
    def _():
        y = ALPHA * x_ref[...] + (0.5 * gt_ref[...]) * acc_ref[...]
        out = _layer_norm(y, g_ref[...], b_ref[...])
        o_ref[...] = out
        if has_kv:
            ksh_ref, ksc_ref, wkv_ref, cos_ref, slo_ref, shi_ref = kv_in
            kv = _dot(_modulate_bf16(out, ksh_ref[...], ksc_ref[...]), wkv_ref[...])
            kv_out[0][...] = _rope(kv[:, :KVW], cos_ref[...], slo_ref[...], shi_ref[...])
            kv_out[1][...] = kv[:, KVW:]


def _ffn_stream(x, mods, layer, first, w_up, w_down, ln_g, ln_b, ln_index, kv=None):
    t = x.shape[0]
    n_blocks = FFN_STREAM_BLOCKS
    width = D_FF // n_blocks
    assert width % LANES == 0
    const = _resident(x.shape)
    in_specs = ([const] + _mod_specs(layer, first, 3, t, False)
                + [pl.BlockSpec((D_MODEL, width), lambda j: (0, j)),
                   pl.BlockSpec((D_MODEL, width), lambda j: (0, n_blocks + j)),
                   pl.BlockSpec((width, D_MODEL), lambda j: (j, 0))]
                + _ln_specs(ln_index))
    args = [x, mods, mods, mods, w_up, w_up, w_down, ln_g, ln_b]
    out_specs = [pl.BlockSpec(x.shape, lambda j: (0, 0))]
    out_shape = [jax.ShapeDtypeStruct((t, D_MODEL), F32)]
    if kv is not None:
        kv_mods, w_kv, tables = kv
        in_specs += (_mod_specs(0, 0, 2, t, False)
                     + [_resident(w_kv.shape)] + [_resident(tab.shape) for tab in tables])
        args += [kv_mods, kv_mods, w_kv, *tables]
        out_specs += [pl.BlockSpec((t, KVW), lambda j: (0, 0))] * 2
        out_shape += [jax.ShapeDtypeStruct((t, KVW), F32)] * 2
    outs = pl.pallas_call(
        functools.partial(_ffn_stream_kernel, has_kv=kv is not None),
        grid=(n_blocks,),
        in_specs=in_specs,
        out_specs=out_specs,
        out_shape=out_shape,
        scratch_shapes=[pltpu.VMEM((t, D_MODEL), BF16), pltpu.VMEM((t, width), BF16),
                        pltpu.VMEM((t, D_MODEL), F32)],
        compiler_params=_params(("arbitrary",)),
        name="ffn_stream" + ("_kv" if kv is not None else ""),
    )(*args)
    return outs[0] if len(outs) == 1 else outs


def _head_norm(hh, w_row):
    mu = jnp.mean(hh, axis=-1, keepdims=True)
    d = hh - mu
    var = jnp.mean(d * d, axis=-1, keepdims=True)
    return d * lax.rsqrt(var + LN_EPS) * w_row


def _split3(x, axis):
    hi = x.astype(BF16)
    r1 = x - hi.astype(F32)
    mid = r1.astype(BF16)
    lo = (r1 - mid.astype(F32)).astype(BF16)
    return jnp.concatenate([hi, mid, lo], axis=axis)


def _tile_lanes(t, reps):
    return jnp.concatenate([t] * reps, axis=1)


def _mlstm_prompt_kernel(x_ref, sh_ref, sc_ref, gt_ref, wp_ref, wg_ref, bg_ref, nw_ref, wo_ref, g_ref, b_ref,
                         o_ref, c_out, m_out, proj_ref, rep_ref, y_ref, c_ref, mm_ref):
    batch = pl.program_id(0)
    step = pl.program_id(1)
    size = ML_CHUNK
    assert size == LANES and ML_DK == LANES

    @pl.when(step == 0)
    def _():
        c_ref[...] = jnp.zeros_like(c_ref)
        mm_ref[...] = jnp.zeros_like(mm_ref)

    x = x_ref[...]
    rows = x.shape[0]
    n_chunks = rows // size
    h = _modulate_bf16(x, _mod(sh_ref, batch), _mod(sc_ref, batch))
    gates = _dot(h, wg_ref[...]) + bg_ref[...]

    r = _iota((size, size), 0)
    c = _iota((size, size), 1)
    causal = c <= r
    eye = c == r
    tri3 = _tile_lanes(jnp.where(causal, 1.0, 0.0).astype(BF16), 3)
    n_rep = 2 * ML_HEADS
    spread = jnp.where(_iota((LANES, n_rep * LANES), 1) // LANES == _iota((LANES, n_rep * LANES), 0),
                       1.0, 0.0).astype(BF16)
    spread3 = jnp.concatenate([spread] * 3, axis=0)
    ones3 = jnp.ones((3 * size, LANES), BF16)
    sr = _iota((2 * SUBLANES, LANES), 0)
    sl = _iota((2 * SUBLANES, LANES), 1)
    pick = (jnp.where((sr < ML_HEADS) & (sl == sr), 1.0, 0.0)
            - jnp.where((sr < ML_HEADS) & (sl == sr + ML_HEADS), 1.0, 0.0)
            + jnp.where((sr >= SUBLANES) & (sr < SUBLANES + ML_HEADS) & (sl == sr - SUBLANES + ML_HEADS),
                        1.0, 0.0))
    pick3 = _tile_lanes(pick.astype(BF16), 3)

    log_f = _log_sigmoid(gates)
    bwide = jnp.concatenate([_dot(tri3, _split3(log_f[ci * size:(ci + 1) * size, :], 0))
                             for ci in range(n_chunks)], axis=0)
    src3 = _split3(jnp.where(_iota((rows, LANES), 1) < ML_HEADS, gates, bwide), 1)
    rep_ref[...] = _dot(src3, spread3)
    ab_rows = _nt(pick3, src3)
    a_rows = ab_rows[:SUBLANES, :]
    b_rows = ab_rows[SUBLANES:, :]
    lane_in_chunk = _iota((SUBLANES, rows), 1) % size
    run = a_rows
    shift = 1
    while shift < size:
        run = jnp.maximum(run, jnp.where(lane_in_chunk >= shift, pltpu.roll(run, shift, 1), -jnp.inf))
        shift *= 2
    m_rows = [mm_ref[...]]
    g_rows = []
    for ci in range(n_chunks):
        run_c = run[:, ci * size:(ci + 1) * size]
        g_rows.append(jnp.maximum(run_c, m_rows[-1]))
        end = (ci + 1) * size - 1
        m_end = b_rows[:, end:end + 1] + jnp.maximum(run[:, end:end + 1], m_rows[-1][:, :1])
        m_rows.append(jnp.broadcast_to(m_end, (SUBLANES, LANES)))
    mm_ref[...] = m_rows[-1]

    proj_ref[...] = _dot(h, wp_ref[...])

    states = [c_ref[hd] for hd in range(ML_HEADS)]
    for ci in range(n_chunks):
        lo, hi = ci * size, (ci + 1) * size
        for hd in range(ML_HEADS):
            i_rep = rep_ref[lo:hi, hd * LANES:(hd + 1) * LANES]
            b_rep = rep_ref[lo:hi, (ML_HEADS + hd) * LANES:(ML_HEADS + hd + 1) * LANES]
            m_prev = m_rows[ci][hd:hd + 1, :]
            m_new = m_rows[ci + 1][hd:hd + 1, :]
            a_row = a_rows[hd:hd + 1, lo:hi]
            g_rep = _dot(_split3(jnp.where(eye, g_rows[ci][hd:hd + 1, :], 0.0), 1), ones3)
            w = jnp.where(causal, jnp.exp(a_row - g_rep), 0.0)
            w_init = jnp.exp(m_prev - g_rep)
            floor = jnp.exp(-(b_rep + g_rep))
            b_last = b_rep[size - 1:size, :]
            w_end = jnp.exp(b_last - b_rep + i_rep - m_new)
            decay = jnp.exp(b_last + m_prev - m_new)

            q = proj_ref[lo:hi, hd * ML_DK:(hd + 1) * ML_DK]
            k = proj_ref[lo:hi, HQK + hd * ML_DK:HQK + (hd + 1) * ML_DK] * (ML_DK ** -0.5)
            v = proj_ref[lo:hi, 2 * HQK + hd * ML_DV:2 * HQK + (hd + 1) * ML_DV]
            qb = q.astype(BF16)
            vb = v.astype(BF16)
            c_mat = states[hd][:, :ML_DV]
            n_row = states[hd][0:1, ML_DV:]
            s = _nt(qb, k.astype(BF16)) * w
            num = _dot(s.astype(BF16), vb) + _tile_lanes(w_init, 2) * _dot(qb, c_mat.astype(BF16))
            den = (jnp.sum(s, axis=-1, keepdims=True)
                   + w_init[:, :1] * jnp.sum(q * n_row, axis=-1, keepdims=True))
            hh = num / jnp.maximum(jnp.abs(den), floor[:, :1])
            kw = k * w_end
            c_new = _tile_lanes(decay, 2) * c_mat + _dot(kw.T.astype(BF16), vb)
            n_new = decay * n_row + jnp.sum(kw, axis=0, keepdims=True)
            states[hd] = jnp.concatenate([c_new, jnp.broadcast_to(n_new, (ML_DK, LANES))], axis=1)
            y_ref[lo:hi, hd * ML_DV:(hd + 1) * ML_DV] = _head_norm(hh, nw_ref[:, hd * ML_DV:(hd + 1) * ML_DV])
    for hd in range(ML_HEADS):
        c_ref[hd] = states[hd]

    o_gate = jax.nn.sigmoid(proj_ref[:, 2 * HQK + HV:2 * HQK + 2 * HV])
    y = _dot((y_ref[...] * o_gate).astype(BF16), wo_ref[...])
    o_ref[...] = _layer_norm(ALPHA * x + _mod(gt_ref, batch) * y, g_ref[...], b_ref[...])

    @pl.when(step == pl.num_programs(1) - 1)
    def _():
        c_out[...] = c_ref[...]
        m_out[...] = mm_ref[...]


def _mlstm_weight_specs(wg, bg, nw, wo):
    return [_const_spec((None, D_MODEL, 2 * HQK + 2 * HV), (0, 0, 0)), _resident(wg.shape),
            _resident(bg.shape), _resident(nw.shape), _resident(wo.shape)] + _ln_specs(1)


def _mlstm_prompt(x, mods, w_in, wg, bg, nw, wo, ln_g, ln_b, n_sample_rows):
    bsz, seq, _ = x.shape
    rows = min(ML_ROWS, seq)
    x_spec = pl.BlockSpec((None, rows, D_MODEL), lambda bi, i: (bi, i, 0))
    state_shape = (ML_HEADS, ML_DK, ML_DV + LANES)
    return pl.pallas_call(
        _mlstm_prompt_kernel,
        grid=(bsz, seq // rows),
        in_specs=[x_spec] + _mod_specs(0, 3, 3, n_sample_rows, True) + _mlstm_weight_specs(wg, bg, nw, wo),
        out_specs=[x_spec, pl.BlockSpec((None,) + state_shape, lambda bi, i: (bi, 0, 0, 0)),
                   pl.BlockSpec((None, SUBLANES, LANES), lambda bi, i: (bi, 0, 0))],
        out_shape=[jax.ShapeDtypeStruct((bsz, seq, D_MODEL), F32),
                   jax.ShapeDtypeStruct((bsz,) + state_shape, F32),
                   jax.ShapeDtypeStruct((bsz, SUBLANES, LANES), F32)],
        scratch_shapes=[pltpu.VMEM((rows, 2 * HQK + 2 * HV), F32),
                        pltpu.VMEM((rows, 2 * ML_HEADS * LANES), F32),
                        pltpu.VMEM((rows, HV), F32),
                        pltpu.VMEM(state_shape, F32),
                        pltpu.VMEM((SUBLANES, LANES), F32)],
        compiler_params=_params(("arbitrary", "arbitrary")),
        name="mlstm_prompt",
    )(x, mods, mods, mods, w_in, wg, bg, nw, wo, ln_g, ln_b)


def _mlstm_sample_kernel(x_ref, sh_ref, sc_ref, gt_ref, wp_ref, wg_ref, bg_ref, nw_ref, wo_ref, g_ref, b_ref,
                         c0_ref, n0_ref, m0_ref, o_ref, c_out, n_out, m_out,
                         proj_ref, gate_ref, y_ref, *, seq):
    step = pl.program_id(0)
    grp = c0_ref.shape[0]
    rows = grp * seq

    @pl.when(step == 0)
    def _():
        h = _modulate_bf16(x_ref[...], sh_ref[...], sc_ref[...])
        proj_ref[...] = _dot(h, wp_ref[...])
        gate_ref[...] = _dot(h, wg_ref[...]) + bg_ref[...]

    r = _iota((rows, rows), 0)
    c = _iota((rows, rows), 1)
    same = (r // seq) == (c // seq)
    causal = same & (c <= r)
    eye = r == c
    last_of_row = c == (r // seq) * seq + (seq - 1)
    same_f = same.astype(F32)
    expand = (_iota((rows, grp), 0) // seq == _iota((rows, grp), 1)).astype(F32)
    pick_last = (_iota((grp, rows), 1) == _iota((grp, rows), 0) * seq + (seq - 1)).astype(F32)
    seq_of_row = _iota((rows, ML_DV), 0) // seq
    col_seq = _iota((ML_DK, rows), 1) // seq
    lane = _iota((rows, LANES), 1)
    to_row = lambda col: jnp.sum(jnp.where(eye, col, 0.0), axis=0, keepdims=True)

    r0 = pl.multiple_of(step * rows, rows)
    gates = gate_ref[pl.ds(r0, rows), :]
    log_f = _log_sigmoid(gates)
    bwide = _dot_exact(causal.astype(F32), log_f)
    bsum = _dot_exact(same_f, log_f)
    m_prev_wide = _dot_exact(expand, m0_ref[...])
    n_rows_wide = _dot_exact(expand, n0_ref[...])
    n_pieces = []
    m_wide = jnp.zeros((rows, LANES), F32)
    for hd in range(ML_HEADS):
        q = proj_ref[pl.ds(r0, rows), hd * ML_DK:(hd + 1) * ML_DK]
        k = proj_ref[pl.ds(r0, rows), HQK + hd * ML_DK:HQK + (hd + 1) * ML_DK] * (ML_DK ** -0.5)
        v = proj_ref[pl.ds(r0, rows), 2 * HQK + hd * ML_DV:2 * HQK + (hd + 1) * ML_DV]
        bcol = bwide[:, ML_HEADS + hd:ML_HEADS + hd + 1]
        b_last = bsum[:, ML_HEADS + hd:ML_HEADS + hd + 1]
        icol = gates[:, hd:hd + 1]
        m_prev = m_prev_wide[:, hd:hd + 1]
        n_rows = n_rows_wide[:, hd * ML_DK:(hd + 1) * ML_DK]
        log_w = jnp.where(causal, bcol - to_row(bcol) + to_row(icol), -jnp.inf)
        log_init = bcol + m_prev
        m_t = jnp.maximum(log_init, jnp.max(log_w, axis=-1, keepdims=True))
        w = jnp.exp(log_w - m_t)
        w_init = jnp.exp(log_init - m_t)
        qb = q.astype(BF16)
        s = _nt(qb, k.astype(BF16)) * w
        inter = None
        for bi in range(grp):
            cand = _dot(qb, c0_ref[bi, hd].astype(BF16))
            inter = cand if inter is None else jnp.where(seq_of_row == bi, cand, inter)
        num = _dot(s.astype(BF16), v.astype(BF16)) + w_init * inter
        den = jnp.sum(s, axis=-1, keepdims=True) + w_init * jnp.sum(q * n_rows, axis=-1, keepdims=True)
        hh = num / jnp.maximum(jnp.abs(den), jnp.exp(-m_t))
        m_new = jnp.sum(jnp.where(last_of_row, to_row(m_t), 0.0), axis=1, keepdims=True)
        w_end = jnp.exp(b_last - bcol + icol - m_new)
        decay = jnp.exp(b_last + m_prev - m_new)
        kw = k * w_end
        kw_t = kw.T
        lhs = jnp.concatenate([jnp.where(col_seq == bi, kw_t, 0.0) for bi in range(grp)], axis=0)
        upd = _dot(lhs.astype(BF16), v.astype(BF16))
        for bi in range(grp):
            c_out[bi, hd] = (decay[bi * seq:bi * seq + 1, :] * c0_ref[bi, hd]
                             + upd[bi * ML_DK:(bi + 1) * ML_DK, :])
        n_pieces.append(decay * n_rows + _dot_exact(same_f, kw))
        m_wide = jnp.where(lane == hd, m_new, m_wide)
        y_ref[pl.ds(r0, rows), hd * ML_DV:(hd + 1) * ML_DV] = _head_norm(
            hh, nw_ref[:, hd * ML_DV:(hd + 1) * ML_DV])
    n_out[...] = _dot_exact(pick_last, jnp.concatenate(n_pieces, axis=1))
    m_out[...] = _dot_exact(pick_last, m_wide)

    @pl.when(step == pl.num_programs(0) - 1)
    def _():
        o_gate = jax.nn.sigmoid(proj_ref[:, 2 * HQK + HV:2 * HQK + 2 * HV])
        y = _dot((y_ref[...] * o_gate).astype(BF16), wo_ref[...])
        o_ref[...] = _layer_norm(ALPHA * x_ref[...] + gt_ref[...] * y, g_ref[...], b_ref[...])


def _mlstm_sample(x, mods, w_in, wg, bg, nw, wo, ln_g, ln_b, c0, n0, m0, seq):
    t = x.shape[0]
    grp = SAMPLE_GROUP
    state_specs = [pl.BlockSpec((grp, ML_HEADS, ML_DK, ML_DV), lambda i: (i, 0, 0, 0)),
                   pl.BlockSpec((grp, HQK), lambda i: (i, 0)),
                   pl.BlockSpec((grp, LANES), lambda i: (i, 0))]
    return pl.pallas_call(
        functools.partial(_mlstm_sample_kernel, seq=seq),
        grid=(t // (grp * seq),),
        in_specs=[_resident(x.shape)] + _mod_specs(0, 3, 3, t, False)
                 + _mlstm_weight_specs(wg, bg, nw, wo) + state_specs,
        out_specs=[pl.BlockSpec((t, D_MODEL), lambda i: (0, 0))] + state_specs,
        out_shape=[jax.ShapeDtypeStruct((t, D_MODEL), F32),
                   jax.ShapeDtypeStruct(c0.shape, F32),
                   jax.ShapeDtypeStruct(n0.shape, F32),
                   jax.ShapeDtypeStruct(m0.shape, F32)],
        scratch_shapes=[pltpu.VMEM((t, 2 * HQK + 2 * HV), F32),
                        pltpu.VMEM((t, LANES), F32),
                        pltpu.VMEM((t, HV), F32)],
        compiler_params=_params(("arbitrary",)),
        name="mlstm_sample",
    )(x, mods, mods, mods, w_in, wg, bg, nw, wo, ln_g, ln_b, c0, n0, m0)


def _sink_column(sink_ref, kh, rows_per_head):
    rid = _iota((ATT_GROUP * rows_per_head, 1), 0)
    col = jnp.full((ATT_GROUP * rows_per_head, 1), sink_ref[kh * ATT_GROUP], F32)
    for gq in range(1, ATT_GROUP):
        col = jnp.where(rid >= gq * rows_per_head, sink_ref[kh * ATT_GROUP + gq], col)
    return col


def _head_lanes(x, kh):
    return jnp.where(_iota(x.shape, 1) // ATT_HD == kh, x, 0.0).astype(BF16)


def _stack_groups(x):
    return jnp.concatenate([x[:, g * KVW:(g + 1) * KVW] for g in range(ATT_GROUP)], axis=0)


def _attn_prompt_kernel(sink_ref, x_ref, sh_ref, sc_ref, gt_ref, wq_ref, cos_ref, slo_ref, shi_ref,
                        kp_ref, kc_ref, vp_ref, vc_ref, wo_ref, g_ref, b_ref, o_ref, a_ref):
    batch = pl.program_id(0)
    step = pl.program_id(1)
    x = x_ref[...]
    rows = x.shape[0]
    n_blk = rows // WINDOW
    q = _rope(_dot(_modulate_bf16(x, _mod(sh_ref, batch), _mod(sc_ref, batch)), wq_ref[...]),
              cos_ref[...], slo_ref[...], shi_ref[...])
    q = (q * (ATT_HD ** -0.5)).astype(BF16)

    keys = jnp.concatenate([kp_ref[...], kc_ref[...]], axis=0)
    vals = jnp.concatenate([vp_ref[...], vc_ref[...]], axis=0)
    k_heads = [_head_lanes(keys, kh) for kh in range(ATT_KV_HEADS)]
    v_heads = [_head_lanes(vals, kh) for kh in range(ATT_KV_HEADS)]

    rid = _iota((ATT_GROUP * WINDOW, WINDOW), 0) % WINDOW
    cid = _iota((ATT_GROUP * WINDOW, WINDOW), 1)
    from_prev = cid > rid
    no_prev = jnp.where(step > 0, 0.0, -jnp.inf)
    sinks = [_sink_column(sink_ref, kh, WINDOW) for kh in range(ATT_KV_HEADS)]

    scores = [[_nt(_stack_groups(q[n * WINDOW:(n + 1) * WINDOW, :]),
                   k_heads[kh][n * WINDOW:(n + 2) * WINDOW, :])
               for kh in range(ATT_KV_HEADS)] for n in range(n_blk)]
    probs = []
    for n in range(n_blk):
        row = []
        for kh in range(ATT_KV_HEADS):
            s_prev = scores[n][kh][:, :WINDOW]
            if n == 0:
                s_prev = s_prev + no_prev
            s = jnp.where(from_prev, s_prev, scores[n][kh][:, WINDOW:])
            mx = jnp.maximum(jnp.max(s, axis=-1, keepdims=True), sinks[kh])
            p = jnp.exp(s - mx)
            den = jnp.sum(p, axis=-1, keepdims=True) + jnp.exp(sinks[kh] - mx)
            pn = p / den
            row.append(jnp.where(from_prev, pn, 0.0).astype(BF16))
            row.append(jnp.where(from_prev, 0.0, pn).astype(BF16))
        probs.append(jnp.concatenate(row, axis=1))
    for n in range(n_blk):
        v_all = jnp.concatenate([v_heads[kh][n * WINDOW:(n + 2) * WINDOW, :]
                                 for kh in range(ATT_KV_HEADS)], axis=0)
        o = _dot(probs[n], v_all)
        for g in range(ATT_GROUP):
            a_ref[n * WINDOW:(n + 1) * WINDOW, g * KVW:(g + 1) * KVW] = o[g * WINDOW:(g + 1) * WINDOW, :]

    y = _dot(a_ref[...].astype(BF16), wo_ref[...])
    o_ref[...] = _layer_norm(ALPHA * x + _mod(gt_ref, batch) * y, g_ref[...], b_ref[...])


def _attn_prompt(x, mods, wq, tables, k, v, sinks, wo, ln_g, ln_b, n_sample_rows):
    bsz, seq, _ = x.shape
    rows = min(ATT_ROWS, seq)
    per = rows // WINDOW
    x_spec = pl.BlockSpec((None, rows, D_MODEL), lambda bi, i: (bi, i, 0))
    cur = pl.BlockSpec((None, rows, KVW), lambda bi, i: (bi, i, 0))
    prev = pl.BlockSpec((None, WINDOW, KVW), lambda bi, i: (bi, jnp.maximum(i * per - 1, 0), 0))
    tab = pl.BlockSpec((rows, LANES), lambda bi, i: (i, 0))
    return pl.pallas_call(
        _attn_prompt_kernel,
        grid=(bsz, seq // rows),
        in_specs=[pl.BlockSpec(memory_space=pltpu.SMEM), x_spec]
                 + _mod_specs(1, 3, 3, n_sample_rows, True)
                 + [_resident(wq.shape), tab, tab, tab, prev, cur, prev, cur, _resident(wo.shape)]
                 + _ln_specs(4),
        out_specs=x_spec,
        out_shape=jax.ShapeDtypeStruct((bsz, seq, D_MODEL), F32),
        scratch_shapes=[pltpu.VMEM((rows, QW), F32)],
        compiler_params=_params(("arbitrary", "arbitrary")),
        name="attn_prompt",
    )(sinks, x, mods, mods, mods, wq, *tables, k, k, v, v, wo, ln_g, ln_b)


def _attn_sample_kernel(sink_ref, x_ref, sh_ref, sc_ref, gt_ref, wq_ref, cos_ref, slo_ref, shi_ref,
                        kn_ref, vn_ref, kb_ref, vb_ref, wo_ref, g_ref, b_ref, o_ref,
                        q_ref, a_ref, *, seq):
    step = pl.program_id(0)
    pair = 2 * seq
    assert pair == SUBLANES
    grp, n_buf, _ = kb_ref.shape

    @pl.when(step == 0)
    def _():
        q = _rope(_dot(_modulate_bf16(x_ref[...], sh_ref[...], sc_ref[...]), wq_ref[...]),
                  cos_ref[...], slo_ref[...], shi_ref[...])
        q_ref[...] = q * (ATT_HD ** -0.5)

    n_rows = ATT_GROUP * pair
    n_all = ATT_KV_HEADS * n_rows
    n_new = 2 * pair
    q_pos = _iota((n_all, n_buf), 0) % seq + PAST_LEN
    k_pos = _iota((n_all, n_buf), 1) + (PAST_LEN - n_buf)
    buf_mask = (k_pos <= q_pos) & (k_pos > q_pos - WINDOW) & (k_pos >= 0)
    q_tok = _iota((n_all, n_new), 0) % seq
    new_col = _iota((n_all, n_new), 1)
    out_lane_head = _iota((n_rows, KVW), 1) // ATT_HD
    low = _iota((n_rows, KVW), 0) % pair < seq
    head_of_row = _iota((n_all, 1), 0) // pair
    sink_col = jnp.full((n_all, 1), sink_ref[0], F32)
    for hq in range(1, ATT_Q_HEADS):
        sink_col = jnp.where(head_of_row >= hq, sink_ref[hq], sink_col)
    pad_rows = jnp.zeros((n_new - pair, KVW), BF16)
    for p in range(grp // 2):
        r0 = pl.multiple_of(step * (grp * seq) + p * pair, pair)
        q_all = _stack_groups(q_ref[pl.ds(r0, pair), :].astype(BF16))
        q_bd = jnp.concatenate([_head_lanes(q_all, kh) for kh in range(ATT_KV_HEADS)], axis=0)
        k_new = jnp.concatenate([kn_ref[pl.ds(r0, pair), :].astype(BF16), pad_rows], axis=0)
        v_new = jnp.concatenate([vn_ref[pl.ds(r0, pair), :].astype(BF16), pad_rows], axis=0)
        s_new_all = _nt(q_bd, k_new)
        halves = []
        for half in range(2):
            bi = 2 * p + half
            tok = new_col - half * seq
            own_new = (tok >= 0) & (tok < seq) & (tok <= q_tok)
            s_buf = jnp.where(buf_mask, _nt(q_bd, kb_ref[bi].astype(BF16)), -jnp.inf)
            s_new = jnp.where(own_new, s_new_all, -jnp.inf)
            mx = jnp.maximum(jnp.maximum(jnp.max(s_buf, axis=-1, keepdims=True),
                                         jnp.max(s_new, axis=-1, keepdims=True)), sink_col)
            e_buf = jnp.exp(s_buf - mx)
            e_new = jnp.exp(s_new - mx)
            den = (jnp.sum(e_buf, axis=-1, keepdims=True) + jnp.sum(e_new, axis=-1, keepdims=True)
                   + jnp.exp(sink_col - mx))
            o_all = (_dot((e_buf / den).astype(BF16), vb_ref[bi].astype(BF16))
                     + _dot((e_new / den).astype(BF16), v_new))
            o_half = None
            for kh in range(ATT_KV_HEADS):
                part = jnp.where(out_lane_head == kh, o_all[kh * n_rows:(kh + 1) * n_rows, :], 0.0)
                o_half = part if o_half is None else o_half + part
            halves.append(o_half)
        o = jnp.where(low, halves[0], halves[1])
        for g in range(ATT_GROUP):
            a_ref[pl.ds(r0, pair), g * KVW:(g + 1) * KVW] = o[g * pair:(g + 1) * pair, :]

    @pl.when(step == pl.num_programs(0) - 1)
    def _():
        y = _dot(a_ref[...].astype(BF16), wo_ref[...])
        o_ref[...] = _layer_norm(ALPHA * x_ref[...] + gt_ref[...] * y, g_ref[...], b_ref[...])


def _attn_sample(x, mods, wq, tables, k_new, v_new, k_buf, v_buf, sinks, wo, ln_g, ln_b, seq):
    t = x.shape[0]
    grp = SAMPLE_GROUP
    buf_spec = pl.BlockSpec((grp,) + k_buf.shape[1:], lambda i: (i, 0, 0))
    return pl.pallas_call(
        functools.partial(_attn_sample_kernel, seq=seq),
        grid=(t // (grp * seq),),
        in_specs=[pl.BlockSpec(memory_space=pltpu.SMEM), _resident(x.shape)]
                 + _mod_specs(1, 3, 3, t, False)
                 + [_resident(wq.shape)] + [_resident(tab.shape) for tab in tables]
                 + [_resident(k_new.shape), _resident(v_new.shape), buf_spec, buf_spec, _resident(wo.shape)]
                 + _ln_specs(4),
        out_specs=pl.BlockSpec((t, D_MODEL), lambda i: (0, 0)),
        out_shape=jax.ShapeDtypeStruct((t, D_MODEL), F32),
        scratch_shapes=[pltpu.VMEM((t, QW), F32), pltpu.VMEM((t, QW), F32)],
        compiler_params=_params(("arbitrary",)),
        name="attn_sample",
    )(sinks, x, mods, mods, mods, wq, *tables, k_new, v_new, k_buf, v_buf, wo, ln_g, ln_b)


def _rope_tables(pos):
    half = ROT_DIM // 2
    inv_freq = ROPE_THETA ** (-jnp.arange(half, dtype=F32) * 2.0 / ROT_DIM)
    ang = pos.astype(F32)[:, None] * inv_freq[None, :]
    cos, sin = jnp.cos(ang), jnp.sin(ang)
    dim = np.arange(LANES) % ATT_HD
    freq = np.arange(half)[:, None]
    lo = (dim[None, :] == freq).astype(np.float32)
    hi = (dim[None, :] == freq + half).astype(np.float32)
    spread = lambda a, m: jnp.dot(a, jnp.asarray(m), precision=HIGHEST)
    return (spread(cos, lo + hi) + jnp.asarray((dim >= ROT_DIM).astype(np.float32)),
            spread(sin, -lo), spread(sin, hi))


def kernel(x_prompt, x_sample, c_prompt, c_sample, state_mlstm_C, state_mlstm_n, state_mlstm_m,
           cache_win_k, cache_win_v, ada_w, ada_b, ffn1_up, ffn1_down, ffn2_up, ffn2_down, ln_g, ln_b,
           ml_w_in, ml_b_gates, ml_norm_w, ml_w_out, kv_ada_w, kv_ada_b, kv_w,
           att_w_q, att_sinks, att_w_o):
    bp, sp, _ = x_prompt.shape
    bs, ss, _ = x_sample.shape
    n_buf = cache_win_k.shape[1]
    ts = bs * ss

    c_all = jnp.concatenate([jnp.repeat(c_sample, ss, axis=0), c_prompt,
                             jnp.zeros((-bp % (2 * SUBLANES), D_MODEL), F32)], axis=0)
    mods = _ada(c_all, ada_w, ada_b[:, None, :])
    kv_mods = _ada(c_all, kv_ada_w[None], kv_ada_b[None, None, :])

    bf = lambda w: w.astype(BF16)
    ffn_f32 = [(ffn1_up, ffn1_down, 0), (ffn2_up, ffn2_down, 0), (ffn1_up, ffn1_down, 1), (ffn2_up, ffn2_down, 1)]
    ffn_w = [(bf(ffn1_up[0]), bf(ffn1_down[0]))]
    w_in = bf(ml_w_in)
    gate_pad = LANES - 2 * ML_HEADS
    w_gate = bf(jnp.pad(ml_w_in[0, :, 2 * HQK + 2 * HV:], ((0, 0), (0, gate_pad))))
    b_gate = jnp.pad(ml_b_gates[0], (0, gate_pad))[None, :]
    norm_w = ml_norm_w[0][None, :]
    w_q = bf(att_w_q[0].reshape(D_MODEL, ATT_KV_HEADS, ATT_GROUP, ATT_HD)
             .transpose(0, 2, 1, 3).reshape(D_MODEL, QW))
    w_o = bf(att_w_o[0].reshape(ATT_KV_HEADS, ATT_GROUP, ATT_HD, D_MODEL)
             .transpose(1, 0, 2, 3).reshape(QW, D_MODEL))
    sinks = att_sinks[0]
    lng = ln_g.reshape(DEPTH * 3, 1, D_MODEL)
    lnb = ln_b.reshape(DEPTH * 3, 1, D_MODEL)

    tab_p = _rope_tables(jnp.arange(sp, dtype=jnp.int32))
    tab_s = _rope_tables(jnp.tile(PAST_LEN + jnp.arange(ss, dtype=jnp.int32), bs))

    xp = x_prompt.reshape(bp * sp, D_MODEL)
    xp, *cast, w_out, w_kv = _ffn(xp, mods, 0, 0, *ffn_w[0], lng, lnb, 0, ts, sp,
                                  cast=ffn_f32[1] + (ml_w_out, kv_w[None]))
    ffn_w.append(tuple(cast))
    w_out, w_kv = w_out[0], w_kv[0]
    xp, state_p, m_p = _mlstm_prompt(xp.reshape(bp, sp, D_MODEL), mods, w_in, w_gate, b_gate,
                                     norm_w, w_out, lng, lnb, ts)
    xp, k_p, v_p, *cast = _ffn(xp.reshape(bp * sp, D_MODEL), mods, 0, 6, *ffn_w[1], lng, lnb, 2, ts, sp,
                               kv=(kv_mods, w_kv, tab_p), cast=ffn_f32[2])
    ffn_w.append(tuple(cast))
    k_p = k_p.reshape(bp, sp, KVW)
    v_p = v_p.reshape(bp, sp, KVW)
    xp, *cast = _ffn(xp, mods, 1, 0, *ffn_w[2], lng, lnb, 3, ts, sp, cast=ffn_f32[3])
    ffn_w.append(tuple(cast))
    xp = _attn_prompt(xp.reshape(bp, sp, D_MODEL), mods, w_q, tab_p, k_p, v_p, sinks, w_o, lng, lnb, ts)
    xp = _ffn(xp.reshape(bp * sp, D_MODEL), mods, 1, 6, *ffn_w[3], lng, lnb, 5, ts, sp)
    y_prompt = xp.reshape(bp, sp, D_MODEL)
    keep = min(WINDOW, sp)
    win_k_p = k_p[:, sp - keep:].reshape(bp, keep, ATT_KV_HEADS, ATT_HD)
    win_v_p = v_p[:, sp - keep:].reshape(bp, keep, ATT_KV_HEADS, ATT_HD)

    xs = x_sample.reshape(ts, D_MODEL)
    xs = _ffn_stream(xs, mods, 0, 0, *ffn_w[0], lng, lnb, 0)
    m0 = jnp.pad(state_mlstm_m[0], ((0, 0), (0, LANES - ML_HEADS)))
    xs, c_s, n_s, m_s = _mlstm_sample(xs, mods, w_in, w_gate, b_gate, norm_w, w_out, lng, lnb,
                                      state_mlstm_C[0], state_mlstm_n[0].reshape(bs, HQK), m0, ss)
    xs, k_s, v_s = _ffn_stream(xs, mods, 0, 6, *ffn_w[1], lng, lnb, 2, kv=(kv_mods, w_kv, tab_s))
    xs = _ffn_stream(xs, mods, 1, 0, *ffn_w[2], lng, lnb, 3)
    xs = _attn_sample(xs, mods, w_q, tab_s, k_s, v_s,
                      cache_win_k.reshape(bs, n_buf, KVW), cache_win_v.reshape(bs, n_buf, KVW),
                      sinks, w_o, lng, lnb, ss)
    xs = _ffn_stream(xs, mods, 1, 6, *ffn_w[3], lng, lnb, 5)
    y_sample = xs.reshape(bs, ss, D_MODEL)
    win_k_s = jnp.concatenate([cache_win_k, k_s.reshape(bs, ss, ATT_KV_HEADS, ATT_HD)], axis=1)[:, -n_buf:]
    win_v_s = jnp.concatenate([cache_win_v, v_s.reshape(bs, ss, ATT_KV_HEADS, ATT_HD)], axis=1)[:, -n_buf:]

    return (y_prompt, y_sample,
            state_p[None, ..., :ML_DV], state_p[None, :, :, 0, ML_DV:], m_p[None, :, :ML_HEADS, 0],
            win_k_p, win_v_p,
            c_s[None], n_s.reshape(1, bs, ML_HEADS, ML_DK), m_s[None, :, :ML_HEADS],
            win_k_s, win_v_s)
```

```python
import functools

import numpy as np
import jax
import jax.numpy as jnp
from jax import lax
from jax.experimental import pallas as pl
from jax.experimental.pallas import tpu as pltpu

F32 = jnp.float32
BF16 = jnp.bfloat16

D_MODEL = 1024
DEPTH = 2
PAST_LEN = 8192
ML_HEADS = 4
ML_DK = D_MODEL // 8
ML_DV = D_MODEL // 4
ATT_Q_HEADS = 16
ATT_KV_HEADS = 4
ATT_GROUP = ATT_Q_HEADS // ATT_KV_HEADS
ATT_HD = 64
WINDOW = 128
ROT_DIM = ATT_HD // 4
ROPE_THETA = 500000.0
D_FF = 2816
ALPHA = (2 * DEPTH) ** 0.25
LN_EPS = 1e-5
HQK = ML_HEADS * ML_DK
HV = ML_HEADS * ML_DV
KVW = ATT_KV_HEADS * ATT_HD
QW = ATT_Q_HEADS * ATT_HD

LANES = 128
SUBLANES = 8
VMEM_LIMIT_BYTES = 60 * 1024 * 1024

FFN_ROWS = 1024
FFN_ROWS_KV = 1024
FFN_CHUNK = 256
ML_ROWS = 1024
ML_CHUNK = 128
ATT_ROWS = 1024
ADA_COLS = (1536, 1024)
SAMPLE_GROUP = 8

NT_DIMS = (((1,), (1,)), ((), ()))
HIGHEST = lax.Precision.HIGHEST


def _params(semantics):
    return pltpu.CompilerParams(dimension_semantics=semantics, vmem_limit_bytes=VMEM_LIMIT_BYTES)


def _const_spec(block, index):
    return pl.BlockSpec(block, lambda *_: index, pipeline_mode=pl.Buffered(1))


def _resident(shape):
    return _const_spec(shape, (0,) * len(shape))


def _dot(a, b):
    return jnp.dot(a, b, preferred_element_type=F32)


def _dot_exact(a, b):
    return jnp.dot(a, b, preferred_element_type=F32, precision=HIGHEST)


def _nt(a, b):
    return lax.dot_general(a, b, NT_DIMS, preferred_element_type=F32)


def _iota(shape, dim):
    return lax.broadcasted_iota(jnp.int32, shape, dim)


def _silu(x):
    return x * jax.nn.sigmoid(x)


def _log_sigmoid(x):
    return jnp.minimum(x, 0.0) - jnp.log1p(jnp.exp(-jnp.abs(x)))


def _layer_norm(y, g, b):
    mu = jnp.mean(y, axis=-1, keepdims=True)
    d = y - mu
    var = jnp.mean(d * d, axis=-1, keepdims=True)
    return d * lax.rsqrt(var + LN_EPS) * g + b


def _mod(ref, batch):
    return ref[...] if batch is None else ref[pl.ds(batch, 1), :]


def _modulate_bf16(x, shift, scale):
    return (x * (1.0 + scale) + shift).astype(BF16)


def _rope(x, cos, sin_lo, sin_hi):
    width = x.shape[1]
    reps = width // LANES
    tile = lambda t: jnp.concatenate([t] * reps, axis=1) if reps > 1 else t
    x_up = pltpu.roll(x, width - ROT_DIM // 2, 1)
    x_dn = pltpu.roll(x, ROT_DIM // 2, 1)
    return x * tile(cos) + x_up * tile(sin_lo) + x_dn * tile(sin_hi)


def _mod_specs(layer, first, count, n_sample_rows, prompt):
    if prompt:
        blk = n_sample_rows // SUBLANES
        return [_const_spec((None, SUBLANES, D_MODEL), (layer, blk, first + j)) for j in range(count)]
    return [_const_spec((None, n_sample_rows, D_MODEL), (layer, 0, first + j)) for j in range(count)]


def _ln_specs(index):
    return [_const_spec((None, 1, D_MODEL), (index, 0, 0))] * 2


def _ada_kernel(c_ref, w_ref, b_ref, o_ref, ca_ref):
    @pl.when((pl.program_id(0) == 0) & (pl.program_id(1) == 0))
    def _():
        ca_ref[...] = _silu(c_ref[...]).astype(BF16)

    o_ref[...] = _dot(ca_ref[...], w_ref[...].astype(BF16)) + b_ref[...]


def _ada(c, w, b):
    n_l, k, n = w.shape
    m = c.shape[0]
    cols = next(width for width in ADA_COLS if n % width == 0)
    return pl.pallas_call(
        _ada_kernel,
        grid=(n_l, n // cols),
        in_specs=[pl.BlockSpec((m, k), lambda l, j: (0, 0)),
                  pl.BlockSpec((None, k, cols), lambda l, j: (l, 0, j)),
                  pl.BlockSpec((None, 1, cols), lambda l, j: (l, 0, j))],
        out_specs=pl.BlockSpec((None, m, cols), lambda l, j: (l, 0, j)),
        out_shape=jax.ShapeDtypeStruct((n_l, m, n), F32),
        scratch_shapes=[pltpu.VMEM((m, k), BF16)],
        compiler_params=_params(("arbitrary", "arbitrary")),
        name="ada",
    )(c, w, b)


def _ffn_kernel(x_ref, sh_ref, sc_ref, gt_ref, wa_ref, wu_ref, wd_ref, g_ref, b_ref, *rest,
                steps_per_seq, has_kv, n_cast, n_win):
    rest = list(rest)
    kv_in = [rest.pop(0) for _ in range(6)] if has_kv else None
    cast_in = [rest.pop(0) for _ in range(n_cast)]
    win_in = [rest.pop(0) for _ in range(2 * n_win)]
    o_ref = rest.pop(0)
    kv_out = [rest.pop(0) for _ in range(2)] if has_kv else None
    cast_out = [rest.pop(0) for _ in range(n_cast)]
    win_out = [rest.pop(0) for _ in range(n_win)]
    h_ref = rest.pop(0)

    def win_copies():
        sem, = rest
        copies = []
        for i in range(n_win):
            old, new, out = win_in[2 * i], win_in[2 * i + 1], win_out[i]
            n_buf, n_new = old.shape[1], new.shape[1]
            copies.append(pltpu.make_async_copy(old.at[:, n_new:], out.at[:, :n_buf - n_new], sem.at[2 * i]))
            copies.append(pltpu.make_async_copy(new, out.at[:, n_buf - n_new:], sem.at[2 * i + 1]))
        return copies

    if n_win:
        @pl.when(pl.program_id(0) == 0)
        def _():
            for cp in win_copies():
                cp.start()

    batch = None if steps_per_seq is None else pl.program_id(0) // steps_per_seq
    x = x_ref[...]
    xm = _modulate_bf16(x, _mod(sh_ref, batch), _mod(sc_ref, batch))
    for c0 in range(0, D_FF, FFN_CHUNK):
        cw = min(FFN_CHUNK, D_FF - c0)
        a = _dot(xm, wa_ref[:, c0:c0 + cw])
        u = _dot(xm, wu_ref[:, c0:c0 + cw])
        h_ref[:, c0:c0 + cw] = (_silu(a) * u).astype(BF16)
    y = ALPHA * x + (0.5 * _mod(gt_ref, batch)) * _dot(h_ref[...], wd_ref[...])
    out = _layer_norm(y, g_ref[...], b_ref[...])
    o_ref[...] = out
    if has_kv:
        ksh_ref, ksc_ref, wkv_ref, cos_ref, slo_ref, shi_ref = kv_in
        kv = _dot(_modulate_bf16(out, _mod(ksh_ref, batch), _mod(ksc_ref, batch)), wkv_ref[...])
        kv_out[0][...] = _rope(kv[:, :KVW], cos_ref[...], slo_ref[...], shi_ref[...])
        kv_out[1][...] = kv[:, KVW:]
    for src, dst in zip(cast_in, cast_out):
        dst[...] = src[...].astype(BF16)

    if n_win:
        @pl.when(pl.program_id(0) == pl.num_programs(0) - 1)
        def _():
            for cp in win_copies():
                cp.wait()


def _ffn(x, mods, layer, first, w_up, w_down, ln_g, ln_b, ln_index, n_sample_rows, seq, kv=None, cast=None,
         windows=()):
    t = x.shape[0]
    rows = min(FFN_ROWS if kv is None else FFN_ROWS_KV, t)
    steps = t // rows
    x_spec = pl.BlockSpec((rows, D_MODEL), lambda i: (i, 0))
    in_specs = ([x_spec] + _mod_specs(layer, first, 3, n_sample_rows, seq is not None)
                + [_const_spec((D_MODEL, D_FF), (0, 0)), _const_spec((D_MODEL, D_FF), (0, 1)),
                   _resident(w_down.shape)]
                + _ln_specs(ln_index))
    args = [x, mods, mods, mods, w_up, w_up, w_down, ln_g, ln_b]
    out_specs = [x_spec]
    out_shape = [jax.ShapeDtypeStruct((t, D_MODEL), F32)]
    if kv is not None:
        kv_mods, w_kv, tables = kv
        pos_blocks = tables[0].shape[0] // rows
        tab_spec = pl.BlockSpec((rows, LANES), lambda i: (i % pos_blocks, 0))
        kv_spec = pl.BlockSpec((rows, KVW), lambda i: (i, 0))
        in_specs += (_mod_specs(0, 0, 2, n_sample_rows, seq is not None)
                     + [_resident(w_kv.shape), tab_spec, tab_spec, tab_spec])
        args += [kv_mods, kv_mods, w_kv, *tables]
        out_specs += [kv_spec, kv_spec]
        out_shape += [jax.ShapeDtypeStruct((t, KVW), F32)] * 2
    n_cast = 0
    if cast is not None:
        up_f32, down_f32, src_layer, *extra = cast
        up_rows = D_MODEL // steps
        down_rows = 2 * D_FF // steps
        assert up_rows % 16 == 0 and down_rows % 16 == 0 and steps % 2 == 0
        in_specs += [pl.BlockSpec((None, up_rows, 2 * D_FF), lambda i: (src_layer, i, 0)),
                     pl.BlockSpec((None, down_rows, D_MODEL), lambda i: (src_layer, i // 2, 0))]
        args += [up_f32, down_f32]
        out_specs += [pl.BlockSpec((up_rows, 2 * D_FF), lambda i: (i, 0)),
                      pl.BlockSpec((down_rows, D_MODEL), lambda i: (i // 2, 0))]
        out_shape += [jax.ShapeDtypeStruct((D_MODEL, 2 * D_FF), BF16),
                      jax.ShapeDtypeStruct((D_FF, D_MODEL), BF16)]
        for w in extra:
            spec = pl.BlockSpec((None, up_rows, w.shape[2]), lambda i: (0, i, 0))
            in_specs.append(spec)
            args.append(w)
            out_specs.append(spec)
            out_shape.append(jax.ShapeDtypeStruct(w.shape, BF16))
        n_cast = 2 + len(extra)
    scratch = [pltpu.VMEM((rows, D_FF), BF16)]
    for old, new in windows:
        in_specs += [pl.BlockSpec(memory_space=pl.ANY)] * 2
        args += [old, new]
        out_specs.append(pl.BlockSpec(memory_space=pl.ANY))
        out_shape.append(jax.ShapeDtypeStruct(old.shape, old.dtype))
    if windows:
        scratch.append(pltpu.SemaphoreType.DMA((2 * len(windows),)))
    outs = pl.pallas_call(
        functools.partial(_ffn_kernel, steps_per_seq=None if seq is None else seq // rows,
                          has_kv=kv is not None, n_cast=n_cast, n_win=len(windows)),
        grid=(steps,),
        in_specs=in_specs,
        out_specs=out_specs,
        out_shape=out_shape,
        scratch_shapes=scratch,
        compiler_params=_params(("arbitrary",)),
        name="ffn" + ("_kv" if kv is not None else "") + ("_cast" if cast is not None else ""),
    )(*args)
    return outs[0] if len(outs) == 1 else outs


def _head_norm(hh, w_row):
    mu = jnp.mean(hh, axis=-1, keepdims=True)
    d = hh - mu
    var = jnp.mean(d * d, axis=-1, keepdims=True)
    return d * lax.rsqrt(var + LN_EPS) * w_row


def _split3(x, axis):
    hi = x.astype(BF16)
    r1 = x - hi.astype(F32)
    mid = r1.astype(BF16)
    lo = (r1 - mid.astype(F32)).astype(BF16)
    return jnp.concatenate([hi, mid, lo], axis=axis)


def _tile_lanes(t, reps):
    return jnp.concatenate([t] * reps, axis=1)


def _mlstm_prompt_kernel(x_ref, sh_ref, sc_ref, gt_ref, wp_ref, wg_ref, bg_ref, nw_ref, wo_ref, g_ref, b_ref,
                         o_ref, c_out, m_out, proj_ref, rep_ref, y_ref, c_ref, mm_ref):
    batch = pl.program_id(0)
    step = pl.program_id(1)
    size = ML_CHUNK
    assert size == LANES and ML_DK == LANES

    @pl.when(step == 0)
    def _():
        c_ref[...] = jnp.zeros_like(c_ref)
        mm_ref[...] = jnp.zeros_like(mm_ref)

    x = x_ref[...]
    rows = x.shape[0]
    n_chunks = rows // size
    h = _modulate_bf16(x, _mod(sh_ref, batch), _mod(sc_ref, batch))
    gates = _dot(h, wg_ref[...]) + bg_ref[...]

    r = _iota((size, size), 0)
    c = _iota((size, size), 1)
    causal = c <= r
    eye = c == r
    tri3 = _tile_lanes(jnp.where(causal, 1.0, 0.0).astype(BF16), 3)
    n_rep = 2 * ML_HEADS
    spread = jnp.where(_iota((LANES, n_rep * LANES), 1) // LANES == _iota((LANES, n_rep * LANES), 0),
                       1.0, 0.0).astype(BF16)
    spread3 = jnp.concatenate([spread] * 3, axis=0)
    ones3 = jnp.ones((3 * size, LANES), BF16)
    sr = _iota((2 * SUBLANES, LANES), 0)
    sl = _iota((2 * SUBLANES, LANES), 1)
    pick = (jnp.where((sr < ML_HEADS) & (sl == sr), 1.0, 0.0)
            - jnp.where((sr < ML_HEADS) & (sl == sr + ML_HEADS), 1.0, 0.0)
            + jnp.where((sr >= SUBLANES) & (sr < SUBLANES + ML_HEADS) & (sl == sr - SUBLANES + ML_HEADS),
                        1.0, 0.0))
    pick3 = _tile_lanes(pick.astype(BF16), 3)

    log_f = _log_sigmoid(gates)
    bwide = jnp.concatenate([_dot(tri3, _split3(log_f[ci * size:(ci + 1) * size, :], 0))
                             for ci in range(n_chunks)], axis=0)
    src3 = _split3(jnp.where(_iota((rows, LANES), 1) < ML_HEADS, gates, bwide), 1)
    rep_ref[...] = _dot(src3, spread3)
    ab_rows = _nt(pick3, src3)
    a_rows = ab_rows[:SUBLANES, :]
    b_rows = ab_rows[SUBLANES:, :]
    lane_in_chunk = _iota((SUBLANES, rows), 1) % size
    run = a_rows
    shift = 1
    while shift < size:
        run = jnp.maximum(run, jnp.where(lane_in_chunk >= shift, pltpu.roll(run, shift, 1), -jnp.inf))
        shift *= 2
    m_rows = [mm_ref[...]]
    g_rows = []
    for ci in range(n_chunks):
        run_c = run[:, ci * size:(ci + 1) * size]
        g_rows.append(jnp.maximum(run_c, m_rows[-1]))
        end = (ci + 1) * size - 1
        m_end = b_rows[:, end:end + 1] + jnp.maximum(run[:, end:end + 1], m_rows[-1][:, :1])
        m_rows.append(jnp.broadcast_to(m_end, (SUBLANES, LANES)))
    mm_ref[...] = m_rows[-1]

    proj_ref[...] = _dot(h, wp_ref[...])

    states = [c_ref[hd] for hd in range(ML_HEADS)]
    for ci in range(n_chunks):
        lo, hi = ci * size, (ci + 1) * size
        for hd in range(ML_HEADS):
            i_rep = rep_ref[lo:hi, hd * LANES:(hd + 1) * LANES]
            b_rep = rep_ref[lo:hi, (ML_HEADS + hd) * LANES:(ML_HEADS + hd + 1) * LANES]
            m_prev = m_rows[ci][hd:hd + 1, :]
            m_new = m_rows[ci + 1][hd:hd + 1, :]
            a_row = a_rows[hd:hd + 1, lo:hi]
            g_rep = _dot(_split3(jnp.where(eye, g_rows[ci][hd:hd + 1, :], 0.0), 1), ones3)
            w = jnp.where(causal, jnp.exp(a_row - g_rep), 0.0)
            w_init = jnp.exp(m_prev - g_rep)
            floor = jnp.exp(-(b_rep + g_rep))
            b_last = b_rep[size - 1:size, :]
            w_end = jnp.exp(b_last - b_rep + i_rep - m_new)
            decay = jnp.exp(b_last + m_prev - m_new)

            q = proj_ref[lo:hi, hd * ML_DK:(hd + 1) * ML_DK]
            k = proj_ref[lo:hi, HQK + hd * ML_DK:HQK + (hd + 1) * ML_DK] * (ML_DK ** -0.5)
            v = proj_ref[lo:hi, 2 * HQK + hd * ML_DV:2 * HQK + (hd + 1) * ML_DV]
            qb = q.astype(BF16)
            vb = v.astype(BF16)
            c_mat = states[hd][:, :ML_DV]
            n_row = states[hd][0:1, ML_DV:]
            s = _nt(qb, k.astype(BF16)) * w
            num = _dot(s.astype(BF16), vb) + _tile_lanes(w_init, 2) * _dot(qb, c_mat.astype(BF16))
            den = (jnp.sum(s, axis=-1, keepdims=True)
                   + w_init[:, :1] * jnp.sum(q * n_row, axis=-1, keepdims=True))
            hh = num / jnp.maximum(jnp.abs(den), floor[:, :1])
            kw = k * w_end
            c_new = _tile_lanes(decay, 2) * c_mat + _dot(kw.T.astype(BF16), vb)
            n_new = decay * n_row + jnp.sum(kw, axis=0, keepdims=True)
            states[hd] = jnp.concatenate([c_new, jnp.broadcast_to(n_new, (ML_DK, LANES))], axis=1)
            y_ref[lo:hi, hd * ML_DV:(hd + 1) * ML_DV] = _head_norm(hh, nw_ref[:, hd * ML_DV:(hd + 1) * ML_DV])
    for hd in range(ML_HEADS):
        c_ref[hd] = states[hd]

    o_gate = jax.nn.sigmoid(proj_ref[:, 2 * HQK + HV:2 * HQK + 2 * HV])
    y = _dot((y_ref[...] * o_gate).astype(BF16), wo_ref[...])
    o_ref[...] = _layer_norm(ALPHA * x + _mod(gt_ref, batch) * y, g_ref[...], b_ref[...])

    @pl.when(step == pl.num_programs(1) - 1)
    def _():
        c_out[...] = c_ref[...]
        m_out[...] = mm_ref[...]


def _mlstm_weight_specs(wg, bg, nw, wo):
    return [_const_spec((None, D_MODEL, 2 * HQK + 2 * HV), (0, 0, 0)), _resident(wg.shape),
            _resident(bg.shape), _resident(nw.shape), _resident(wo.shape)] + _ln_specs(1)


def _mlstm_prompt(x, mods, w_in, wg, bg, nw, wo, ln_g, ln_b, n_sample_rows):
    bsz, seq, _ = x.shape
    rows = min(ML_ROWS, seq)
    x_spec = pl.BlockSpec((None, rows, D_MODEL), lambda bi, i: (bi, i, 0))
    state_shape = (ML_HEADS, ML_DK, ML_DV + LANES)
    return pl.pallas_call(
        _mlstm_prompt_kernel,
        grid=(bsz, seq // rows),
        in_specs=[x_spec] + _mod_specs(0, 3, 3, n_sample_rows, True) + _mlstm_weight_specs(wg, bg, nw, wo),
        out_specs=[x_spec, pl.BlockSpec((None,) + state_shape, lambda bi, i: (bi, 0, 0, 0)),
                   pl.BlockSpec((None, SUBLANES, LANES), lambda bi, i: (bi, 0, 0))],
        out_shape=[jax.ShapeDtypeStruct((bsz, seq, D_MODEL), F32),
                   jax.ShapeDtypeStruct((bsz,) + state_shape, F32),
                   jax.ShapeDtypeStruct((bsz, SUBLANES, LANES), F32)],
        scratch_shapes=[pltpu.VMEM((rows, 2 * HQK + 2 * HV), F32),
                        pltpu.VMEM((rows, 2 * ML_HEADS * LANES), F32),
                        pltpu.VMEM((rows, HV), F32),
                        pltpu.VMEM(state_shape, F32),
                        pltpu.VMEM((SUBLANES, LANES), F32)],
        compiler_params=_params(("arbitrary", "arbitrary")),
        name="mlstm_prompt",
    )(x, mods, mods, mods, w_in, wg, bg, nw, wo, ln_g, ln_b)


def _mlstm_sample_kernel(x_ref, sh_ref, sc_ref, gt_ref, wp_ref, wg_ref, bg_ref, nw_ref, wo_ref, g_ref, b_ref,
                         c0_ref, n0_ref, m0_ref, o_ref, c_out, n_out, m_out,
                         proj_ref, gate_ref, y_ref, *, seq):
    step = pl.program_id(0)
    grp = c0_ref.shape[0]
    rows = grp * seq

    @pl.when(step == 0)
    def _():
        h = _modulate_bf16(x_ref[...], sh_ref[...], sc_ref[...])
        proj_ref[...] = _dot(h, wp_ref[...])
        gate_ref[...] = _dot(h, wg_ref[...]) + bg_ref[...]

    r = _iota((rows, rows), 0)
    c = _iota((rows, rows), 1)
    same = (r // seq) == (c // seq)
    causal = same & (c <= r)
    eye = r == c
    last_of_row = c == (r // seq) * seq + (seq - 1)
    same_f = same.astype(F32)
    expand = (_iota((rows, grp), 0) // seq == _iota((rows, grp), 1)).astype(F32)
    pick_last = (_iota((grp, rows), 1) == _iota((grp, rows), 0) * seq + (seq - 1)).astype(F32)
    seq_of_row = _iota((rows, ML_DV), 0) // seq
    col_seq = _iota((ML_DK, rows), 1) // seq
    lane = _iota((rows, LANES), 1)
    to_row = lambda col: jnp.sum(jnp.where(eye, col, 0.0), axis=0, keepdims=True)

    r0 = pl.multiple_of(step * rows, rows)
    gates = gate_ref[pl.ds(r0, rows), :]
    log_f = _log_sigmoid(gates)
    bwide = _dot_exact(causal.astype(F32), log_f)
    bsum = _dot_exact(same_f, log_f)
    m_prev_wide = _dot_exact(expand, m0_ref[...])
    n_rows_wide = _dot_exact(expand, n0_ref[...])
    n_pieces = []
    m_wide = jnp.zeros((rows, LANES), F32)
    for hd in range(ML_HEADS):
        q = proj_ref[pl.ds(r0, rows), hd * ML_DK:(hd + 1) * ML_DK]
        k = proj_ref[pl.ds(r0, rows), HQK + hd * ML_DK:HQK + (hd + 1) * ML_DK] * (ML_DK ** -0.5)
        v = proj_ref[pl.ds(r0, rows), 2 * HQK + hd * ML_DV:2 * HQK + (hd + 1) * ML_DV]
        bcol = bwide[:, ML_HEADS + hd:ML_HEADS + hd + 1]
        b_last = bsum[:, ML_HEADS + hd:ML_HEADS + hd + 1]
        icol = gates[:, hd:hd + 1]
        m_prev = m_prev_wide[:, hd:hd + 1]
        n_rows = n_rows_wide[:, hd * ML_DK:(hd + 1) * ML_DK]
        log_w = jnp.where(causal, bcol - to_row(bcol) + to_row(icol), -jnp.inf)
        log_init = bcol + m_prev
        m_t = jnp.maximum(log_init, jnp.max(log_w, axis=-1, keepdims=True))
        w = jnp.exp(log_w - m_t)
        w_init = jnp.exp(log_init - m_t)
        qb = q.astype(BF16)
        s = _nt(qb, k.astype(BF16)) * w
        inter = None
        for bi in range(grp):
            cand = _dot(qb, c0_ref[bi, hd].astype(BF16))
            inter = cand if inter is None else jnp.where(seq_of_row == bi, cand, inter)
        num = _dot(s.astype(BF16), v.astype(BF16)) + w_init * inter
        den = jnp.sum(s, axis=-1, keepdims=True) + w_init * jnp.sum(q * n_rows, axis=-1, keepdims=True)
        hh = num / jnp.maximum(jnp.abs(den), jnp.exp(-m_t))
        m_new = jnp.sum(jnp.where(last_of_row, to_row(m_t), 0.0), axis=1, keepdims=True)
        w_end = jnp.exp(b_last - bcol + icol - m_new)
        decay = jnp.exp(b_last + m_prev - m_new)
        kw = k * w_end
        kw_t = kw.T
        lhs = jnp.concatenate([jnp.where(col_seq == bi, kw_t, 0.0) for bi in range(grp)], axis=0)
        upd = _dot(lhs.astype(BF16), v.astype(BF16))
        for bi in range(grp):
            c_out[bi, hd] = (decay[bi * seq:bi * seq + 1, :] * c0_ref[bi, hd]
                             + upd[bi * ML_DK:(bi + 1) * ML_DK, :])
        n_pieces.append(decay * n_rows + _dot_exact(same_f, kw))
        m_wide = jnp.where(lane == hd, m_new, m_wide)
        y_ref[pl.ds(r0, rows), hd * ML_DV:(hd + 1) * ML_DV] = _head_norm(
            hh, nw_ref[:, hd * ML_DV:(hd + 1) * ML_DV])
    n_out[...] = _dot_exact(pick_last, jnp.concatenate(n_pieces, axis=1))
    m_out[...] = _dot_exact(pick_last, m_wide)

    @pl.when(step == pl.num_programs(0) - 1)
    def _():
        o_gate = jax.nn.sigmoid(proj_ref[:, 2 * HQK + HV:2 * HQK + 2 * HV])
        y = _dot((y_ref[...] * o_gate).astype(BF16), wo_ref[...])
        o_ref[...] = _layer_norm(ALPHA * x_ref[...] + gt_ref[...] * y, g_ref[...], b_ref[...])


def _mlstm_sample(x, mods, w_in, wg, bg, nw, wo, ln_g, ln_b, c0, n0, m0, seq):
    t = x.shape[0]
    grp = SAMPLE_GROUP
    state_specs = [pl.BlockSpec((grp, ML_HEADS, ML_DK, ML_DV), lambda i: (i, 0, 0, 0)),
                   pl.BlockSpec((grp, HQK), lambda i: (i, 0)),
                   pl.BlockSpec((grp, LANES), lambda i: (i, 0))]
    return pl.pallas_call(
        functools.partial(_mlstm_sample_kernel, seq=seq),
        grid=(t // (grp * seq),),
        in_specs=[_resident(x.shape)] + _mod_specs(0, 3, 3, t, False)
                 + _mlstm_weight_specs(wg, bg, nw, wo) + state_specs,
        out_specs=[pl.BlockSpec((t, D_MODEL), lambda i: (0, 0))] + state_specs,
        out_shape=[jax.ShapeDtypeStruct((t, D_MODEL), F32),
                   jax.ShapeDtypeStruct(c0.shape, F32),
                   jax.ShapeDtypeStruct(n0.shape, F32),
                   jax.ShapeDtypeStruct(m0.shape, F32)],
        scratch_shapes=[pltpu.VMEM((t, 2 * HQK + 2 * HV), F32),
                        pltpu.VMEM((t, LANES), F32),
                        pltpu.VMEM((t, HV), F32)],
        compiler_params=_params(("arbitrary",)),
        name="mlstm_sample",
    )(x, mods, mods, mods, w_in, wg, bg, nw, wo, ln_g, ln_b, c0, n0, m0)


def _sink_column(sink_ref, kh, rows_per_head):
    rid = _iota((ATT_GROUP * rows_per_head, 1), 0)
    col = jnp.full((ATT_GROUP * rows_per_head, 1), sink_ref[kh * ATT_GROUP], F32)
    for gq in range(1, ATT_GROUP):
        col = jnp.where(rid >= gq * rows_per_head, sink_ref[kh * ATT_GROUP + gq], col)
    return col


def _head_lanes(x, kh):
    return jnp.where(_iota(x.shape, 1) // ATT_HD == kh, x, 0.0).astype(BF16)


def _stack_groups(x):
    return jnp.concatenate([x[:, g * KVW:(g + 1) * KVW] for g in range(ATT_GROUP)], axis=0)


def _attn_prompt_kernel(sink_ref, x_ref, sh_ref, sc_ref, gt_ref, wq_ref, cos_ref, slo_ref, shi_ref,
                        kp_ref, kc_ref, vp_ref, vc_ref, wo_ref, g_ref, b_ref, o_ref, a_ref):
    batch = pl.program_id(0)
    step = pl.program_id(1)
    x = x_ref[...]
    rows = x.shape[0]
    n_blk = rows // WINDOW
    q = _rope(_dot(_modulate_bf16(x, _mod(sh_ref, batch), _mod(sc_ref, batch)), wq_ref[...]),
              cos_ref[...], slo_ref[...], shi_ref[...])
    q = (q * (ATT_HD ** -0.5)).astype(BF16)

    keys = jnp.concatenate([kp_ref[...], kc_ref[...]], axis=0)
    vals = jnp.concatenate([vp_ref[...], vc_ref[...]], axis=0)
    k_heads = [_head_lanes(keys, kh) for kh in range(ATT_KV_HEADS)]
    v_heads = [_head_lanes(vals, kh) for kh in range(ATT_KV_HEADS)]

    rid = _iota((ATT_GROUP * WINDOW, WINDOW), 0) % WINDOW
    cid = _iota((ATT_GROUP * WINDOW, WINDOW), 1)
    from_prev = cid > rid
    no_prev = jnp.where(step > 0, 0.0, -jnp.inf)
    sinks = [_sink_column(sink_ref, kh, WINDOW) for kh in range(ATT_KV_HEADS)]

    scores = [[_nt(_stack_groups(q[n * WINDOW:(n + 1) * WINDOW, :]),
                   k_heads[kh][n * WINDOW:(n + 2) * WINDOW, :])
               for kh in range(ATT_KV_HEADS)] for n in range(n_blk)]
    probs = []
    for n in range(n_blk):
        row = []
        for kh in range(ATT_KV_HEADS):
            s_prev = scores[n][kh][:, :WINDOW]
            if n == 0:
                s_prev = s_prev + no_prev
            s = jnp.where(from_prev, s_prev, scores[n][kh][:, WINDOW:])
            mx = jnp.maximum(jnp.max(s, axis=-1, keepdims=True), sinks[kh])
            p = jnp.exp(s - mx)
            den = jnp.sum(p, axis=-1, keepdims=True) + jnp.exp(sinks[kh] - mx)
            pn = p / den
            row.append(jnp.where(from_prev, pn, 0.0).astype(BF16))
            row.append(jnp.where(from_prev, 0.0, pn).astype(BF16))
        probs.append(jnp.concatenate(row, axis=1))
    for n in range(n_blk):
        v_all = jnp.concatenate([v_heads[kh][n * WINDOW:(n + 2) * WINDOW, :]
                                 for kh in range(ATT_KV_HEADS)], axis=0)
        o = _dot(probs[n], v_all)
        for g in range(ATT_GROUP):
            a_ref[n * WINDOW:(n + 1) * WINDOW, g * KVW:(g + 1) * KVW] = o[g * WINDOW:(g + 1) * WINDOW, :]

    y = _dot(a_ref[...].astype(BF16), wo_ref[...])
    o_ref[...] = _layer_norm(ALPHA * x + _mod(gt_ref, batch) * y, g_ref[...], b_ref[...])


def _attn_prompt(x, mods, wq, tables, k, v, sinks, wo, ln_g, ln_b, n_sample_rows):
    bsz, seq, _ = x.shape
    rows = min(ATT_ROWS, seq)
    per = rows // WINDOW
    x_spec = pl.BlockSpec((None, rows, D_MODEL), lambda bi, i: (bi, i, 0))
    cur = pl.BlockSpec((None, rows, KVW), lambda bi, i: (bi, i, 0))
    prev = pl.BlockSpec((None, WINDOW, KVW), lambda bi, i: (bi, jnp.maximum(i * per - 1, 0), 0))
    tab = pl.BlockSpec((rows, LANES), lambda bi, i: (i, 0))
    return pl.pallas_call(
        _attn_prompt_kernel,
        grid=(bsz, seq // rows),
        in_specs=[pl.BlockSpec(memory_space=pltpu.SMEM), x_spec]
                 + _mod_specs(1, 3, 3, n_sample_rows, True)
                 + [_resident(wq.shape), tab, tab, tab, prev, cur, prev, cur, _resident(wo.shape)]
                 + _ln_specs(4),
        out_specs=x_spec,
        out_shape=jax.ShapeDtypeStruct((bsz, seq, D_MODEL), F32),
        scratch_shapes=[pltpu.VMEM((rows, QW), F32)],
        compiler_params=_params(("arbitrary", "arbitrary")),
        name="attn_prompt",
    )(sinks, x, mods, mods, mods, wq, *tables, k, k, v, v, wo, ln_g, ln_b)


def _attn_sample_kernel(sink_ref, x_ref, sh_ref, sc_ref, gt_ref, wq_ref, cos_ref, slo_ref, shi_ref,
                        kn_ref, vn_ref, kb_ref, vb_ref, wo_ref, g_ref, b_ref, o_ref,
                        q_ref, a_ref, *, seq):
    step = pl.program_id(0)
    pair = 2 * seq
    assert pair == SUBLANES
    grp, n_buf, _ = kb_ref.shape

    @pl.when(step == 0)
    def _():
        q = _rope(_dot(_modulate_bf16(x_ref[...], sh_ref[...], sc_ref[...]), wq_ref[...]),
                  cos_ref[...], slo_ref[...], shi_ref[...])
        q_ref[...] = q * (ATT_HD ** -0.5)

    n_rows = ATT_GROUP * pair
    n_all = ATT_KV_HEADS * n_rows
    n_new = 2 * pair
    q_pos = _iota((n_all, n_buf), 0) % seq + PAST_LEN
    k_pos = _iota((n_all, n_buf), 1) + (PAST_LEN - n_buf)
    buf_mask = (k_pos <= q_pos) & (k_pos > q_pos - WINDOW) & (k_pos >= 0)
    q_tok = _iota((n_all, n_new), 0) % seq
    new_col = _iota((n_all, n_new), 1)
    out_lane_head = _iota((n_rows, KVW), 1) // ATT_HD
    low = _iota((n_rows, KVW), 0) % pair < seq
    head_of_row = _iota((n_all, 1), 0) // pair
    sink_col = jnp.full((n_all, 1), sink_ref[0], F32)
    for hq in range(1, ATT_Q_HEADS):
        sink_col = jnp.where(head_of_row >= hq, sink_ref[hq], sink_col)
    pad_rows = jnp.zeros((n_new - pair, KVW), BF16)
    for p in range(grp // 2):
        r0 = pl.multiple_of(step * (grp * seq) + p * pair, pair)
        q_all = _stack_groups(q_ref[pl.ds(r0, pair), :].astype(BF16))
        q_bd = jnp.concatenate([_head_lanes(q_all, kh) for kh in range(ATT_KV_HEADS)], axis=0)
        k_new = jnp.concatenate([kn_ref[pl.ds(r0, pair), :].astype(BF16), pad_rows], axis=0)
        v_new = jnp.concatenate([vn_ref[pl.ds(r0, pair), :].astype(BF16), pad_rows], axis=0)
        s_new_all = _nt(q_bd, k_new)
        halves = []
        for half in range(2):
            bi = 2 * p + half
            tok = new_col - half * seq
            own_new = (tok >= 0) & (tok < seq) & (tok <= q_tok)
            s_buf = jnp.where(buf_mask, _nt(q_bd, kb_ref[bi].astype(BF16)), -jnp.inf)
            s_new = jnp.where(own_new, s_new_all, -jnp.inf)
            mx = jnp.maximum(jnp.maximum(jnp.max(s_buf, axis=-1, keepdims=True),
                                         jnp.max(s_new, axis=-1, keepdims=True)), sink_col)
            e_buf = jnp.exp(s_buf - mx)
            e_new = jnp.exp(s_new - mx)
            den = (jnp.sum(e_buf, axis=-1, keepdims=True) + jnp.sum(e_new, axis=-1, keepdims=True)
                   + jnp.exp(sink_col - mx))
            o_all = (_dot((e_buf / den).astype(BF16), vb_ref[bi].astype(BF16))
                     + _dot((e_new / den).astype(BF16), v_new))
            o_half = None
            for kh in range(ATT_KV_HEADS):
                part = jnp.where(out_lane_head == kh, o_all[kh * n_rows:(kh + 1) * n_rows, :], 0.0)
                o_half = part if o_half is None else o_half + part
            halves.append(o_half)
        o = jnp.where(low, halves[0], halves[1])
        for g in range(ATT_GROUP):
            a_ref[pl.ds(r0, pair), g * KVW:(g + 1) * KVW] = o[g * pair:(g + 1) * pair, :]

    @pl.when(step == pl.num_programs(0) - 1)
    def _():
        y = _dot(a_ref[...].astype(BF16), wo_ref[...])
        o_ref[...] = _layer_norm(ALPHA * x_ref[...] + gt_ref[...] * y, g_ref[...], b_ref[...])


def _attn_sample(x, mods, wq, tables, k_new, v_new, k_buf, v_buf, sinks, wo, ln_g, ln_b, seq):
    t = x.shape[0]
    grp = SAMPLE_GROUP
    buf_spec = pl.BlockSpec((grp,) + k_buf.shape[1:], lambda i: (i, 0, 0))
    return pl.pallas_call(
        functools.partial(_attn_sample_kernel, seq=seq),
        grid=(t // (grp * seq),),
        in_specs=[pl.BlockSpec(memory_space=pltpu.SMEM), _resident(x.shape)]
                 + _mod_specs(1, 3, 3, t, False)
                 + [_resident(wq.shape)] + [_resident(tab.shape) for tab in tables]
                 + [_resident(k_new.shape), _resident(v_new.shape), buf_spec, buf_spec, _resident(wo.shape)]
                 + _ln_specs(4),
        out_specs=pl.BlockSpec((t, D_MODEL), lambda i: (0, 0)),
        out_shape=jax.ShapeDtypeStruct((t, D_MODEL), F32),
        scratch_shapes=[pltpu.VMEM((t, QW), F32), pltpu.VMEM((t, QW), F32)],
        compiler_params=_params(("arbitrary",)),
        name="attn_sample",
    )(sinks, x, mods, mods, mods, wq, *tables, k_new, v_new, k_buf, v_buf, wo, ln_g, ln_b)


def _rope_tables(pos):
    half = ROT_DIM // 2
    inv_freq = ROPE_THETA ** (-jnp.arange(half, dtype=F32) * 2.0 / ROT_DIM)
    ang = pos.astype(F32)[:, None] * inv_freq[None, :]
    cos, sin = jnp.cos(ang), jnp.sin(ang)
    dim = np.arange(LANES) % ATT_HD
    freq = np.arange(half)[:, None]
    lo = (dim[None, :] == freq).astype(np.float32)
    hi = (dim[None, :] == freq + half).astype(np.float32)
    spread = lambda a, m: jnp.dot(a, jnp.asarray(m), precision=HIGHEST)
    return (spread(cos, lo + hi) + jnp.asarray((dim >= ROT_DIM).astype(np.float32)),
            spread(sin, -lo), spread(sin, hi))


def kernel(x_prompt, x_sample, c_prompt, c_sample, state_mlstm_C, state_mlstm_n, state_mlstm_m,
           cache_win_k, cache_win_v, ada_w, ada_b, ffn1_up, ffn1_down, ffn2_up, ffn2_down, ln_g, ln_b,
           ml_w_in, ml_b_gates, ml_norm_w, ml_w_out, kv_ada_w, kv_ada_b, kv_w,
           att_w_q, att_sinks, att_w_o):
    bp, sp, _ = x_prompt.shape
    bs, ss, _ = x_sample.shape
    n_buf = cache_win_k.shape[1]
    ts = bs * ss

    c_all = jnp.concatenate([jnp.repeat(c_sample, ss, axis=0), c_prompt,
                             jnp.zeros((-bp % (2 * SUBLANES), D_MODEL), F32)], axis=0)
    mods = _ada(c_all, ada_w, ada_b[:, None, :])
    kv_mods = _ada(c_all, kv_ada_w[None], kv_ada_b[None, None, :])

    bf = lambda w: w.astype(BF16)
    ffn_f32 = [(ffn1_up, ffn1_down, 0), (ffn2_up, ffn2_down, 0), (ffn1_up, ffn1_down, 1), (ffn2_up, ffn2_down, 1)]
    ffn_w = [(bf(ffn1_up[0]), bf(ffn1_down[0]))]
    w_in = bf(ml_w_in)
    gate_pad = LANES - 2 * ML_HEADS
    w_gate = bf(jnp.pad(ml_w_in[0, :, 2 * HQK + 2 * HV:], ((0, 0), (0, gate_pad))))
    b_gate = jnp.pad(ml_b_gates[0], (0, gate_pad))[None, :]
    norm_w = ml_norm_w[0][None, :]
    w_q = bf(att_w_q[0].reshape(D_MODEL, ATT_KV_HEADS, ATT_GROUP, ATT_HD)
             .transpose(0, 2, 1, 3).reshape(D_MODEL, QW))
    w_o = bf(att_w_o[0].reshape(ATT_KV_HEADS, ATT_GROUP, ATT_HD, D_MODEL)
             .transpose(1, 0, 2, 3).reshape(QW, D_MODEL))
    sinks = att_sinks[0]
    lng = ln_g.reshape(DEPTH * 3, 1, D_MODEL)
    lnb = ln_b.reshape(DEPTH * 3, 1, D_MODEL)

    tab_p = _rope_tables(jnp.arange(sp, dtype=jnp.int32))
    tab_s = _rope_tables(jnp.tile(PAST_LEN + jnp.arange(ss, dtype=jnp.int32), bs))

    xp = x_prompt.reshape(bp * sp, D_MODEL)
    xp, *cast, w_out, w_kv = _ffn(xp, mods, 0, 0, *ffn_w[0], lng, lnb, 0, ts, sp,
                                  cast=ffn_f32[1] + (ml_w_out, kv_w[None]))
    ffn_w.append(tuple(cast))
    w_out, w_kv = w_out[0], w_kv[0]

    xs = x_sample.reshape(ts, D_MODEL)
    xs = _ffn(xs, mods, 0, 0, *ffn_w[0], lng, lnb, 0, ts, None)
    m0 = jnp.pad(state_mlstm_m[0], ((0, 0), (0, LANES - ML_HEADS)))
    xs, c_s, n_s, m_s = _mlstm_sample(xs, mods, w_in, w_gate, b_gate, norm_w, w_out, lng, lnb,
                                      state_mlstm_C[0], state_mlstm_n[0].reshape(bs, HQK), m0, ss)
    xs, k_s, v_s = _ffn(xs, mods, 0, 6, *ffn_w[1], lng, lnb, 2, ts, None, kv=(kv_mods, w_kv, tab_s))
    new_k = k_s.reshape(bs, ss, ATT_KV_HEADS, ATT_HD)
    new_v = v_s.reshape(bs, ss, ATT_KV_HEADS, ATT_HD)

    xp, state_p, m_p = _mlstm_prompt(xp.reshape(bp, sp, D_MODEL), mods, w_in, w_gate, b_gate,
                                     norm_w, w_out, lng, lnb, ts)
    xp, k_p, v_p, *cast = _ffn(xp.reshape(bp * sp, D_MODEL), mods, 0, 6, *ffn_w[1], lng, lnb, 2, ts, sp,
                               kv=(kv_mods, w_kv, tab_p), cast=ffn_f32[2])
    ffn_w.append(tuple(cast))
    k_p = k_p.reshape(bp, sp, KVW)
    v_p = v_p.reshape(bp, sp, KVW)
    xp, *cast = _ffn(xp, mods, 1, 0, *ffn_w[2], lng, lnb, 3, ts, sp, cast=ffn_f32[3])
    ffn_w.append(tuple(cast))
    xp = _attn_prompt(xp.reshape(bp, sp, D_MODEL), mods, w_q, tab_p, k_p, v_p, sinks, w_o, lng, lnb, ts)
    xp, win_k_s, win_v_s = _ffn(xp.reshape(bp * sp, D_MODEL), mods, 1, 6, *ffn_w[3], lng, lnb, 5, ts, sp,
                                windows=((cache_win_k, new_k), (cache_win_v, new_v)))
    y_prompt = xp.reshape(bp, sp, D_MODEL)
    keep = min(WINDOW, sp)
    win_k_p = k_p[:, sp - keep:].reshape(bp, keep, ATT_KV_HEADS, ATT_HD)
    win_v_p = v_p[:, sp - keep:].reshape(bp, keep, ATT_KV_HEADS, ATT_HD)

    xs = _ffn(xs, mods, 1, 0, *ffn_w[2], lng, lnb, 3, ts, None)
    xs = _attn_sample(xs, mods, w_q, tab_s, k_s, v_s,
                      cache_win_k.reshape(bs, n_buf, KVW), cache_win_v.reshape(bs, n_buf, KVW),
                      sinks, w_o, lng, lnb, ss)
    xs = _ffn(xs, mods, 1, 6, *ffn_w[3], lng, lnb, 5, ts, None)
    y_sample = xs.reshape(bs, ss, D_MODEL)

    return (y_prompt, y_sample,
            state_p[None, ..., :ML_DV], state_p[None, :, :, 0, ML_DV:], m_p[None, :, :ML_HEADS, 0],
            win_k_p, win_v_p,
            c_s[None], n_s.reshape(1, bs, ML_HEADS, ML_DK), m_s[None, :, :ML_HEADS],
            win_k_s, win_v_s)
```

```python
import functools

import numpy as np
import jax
import jax.numpy as jnp
from jax import lax
from jax.experimental import pallas as pl
from jax.experimental.pallas import tpu as pltpu

F32 = jnp.float32
BF16 = jnp.bfloat16

D_MODEL = 1024
DEPTH = 2
PAST_LEN = 8192
ML_HEADS = 4
ML_DK = D_MODEL // 8
ML_DV = D_MODEL // 4
ATT_Q_HEADS = 16
ATT_KV_HEADS = 4
ATT_GROUP = ATT_Q_HEADS // ATT_KV_HEADS
ATT_HD = 64
WINDOW = 128
ROT_DIM = ATT_HD // 4
ROPE_THETA = 500000.0
D_FF = 2816
ALPHA = (2 * DEPTH) ** 0.25
LN_EPS = 1e-5
HQK = ML_HEADS * ML_DK
HV = ML_HEADS * ML_DV
KVW = ATT_KV_HEADS * ATT_HD
QW = ATT_Q_HEADS * ATT_HD

LANES = 128
SUBLANES = 8
VMEM_LIMIT_BYTES = 60 * 1024 * 1024

FFN_ROWS = 1024
FFN_ROWS_KV = 1024
FFN_CHUNK = 256
ML_ROWS = 1024
ML_CHUNK = 128
ATT_ROWS = 1024
ADA_COLS = (1536, 1024)
SAMPLE_GROUP = 8
ATT_SAMPLE_GROUP = 16

NT_DIMS = (((1,), (1,)), ((), ()))
HIGHEST = lax.Precision.HIGHEST


def _params(semantics):
    return pltpu.CompilerParams(dimension_semantics=semantics, vmem_limit_bytes=VMEM_LIMIT_BYTES)


def _const_spec(block, index):
    return pl.BlockSpec(block, lambda *_: index, pipeline_mode=pl.Buffered(1))


def _resident(shape):
    return _const_spec(shape, (0,) * len(shape))


def _dot(a, b):
    return jnp.dot(a, b, preferred_element_type=F32)


def _dot_exact(a, b):
    return jnp.dot(a, b, preferred_element_type=F32, precision=HIGHEST)


def _nt(a, b):
    return lax.dot_general(a, b, NT_DIMS, preferred_element_type=F32)


def _iota(shape, dim):
    return lax.broadcasted_iota(jnp.int32, shape, dim)


def _silu(x):
    return x * jax.nn.sigmoid(x)


def _log_sigmoid(x):
    return jnp.minimum(x, 0.0) - jnp.log1p(jnp.exp(-jnp.abs(x)))


def _layer_norm(y, g, b):
    mu = jnp.mean(y, axis=-1, keepdims=True)
    d = y - mu
    var = jnp.mean(d * d, axis=-1, keepdims=True)
    return d * lax.rsqrt(var + LN_EPS) * g + b


def _mod(ref, batch):
    return ref[...] if batch is None else ref[pl.ds(batch, 1), :]


def _modulate_bf16(x, shift, scale):
    return (x * (1.0 + scale) + shift).astype(BF16)


def _rope(x, cos, sin_lo, sin_hi):
    width = x.shape[1]
    reps = width // LANES
    tile = lambda t: jnp.concatenate([t] * reps, axis=1) if reps > 1 else t
    x_up = pltpu.roll(x, width - ROT_DIM // 2, 1)
    x_dn = pltpu.roll(x, ROT_DIM // 2, 1)
    return x * tile(cos) + x_up * tile(sin_lo) + x_dn * tile(sin_hi)


def _mod_specs(layer, first, count, n_sample_rows, prompt):
    if prompt:
        blk = n_sample_rows // SUBLANES
        return [_const_spec((None, SUBLANES, D_MODEL), (layer, blk, first + j)) for j in range(count)]
    return [_const_spec((None, n_sample_rows, D_MODEL), (layer, 0, first + j)) for j in range(count)]


def _ln_specs(index):
    return [_const_spec((None, 1, D_MODEL), (index, 0, 0))] * 2


def _ada_kernel(c_ref, w_ref, b_ref, o_ref, ca_ref):
    @pl.when((pl.program_id(0) == 0) & (pl.program_id(1) == 0))
    def _():
        ca_ref[...] = _silu(c_ref[...]).astype(BF16)

    o_ref[...] = _dot(ca_ref[...], w_ref[...].astype(BF16)) + b_ref[...]


def _ada(c, w, b):
    n_l, k, n = w.shape
    m = c.shape[0]
    cols = next(width for width in ADA_COLS if n % width == 0)
    return pl.pallas_call(
        _ada_kernel,
        grid=(n_l, n // cols),
        in_specs=[pl.BlockSpec((m, k), lambda l, j: (0, 0)),
                  pl.BlockSpec((None, k, cols), lambda l, j: (l, 0, j)),
                  pl.BlockSpec((None, 1, cols), lambda l, j: (l, 0, j))],
        out_specs=pl.BlockSpec((None, m, cols), lambda l, j: (l, 0, j)),
        out_shape=jax.ShapeDtypeStruct((n_l, m, n), F32),
        scratch_shapes=[pltpu.VMEM((m, k), BF16)],
        compiler_params=_params(("arbitrary", "arbitrary")),
        name="ada",
    )(c, w, b)


def _ffn_kernel(x_ref, sh_ref, sc_ref, gt_ref, wa_ref, wu_ref, wd_ref, g_ref, b_ref, *rest,
                steps_per_seq, has_kv, n_cast):
    rest = list(rest)
    kv_in = [rest.pop(0) for _ in range(6)] if has_kv else None
    cast_in = [rest.pop(0) for _ in range(n_cast)]
    o_ref = rest.pop(0)
    kv_out = [rest.pop(0) for _ in range(2)] if has_kv else None
    cast_out = [rest.pop(0) for _ in range(n_cast)]
    h_ref, = rest

    batch = None if steps_per_seq is None else pl.program_id(0) // steps_per_seq
    x = x_ref[...]
    xm = _modulate_bf16(x, _mod(sh_ref, batch), _mod(sc_ref, batch))
    for c0 in range(0, D_FF, FFN_CHUNK):
        cw = min(FFN_CHUNK, D_FF - c0)
        a = _dot(xm, wa_ref[:, c0:c0 + cw])
        u = _dot(xm, wu_ref[:, c0:c0 + cw])
        h_ref[:, c0:c0 + cw] = (_silu(a) * u).astype(BF16)
    y = ALPHA * x + (0.5 * _mod(gt_ref, batch)) * _dot(h_ref[...], wd_ref[...])
    out = _layer_norm(y, g_ref[...], b_ref[...])
    o_ref[...] = out
    if has_kv:
        ksh_ref, ksc_ref, wkv_ref, cos_ref, slo_ref, shi_ref = kv_in
        kv = _dot(_modulate_bf16(out, _mod(ksh_ref, batch), _mod(ksc_ref, batch)), wkv_ref[...])
        kv_out[0][...] = _rope(kv[:, :KVW], cos_ref[...], slo_ref[...], shi_ref[...])
        kv_out[1][...] = kv[:, KVW:]
    for src, dst in zip(cast_in, cast_out):
        dst[...] = src[...].astype(BF16)


def _ffn(x, mods, layer, first, w_up, w_down, ln_g, ln_b, ln_index, n_sample_rows, seq, kv=None, cast=None):
    t = x.shape[0]
    rows = min(FFN_ROWS if kv is None else FFN_ROWS_KV, t)
    steps = t // rows
    x_spec = pl.BlockSpec((rows, D_MODEL), lambda i: (i, 0))
    in_specs = ([x_spec] + _mod_specs(layer, first, 3, n_sample_rows, seq is not None)
                + [_const_spec((D_MODEL, D_FF), (0, 0)), _const_spec((D_MODEL, D_FF), (0, 1)),
                   _resident(w_down.shape)]
                + _ln_specs(ln_index))
    args = [x, mods, mods, mods, w_up, w_up, w_down, ln_g, ln_b]
    out_specs = [x_spec]
    out_shape = [jax.ShapeDtypeStruct((t, D_MODEL), F32)]
    if kv is not None:
        kv_mods, w_kv, tables = kv
        pos_blocks = tables[0].shape[0] // rows
        tab_spec = pl.BlockSpec((rows, LANES), lambda i: (i % pos_blocks, 0))
        kv_spec = pl.BlockSpec((rows, KVW), lambda i: (i, 0))
        in_specs += (_mod_specs(0, 0, 2, n_sample_rows, seq is not None)
                     + [_resident(w_kv.shape), tab_spec, tab_spec, tab_spec])
        args += [kv_mods, kv_mods, w_kv, *tables]
        out_specs += [kv_spec, kv_spec]
        out_shape += [jax.ShapeDtypeStruct((t, KVW), F32)] * 2
    n_cast = 0
    if cast is not None:
        up_f32, down_f32, src_layer, *extra = cast
        up_rows = D_MODEL // steps
        down_rows = 2 * D_FF // steps
        assert up_rows % 16 == 0 and down_rows % 16 == 0 and steps % 2 == 0
        in_specs += [pl.BlockSpec((None, up_rows, 2 * D_FF), lambda i: (src_layer, i, 0)),
                     pl.BlockSpec((None, down_rows, D_MODEL), lambda i: (src_layer, i // 2, 0))]
        args += [up_f32, down_f32]
        out_specs += [pl.BlockSpec((up_rows, 2 * D_FF), lambda i: (i, 0)),
                      pl.BlockSpec((down_rows, D_MODEL), lambda i: (i // 2, 0))]
        out_shape += [jax.ShapeDtypeStruct((D_MODEL, 2 * D_FF), BF16),
                      jax.ShapeDtypeStruct((D_FF, D_MODEL), BF16)]
        for w in extra:
            spec = pl.BlockSpec((None, up_rows, w.shape[2]), lambda i: (0, i, 0))
            in_specs.append(spec)
            args.append(w)
            out_specs.append(spec)
            out_shape.append(jax.ShapeDtypeStruct(w.shape, BF16))
        n_cast = 2 + len(extra)
    outs = pl.pallas_call(
        functools.partial(_ffn_kernel, steps_per_seq=None if seq is None else seq // rows,
                          has_kv=kv is not None, n_cast=n_cast),
        grid=(steps,),
        in_specs=in_specs,
        out_specs=out_specs,
        out_shape=out_shape,
        scratch_shapes=[pltpu.VMEM((rows, D_FF), BF16)],
        compiler_params=_params(("arbitrary",)),
        name="ffn" + ("_kv" if kv is not None else "") + ("_cast" if cast is not None else ""),
    )(*args)
    return outs[0] if len(outs) == 1 else outs


def _head_norm(hh, w_row):
    mu = jnp.mean(hh, axis=-1, keepdims=True)
    d = hh - mu
    var = jnp.mean(d * d, axis=-1, keepdims=True)
    return d * lax.rsqrt(var + LN_EPS) * w_row


def _split3(x, axis):
    hi = x.astype(BF16)
    r1 = x - hi.astype(F32)
    mid = r1.astype(BF16)
    lo = (r1 - mid.astype(F32)).astype(BF16)
    return jnp.concatenate([hi, mid, lo], axis=axis)


def _tile_lanes(t, reps):
    return jnp.concatenate([t] * reps, axis=1)


def _mlstm_prompt_kernel(x_ref, sh_ref, sc_ref, gt_ref, wp_ref, wg_ref, bg_ref, nw_ref, wo_ref, g_ref, b_ref,
                         o_ref, c_out, m_out, proj_ref, rep_ref, y_ref, c_ref, mm_ref):
    batch = pl.program_id(0)
    step = pl.program_id(1)
    size = ML_CHUNK
    assert size == LANES and ML_DK == LANES

    @pl.when(step == 0)
    def _():
        c_ref[...] = jnp.zeros_like(c_ref)
        mm_ref[...] = jnp.zeros_like(mm_ref)

    x = x_ref[...]
    rows = x.shape[0]
    n_chunks = rows // size
    h = _modulate_bf16(x, _mod(sh_ref, batch), _mod(sc_ref, batch))
    gates = _dot(h, wg_ref[...]) + bg_ref[...]

    r = _iota((size, size), 0)
    c = _iota((size, size), 1)
    causal = c <= r
    eye = c == r
    tri3 = _tile_lanes(jnp.where(causal, 1.0, 0.0).astype(BF16), 3)
    n_rep = 2 * ML_HEADS
    spread = jnp.where(_iota((LANES, n_rep * LANES), 1) // LANES == _iota((LANES, n_rep * LANES), 0),
                       1.0, 0.0).astype(BF16)
    spread3 = jnp.concatenate([spread] * 3, axis=0)
    ones3 = jnp.ones((3 * size, LANES), BF16)
    sr = _iota((2 * SUBLANES, LANES), 0)
    sl = _iota((2 * SUBLANES, LANES), 1)
    pick = (jnp.where((sr < ML_HEADS) & (sl == sr), 1.0, 0.0)
            - jnp.where((sr < ML_HEADS) & (sl == sr + ML_HEADS), 1.0, 0.0)
            + jnp.where((sr >= SUBLANES) & (sr < SUBLANES + ML_HEADS) & (sl == sr - SUBLANES + ML_HEADS),
                        1.0, 0.0))
    pick3 = _tile_lanes(pick.astype(BF16), 3)

    log_f = _log_sigmoid(gates)
    bwide = jnp.concatenate([_dot(tri3, _split3(log_f[ci * size:(ci + 1) * size, :], 0))
                             for ci in range(n_chunks)], axis=0)
    src3 = _split3(jnp.where(_iota((rows, LANES), 1) < ML_HEADS, gates, bwide), 1)
    rep_ref[...] = _dot(src3, spread3)
    ab_rows = _nt(pick3, src3)
    a_rows = ab_rows[:SUBLANES, :]
    b_rows = ab_rows[SUBLANES:, :]
    lane_in_chunk = _iota((SUBLANES, rows), 1) % size
    run = a_rows
    shift = 1
    while shift < size:
        run = jnp.maximum(run, jnp.where(lane_in_chunk >= shift, pltpu.roll(run, shift, 1), -jnp.inf))
        shift *= 2
    m_rows = [mm_ref[...]]
    g_rows = []
    for ci in range(n_chunks):
        run_c = run[:, ci * size:(ci + 1) * size]
        g_rows.append(jnp.maximum(run_c, m_rows[-1]))
        end = (ci + 1) * size - 1
        m_end = b_rows[:, end:end + 1] + jnp.maximum(run[:, end:end + 1], m_rows[-1][:, :1])
        m_rows.append(jnp.broadcast_to(m_end, (SUBLANES, LANES)))
    mm_ref[...] = m_rows[-1]

    proj_ref[...] = _dot(h, wp_ref[...])

    states = [c_ref[hd] for hd in range(ML_HEADS)]
    for ci in range(n_chunks):
        lo, hi = ci * size, (ci + 1) * size
        for hd in range(ML_HEADS):
            i_rep = rep_ref[lo:hi, hd * LANES:(hd + 1) * LANES]
            b_rep = rep_ref[lo:hi, (ML_HEADS + hd) * LANES:(ML_HEADS + hd + 1) * LANES]
            m_prev = m_rows[ci][hd:hd + 1, :]
            m_new = m_rows[ci + 1][hd:hd + 1, :]
            a_row = a_rows[hd:hd + 1, lo:hi]
            g_rep = _dot(_split3(jnp.where(eye, g_rows[ci][hd:hd + 1, :], 0.0), 1), ones3)
            w = jnp.where(causal, jnp.exp(a_row - g_rep), 0.0)
            w_init = jnp.exp(m_prev - g_rep)
            floor = jnp.exp(-(b_rep + g_rep))
            b_last = b_rep[size - 1:size, :]
            w_end = jnp.exp(b_last - b_rep + i_rep - m_new)
            decay = jnp.exp(b_last + m_prev - m_new)

            q = proj_ref[lo:hi, hd * ML_DK:(hd + 1) * ML_DK]
            k = proj_ref[lo:hi, HQK + hd * ML_DK:HQK + (hd + 1) * ML_DK] * (ML_DK ** -0.5)
            v = proj_ref[lo:hi, 2 * HQK + hd * ML_DV:2 * HQK + (hd + 1) * ML_DV]
            qb = q.astype(BF16)
            vb = v.astype(BF16)
            c_mat = states[hd][:, :ML_DV]
            n_row = states[hd][0:1, ML_DV:]
            s = _nt(qb, k.astype(BF16)) * w
            num = _dot(s.astype(BF16), vb) + _tile_lanes(w_init, 2) * _dot(qb, c_mat.astype(BF16))
            den = (jnp.sum(s, axis=-1, keepdims=True)
                   + w_init[:, :1] * jnp.sum(q * n_row, axis=-1, keepdims=True))
            hh = num / jnp.maximum(jnp.abs(den), floor[:, :1])
            kw = k * w_end
            c_new = _tile_lanes(decay, 2) * c_mat + _dot(kw.T.astype(BF16), vb)
            n_new = decay * n_row + jnp.sum(kw, axis=0, keepdims=True)
            states[hd] = jnp.concatenate([c_new, jnp.broadcast_to(n_new, (ML_DK, LANES))], axis=1)
            y_ref[lo:hi, hd * ML_DV:(hd + 1) * ML_DV] = _head_norm(hh, nw_ref[:, hd * ML_DV:(hd + 1) * ML_DV])
    for hd in range(ML_HEADS):
        c_ref[hd] = states[hd]

    o_gate = jax.nn.sigmoid(proj_ref[:, 2 * HQK + HV:2 * HQK + 2 * HV])
    y = _dot((y_ref[...] * o_gate).astype(BF16), wo_ref[...])
    o_ref[...] = _layer_norm(ALPHA * x + _mod(gt_ref, batch) * y, g_ref[...], b_ref[...])

    @pl.when(step == pl.num_programs(1) - 1)
    def _():
        c_out[...] = c_ref[...]
        m_out[...] = mm_ref[...]


def _mlstm_weight_specs(wg, bg, nw, wo):
    return [_const_spec((None, D_MODEL, 2 * HQK + 2 * HV), (0, 0, 0)), _resident(wg.shape),
            _resident(bg.shape), _resident(nw.shape), _resident(wo.shape)] + _ln_specs(1)


def _mlstm_prompt(x, mods, w_in, wg, bg, nw, wo, ln_g, ln_b, n_sample_rows):
    bsz, seq, _ = x.shape
    rows = min(ML_ROWS, seq)
    x_spec = pl.BlockSpec((None, rows, D_MODEL), lambda bi, i: (bi, i, 0))
    state_shape = (ML_HEADS, ML_DK, ML_DV + LANES)
    return pl.pallas_call(
        _mlstm_prompt_kernel,
        grid=(bsz, seq // rows),
        in_specs=[x_spec] + _mod_specs(0, 3, 3, n_sample_rows, True) + _mlstm_weight_specs(wg, bg, nw, wo),
        out_specs=[x_spec, pl.BlockSpec((None,) + state_shape, lambda bi, i: (bi, 0, 0, 0)),
                   pl.BlockSpec((None, SUBLANES, LANES), lambda bi, i: (bi, 0, 0))],
        out_shape=[jax.ShapeDtypeStruct((bsz, seq, D_MODEL), F32),
                   jax.ShapeDtypeStruct((bsz,) + state_shape, F32),
                   jax.ShapeDtypeStruct((bsz, SUBLANES, LANES), F32)],
        scratch_shapes=[pltpu.VMEM((rows, 2 * HQK + 2 * HV), F32),
                        pltpu.VMEM((rows, 2 * ML_HEADS * LANES), F32),
                        pltpu.VMEM((rows, HV), F32),
                        pltpu.VMEM(state_shape, F32),
                        pltpu.VMEM((SUBLANES, LANES), F32)],
        compiler_params=_params(("arbitrary", "arbitrary")),
        name="mlstm_prompt",
    )(x, mods, mods, mods, w_in, wg, bg, nw, wo, ln_g, ln_b)


def _mlstm_sample_kernel(x_ref, sh_ref, sc_ref, gt_ref, wp_ref, wg_ref, bg_ref, nw_ref, wo_ref, g_ref, b_ref,
                         c0_ref, n0_ref, m0_ref, o_ref, c_out, n_out, m_out,
                         proj_ref, gate_ref, y_ref, *, seq):
    step = pl.program_id(0)
    grp = c0_ref.shape[0]
    rows = grp * seq

    @pl.when(step == 0)
    def _():
        h = _modulate_bf16(x_ref[...], sh_ref[...], sc_ref[...])
        proj_ref[...] = _dot(h, wp_ref[...])
        gate_ref[...] = _dot(h, wg_ref[...]) + bg_ref[...]

    r = _iota((rows, rows), 0)
    c = _iota((rows, rows), 1)
    same = (r // seq) == (c // seq)
    causal = same & (c <= r)
    eye = r == c
    last_of_row = c == (r // seq) * seq + (seq - 1)
    same_f = same.astype(F32)
    expand = (_iota((rows, grp), 0) // seq == _iota((rows, grp), 1)).astype(F32)
    pick_last = (_iota((grp, rows), 1) == _iota((grp, rows), 0) * seq + (seq - 1)).astype(F32)
    seq_of_row = _iota((rows, ML_DV), 0) // seq
    col_seq = _iota((ML_DK, rows), 1) // seq
    lane = _iota((rows, LANES), 1)
    to_row = lambda col: jnp.sum(jnp.where(eye, col, 0.0), axis=0, keepdims=True)

    r0 = pl.multiple_of(step * rows, rows)
    gates = gate_ref[pl.ds(r0, rows), :]
    log_f = _log_sigmoid(gates)
    bwide = _dot_exact(causal.astype(F32), log_f)
    bsum = _dot_exact(same_f, log_f)
    m_prev_wide = _dot_exact(expand, m0_ref[...])
    n_rows_wide = _dot_exact(expand, n0_ref[...])
    n_pieces = []
    m_wide = jnp.zeros((rows, LANES), F32)
    for hd in range(ML_HEADS):
        q = proj_ref[pl.ds(r0, rows), hd * ML_DK:(hd + 1) * ML_DK]
        k = proj_ref[pl.ds(r0, rows), HQK + hd * ML_DK:HQK + (hd + 1) * ML_DK] * (ML_DK ** -0.5)
        v = proj_ref[pl.ds(r0, rows), 2 * HQK + hd * ML_DV:2 * HQK + (hd + 1) * ML_DV]
        bcol = bwide[:, ML_HEADS + hd:ML_HEADS + hd + 1]
        b_last = bsum[:, ML_HEADS + hd:ML_HEADS + hd + 1]
        icol = gates[:, hd:hd + 1]
        m_prev = m_prev_wide[:, hd:hd + 1]
        n_rows = n_rows_wide[:, hd * ML_DK:(hd + 1) * ML_DK]
        log_w = jnp.where(causal, bcol - to_row(bcol) + to_row(icol), -jnp.inf)
        log_init = bcol + m_prev
        m_t = jnp.maximum(log_init, jnp.max(log_w, axis=-1, keepdims=True))
        w = jnp.exp(log_w - m_t)
        w_init = jnp.exp(log_init - m_t)
        qb = q.astype(BF16)
        s = _nt(qb, k.astype(BF16)) * w
        inter = None
        for bi in range(grp):
            cand = _dot(qb, c0_ref[bi, hd].astype(BF16))
            inter = cand if inter is None else jnp.where(seq_of_row == bi, cand, inter)
        num = _dot(s.astype(BF16), v.astype(BF16)) + w_init * inter
        den = jnp.sum(s, axis=-1, keepdims=True) + w_init * jnp.sum(q * n_rows, axis=-1, keepdims=True)
        hh = num / jnp.maximum(jnp.abs(den), jnp.exp(-m_t))
        m_new = jnp.sum(jnp.where(last_of_row, to_row(m_t), 0.0), axis=1, keepdims=True)
        w_end = jnp.exp(b_last - bcol + icol - m_new)
        decay = jnp.exp(b_last + m_prev - m_new)
        kw = k * w_end
        kw_t = kw.T
        lhs = jnp.concatenate([jnp.where(col_seq == bi, kw_t, 0.0) for bi in range(grp)], axis=0)
        upd = _dot(lhs.astype(BF16), v.astype(BF16))
        for bi in range(grp):
            c_out[bi, hd] = (decay[bi * seq:bi * seq + 1, :] * c0_ref[bi, hd]
                             + upd[bi * ML_DK:(bi + 1) * ML_DK, :])
        n_pieces.append(decay * n_rows + _dot_exact(same_f, kw))
        m_wide = jnp.where(lane == hd, m_new, m_wide)
        y_ref[pl.ds(r0, rows), hd * ML_DV:(hd + 1) * ML_DV] = _head_norm(
            hh, nw_ref[:, hd * ML_DV:(hd + 1) * ML_DV])
    n_out[...] = _dot_exact(pick_last, jnp.concatenate(n_pieces, axis=1))
    m_out[...] = _dot_exact(pick_last, m_wide)

    @pl.when(step == pl.num_programs(0) - 1)
    def _():
        o_gate = jax.nn.sigmoid(proj_ref[:, 2 * HQK + HV:2 * HQK + 2 * HV])
        y = _dot((y_ref[...] * o_gate).astype(BF16), wo_ref[...])
        o_ref[...] = _layer_norm(ALPHA * x_ref[...] + gt_ref[...] * y, g_ref[...], b_ref[...])


def _mlstm_sample(x, mods, w_in, wg, bg, nw, wo, ln_g, ln_b, c0, n0, m0, seq):
    t = x.shape[0]
    grp = SAMPLE_GROUP
    state_specs = [pl.BlockSpec((grp, ML_HEADS, ML_DK, ML_DV), lambda i: (i, 0, 0, 0)),
                   pl.BlockSpec((grp, HQK), lambda i: (i, 0)),
                   pl.BlockSpec((grp, LANES), lambda i: (i, 0))]
    return pl.pallas_call(
        functools.partial(_mlstm_sample_kernel, seq=seq),
        grid=(t // (grp * seq),),
        in_specs=[_resident(x.shape)] + _mod_specs(0, 3, 3, t, False)
                 + _mlstm_weight_specs(wg, bg, nw, wo) + state_specs,
        out_specs=[pl.BlockSpec((t, D_MODEL), lambda i: (0, 0))] + state_specs,
        out_shape=[jax.ShapeDtypeStruct((t, D_MODEL), F32),
                   jax.ShapeDtypeStruct(c0.shape, F32),
                   jax.ShapeDtypeStruct(n0.shape, F32),
                   jax.ShapeDtypeStruct(m0.shape, F32)],
        scratch_shapes=[pltpu.VMEM((t, 2 * HQK + 2 * HV), F32),
                        pltpu.VMEM((t, LANES), F32),
                        pltpu.VMEM((t, HV), F32)],
        compiler_params=_params(("arbitrary",)),
        name="mlstm_sample",
    )(x, mods, mods, mods, w_in, wg, bg, nw, wo, ln_g, ln_b, c0, n0, m0)


def _sink_column(sink_ref, kh, rows_per_head):
    rid = _iota((ATT_GROUP * rows_per_head, 1), 0)
    col = jnp.full((ATT_GROUP * rows_per_head, 1), sink_ref[kh * ATT_GROUP], F32)
    for gq in range(1, ATT_GROUP):
        col = jnp.where(rid >= gq * rows_per_head, sink_ref[kh * ATT_GROUP + gq], col)
    return col


def _head_lanes(x, kh):
    return jnp.where(_iota(x.shape, 1) // ATT_HD == kh, x, 0.0).astype(BF16)


def _stack_groups(x):
    return jnp.concatenate([x[:, g * KVW:(g + 1) * KVW] for g in range(ATT_GROUP)], axis=0)


def _attn_prompt_kernel(sink_ref, x_ref, sh_ref, sc_ref, gt_ref, wq_ref, cos_ref, slo_ref, shi_ref,
                        kp_ref, kc_ref, vp_ref, vc_ref, wo_ref, g_ref, b_ref, o_ref, a_ref):
    batch = pl.program_id(0)
    step = pl.program_id(1)
    x = x_ref[...]
    rows = x.shape[0]
    n_blk = rows // WINDOW
    q = _rope(_dot(_modulate_bf16(x, _mod(sh_ref, batch), _mod(sc_ref, batch)), wq_ref[...]),
              cos_ref[...], slo_ref[...], shi_ref[...])
    q = (q * (ATT_HD ** -0.5)).astype(BF16)

    keys = jnp.concatenate([kp_ref[...], kc_ref[...]], axis=0)
    vals = jnp.concatenate([vp_ref[...], vc_ref[...]], axis=0)
    k_heads = [_head_lanes(keys, kh) for kh in range(ATT_KV_HEADS)]
    v_heads = [_head_lanes(vals, kh) for kh in range(ATT_KV_HEADS)]

    rid = _iota((ATT_GROUP * WINDOW, WINDOW), 0) % WINDOW
    cid = _iota((ATT_GROUP * WINDOW, WINDOW), 1)
    from_prev = cid > rid
    no_prev = jnp.where(step > 0, 0.0, -jnp.inf)
    sinks = [_sink_column(sink_ref, kh, WINDOW) for kh in range(ATT_KV_HEADS)]

    scores = [[_nt(_stack_groups(q[n * WINDOW:(n + 1) * WINDOW, :]),
                   k_heads[kh][n * WINDOW:(n + 2) * WINDOW, :])
               for kh in range(ATT_KV_HEADS)] for n in range(n_blk)]
    probs = []
    for n in range(n_blk):
        row = []
        for kh in range(ATT_KV_HEADS):
            s_prev = scores[n][kh][:, :WINDOW]
            if n == 0:
                s_prev = s_prev + no_prev
            s = jnp.where(from_prev, s_prev, scores[n][kh][:, WINDOW:])
            mx = jnp.maximum(jnp.max(s, axis=-1, keepdims=True), sinks[kh])
            p = jnp.exp(s - mx)
            den = jnp.sum(p, axis=-1, keepdims=True) + jnp.exp(sinks[kh] - mx)
            pn = p / den
            row.append(jnp.where(from_prev, pn, 0.0).astype(BF16))
            row.append(jnp.where(from_prev, 0.0, pn).astype(BF16))
        probs.append(jnp.concatenate(row, axis=1))
    for n in range(n_blk):
        v_all = jnp.concatenate([v_heads[kh][n * WINDOW:(n + 2) * WINDOW, :]
                                 for kh in range(ATT_KV_HEADS)], axis=0)
        o = _dot(probs[n], v_all)
        for g in range(ATT_GROUP):
            a_ref[n * WINDOW:(n + 1) * WINDOW, g * KVW:(g + 1) * KVW] = o[g * WINDOW:(g + 1) * WINDOW, :]

    y = _dot(a_ref[...].astype(BF16), wo_ref[...])
    o_ref[...] = _layer_norm(ALPHA * x + _mod(gt_ref, batch) * y, g_ref[...], b_ref[...])


def _attn_prompt(x, mods, wq, tables, k, v, sinks, wo, ln_g, ln_b, n_sample_rows):
    bsz, seq, _ = x.shape
    rows = min(ATT_ROWS, seq)
    per = rows // WINDOW
    x_spec = pl.BlockSpec((None, rows, D_MODEL), lambda bi, i: (bi, i, 0))
    cur = pl.BlockSpec((None, rows, KVW), lambda bi, i: (bi, i, 0))
    prev = pl.BlockSpec((None, WINDOW, KVW), lambda bi, i: (bi, jnp.maximum(i * per - 1, 0), 0))
    tab = pl.BlockSpec((rows, LANES), lambda bi, i: (i, 0))
    return pl.pallas_call(
        _attn_prompt_kernel,
        grid=(bsz, seq // rows),
        in_specs=[pl.BlockSpec(memory_space=pltpu.SMEM), x_spec]
                 + _mod_specs(1, 3, 3, n_sample_rows, True)
                 + [_resident(wq.shape), tab, tab, tab, prev, cur, prev, cur, _resident(wo.shape)]
                 + _ln_specs(4),
        out_specs=x_spec,
        out_shape=jax.ShapeDtypeStruct((bsz, seq, D_MODEL), F32),
        scratch_shapes=[pltpu.VMEM((rows, QW), F32)],
        compiler_params=_params(("arbitrary", "arbitrary")),
        name="attn_prompt",
    )(sinks, x, mods, mods, mods, wq, *tables, k, k, v, v, wo, ln_g, ln_b)


def _attn_sample_kernel(sink_ref, x_ref, sh_ref, sc_ref, gt_ref, wq_ref, cos_ref, slo_ref, shi_ref,
                        kn_ref, vn_ref, kb_ref, vb_ref, wo_ref, g_ref, b_ref, o_ref,
                        q_ref, a_ref, *, seq):
    step = pl.program_id(0)
    pair = 2 * seq
    assert pair == SUBLANES
    grp, n_buf, _ = kb_ref.shape

    @pl.when(step == 0)
    def _():
        q = _rope(_dot(_modulate_bf16(x_ref[...], sh_ref[...], sc_ref[...]), wq_ref[...]),
                  cos_ref[...], slo_ref[...], shi_ref[...])
        q_ref[...] = q * (ATT_HD ** -0.5)

    n_rows = ATT_GROUP * pair
    n_all = ATT_KV_HEADS * n_rows
    n_new = 2 * pair
    q_pos = _iota((n_all, n_buf), 0) % seq + PAST_LEN
    k_pos = _iota((n_all, n_buf), 1) + (PAST_LEN - n_buf)
    buf_mask = (k_pos <= q_pos) & (k_pos > q_pos - WINDOW) & (k_pos >= 0)
    q_tok = _iota((n_all, n_new), 0) % seq
    new_col = _iota((n_all, n_new), 1)
    out_lane_head = _iota((n_rows, KVW), 1) // ATT_HD
    low = _iota((n_rows, KVW), 0) % pair < seq
    head_of_row = _iota((n_all, 1), 0) // pair
    sink_col = jnp.full((n_all, 1), sink_ref[0], F32)
    for hq in range(1, ATT_Q_HEADS):
        sink_col = jnp.where(head_of_row >= hq, sink_ref[hq], sink_col)
    pad_rows = jnp.zeros((n_new - pair, KVW), BF16)
    for p in range(grp // 2):
        r0 = pl.multiple_of(step * (grp * seq) + p * pair, pair)
        q_all = _stack_groups(q_ref[pl.ds(r0, pair), :].astype(BF16))
        q_bd = jnp.concatenate([_head_lanes(q_all, kh) for kh in range(ATT_KV_HEADS)], axis=0)
        k_new = jnp.concatenate([kn_ref[pl.ds(r0, pair), :].astype(BF16), pad_rows], axis=0)
        v_new = jnp.concatenate([vn_ref[pl.ds(r0, pair), :].astype(BF16), pad_rows], axis=0)
        s_new_all = _nt(q_bd, k_new)
        halves = []
        for half in range(2):
            bi = 2 * p + half
            tok = new_col - half * seq
            own_new = (tok >= 0) & (tok < seq) & (tok <= q_tok)
            s_buf = jnp.where(buf_mask, _nt(q_bd, kb_ref[bi].astype(BF16)), -jnp.inf)
            s_new = jnp.where(own_new, s_new_all, -jnp.inf)
            mx = jnp.maximum(jnp.maximum(jnp.max(s_buf, axis=-1, keepdims=True),
                                         jnp.max(s_new, axis=-1, keepdims=True)), sink_col)
            e_buf = jnp.exp(s_buf - mx)
            e_new = jnp.exp(s_new - mx)
            den = (jnp.sum(e_buf, axis=-1, keepdims=True) + jnp.sum(e_new, axis=-1, keepdims=True)
                   + jnp.exp(sink_col - mx))
            o_all = (_dot((e_buf / den).astype(BF16), vb_ref[bi].astype(BF16))
                     + _dot((e_new / den).astype(BF16), v_new))
            o_half = None
            for kh in range(ATT_KV_HEADS):
                part = jnp.where(out_lane_head == kh, o_all[kh * n_rows:(kh + 1) * n_rows, :], 0.0)
                o_half = part if o_half is None else o_half + part
            halves.append(o_half)
        o = jnp.where(low, halves[0], halves[1])
        for g in range(ATT_GROUP):
            a_ref[pl.ds(r0, pair), g * KVW:(g + 1) * KVW] = o[g * pair:(g + 1) * pair, :]

    @pl.when(step == pl.num_programs(0) - 1)
    def _():
        y = _dot(a_ref[...].astype(BF16), wo_ref[...])
        o_ref[...] = _layer_norm(ALPHA * x_ref[...] + gt_ref[...] * y, g_ref[...], b_ref[...])


def _attn_sample(x, mods, wq, tables, k_new, v_new, k_buf, v_buf, sinks, wo, ln_g, ln_b, seq):
    t = x.shape[0]
    grp = ATT_SAMPLE_GROUP
    buf_spec = pl.BlockSpec((grp,) + k_buf.shape[1:], lambda i: (i, 0, 0))
    return pl.pallas_call(
        functools.partial(_attn_sample_kernel, seq=seq),
        grid=(t // (grp * seq),),
        in_specs=[pl.BlockSpec(memory_space=pltpu.SMEM), _resident(x.shape)]
                 + _mod_specs(1, 3, 3, t, False)
                 + [_resident(wq.shape)] + [_resident(tab.shape) for tab in tables]
                 + [_resident(k_new.shape), _resident(v_new.shape), buf_spec, buf_spec, _resident(wo.shape)]
                 + _ln_specs(4),
        out_specs=pl.BlockSpec((t, D_MODEL), lambda i: (0, 0)),
        out_shape=jax.ShapeDtypeStruct((t, D_MODEL), F32),
        scratch_shapes=[pltpu.VMEM((t, QW), F32), pltpu.VMEM((t, QW), F32)],
        compiler_params=_params(("arbitrary",)),
        name="attn_sample",
    )(sinks, x, mods, mods, mods, wq, *tables, k_new, v_new, k_buf, v_buf, wo, ln_g, ln_b)


def _rope_tables(pos):
    half = ROT_DIM // 2
    inv_freq = ROPE_THETA ** (-jnp.arange(half, dtype=F32) * 2.0 / ROT_DIM)
    ang = pos.astype(F32)[:, None] * inv_freq[None, :]
    cos, sin = jnp.cos(ang), jnp.sin(ang)
    dim = np.arange(LANES) % ATT_HD
    freq = np.arange(half)[:, None]
    lo = (dim[None, :] == freq).astype(np.float32)
    hi = (dim[None, :] == freq + half).astype(np.float32)
    spread = lambda a, m: jnp.dot(a, jnp.asarray(m), precision=HIGHEST)
    return (spread(cos, lo + hi) + jnp.asarray((dim >= ROT_DIM).astype(np.float32)),
            spread(sin, -lo), spread(sin, hi))


def kernel(x_prompt, x_sample, c_prompt, c_sample, state_mlstm_C, state_mlstm_n, state_mlstm_m,
           cache_win_k, cache_win_v, ada_w, ada_b, ffn1_up, ffn1_down, ffn2_up, ffn2_down, ln_g, ln_b,
           ml_w_in, ml_b_gates, ml_norm_w, ml_w_out, kv_ada_w, kv_ada_b, kv_w,
           att_w_q, att_sinks, att_w_o):
    bp, sp, _ = x_prompt.shape
    bs, ss, _ = x_sample.shape
    n_buf = cache_win_k.shape[1]
    ts = bs * ss

    c_all = jnp.concatenate([jnp.repeat(c_sample, ss, axis=0), c_prompt,
                             jnp.zeros((-bp % (2 * SUBLANES), D_MODEL), F32)], axis=0)
    mods = _ada(c_all, ada_w, ada_b[:, None, :])
    kv_mods = _ada(c_all, kv_ada_w[None], kv_ada_b[None, None, :])

    bf = lambda w: w.astype(BF16)
    ffn_f32 = [(ffn1_up, ffn1_down, 0), (ffn2_up, ffn2_down, 0), (ffn1_up, ffn1_down, 1), (ffn2_up, ffn2_down, 1)]
    ffn_w = [(bf(ffn1_up[0]), bf(ffn1_down[0]))]
    w_in = bf(ml_w_in)
    gate_pad = LANES - 2 * ML_HEADS
    w_gate = bf(jnp.pad(ml_w_in[0, :, 2 * HQK + 2 * HV:], ((0, 0), (0, gate_pad))))
    b_gate = jnp.pad(ml_b_gates[0], (0, gate_pad))[None, :]
    norm_w = ml_norm_w[0][None, :]
    w_q = bf(att_w_q[0].reshape(D_MODEL, ATT_KV_HEADS, ATT_GROUP, ATT_HD)
             .transpose(0, 2, 1, 3).reshape(D_MODEL, QW))
    w_o = bf(att_w_o[0].reshape(ATT_KV_HEADS, ATT_GROUP, ATT_HD, D_MODEL)
             .transpose(1, 0, 2, 3).reshape(QW, D_MODEL))
    sinks = att_sinks[0]
    lng = ln_g.reshape(DEPTH * 3, 1, D_MODEL)
    lnb = ln_b.reshape(DEPTH * 3, 1, D_MODEL)

    tab_p = _rope_tables(jnp.arange(sp, dtype=jnp.int32))
    tab_s = _rope_tables(jnp.tile(PAST_LEN + jnp.arange(ss, dtype=jnp.int32), bs))

    xp = x_prompt.reshape(bp * sp, D_MODEL)
    xp, *cast, w_out, w_kv = _ffn(xp, mods, 0, 0, *ffn_w[0], lng, lnb, 0, ts, sp,
                                  cast=ffn_f32[1] + (ml_w_out, kv_w[None]))
    ffn_w.append(tuple(cast))
    w_out, w_kv = w_out[0], w_kv[0]
    xp, state_p, m_p = _mlstm_prompt(xp.reshape(bp, sp, D_MODEL), mods, w_in, w_gate, b_gate,
                                     norm_w, w_out, lng, lnb, ts)
    xp, k_p, v_p, *cast = _ffn(xp.reshape(bp * sp, D_MODEL), mods, 0, 6, *ffn_w[1], lng, lnb, 2, ts, sp,
                               kv=(kv_mods, w_kv, tab_p), cast=ffn_f32[2])
    ffn_w.append(tuple(cast))
    k_p = k_p.reshape(bp, sp, KVW)
    v_p = v_p.reshape(bp, sp, KVW)
    xp, *cast = _ffn(xp, mods, 1, 0, *ffn_w[2], lng, lnb, 3, ts, sp, cast=ffn_f32[3])
    ffn_w.append(tuple(cast))
    xp = _attn_prompt(xp.reshape(bp, sp, D_MODEL), mods, w_q, tab_p, k_p, v_p, sinks, w_o, lng, lnb, ts)
    xp = _ffn(xp.reshape(bp * sp, D_MODEL), mods, 1, 6, *ffn_w[3], lng, lnb, 5, ts, sp)
    y_prompt = xp.reshape(bp, sp, D_MODEL)
    keep = min(WINDOW, sp)
    win_k_p = k_p[:, sp - keep:].reshape(bp, keep, ATT_KV_HEADS, ATT_HD)
    win_v_p = v_p[:, sp - keep:].reshape(bp, keep, ATT_KV_HEADS, ATT_HD)

    xs = x_sample.reshape(ts, D_MODEL)
    xs = _ffn(xs, mods, 0, 0, *ffn_w[0], lng, lnb, 0, ts, None)
    m0 = jnp.pad(state_mlstm_m[0], ((0, 0), (0, LANES - ML_HEADS)))
    xs, c_s, n_s, m_s = _mlstm_sample(xs, mods, w_in, w_gate, b_gate, norm_w, w_out, lng, lnb,
                                      state_mlstm_C[0], state_mlstm_n[0].reshape(bs, HQK), m0, ss)
    xs, k_s, v_s = _ffn(xs, mods, 0, 6, *ffn_w[1], lng, lnb, 2, ts, None, kv=(kv_mods, w_kv, tab_s))
    xs = _ffn(xs, mods, 1, 0, *ffn_w[2], lng, lnb, 3, ts, None)
    xs = _attn_sample(xs, mods, w_q, tab_s, k_s, v_s,
                      cache_win_k.reshape(bs, n_buf, KVW), cache_win_v.reshape(bs, n_buf, KVW),
                      sinks, w_o, lng, lnb, ss)
    xs = _ffn(xs, mods, 1, 6, *ffn_w[3], lng, lnb, 5, ts, None)
    y_sample = xs.reshape(bs, ss, D_MODEL)
    win_k_s = jnp.concatenate([cache_win_k, k_s.reshape(bs, ss, ATT_KV_HEADS, ATT_HD)], axis=1)[:, -n_buf:]
    win_v_s = jnp.concatenate([cache_win_v, v_s.reshape(bs, ss, ATT_KV_HEADS, ATT_HD)], axis=1)[:, -n_buf:]

    return (y_prompt, y_sample,
            state_p[None, ..., :ML_DV], state_p[None, :, :, 0, ML_DV:], m_p[None, :, :ML_HEADS, 0],
            win_k_p, win_v_p,
            c_s[None], n_s.reshape(1, bs, ML_HEADS, ML_DK), m_s[None, :, :ML_HEADS],
            win_k_s, win_v_s)
```

```python
import functools

import numpy as np
import jax
import jax.numpy as jnp
from jax import lax
from jax.experimental import pallas as pl
from jax.experimental.pallas import tpu as pltpu

F32 = jnp.float32
BF16 = jnp.bfloat16

D_MODEL = 1024
DEPTH = 2
PAST_LEN = 8192
ML_HEADS = 4
ML_DK = D_MODEL // 8
ML_DV = D_MODEL // 4
ATT_Q_HEADS = 16
ATT_KV_HEADS = 4
ATT_GROUP = ATT_Q_HEADS // ATT_KV_HEADS
ATT_HD = 64
WINDOW = 128
ROT_DIM = ATT_HD // 4
ROPE_THETA = 500000.0
D_FF = 2816
ALPHA = (2 * DEPTH) ** 0.25
LN_EPS = 1e-5
HQK = ML_HEADS * ML_DK
HV = ML_HEADS * ML_DV
KVW = ATT_KV_HEADS * ATT_HD
QW = ATT_Q_HEADS * ATT_HD

LANES = 128
SUBLANES = 8
VMEM_LIMIT_BYTES = 60 * 1024 * 1024

FFN_ROWS = 1024
FFN_ROWS_KV = 1024
FFN_CHUNK = 256
ML_ROWS = 1024
ML_CHUNK = 128
ATT_ROWS = 1024
ADA_COLS = (1536, 1024)
SAMPLE_GROUP = 8

NT_DIMS = (((1,), (1,)), ((), ()))
HIGHEST = lax.Precision.HIGHEST


def _params(semantics):
    return pltpu.CompilerParams(dimension_semantics=semantics, vmem_limit_bytes=VMEM_LIMIT_BYTES)


def _const_spec(block, index):
    return pl.BlockSpec(block, lambda *_: index, pipeline_mode=pl.Buffered(1))


def _resident(shape):
    return _const_spec(shape, (0,) * len(shape))


def _dot(a, b):
    return jnp.dot(a, b, preferred_element_type=F32)


def _dot_exact(a, b):
    return jnp.dot(a, b, preferred_element_type=F32, precision=HIGHEST)


def _nt(a, b):
    return lax.dot_general(a, b, NT_DIMS, preferred_element_type=F32)


def _iota(shape, dim):
    return lax.broadcasted_iota(jnp.int32, shape, dim)


def _silu(x):
    half = 0.5 * x
    return half * jnp.tanh(half) + half


def _log_sigmoid(x):
    return jnp.minimum(x, 0.0) - jnp.log1p(jnp.exp(-jnp.abs(x)))


def _layer_norm(y, g, b):
    mu = jnp.mean(y, axis=-1, keepdims=True)
    d = y - mu
    var = jnp.mean(d * d, axis=-1, keepdims=True)
    return d * lax.rsqrt(var + LN_EPS) * g + b


def _mod(ref, batch):
    return ref[...] if batch is None else ref[pl.ds(batch, 1), :]


def _modulate_bf16(x, shift, scale):
    return (x * (1.0 + scale) + shift).astype(BF16)


def _rope(x, cos, sin_lo, sin_hi):
    width = x.shape[1]
    reps = width // LANES
    tile = lambda t: jnp.concatenate([t] * reps, axis=1) if reps > 1 else t
    x_up = pltpu.roll(x, width - ROT_DIM // 2, 1)
    x_dn = pltpu.roll(x, ROT_DIM // 2, 1)
    return x * tile(cos) + x_up * tile(sin_lo) + x_dn * tile(sin_hi)


def _mod_specs(layer, first, count, n_sample_rows, prompt):
    if prompt:
        blk = n_sample_rows // SUBLANES
        return [_const_spec((None, SUBLANES, D_MODEL), (layer, blk, first + j)) for j in range(count)]
    return [_const_spec((None, n_sample_rows, D_MODEL), (layer, 0, first + j)) for j in range(count)]


def _ln_specs(index):
    return [_const_spec((None, 1, D_MODEL), (index, 0, 0))] * 2


def _ada_kernel(c_ref, w_ref, b_ref, o_ref, ca_ref):
    @pl.when((pl.program_id(0) == 0) & (pl.program_id(1) == 0))
    def _():
        ca_ref[...] = _silu(c_ref[...]).astype(BF16)

    o_ref[...] = _dot(ca_ref[...], w_ref[...].astype(BF16)) + b_ref[...]


def _ada(c, w, b):
    n_l, k, n = w.shape
    m = c.shape[0]
    cols = next(width for width in ADA_COLS if n % width == 0)
    return pl.pallas_call(
        _ada_kernel,
        grid=(n_l, n // cols),
        in_specs=[pl.BlockSpec((m, k), lambda l, j: (0, 0)),
                  pl.BlockSpec((None, k, cols), lambda l, j: (l, 0, j)),
                  pl.BlockSpec((None, 1, cols), lambda l, j: (l, 0, j))],
        out_specs=pl.BlockSpec((None, m, cols), lambda l, j: (l, 0, j)),
        out_shape=jax.ShapeDtypeStruct((n_l, m, n), F32),
        scratch_shapes=[pltpu.VMEM((m, k), BF16)],
        compiler_params=_params(("arbitrary", "arbitrary")),
        name="ada",
    )(c, w, b)


def _ffn_kernel(x_ref, sh_ref, sc_ref, gt_ref, wa_ref, wu_ref, wd_ref, g_ref, b_ref, *rest,
                steps_per_seq, has_kv, n_cast):
    rest = list(rest)
    kv_in = [rest.pop(0) for _ in range(6)] if has_kv else None
    cast_in = [rest.pop(0) for _ in range(n_cast)]
    o_ref = rest.pop(0)
    kv_out = [rest.pop(0) for _ in range(2)] if has_kv else None
    cast_out = [rest.pop(0) for _ in range(n_cast)]
    h_ref, = rest

    batch = None if steps_per_seq is None else pl.program_id(0) // steps_per_seq
    x = x_ref[...]
    xm = _modulate_bf16(x, _mod(sh_ref, batch), _mod(sc_ref, batch))
    for c0 in range(0, D_FF, FFN_CHUNK):
        cw = min(FFN_CHUNK, D_FF - c0)
        a = _dot(xm, wa_ref[:, c0:c0 + cw])
        u = _dot(xm, wu_ref[:, c0:c0 + cw])
        h_ref[:, c0:c0 + cw] = (_silu(a) * u).astype(BF16)
    y = ALPHA * x + (0.5 * _mod(gt_ref, batch)) * _dot(h_ref[...], wd_ref[...])
    out = _layer_norm(y, g_ref[...], b_ref[...])
    o_ref[...] = out
    if has_kv:
        ksh_ref, ksc_ref, wkv_ref, cos_ref, slo_ref, shi_ref = kv_in
        kv = _dot(_modulate_bf16(out, _mod(ksh_ref, batch), _mod(ksc_ref, batch)), wkv_ref[...])
        kv_out[0][...] = _rope(kv[:, :KVW], cos_ref[...], slo_ref[...], shi_ref[...])
        kv_out[1][...] = kv[:, KVW:]
    for src, dst in zip(cast_in, cast_out):
        dst[...] = src[...].astype(BF16)


def _ffn(x, mods, layer, first, w_up, w_down, ln_g, ln_b, ln_index, n_sample_rows, seq, kv=None, cast=None):
    t = x.shape[0]
    rows = min(FFN_ROWS if kv is None else FFN_ROWS_KV, t)
    steps = t // rows
    x_spec = pl.BlockSpec((rows, D_MODEL), lambda i: (i, 0))
    in_specs = ([x_spec] + _mod_specs(layer, first, 3, n_sample_rows, seq is not None)
                + [_const_spec((D_MODEL, D_FF), (0, 0)), _const_spec((D_MODEL, D_FF), (0, 1)),
                   _resident(w_down.shape)]
                + _ln_specs(ln_index))
    args = [x, mods, mods, mods, w_up, w_up, w_down, ln_g, ln_b]
    out_specs = [x_spec]
    out_shape = [jax.ShapeDtypeStruct((t, D_MODEL), F32)]
    if kv is not None:
        kv_mods, w_kv, tables = kv
        pos_blocks = tables[0].shape[0] // rows
        tab_spec = pl.BlockSpec((rows, LANES), lambda i: (i % pos_blocks, 0))
        kv_spec = pl.BlockSpec((rows, KVW), lambda i: (i, 0))
        in_specs += (_mod_specs(0, 0, 2, n_sample_rows, seq is not None)
                     + [_resident(w_kv.shape), tab_spec, tab_spec, tab_spec])
        args += [kv_mods, kv_mods, w_kv, *tables]
        out_specs += [kv_spec, kv_spec]
        out_shape += [jax.ShapeDtypeStruct((t, KVW), F32)] * 2
    n_cast = 0
    if cast is not None:
        up_f32, down_f32, src_layer, *extra = cast
        up_rows = D_MODEL // steps
        down_rows = 2 * D_FF // steps
        assert up_rows % 16 == 0 and down_rows % 16 == 0 and steps % 2 == 0
        in_specs += [pl.BlockSpec((None, up_rows, 2 * D_FF), lambda i: (src_layer, i, 0)),
                     pl.BlockSpec((None, down_rows, D_MODEL), lambda i: (src_layer, i // 2, 0))]
        args += [up_f32, down_f32]
        out_specs += [pl.BlockSpec((up_rows, 2 * D_FF), lambda i: (i, 0)),
                      pl.BlockSpec((down_rows, D_MODEL), lambda i: (i // 2, 0))]
        out_shape += [jax.ShapeDtypeStruct((D_MODEL, 2 * D_FF), BF16),
                      jax.ShapeDtypeStruct((D_FF, D_MODEL), BF16)]
        for w in extra:
            spec = pl.BlockSpec((None, up_rows, w.shape[2]), lambda i: (0, i, 0))
            in_specs.append(spec)
            args.append(w)
            out_specs.append(spec)
            out_shape.append(jax.ShapeDtypeStruct(w.shape, BF16))
        n_cast = 2 + len(extra)
    outs = pl.pallas_call(
        functools.partial(_ffn_kernel, steps_per_seq=None if seq is None else seq // rows,
                          has_kv=kv is not None, n_cast=n_cast),
        grid=(steps,),
        in_specs=in_specs,
        out_specs=out_specs,
        out_shape=out_shape,
        scratch_shapes=[pltpu.VMEM((rows, D_FF), BF16)],
        compiler_params=_params(("arbitrary",)),
        name="ffn" + ("_kv" if kv is not None else "") + ("_cast" if cast is not None else ""),
    )(*args)
    return outs[0] if len(outs) == 1 else outs


def _head_norm(hh, w_row):
    mu = jnp.mean(hh, axis=-1, keepdims=True)
    d = hh - mu
    var = jnp.mean(d * d, axis=-1, keepdims=True)
    return d * lax.rsqrt(var + LN_EPS) * w_row


def _split3(x, axis):
    hi = x.astype(BF16)
    r1 = x - hi.astype(F32)
    mid = r1.astype(BF16)
    lo = (r1 - mid.astype(F32)).astype(BF16)
    return jnp.concatenate([hi, mid, lo], axis=axis)


def _tile_lanes(t, reps):
    return jnp.concatenate([t] * reps, axis=1)


def _mlstm_prompt_kernel(x_ref, sh_ref, sc_ref, gt_ref, wp_ref, wg_ref, bg_ref, nw_ref, wo_ref, g_ref, b_ref,
                         o_ref, c_out, m_out, proj_ref, rep_ref, y_ref, c_ref, mm_ref):
    batch = pl.program_id(0)
    step = pl.program_id(1)
    size = ML_CHUNK
    assert size == LANES and ML_DK == LANES

    @pl.when(step == 0)
    def _():
        c_ref[...] = jnp.zeros_like(c_ref)
        mm_ref[...] = jnp.zeros_like(mm_ref)

    x = x_ref[...]
    rows = x.shape[0]
    n_chunks = rows // size
    h = _modulate_bf16(x, _mod(sh_ref, batch), _mod(sc_ref, batch))
    gates = _dot(h, wg_ref[...]) + bg_ref[...]

    r = _iota((size, size), 0)
    c = _iota((size, size), 1)
    causal = c <= r
    eye = c == r
    tri3 = _tile_lanes(jnp.where(causal, 1.0, 0.0).astype(BF16), 3)
    n_rep = 2 * ML_HEADS
    spread = jnp.where(_iota((LANES, n_rep * LANES), 1) // LANES == _iota((LANES, n_rep * LANES), 0),
                       1.0, 0.0).astype(BF16)
    spread3 = jnp.concatenate([spread] * 3, axis=0)
    ones3 = jnp.ones((3 * size, LANES), BF16)
    sr = _iota((2 * SUBLANES, LANES), 0)
    sl = _iota((2 * SUBLANES, LANES), 1)
    pick = (jnp.where((sr < ML_HEADS) & (sl == sr), 1.0, 0.0)
            - jnp.where((sr < ML_HEADS) & (sl == sr + ML_HEADS), 1.0, 0.0)
            + jnp.where((sr >= SUBLANES) & (sr < SUBLANES + ML_HEADS) & (sl == sr - SUBLANES + ML_HEADS),
                        1.0, 0.0))
    pick3 = _tile_lanes(pick.astype(BF16), 3)

    log_f = _log_sigmoid(gates)
    bwide = jnp.concatenate([_dot(tri3, _split3(log_f[ci * size:(ci + 1) * size, :], 0))
                             for ci in range(n_chunks)], axis=0)
    src3 = _split3(jnp.where(_iota((rows, LANES), 1) < ML_HEADS, gates, bwide), 1)
    rep_ref[...] = _dot(src3, spread3)
    ab_rows = _nt(pick3, src3)
    a_rows = ab_rows[:SUBLANES, :]
    b_rows = ab_rows[SUBLANES:, :]
    lane_in_chunk = _iota((SUBLANES, rows), 1) % size
    run = a_rows
    shift = 1
    while shift < size:
        run = jnp.maximum(run, jnp.where(lane_in_chunk >= shift, pltpu.roll(run, shift, 1), -jnp.inf))
        shift *= 2
    m_rows = [mm_ref[...]]
    g_rows = []
    for ci in range(n_chunks):
        run_c = run[:, ci * size:(ci + 1) * size]
        g_rows.append(jnp.maximum(run_c, m_rows[-1]))
        end = (ci + 1) * size - 1
        m_end = b_rows[:, end:end + 1] + jnp.maximum(run[:, end:end + 1], m_rows[-1][:, :1])
        m_rows.append(jnp.broadcast_to(m_end, (SUBLANES, LANES)))
    mm_ref[...] = m_rows[-1]

    proj_ref[...] = _dot(h, wp_ref[...])

    states = [c_ref[hd] for hd in range(ML_HEADS)]
    for ci in range(n_chunks):
        lo, hi = ci * size, (ci + 1) * size
        for hd in range(ML_HEADS):
            i_rep = rep_ref[lo:hi, hd * LANES:(hd + 1) * LANES]
            b_rep = rep_ref[lo:hi, (ML_HEADS + hd) * LANES:(ML_HEADS + hd + 1) * LANES]
            m_prev = m_rows[ci][hd:hd + 1, :]
            m_new = m_rows[ci + 1][hd:hd + 1, :]
            a_row = a_rows[hd:hd + 1, lo:hi]
            g_rep = _dot(_split3(jnp.where(eye, g_rows[ci][hd:hd + 1, :], 0.0), 1), ones3)
            w = jnp.where(causal, jnp.exp(a_row - g_rep), 0.0)
            w_init = jnp.exp(m_prev - g_rep)
            floor = jnp.exp(-(b_rep + g_rep))
            b_last = b_rep[size - 1:size, :]
            w_end = jnp.exp(b_last - b_rep + i_rep - m_new)
            decay = jnp.exp(b_last + m_prev - m_new)

            q = proj_ref[lo:hi, hd * ML_DK:(hd + 1) * ML_DK]
            k = proj_ref[lo:hi, HQK + hd * ML_DK:HQK + (hd + 1) * ML_DK] * (ML_DK ** -0.5)
            v = proj_ref[lo:hi, 2 * HQK + hd * ML_DV:2 * HQK + (hd + 1) * ML_DV]
            qb = q.astype(BF16)
            vb = v.astype(BF16)
            c_mat = states[hd][:, :ML_DV]
            n_row = states[hd][0:1, ML_DV:]
            s = _nt(qb, k.astype(BF16)) * w
            num = _dot(s.astype(BF16), vb) + _tile_lanes(w_init, 2) * _dot(qb, c_mat.astype(BF16))
            den = (jnp.sum(s, axis=-1, keepdims=True)
                   + w_init[:, :1] * jnp.sum(q * n_row, axis=-1, keepdims=True))
            hh = num / jnp.maximum(jnp.abs(den), floor[:, :1])
            kw = k * w_end
            c_new = _tile_lanes(decay, 2) * c_mat + _dot(kw.T.astype(BF16), vb)
            n_new = decay * n_row + jnp.sum(kw, axis=0, keepdims=True)
            states[hd] = jnp.concatenate([c_new, jnp.broadcast_to(n_new, (ML_DK, LANES))], axis=1)
            y_ref[lo:hi, hd * ML_DV:(hd + 1) * ML_DV] = _head_norm(hh, nw_ref[:, hd * ML_DV:(hd + 1) * ML_DV])
    for hd in range(ML_HEADS):
        c_ref[hd] = states[hd]

    o_gate = jax.nn.sigmoid(proj_ref[:, 2 * HQK + HV:2 * HQK + 2 * HV])
    y = _dot((y_ref[...] * o_gate).astype(BF16), wo_ref[...])
    o_ref[...] = _layer_norm(ALPHA * x + _mod(gt_ref, batch) * y, g_ref[...], b_ref[...])

    @pl.when(step == pl.num_programs(1) - 1)
    def _():
        c_out[...] = c_ref[...]
        m_out[...] = mm_ref[...]


def _mlstm_weight_specs(wg, bg, nw, wo):
    return [_const_spec((None, D_MODEL, 2 * HQK + 2 * HV), (0, 0, 0)), _resident(wg.shape),
            _resident(bg.shape), _resident(nw.shape), _resident(wo.shape)] + _ln_specs(1)


def _mlstm_prompt(x, mods, w_in, wg, bg, nw, wo, ln_g, ln_b, n_sample_rows):
    bsz, seq, _ = x.shape
    rows = min(ML_ROWS, seq)
    x_spec = pl.BlockSpec((None, rows, D_MODEL), lambda bi, i: (bi, i, 0))
    state_shape = (ML_HEADS, ML_DK, ML_DV + LANES)
    return pl.pallas_call(
        _mlstm_prompt_kernel,
        grid=(bsz, seq // rows),
        in_specs=[x_spec] + _mod_specs(0, 3, 3, n_sample_rows, True) + _mlstm_weight_specs(wg, bg, nw, wo),
        out_specs=[x_spec, pl.BlockSpec((None,) + state_shape, lambda bi, i: (bi, 0, 0, 0)),
                   pl.BlockSpec((None, SUBLANES, LANES), lambda bi, i: (bi, 0, 0))],
        out_shape=[jax.ShapeDtypeStruct((bsz, seq, D_MODEL), F32),
                   jax.ShapeDtypeStruct((bsz,) + state_shape, F32),
                   jax.ShapeDtypeStruct((bsz, SUBLANES, LANES), F32)],
        scratch_shapes=[pltpu.VMEM((rows, 2 * HQK + 2 * HV), F32),
                        pltpu.VMEM((rows, 2 * ML_HEADS * LANES), F32),
                        pltpu.VMEM((rows, HV), F32),
                        pltpu.VMEM(state_shape, F32),
                        pltpu.VMEM((SUBLANES, LANES), F32)],
        compiler_params=_params(("arbitrary", "arbitrary")),
        name="mlstm_prompt",
    )(x, mods, mods, mods, w_in, wg, bg, nw, wo, ln_g, ln_b)


def _mlstm_sample_kernel(x_ref, sh_ref, sc_ref, gt_ref, wp_ref, wg_ref, bg_ref, nw_ref, wo_ref, g_ref, b_ref,
                         c0_ref, n0_ref, m0_ref, o_ref, c_out, n_out, m_out,
                         proj_ref, gate_ref, y_ref, *, seq):
    step = pl.program_id(0)
    grp = c0_ref.shape[0]
    rows = grp * seq

    @pl.when(step == 0)
    def _():
        h = _modulate_bf16(x_ref[...], sh_ref[...], sc_ref[...])
        proj_ref[...] = _dot(h, wp_ref[...])
        gate_ref[...] = _dot(h, wg_ref[...]) + bg_ref[...]

    r = _iota((rows, rows), 0)
    c = _iota((rows, rows), 1)
    same = (r // seq) == (c // seq)
    causal = same & (c <= r)
    eye = r == c
    last_of_row = c == (r // seq) * seq + (seq - 1)
    same_f = same.astype(F32)
    expand = (_iota((rows, grp), 0) // seq == _iota((rows, grp), 1)).astype(F32)
    pick_last = (_iota((grp, rows), 1) == _iota((grp, rows), 0) * seq + (seq - 1)).astype(F32)
    seq_of_row = _iota((rows, ML_DV), 0) // seq
    col_seq = _iota((ML_DK, rows), 1) // seq
    lane = _iota((rows, LANES), 1)
    to_row = lambda col: jnp.sum(jnp.where(eye, col, 0.0), axis=0, keepdims=True)

    r0 = pl.multiple_of(step * rows, rows)
    gates = gate_ref[pl.ds(r0, rows), :]
    log_f = _log_sigmoid(gates)
    bwide = _dot_exact(causal.astype(F32), log_f)
    bsum = _dot_exact(same_f, log_f)
    m_prev_wide = _dot_exact(expand, m0_ref[...])
    n_rows_wide = _dot_exact(expand, n0_ref[...])
    n_pieces = []
    m_wide = jnp.zeros((rows, LANES), F32)
    for hd in range(ML_HEADS):
        q = proj_ref[pl.ds(r0, rows), hd * ML_DK:(hd + 1) * ML_DK]
        k = proj_ref[pl.ds(r0, rows), HQK + hd * ML_DK:HQK + (hd + 1) * ML_DK] * (ML_DK ** -0.5)
        v = proj_ref[pl.ds(r0, rows), 2 * HQK + hd * ML_DV:2 * HQK + (hd + 1) * ML_DV]
        bcol = bwide[:, ML_HEADS + hd:ML_HEADS + hd + 1]
        b_last = bsum[:, ML_HEADS + hd:ML_HEADS + hd + 1]
        icol = gates[:, hd:hd + 1]
        m_prev = m_prev_wide[:, hd:hd + 1]
        n_rows = n_rows_wide[:, hd * ML_DK:(hd + 1) * ML_DK]
        log_w = jnp.where(causal, bcol - to_row(bcol) + to_row(icol), -jnp.inf)
        log_init = bcol + m_prev
        m_t = jnp.maximum(log_init, jnp.max(log_w, axis=-1, keepdims=True))
        w = jnp.exp(log_w - m_t)
        w_init = jnp.exp(log_init - m_t)
        qb = q.astype(BF16)
        s = _nt(qb, k.astype(BF16)) * w
        inter = None
        for bi in range(grp):
            cand = _dot(qb, c0_ref[bi, hd].astype(BF16))
            inter = cand if inter is None else jnp.where(seq_of_row == bi, cand, inter)
        num = _dot(s.astype(BF16), v.astype(BF16)) + w_init * inter
        den = jnp.sum(s, axis=-1, keepdims=True) + w_init * jnp.sum(q * n_rows, axis=-1, keepdims=True)
        hh = num / jnp.maximum(jnp.abs(den), jnp.exp(-m_t))
        m_new = jnp.sum(jnp.where(last_of_row, to_row(m_t), 0.0), axis=1, keepdims=True)
        w_end = jnp.exp(b_last - bcol + icol - m_new)
        decay = jnp.exp(b_last + m_prev - m_new)
        kw = k * w_end
        kw_t = kw.T
        lhs = jnp.concatenate([jnp.where(col_seq == bi, kw_t, 0.0) for bi in range(grp)], axis=0)
        upd = _dot(lhs.astype(BF16), v.astype(BF16))
        for bi in range(grp):
            c_out[bi, hd] = (decay[bi * seq:bi * seq + 1, :] * c0_ref[bi, hd]
                             + upd[bi * ML_DK:(bi + 1) * ML_DK, :])
        n_pieces.append(decay * n_rows + _dot_exact(same_f, kw))
        m_wide = jnp.where(lane == hd, m_new, m_wide)
        y_ref[pl.ds(r0, rows), hd * ML_DV:(hd + 1) * ML_DV] = _head_norm(
            hh, nw_ref[:, hd * ML_DV:(hd + 1) * ML_DV])
    n_out[...] = _dot_exact(pick_last, jnp.concatenate(n_pieces, axis=1))
    m_out[...] = _dot_exact(pick_last, m_wide)

    @pl.when(step == pl.num_programs(0) - 1)
    def _():
        o_gate = jax.nn.sigmoid(proj_ref[:, 2 * HQK + HV:2 * HQK + 2 * HV])
        y = _dot((y_ref[...] * o_gate).astype(BF16), wo_ref[...])
        o_ref[...] = _layer_norm(ALPHA * x_ref[...] + gt_ref[...] * y, g_ref[...], b_ref[...])


def _mlstm_sample(x, mods, w_in, wg, bg, nw, wo, ln_g, ln_b, c0, n0, m0, seq):
    t = x.shape[0]
    grp = SAMPLE_GROUP
    state_specs = [pl.BlockSpec((grp, ML_HEADS, ML_DK, ML_DV), lambda i: (i, 0, 0, 0)),
                   pl.BlockSpec((grp, HQK), lambda i: (i, 0)),
                   pl.BlockSpec((grp, LANES), lambda i: (i, 0))]
    return pl.pallas_call(
        functools.partial(_mlstm_sample_kernel, seq=seq),
        grid=(t // (grp * seq),),
        in_specs=[_resident(x.shape)] + _mod_specs(0, 3, 3, t, False)
                 + _mlstm_weight_specs(wg, bg, nw, wo) + state_specs,
        out_specs=[pl.BlockSpec((t, D_MODEL), lambda i: (0, 0))] + state_specs,
        out_shape=[jax.ShapeDtypeStruct((t, D_MODEL), F32),
                   jax.ShapeDtypeStruct(c0.shape, F32),
                   jax.ShapeDtypeStruct(n0.shape, F32),
                   jax.ShapeDtypeStruct(m0.shape, F32)],
        scratch_shapes=[pltpu.VMEM((t, 2 * HQK + 2 * HV), F32),
                        pltpu.VMEM((t, LANES), F32),
                        pltpu.VMEM((t, HV), F32)],
        compiler_params=_params(("arbitrary",)),
        name="mlstm_sample",
    )(x, mods, mods, mods, w_in, wg, bg, nw, wo, ln_g, ln_b, c0, n0, m0)


def _sink_column(sink_ref, kh, rows_per_head):
    rid = _iota((ATT_GROUP * rows_per_head, 1), 0)
    col = jnp.full((ATT_GROUP * rows_per_head, 1), sink_ref[kh * ATT_GROUP], F32)
    for gq in range(1, ATT_GROUP):
        col = jnp.where(rid >= gq * rows_per_head, sink_ref[kh * ATT_GROUP + gq], col)
    return col


def _head_lanes(x, kh):
    return jnp.where(_iota(x.shape, 1) // ATT_HD == kh, x, 0.0).astype(BF16)


def _stack_groups(x):
    return jnp.concatenate([x[:, g * KVW:(g + 1) * KVW] for g in range(ATT_GROUP)], axis=0)


def _attn_prompt_kernel(sink_ref, x_ref, sh_ref, sc_ref, gt_ref, wq_ref, cos_ref, slo_ref, shi_ref,
                        kp_ref, kc_ref, vp_ref, vc_ref, wo_ref, g_ref, b_ref, o_ref, a_ref):
    batch = pl.program_id(0)
    step = pl.program_id(1)
    x = x_ref[...]
    rows = x.shape[0]
    n_blk = rows // WINDOW
    q = _rope(_dot(_modulate_bf16(x, _mod(sh_ref, batch), _mod(sc_ref, batch)), wq_ref[...]),
              cos_ref[...], slo_ref[...], shi_ref[...])
    q = (q * (ATT_HD ** -0.5)).astype(BF16)

    keys = jnp.concatenate([kp_ref[...], kc_ref[...]], axis=0)
    vals = jnp.concatenate([vp_ref[...], vc_ref[...]], axis=0)
    k_heads = [_head_lanes(keys, kh) for kh in range(ATT_KV_HEADS)]
    v_heads = [_head_lanes(vals, kh) for kh in range(ATT_KV_HEADS)]

    rid = _iota((ATT_GROUP * WINDOW, WINDOW), 0) % WINDOW
    cid = _iota((ATT_GROUP * WINDOW, WINDOW), 1)
    from_prev = cid > rid
    no_prev = jnp.where(step > 0, 0.0, -jnp.inf)
    sinks = [_sink_column(sink_ref, kh, WINDOW) for kh in range(ATT_KV_HEADS)]

    scores = [[_nt(_stack_groups(q[n * WINDOW:(n + 1) * WINDOW, :]),
                   k_heads[kh][n * WINDOW:(n + 2) * WINDOW, :])
               for kh in range(ATT_KV_HEADS)] for n in range(n_blk)]
    probs = []
    for n in range(n_blk):
        row = []
        for kh in range(ATT_KV_HEADS):
            s_prev = scores[n][kh][:, :WINDOW]
            if n == 0:
                s_prev = s_prev + no_prev
            s = jnp.where(from_prev, s_prev, scores[n][kh][:, WINDOW:])
            mx = jnp.maximum(jnp.max(s, axis=-1, keepdims=True), sinks[kh])
            p = jnp.exp(s - mx)
            den = jnp.sum(p, axis=-1, keepdims=True) + jnp.exp(sinks[kh] - mx)
            pn = p / den
            row.append(jnp.where(from_prev, pn, 0.0).astype(BF16))
            row.append(jnp.where(from_prev, 0.0, pn).astype(BF16))
        probs.append(jnp.concatenate(row, axis=1))
    for n in range(n_blk):
        v_all = jnp.concatenate([v_heads[kh][n * WINDOW:(n + 2) * WINDOW, :]
                                 for kh in range(ATT_KV_HEADS)], axis=0)
        o = _dot(probs[n], v_all)
        for g in range(ATT_GROUP):
            a_ref[n * WINDOW:(n + 1) * WINDOW, g * KVW:(g + 1) * KVW] = o[g * WINDOW:(g + 1) * WINDOW, :]

    y = _dot(a_ref[...].astype(BF16), wo_ref[...])
    o_ref[...] = _layer_norm(ALPHA * x + _mod(gt_ref, batch) * y, g_ref[...], b_ref[...])


def _attn_prompt(x, mods, wq, tables, k, v, sinks, wo, ln_g, ln_b, n_sample_rows):
    bsz, seq, _ = x.shape
    rows = min(ATT_ROWS, seq)
    per = rows // WINDOW
    x_spec = pl.BlockSpec((None, rows, D_MODEL), lambda bi, i: (bi, i, 0))
    cur = pl.BlockSpec((None, rows, KVW), lambda bi, i: (bi, i, 0))
    prev = pl.BlockSpec((None, WINDOW, KVW), lambda bi, i: (bi, jnp.maximum(i * per - 1, 0), 0))
    tab = pl.BlockSpec((rows, LANES), lambda bi, i: (i, 0))
    return pl.pallas_call(
        _attn_prompt_kernel,
        grid=(bsz, seq // rows),
        in_specs=[pl.BlockSpec(memory_space=pltpu.SMEM), x_spec]
                 + _mod_specs(1, 3, 3, n_sample_rows, True)
                 + [_resident(wq.shape), tab, tab, tab, prev, cur, prev, cur, _resident(wo.shape)]
                 + _ln_specs(4),
        out_specs=x_spec,
        out_shape=jax.ShapeDtypeStruct((bsz, seq, D_MODEL), F32),
        scratch_shapes=[pltpu.VMEM((rows, QW), F32)],
        compiler_params=_params(("arbitrary", "arbitrary")),
        name="attn_prompt",
    )(sinks, x, mods, mods, mods, wq, *tables, k, k, v, v, wo, ln_g, ln_b)


def _attn_sample_kernel(sink_ref, x_ref, sh_ref, sc_ref, gt_ref, wq_ref, cos_ref, slo_ref, shi_ref,
                        kn_ref, vn_ref, kb_ref, vb_ref, wo_ref, g_ref, b_ref, o_ref,
                        q_ref, a_ref, *, seq):
    step = pl.program_id(0)
    pair = 2 * seq
    assert pair == SUBLANES
    grp, n_buf, _ = kb_ref.shape

    @pl.when(step == 0)
    def _():
        q = _rope(_dot(_modulate_bf16(x_ref[...], sh_ref[...], sc_ref[...]), wq_ref[...]),
                  cos_ref[...], slo_ref[...], shi_ref[...])
        q_ref[...] = q * (ATT_HD ** -0.5)

    n_rows = ATT_GROUP * pair
    n_all = ATT_KV_HEADS * n_rows
    n_new = 2 * pair
    q_pos = _iota((n_all, n_buf), 0) % seq + PAST_LEN
    k_pos = _iota((n_all, n_buf), 1) + (PAST_LEN - n_buf)
    buf_mask = (k_pos <= q_pos) & (k_pos > q_pos - WINDOW) & (k_pos >= 0)
    q_tok = _iota((n_all, n_new), 0) % seq
    new_col = _iota((n_all, n_new), 1)
    out_lane_head = _iota((n_rows, KVW), 1) // ATT_HD
    low = _iota((n_rows, KVW), 0) % pair < seq
    head_of_row = _iota((n_all, 1), 0) // pair
    sink_col = jnp.full((n_all, 1), sink_ref[0], F32)
    for hq in range(1, ATT_Q_HEADS):
        sink_col = jnp.where(head_of_row >= hq, sink_ref[hq], sink_col)
    pad_rows = jnp.zeros((n_new - pair, KVW), BF16)
    for p in range(grp // 2):
        r0 = pl.multiple_of(step * (grp * seq) + p * pair, pair)
        q_all = _stack_groups(q_ref[pl.ds(r0, pair), :].astype(BF16))
        q_bd = jnp.concatenate([_head_lanes(q_all, kh) for kh in range(ATT_KV_HEADS)], axis=0)
        k_new = jnp.concatenate([kn_ref[pl.ds(r0, pair), :].astype(BF16), pad_rows], axis=0)
        v_new = jnp.concatenate([vn_ref[pl.ds(r0, pair), :].astype(BF16), pad_rows], axis=0)
        s_new_all = _nt(q_bd, k_new)
        halves = []
        for half in range(2):
            bi = 2 * p + half
            tok = new_col - half * seq
            own_new = (tok >= 0) & (tok < seq) & (tok <= q_tok)
            s_buf = jnp.where(buf_mask, _nt(q_bd, kb_ref[bi].astype(BF16)), -jnp.inf)
            s_new = jnp.where(own_new, s_new_all, -jnp.inf)
            mx = jnp.maximum(jnp.maximum(jnp.max(s_buf, axis=-1, keepdims=True),
                                         jnp.max(s_new, axis=-1, keepdims=True)), sink_col)
            e_buf = jnp.exp(s_buf - mx)
            e_new = jnp.exp(s_new - mx)
            den = (jnp.sum(e_buf, axis=-1, keepdims=True) + jnp.sum(e_new, axis=-1, keepdims=True)
                   + jnp.exp(sink_col - mx))
            o_all = (_dot((e_buf / den).astype(BF16), vb_ref[bi].astype(BF16))
                     + _dot((e_new / den).astype(BF16), v_new))
            o_half = None
            for kh in range(ATT_KV_HEADS):
                part = jnp.where(out_lane_head == kh, o_all[kh * n_rows:(kh + 1) * n_rows, :], 0.0)
                o_half = part if o_half is None else o_half + part
            halves.append(o_half)
        o = jnp.where(low, halves[0], halves[1])
        for g in range(ATT_GROUP):
            a_ref[pl.ds(r0, pair), g * KVW:(g + 1) * KVW] = o[g * pair:(g + 1) * pair, :]

    @pl.when(step == pl.num_programs(0) - 1)
    def _():
        y = _dot(a_ref[...].astype(BF16), wo_ref[...])
        o_ref[...] = _layer_norm(ALPHA * x_ref[...] + gt_ref[...] * y, g_ref[...], b_ref[...])


def _attn_sample(x, mods, wq, tables, k_new, v_new, k_buf, v_buf, sinks, wo, ln_g, ln_b, seq):
    t = x.shape[0]
    grp = SAMPLE_GROUP
    buf_spec = pl.BlockSpec((grp,) + k_buf.shape[1:], lambda i: (i, 0, 0))
    return pl.pallas_call(
        functools.partial(_attn_sample_kernel, seq=seq),
        grid=(t // (grp * seq),),
        in_specs=[pl.BlockSpec(memory_space=pltpu.SMEM), _resident(x.shape)]
                 + _mod_specs(1, 3, 3, t, False)
                 + [_resident(wq.shape)] + [_resident(tab.shape) for tab in tables]
                 + [_resident(k_new.shape), _resident(v_new.shape), buf_spec, buf_spec, _resident(wo.shape)]
                 + _ln_specs(4),
        out_specs=pl.BlockSpec((t, D_MODEL), lambda i: (0, 0)),
        out_shape=jax.ShapeDtypeStruct((t, D_MODEL), F32),
        scratch_shapes=[pltpu.VMEM((t, QW), F32), pltpu.VMEM((t, QW), F32)],
        compiler_params=_params(("arbitrary",)),
        name="attn_sample",
    )(sinks, x, mods, mods, mods, wq, *tables, k_new, v_new, k_buf, v_buf, wo, ln_g, ln_b)


def _rope_tables(pos):
    half = ROT_DIM // 2
    inv_freq = ROPE_THETA ** (-jnp.arange(half, dtype=F32) * 2.0 / ROT_DIM)
    ang = pos.astype(F32)[:, None] * inv_freq[None, :]
    cos, sin = jnp.cos(ang), jnp.sin(ang)
    dim = np.arange(LANES) % ATT_HD
    freq = np.arange(half)[:, None]
    lo = (dim[None, :] == freq).astype(np.float32)
    hi = (dim[None, :] == freq + half).astype(np.float32)
    spread = lambda a, m: jnp.dot(a, jnp.asarray(m), precision=HIGHEST)
    return (spread(cos, lo + hi) + jnp.asarray((dim >= ROT_DIM).astype(np.float32)),
            spread(sin, -lo), spread(sin, hi))


def kernel(x_prompt, x_sample, c_prompt, c_sample, state_mlstm_C, state_mlstm_n, state_mlstm_m,
           cache_win_k, cache_win_v, ada_w, ada_b, ffn1_up, ffn1_down, ffn2_up, ffn2_down, ln_g, ln_b,
           ml_w_in, ml_b_gates, ml_norm_w, ml_w_out, kv_ada_w, kv_ada_b, kv_w,
           att_w_q, att_sinks, att_w_o):
    bp, sp, _ = x_prompt.shape
    bs, ss, _ = x_sample.shape
    n_buf = cache_win_k.shape[1]
    ts = bs * ss

    c_all = jnp.concatenate([jnp.repeat(c_sample, ss, axis=0), c_prompt,
                             jnp.zeros((-bp % (2 * SUBLANES), D_MODEL), F32)], axis=0)
    mods = _ada(c_all, ada_w, ada_b[:, None, :])
    kv_mods = _ada(c_all, kv_ada_w[None], kv_ada_b[None, None, :])

    bf = lambda w: w.astype(BF16)
    ffn_f32 = [(ffn1_up, ffn1_down, 0), (ffn2_up, ffn2_down, 0), (ffn1_up, ffn1_down, 1), (ffn2_up, ffn2_down, 1)]
    ffn_w = [(bf(ffn1_up[0]), bf(ffn1_down[0]))]
    w_in = bf(ml_w_in)
    gate_pad = LANES - 2 * ML_HEADS
    w_gate = bf(jnp.pad(ml_w_in[0, :, 2 * HQK + 2 * HV:], ((0, 0), (0, gate_pad))))
    b_gate = jnp.pad(ml_b_gates[0], (0, gate_pad))[None, :]
    norm_w = ml_norm_w[0][None, :]
    w_q = bf(att_w_q[0].reshape(D_MODEL, ATT_KV_HEADS, ATT_GROUP, ATT_HD)
             .transpose(0, 2, 1, 3).reshape(D_MODEL, QW))
    w_o = bf(att_w_o[0].reshape(ATT_KV_HEADS, ATT_GROUP, ATT_HD, D_MODEL)
             .transpose(1, 0, 2, 3).reshape(QW, D_MODEL))
    sinks = att_sinks[0]
    lng = ln_g.reshape(DEPTH * 3, 1, D_MODEL)
    lnb = ln_b.reshape(DEPTH * 3, 1, D_MODEL)

    tab_p = _rope_tables(jnp.arange(sp, dtype=jnp.int32))
    tab_s = _rope_tables(jnp.tile(PAST_LEN + jnp.arange(ss, dtype=jnp.int32), bs))

    xp = x_prompt.reshape(bp * sp, D_MODEL)
    xp, *cast, w_out, w_kv = _ffn(xp, mods, 0, 0, *ffn_w[0], lng, lnb, 0, ts, sp,
                                  cast=ffn_f32[1] + (ml_w_out, kv_w[None]))
    ffn_w.append(tuple(cast))
    w_out, w_kv = w_out[0], w_kv[0]
    xp, state_p, m_p = _mlstm_prompt(xp.reshape(bp, sp, D_MODEL), mods, w_in, w_gate, b_gate,
                                     norm_w, w_out, lng, lnb, ts)
    xp, k_p, v_p, *cast = _ffn(xp.reshape(bp * sp, D_MODEL), mods, 0, 6, *ffn_w[1], lng, lnb, 2, ts, sp,
                               kv=(kv_mods, w_kv, tab_p), cast=ffn_f32[2])
    ffn_w.append(tuple(cast))
    k_p = k_p.reshape(bp, sp, KVW)
    v_p = v_p.reshape(bp, sp, KVW)
    xp, *cast = _ffn(xp, mods, 1, 0, *ffn_w[2], lng, lnb, 3, ts, sp, cast=ffn_f32[3])
    ffn_w.append(tuple(cast))
    xp = _attn_prompt(xp.reshape(bp, sp, D_MODEL), mods, w_q, tab_p, k_p, v_p, sinks, w_o, lng, lnb, ts)
    xp = _ffn(xp.reshape(bp * sp, D_MODEL), mods, 1, 6, *ffn_w[3], lng, lnb, 5, ts, sp)
    y_prompt = xp.reshape(bp, sp, D_MODEL)
    keep = min(WINDOW, sp)
    win_k_p = k_p[:, sp - keep:].reshape(bp, keep, ATT_KV_HEADS, ATT_HD)
    win_v_p = v_p[:, sp - keep:].reshape(bp, keep, ATT_KV_HEADS, ATT_HD)

    xs = x_sample.reshape(ts, D_MODEL)
    xs = _ffn(xs, mods, 0, 0, *ffn_w[0], lng, lnb, 0, ts, None)
    m0 = jnp.pad(state_mlstm_m[0], ((0, 0), (0, LANES - ML_HEADS)))
    xs, c_s, n_s, m_s = _mlstm_sample(xs, mods, w_in, w_gate, b_gate, norm_w, w_out, lng, lnb,
                                      state_mlstm_C[0], state_mlstm_n[0].reshape(bs, HQK), m0, ss)
    xs, k_s, v_s = _ffn(xs, mods, 0, 6, *ffn_w[1], lng, lnb, 2, ts, None, kv=(kv_mods, w_kv, tab_s))
    xs = _ffn(xs, mods, 1, 0, *ffn_w[2], lng, lnb, 3, ts, None)
    xs = _attn_sample(xs, mods, w_q, tab_s, k_s, v_s,
                      cache_win_k.reshape(bs, n_buf, KVW), cache_win_v.reshape(bs, n_buf, KVW),
                      sinks, w_o, lng, lnb, ss)
    xs = _ffn(xs, mods, 1, 6, *ffn_w[3], lng, lnb, 5, ts, None)
    y_sample = xs.reshape(bs, ss, D_MODEL)
    win_k_s = jnp.concatenate([cache_win_k, k_s.reshape(bs, ss, ATT_KV_HEADS, ATT_HD)], axis=1)[:, -n_buf:]
    win_v_s = jnp.concatenate([cache_win_v, v_s.reshape(bs, ss, ATT_KV_HEADS, ATT_HD)], axis=1)[:, -n_buf:]

    return (y_prompt, y_sample,
            state_p[None, ..., :ML_DV], state_p[None, :, :, 0, ML_DV:], m_p[None, :, :ML_HEADS, 0],
            win_k_p, win_v_p,
            c_s[None], n_s.reshape(1, bs, ML_HEADS, ML_DK), m_s[None, :, :ML_HEADS],
            win_k_s, win_v_s)
```
